```python
import jax, jax.numpy as jnp
from jax import lax
import numpy as np

D_MODEL = 1024
BATCH = 8
SEQ = 4096
DEPTH = 2

ATTN_HEAD_DIM = 64
ATTN_GROUPS = ((128, 1), (512, 4), (2048, 16))
ATTN_HEADS_PER_GROUP = 4
ATTN_HEADS = ATTN_HEADS_PER_GROUP * len(ATTN_GROUPS)
ATTN_WIDTH = ATTN_HEADS * ATTN_HEAD_DIM
ATTN_OUT_WIDTH = ATTN_HEADS_PER_GROUP * ATTN_HEAD_DIM
BAND_BLOCK = 128
CONV_WIDTH = 3 * D_MODEL // 4
CONV_K = 3
HGRN_HEAD_DIM = 128
HGRN_WIDTH = 3 * D_MODEL // 4
HGRN_HEADS = HGRN_WIDTH // HGRN_HEAD_DIM
HGRN_CHUNK = 64
LB_FLOOR = 1e-30
N_BRANCHES = 3
SPLIT_SIZES = (ATTN_WIDTH,) * 3 + (CONV_WIDTH,) * 3 + (HGRN_WIDTH,) * 4 + (D_MODEL,) * N_BRANCHES
IN_COLS = sum(SPLIT_SIZES)
D_FF_DENSE = 2816
N_EXPERTS = 8
TOP_K = 2
D_FF_EXPERT = 3584
N_DENSE = (DEPTH + 1) // 2
N_MOE = DEPTH // 2
RMS_EPS = 1e-6
NEG_BIG = -1e30

kernel_name = "hybrid_dilated_attn_shortconv_hgrn2_moe"


def _rmsnorm(x, gain):
    xf = x.astype(jnp.float32)
    y = xf * lax.rsqrt(jnp.mean(xf * xf, axis=-1, keepdims=True) + RMS_EPS)
    return (y * gain).astype(x.dtype)


def _swiglu(h, w1, w3, w2):
    return (jax.nn.silu(h @ w1) * (h @ w3)) @ w2


def _to_residue(t, d):
    B, S = t.shape[:2]
    t = t.reshape((B, S // d, d) + t.shape[2:])
    t = jnp.moveaxis(t, 2, 1)
    return t.reshape((B * d, S // d) + t.shape[3:])


def _from_residue(t, d, B):
    L = t.shape[1]
    t = t.reshape((B, d, L) + t.shape[2:])
    t = jnp.moveaxis(t, 1, 2)
    return t.reshape((B, L * d) + t.shape[3:])


def _banded_window_attention(q, k, v, win):
    N, L, H, E = q.shape
    nb = -(-L // BAND_BLOCK)
    pad_end = nb * BAND_BLOCK - L
    qb = jnp.pad(q, ((0, 0), (0, pad_end), (0, 0), (0, 0))).reshape(N, nb, BAND_BLOCK, H, E)

    def kv_blocks(t):
        tp = jnp.pad(t, ((0, 0), (BAND_BLOCK, pad_end), (0, 0), (0, 0)))
        tp = tp.reshape(N, nb + 1, BAND_BLOCK, H, E)
        return jnp.concatenate([tp[:, :-1], tp[:, 1:]], axis=2)

    kb, vb = kv_blocks(k), kv_blocks(v)
    s = jnp.einsum('nbqhe,nbkhe->nbhqk', qb, kb).astype(jnp.float32) * (E ** -0.5)
    qi = jnp.arange(BAND_BLOCK)[:, None]
    kj = jnp.arange(2 * BAND_BLOCK)[None, :]
    rel = qi + BAND_BLOCK - kj
    key_abs = (jnp.arange(nb) * BAND_BLOCK - BAND_BLOCK)[:, None, None] + kj
    valid = (rel >= 0) & (rel <= win) & (key_abs >= 0)
    s = jnp.where(valid[None, :, None], s, NEG_BIG)
    m = jnp.max(s, axis=-1, keepdims=True)
    p = jnp.exp(s - m)
    den = jnp.sum(p, axis=-1)
    o = jnp.einsum('nbhqk,nbkhe->nbqhe', p, vb.astype(jnp.float32))
    o = o / jnp.transpose(den, (0, 1, 3, 2))[..., None]
    lse = jnp.transpose(m[..., 0] + jnp.log(den), (0, 1, 3, 2))
    o = o.reshape(N, nb * BAND_BLOCK, H, E)[:, :L]
    lse = lse.reshape(N, nb * BAND_BLOCK, H)[:, :L]
    return o, lse


def _dilated_attention(q, k, v, q_gain, k_gain):
    B, S, _ = q.shape
    shp = (B, S, ATTN_HEADS, ATTN_HEAD_DIM)
    q = _rmsnorm(q.reshape(shp), q_gain)
    k = _rmsnorm(k.reshape(shp), k_gain)
    v = v.reshape(shp)
    outs, lses = [], []
    for g, (window, dilation) in enumerate(ATTN_GROUPS):
        hs = slice(g * ATTN_HEADS_PER_GROUP, (g + 1) * ATTN_HEADS_PER_GROUP)
        o, lse = _banded_window_attention(_to_residue(q[:, :, hs], dilation),
                                          _to_residue(k[:, :, hs], dilation),
                                          _to_residue(v[:, :, hs], dilation),
                                          window // dilation)
        outs.append(_from_residue(o, dilation, B))
        lses.append(_from_residue(lse, dilation, B))
    o = jnp.stack(outs)
    w = jax.nn.softmax(jnp.stack(lses), axis=0)
    o = jnp.einsum('gbsh,gbshe->bshe', w, o)
    return o.reshape(B, S, ATTN_OUT_WIDTH).astype(q.dtype)


def _short_conv(xb, b_gate, c_gate, conv_w):
    u = c_gate * xb
    S = u.shape[1]
    up = jnp.pad(u, ((0, 0), (CONV_K - 1, 0), (0, 0)))
    y = conv_w[0] * up[:, CONV_K - 1:CONV_K - 1 + S]
    for j in range(1, CONV_K):
        y = y + conv_w[j] * up[:, CONV_K - 1 - j:CONV_K - 1 - j + S]
    return b_gate * y


def _hgrn2_chunked(q, k, log_f, v):
    B, S, H, K = q.shape
    V = v.shape[-1]
    nc = S // HGRN_CHUNK

    def chunks(t):
        t = t.astype(jnp.float32).reshape(B, nc, HGRN_CHUNK, H, t.shape[-1])
        return jnp.transpose(t, (1, 0, 3, 2, 4))

    causal = jnp.tril(jnp.ones((HGRN_CHUNK, HGRN_CHUNK), dtype=bool))[:, :, None]

    def step(state, xs):
        qc, kc, gc, vc = xs
        b = jnp.cumsum(gc, axis=2)
        diff = b[:, :, :, None, :] - b[:, :, None, :, :]
        decay = jnp.where(causal, jnp.exp(jnp.where(causal, diff, 0.0)), 0.0)
        scores = jnp.einsum('bhtk,bhsk,bhtsk->bhts', qc, kc, decay)
        o = scores @ vc + jnp.einsum('bhtk,bhkv->bhtv', qc * jnp.exp(b), state)
        b_last = b[:, :, -1:, :]
        state = (jnp.exp(b_last[:, :, 0, :])[..., None] * state
                 + jnp.einsum('bhsk,bhsv->bhkv', kc * jnp.exp(b_last - b), vc))
        return state, o

    state0 = jnp.zeros((B, H, K, V), jnp.float32)
    _, o = lax.scan(step, state0, (chunks(q), chunks(k), chunks(log_f), chunks(v)))
    return jnp.transpose(o, (1, 0, 3, 2, 4)).reshape(B, S, H, V)


def _hgrn2(q, f_logit, i, g, lb, out_gain):
    B, S, _ = q.shape
    shp = (B, S, HGRN_HEADS, HGRN_HEAD_DIM)
    qh = jax.nn.silu(q).reshape(shp)
    lbh = lb.reshape(HGRN_HEADS, HGRN_HEAD_DIM)
    log_f = jnp.logaddexp(jnp.log(jnp.maximum(lbh, LB_FLOOR)),
                          jnp.log1p(-lbh) + jax.nn.log_sigmoid(f_logit.astype(jnp.float32).reshape(shp)))
    k = -jnp.expm1(log_f)
    o = _hgrn2_chunked(qh, k, log_f, i.reshape(shp))
    o = _rmsnorm(o, out_gain) * jax.nn.silu(g.astype(jnp.float32).reshape(shp))
    return o.reshape(B, S, HGRN_WIDTH).astype(q.dtype)


def _moe(h, router_w, w1, w3, w2):
    logits = (h @ router_w).astype(jnp.float32)
    top_vals, top_idx = lax.top_k(logits, TOP_K)
    gates = jax.nn.softmax(top_vals, axis=-1)
    combine = jnp.sum(jax.nn.one_hot(top_idx, N_EXPERTS, dtype=jnp.float32) * gates[..., None], axis=-2)
    out = jnp.zeros_like(h)
    for e in range(N_EXPERTS):
        out = out + combine[..., e:e + 1].astype(h.dtype) * _swiglu(h, w1[e], w3[e], w2[e])
    return out


def setup_inputs(seed: int = 0) -> dict:
    key = jax.random.key(seed)
    ks = jax.random.split(key, 22)
    f32 = jnp.float32

    def nrm(k, shape, fan_in):
        return jax.random.normal(k, shape, f32) * (fan_in ** -0.5)

    def gain(k, shape):
        return 1.0 + 0.1 * jax.random.normal(k, shape, f32)

    return {
        "x": jax.random.normal(ks[0], (BATCH, SEQ, D_MODEL), f32),
        "attn_norm": gain(ks[1], (DEPTH, D_MODEL)),
        "ffn_norm": gain(ks[2], (DEPTH, D_MODEL)),
        "w_in": nrm(ks[3], (DEPTH, D_MODEL, IN_COLS), D_MODEL),
        "q_norm": gain(ks[4], (DEPTH, ATTN_HEAD_DIM)),
        "k_norm": gain(ks[5], (DEPTH, ATTN_HEAD_DIM)),
        "conv_w": nrm(ks[6], (DEPTH, CONV_K, CONV_WIDTH), CONV_K),
        "hgrn_lower_bounds": 0.5 * jax.random.normal(ks[7], (DEPTH, HGRN_WIDTH), f32),
        "hgrn_out_norm": gain(ks[8], (DEPTH, HGRN_HEAD_DIM)),
        "w_proj_a": nrm(ks[9], (DEPTH, ATTN_OUT_WIDTH, D_MODEL), ATTN_OUT_WIDTH),
        "w_proj_b": nrm(ks[10], (DEPTH, CONV_WIDTH, D_MODEL), CONV_WIDTH),
        "w_proj_c": nrm(ks[11], (DEPTH, HGRN_WIDTH, D_MODEL), HGRN_WIDTH),
        "w_out": nrm(ks[12], (DEPTH, D_MODEL, D_MODEL), D_MODEL),
        "dense_w1": nrm(ks[13], (N_DENSE, D_MODEL, D_FF_DENSE), D_MODEL),
        "dense_w3": nrm(ks[14], (N_DENSE, D_MODEL, D_FF_DENSE), D_MODEL),
        "dense_w2": nrm(ks[15], (N_DENSE, D_FF_DENSE, D_MODEL), D_FF_DENSE),
        "router_w": nrm(ks[16], (N_MOE, D_MODEL, N_EXPERTS), D_MODEL),
        "moe_w1": nrm(ks[17], (N_MOE, N_EXPERTS, D_MODEL, D_FF_EXPERT), D_MODEL),
        "moe_w3": nrm(ks[18], (N_MOE, N_EXPERTS, D_MODEL, D_FF_EXPERT), D_MODEL),
        "moe_w2": nrm(ks[19], (N_MOE, N_EXPERTS, D_FF_EXPERT, D_MODEL), D_FF_EXPERT),
    }


def reference(x, attn_norm, ffn_norm, w_in, q_norm, k_norm, conv_w, hgrn_lower_bounds,
              hgrn_out_norm, w_proj_a, w_proj_b, w_proj_c, w_out, dense_w1, dense_w3,
              dense_w2, router_w, moe_w1, moe_w3, moe_w2):
    split_points = [int(p) for p in np.cumsum(SPLIT_SIZES)[:-1]]
    lbs = jax.nn.softmax(hgrn_lower_bounds.astype(jnp.float32), axis=0)
    lower_bound = jnp.cumsum(lbs, axis=0) - lbs[0]
    for layer in range(DEPTH):
        h = _rmsnorm(x, attn_norm[layer])
        proj = h @ w_in[layer]
        (qa, ka, va, xb, bb, cb, qc, fc, ic, gc,
         gate_a, gate_b, gate_c) = jnp.split(proj, split_points, axis=-1)
        ya = _dilated_attention(qa, ka, va, q_norm[layer], k_norm[layer]) @ w_proj_a[layer]
        yb = _short_conv(xb, bb, cb, conv_w[layer]) @ w_proj_b[layer]
        yc = _hgrn2(qc, fc, ic, gc, lower_bound[layer], hgrn_out_norm[layer]) @ w_proj_c[layer]
        merged = (jax.nn.sigmoid(gate_a) * ya + jax.nn.sigmoid(gate_b) * yb
                  + jax.nn.sigmoid(gate_c) * yc)
        x = x + merged @ w_out[layer]
        h = _rmsnorm(x, ffn_norm[layer])
        if layer % 2 == 0:
            j = layer // 2
            x = x + _swiglu(h, dense_w1[j], dense_w3[j], dense_w2[j])
        else:
            j = layer // 2
            x = x + _moe(h, router_w[j], moe_w1[j], moe_w3[j], moe_w2[j])
    return x
```

```python
import functools

import jax
import jax.numpy as jnp
from jax import lax
from jax.experimental import pallas as pl
from jax.experimental.pallas import tpu as pltpu

F32 = jnp.float32
BF16 = jnp.bfloat16

D_MODEL = 1024
HEAD_DIM = 64
ATTN_GROUPS = ((128, 1), (512, 4), (2048, 16))
HEADS_PER_GROUP = 4
GROUP_WIDTH = HEADS_PER_GROUP * HEAD_DIM
ATTN_WIDTH = GROUP_WIDTH * len(ATTN_GROUPS)
BAND = 128
CONV_WIDTH = 768
CONV_K = 3
HGRN_WIDTH = 768
HGRN_HEAD = 128
HGRN_HEADS = HGRN_WIDTH // HGRN_HEAD
LB_FLOOR = 1e-30
N_EXPERTS = 8
RMS_EPS = 1e-6
NEG_BIG = -1e30

LANES = 128
SUBLANES = 8

TOKEN_TILE = 512
HGRN_CHUNK = 128
MOE_ROW_TILE = 1024
MOE_FF_TILE = 512
GATHER_TILE = 1024
VMEM_LIMIT = 56 * 1024 * 1024


def _dot(a, b):
    return jnp.dot(a, b, preferred_element_type=F32)


def _dot_nt(a, b):
    return lax.dot_general(a, b, (((1,), (1,)), ((), ())), preferred_element_type=F32)


def _dot_tn(a, b):
    return lax.dot_general(a, b, (((0,), (0,)), ((), ())), preferred_element_type=F32)


def _rms_rows(xf, gain):
    ms = jnp.mean(xf * xf, axis=-1, keepdims=True)
    return xf * lax.rsqrt(ms + RMS_EPS) * gain


def _sigmoid(z):
    return 1.0 / (1.0 + jnp.exp(-z))


def _params(*sem):
    return pltpu.CompilerParams(dimension_semantics=sem, vmem_limit_bytes=VMEM_LIMIT)


def _full(shape):
    return pl.BlockSpec(shape, lambda *_: (0,) * len(shape))


def _qkv_kernel(x_ref, g_ref, w_ref, qg_ref, kg_ref, hsum_ref, hexp_ref, q_ref, k_ref, v_ref):
    h = _rms_rows(x_ref[...], g_ref[...]).astype(BF16)
    p = _dot(h, w_ref[...])

    def head_norm(t, gain):
        ms = _dot((t * t).astype(BF16), hsum_ref[...]) * (1.0 / HEAD_DIM)
        r = lax.rsqrt(ms + RMS_EPS)
        r_hi = r.astype(BF16)
        r_lo = (r - r_hi.astype(F32)).astype(BF16)
        r_cols = _dot(r_hi, hexp_ref[...]) + _dot(r_lo, hexp_ref[...])
        return (t * r_cols * gain).astype(BF16)

    q_ref[...] = head_norm(p[:, :ATTN_WIDTH], qg_ref[...])
    k_ref[...] = head_norm(p[:, ATTN_WIDTH:2 * ATTN_WIDTH], kg_ref[...])
    v_ref[...] = p[:, 2 * ATTN_WIDTH:].astype(BF16)


def _qkv_call(x, gain, w, qg, kg, hsum, hexp):
    n = x.shape[0]
    t = TOKEN_TILE
    row = lambda width: pl.BlockSpec((t, width), lambda i: (i, 0))
    out = jax.ShapeDtypeStruct((n, ATTN_WIDTH), BF16)
    return pl.pallas_call(
        _qkv_kernel,
        grid=(n // t,),
        in_specs=[row(D_MODEL), _full((1, D_MODEL)), _full(w.shape), _full((1, ATTN_WIDTH)),
                  _full((1, ATTN_WIDTH)), _full(hsum.shape), _full(hexp.shape)],
        out_specs=[row(ATTN_WIDTH)] * 3,
        out_shape=[out] * 3,
        compiler_params=_params("parallel"),
        name="qkv_proj",
    )(x, gain, w, qg, kg, hsum, hexp)


def _attn_kernel(q_ref, kc_ref, vc_ref, kp_ref, vp_ref, o_ref, l_ref, kbuf, vbuf, *, qb):
    first_tile = pl.program_id(2) == 0
    kbuf[0:BAND, :] = kp_ref[0]
    kbuf[BAND:BAND + qb, :] = kc_ref[0]
    vbuf[0:BAND, :] = vp_ref[0]
    vbuf[BAND:BAND + qb, :] = vc_ref[0]
    qi = lax.broadcasted_iota(jnp.int32, (BAND, 2 * BAND), 0)
    kj = lax.broadcasted_iota(jnp.int32, (BAND, 2 * BAND), 1)
    rel = qi + BAND - kj
    in_window = (rel >= 0) & (rel <= BAND)
    for j in range(qb // BAND):
        if j == 0:
            kmin = jnp.where(first_tile, BAND, 0)
            valid = in_window & (kj >= kmin)
        else:
            valid = in_window
        rows = slice(j * BAND, (j + 1) * BAND)
        keys = slice(j * BAND, (j + 2) * BAND)
        o_parts, l_parts = [], []
        for hh in range(HEADS_PER_GROUP):
            cols = slice(hh * HEAD_DIM, (hh + 1) * HEAD_DIM)
            s = _dot_nt(q_ref[0, rows, cols], kbuf[keys, cols])
            s = jnp.where(valid, s, NEG_BIG)
            m = jnp.max(s, axis=-1, keepdims=True)
            p = jnp.exp(s - m)
            den = jnp.sum(p, axis=-1, keepdims=True)
            o = _dot(p.astype(BF16), vbuf[keys, cols]) * (1.0 / den)
            o_parts.append(o)
            l_parts.append(jnp.broadcast_to(m + jnp.log(den), (BAND, HEAD_DIM)))
        o_ref[0, rows, :] = jnp.concatenate(o_parts, axis=1).astype(BF16)
        l_ref[0, rows, :] = jnp.concatenate(l_parts, axis=1)


def _attn_call(q, k, v, group, dilation, batch, seq):
    length = seq // dilation
    qb = min(length, 512)
    width = dilation * ATTN_WIDTH
    view = lambda a: a.reshape(batch, length, width)
    col = lambda r: 3 * r + group
    cur = pl.BlockSpec((1, qb, GROUP_WIDTH), lambda b, r, i: (b, i, col(r)))
    prev = pl.BlockSpec((1, BAND, GROUP_WIDTH),
                        lambda b, r, i: (b, jnp.maximum(i * (qb // BAND) - 1, 0), col(r)))
    out_spec = pl.BlockSpec((1, qb, GROUP_WIDTH), lambda b, r, i: (b, i, r))
    out_w = dilation * GROUP_WIDTH
    o, lse = pl.pallas_call(
        functools.partial(_attn_kernel, qb=qb),
        grid=(batch, dilation, length // qb),
        in_specs=[cur, cur, cur, prev, prev],
        out_specs=[out_spec, out_spec],
        out_shape=[jax.ShapeDtypeStruct((batch, length, out_w), BF16),
                   jax.ShapeDtypeStruct((batch, length, out_w), F32)],
        scratch_shapes=[pltpu.VMEM((BAND + qb, GROUP_WIDTH), BF16),
                        pltpu.VMEM((BAND + qb, GROUP_WIDTH), BF16)],
        compiler_params=_params("parallel", "parallel", "arbitrary"),
        name=f"band_attn_d{dilation}",
    )(view(q), view(k), view(v), view(k), view(v))
    n = batch * seq
    return o.reshape(n, GROUP_WIDTH), lse.reshape(n, GROUP_WIDTH)


def _hgrn_kernel(x_ref, g_ref, w_ref, lb_ref, og_ref, wp_ref, tri_ref, y_ref,
                 state_ref, proj_ref, gated_ref, *, tile, chunk):
    @pl.when(pl.program_id(1) == 0)
    def _():
        state_ref[...] = jnp.zeros_like(state_ref)

    h = _rms_rows(x_ref[0], g_ref[...]).astype(BF16)
    proj_ref[...] = _dot(h, w_ref[...])

    lb = lb_ref[...]
    lb_floor = jnp.maximum(lb, LB_FLOOR)
    one_minus_lb = 1.0 - lb
    floor_gap = lb_floor - lb
    row = lax.broadcasted_iota(jnp.int32, (chunk, HGRN_HEAD), 0)
    tt = lax.broadcasted_iota(jnp.int32, (chunk, chunk), 0)
    ss = lax.broadcasted_iota(jnp.int32, (chunk, chunk), 1)
    t_xor_s = tt ^ ss
    w = HGRN_WIDTH

    def chunk_body(c, carry):
        r0 = pl.multiple_of(c * chunk, chunk)
        rows = pl.ds(r0, chunk)
        z = proj_ref[rows, w:2 * w]
        e = jnp.exp(-jnp.abs(z))
        inv = 1.0 / (1.0 + e)
        sig = jnp.where(z >= 0, inv, e * inv)
        nsig = jnp.where(z >= 0, e * inv, inv)
        log_f = jnp.log(lb_floor + one_minus_lb * sig)
        k_all = one_minus_lb * nsig - floor_gap
        b_all = jnp.dot(tri_ref[...], log_f, preferred_element_type=F32,
                        precision=lax.Precision.HIGHEST)
        q_all = proj_ref[rows, 0:w]
        q_all = q_all * _sigmoid(q_all)
        v_all = proj_ref[rows, 2 * w:3 * w]
        g_all = proj_ref[rows, 3 * w:4 * w]
        g_all = g_all * _sigmoid(g_all)
        for hd in range(HGRN_HEADS):
            cols = slice(hd * HGRN_HEAD, (hd + 1) * HGRN_HEAD)
            b, q, k = b_all[:, cols], q_all[:, cols], k_all[:, cols]
            vb = v_all[:, cols].astype(BF16)
            scores = jnp.where(tt == ss, _dot_nt(q.astype(BF16), k.astype(BF16)), 0.0)
            last_of_block = b
            m = 1
            while m < chunk:
                lower = (row & m) == 0
                boundary = jnp.where(lower, last_of_block, pltpu.roll(last_of_block, m, 0))
                side = jnp.where(lower, k, q) * jnp.exp(-jnp.abs(b - boundary))
                side = side.astype(BF16)
                pair = _dot_nt(side, side)
                level = ((t_xor_s & (-m)) == m) & ((tt & m) != 0)
                scores = jnp.where(level, pair, scores)
                if 2 * m < chunk:
                    last_of_block = jnp.where(
                        lower, pltpu.roll(last_of_block, chunk - m, 0), last_of_block)
                m *= 2
            state_t = state_ref[hd]
            q_dec = (q * jnp.exp(b)).astype(BF16)
            o = _dot(scores.astype(BF16), vb) + _dot_nt(q_dec, state_t.astype(BF16))
            b_last = b[chunk - 1:chunk, :]
            k_dec = (k * jnp.exp(b_last - b)).astype(BF16)
            state_ref[hd] = state_t * jnp.exp(b_last) + _dot_tn(vb, k_dec)
            o = _rms_rows(o, og_ref[...]) * g_all[:, cols]
            gated_ref[rows, cols] = o.astype(BF16)
        return carry

    lax.fori_loop(0, tile // chunk, chunk_body, 0)
    y_ref[0] = _dot(gated_ref[...], wp_ref[...]).astype(BF16)


def _hgrn_call(x, gain, w, lb, out_gain, w_proj, batch, seq):
    tile, chunk = TOKEN_TILE, HGRN_CHUNK
    tri = jnp.tril(jnp.ones((chunk, chunk), F32))
    blk = lambda width: pl.BlockSpec((1, tile, width), lambda b, i: (b, i, 0))
    y = pl.pallas_call(
        functools.partial(_hgrn_kernel, tile=tile, chunk=chunk),
        grid=(batch, seq // tile),
        in_specs=[blk(D_MODEL), _full((1, D_MODEL)), _full(w.shape), _full((1, HGRN_WIDTH)),
                  _full((1, HGRN_HEAD)), _full(w_proj.shape), _full((chunk, chunk))],
        out_specs=blk(D_MODEL),
        out_shape=jax.ShapeDtypeStruct((batch, seq, D_MODEL), BF16),
        scratch_shapes=[pltpu.VMEM((HGRN_HEADS, HGRN_HEAD, HGRN_HEAD), F32),
                        pltpu.VMEM((tile, 4 * HGRN_WIDTH), F32),
                        pltpu.VMEM((tile, HGRN_WIDTH), BF16)],
        compiler_params=_params("parallel", "arbitrary"),
        name="hgrn_branch",
    )(x.reshape(batch, seq, D_MODEL), gain, w, lb, out_gain, w_proj, tri)
    return y.reshape(batch * seq, D_MODEL)


def _merge_kernel(x_ref, g_ref, w_ref, cw_ref, oa0, oa1, oa2, la0, la1, la2, yc_ref,
                  wpa_ref, wpb_ref, wo_ref, out_ref, u_ext, *, tile, tiles_per_seq):
    @pl.when(pl.program_id(0) % tiles_per_seq == 0)
    def _():
        u_ext[0:SUBLANES, :] = jnp.zeros((SUBLANES, CONV_WIDTH), F32)

    x = x_ref[...]
    h = _rms_rows(x, g_ref[...]).astype(BF16)
    p = _dot(h, w_ref[...])
    cw = CONV_WIDTH
    u = p[:, 2 * cw:3 * cw] * p[:, 0:cw]
    u_ext[SUBLANES:SUBLANES + tile, :] = u
    conv = (cw_ref[0:1, :] * u
            + cw_ref[1:2, :] * u_ext[SUBLANES - 1:SUBLANES - 1 + tile, :]
            + cw_ref[2:3, :] * u_ext[SUBLANES - 2:SUBLANES - 2 + tile, :])
    u_ext[0:SUBLANES, :] = u[tile - SUBLANES:, :]
    yb = _dot((p[:, cw:2 * cw] * conv).astype(BF16), wpb_ref[...])

    l0, l1, l2 = la0[...], la1[...], la2[...]
    top = jnp.maximum(jnp.maximum(l0, l1), l2)
    e0, e1, e2 = jnp.exp(l0 - top), jnp.exp(l1 - top), jnp.exp(l2 - top)
    oa = (e0 * oa0[...].astype(F32) + e1 * oa1[...].astype(F32) + e2 * oa2[...].astype(F32))
    oa = oa * (1.0 / (e0 + e1 + e2))
    ya = _dot(oa.astype(BF16), wpa_ref[...])

    d = D_MODEL
    gates = p[:, 3 * cw:]
    merged = (_sigmoid(gates[:, 0:d]) * ya + _sigmoid(gates[:, d:2 * d]) * yb
              + _sigmoid(gates[:, 2 * d:3 * d]) * yc_ref[...].astype(F32))
    out_ref[...] = x + _dot(merged.astype(BF16), wo_ref[...])


def _merge_call(x, gain, w, conv_w, oas, las, yc, wpa, wpb, wo, seq):
    n = x.shape[0]
    tile = TOKEN_TILE
    row = lambda width: pl.BlockSpec((tile, width), lambda i: (i, 0))
    return pl.pallas_call(
        functools.partial(_merge_kernel, tile=tile, tiles_per_seq=seq // tile),
        grid=(n // tile,),
        in_specs=[row(D_MODEL), _full((1, D_MODEL)), _full(w.shape), _full(conv_w.shape)]
                 + [row(GROUP_WIDTH)] * 6
                 + [row(D_MODEL), _full(wpa.shape), _full(wpb.shape), _full(wo.shape)],
        out_specs=row(D_MODEL),
        out_shape=jax.ShapeDtypeStruct((n, D_MODEL), F32),
        scratch_shapes=[pltpu.VMEM((SUBLANES + tile, CONV_WIDTH), F32)],
        compiler_params=_params("arbitrary"),
        name="merge_out",
    )(x, gain, w, conv_w, *oas, *las, yc, wpa, wpb, wo)


def _ffn_kernel(x_ref, g_ref, w1_ref, w3_ref, w2_ref, out_ref):
    x = x_ref[...]
    h = _rms_rows(x, g_ref[...]).astype(BF16)
    a = _dot(h, w1_ref[...])
    z = (a * _sigmoid(a) * _dot(h, w3_ref[...])).astype(BF16)
    out_ref[...] = x + _dot(z, w2_ref[...])


def _ffn_call(x, gain, w1, w3, w2):
    n = x.shape[0]
    tile = TOKEN_TILE
    row = pl.BlockSpec((tile, D_MODEL), lambda i: (i, 0))
    return pl.pallas_call(
        _ffn_kernel,
        grid=(n // tile,),
        in_specs=[row, _full((1, D_MODEL)), _full(w1.shape), _full(w3.shape), _full(w2.shape)],
        out_specs=row,
        out_shape=jax.ShapeDtypeStruct((n, D_MODEL), F32),
        compiler_params=_params("parallel"),
        name="dense_ffn",
    )(x, gain, w1, w3, w2)


def _router_kernel(x_ref, g_ref, rw_hi_ref, rw_lo_ref, route_ref):
    h = _rms_rows(x_ref[...], g_ref[...]).astype(BF16)
    logits = _dot(h, rw_hi_ref[...]) + _dot(h, rw_lo_ref[...])
    lane = lax.broadcasted_iota(jnp.int32, logits.shape, 1)
    logits = jnp.where(lane < N_EXPERTS, logits, -jnp.inf)
    m1 = jnp.max(logits, axis=-1, keepdims=True)
    i1 = jnp.min(jnp.where(logits == m1, lane, LANES), axis=-1, keepdims=True)
    rest = jnp.where(lane == i1, -jnp.inf, logits)
    m2 = jnp.max(rest, axis=-1, keepdims=True)
    i2 = jnp.min(jnp.where(rest == m2, lane, LANES), axis=-1, keepdims=True)
    e = jnp.exp(m2 - m1)
    g1 = 1.0 / (1.0 + e)
    g2 = e * g1
    route_ref[...] = jnp.where(
        lane == 0, g1, jnp.where(lane == 1, g2, jnp.where(
            lane == 2, i1.astype(F32), jnp.where(lane == 3, i2.astype(F32), 0.0))))


def _router_call(x, gain, rw_hi, rw_lo):
    n = x.shape[0]
    tile = TOKEN_TILE
    return pl.pallas_call(
        _router_kernel,
        grid=(n // tile,),
        in_specs=[pl.BlockSpec((tile, D_MODEL), lambda i: (i, 0)), _full((1, D_MODEL)),
                  _full(rw_hi.shape), _full(rw_lo.shape)],
        out_specs=pl.BlockSpec((tile, LANES), lambda i: (i, 0)),
        out_shape=jax.ShapeDtypeStruct((n, LANES), F32),
        compiler_params=_params("parallel"),
        name="router",
    )(x, gain, rw_hi, rw_lo)


def _gather_copy(src_hbm, dst_hbm, src_row, dst_row, sem):
    return pltpu.make_async_copy(src_hbm.at[src_row], dst_hbm.at[dst_row], sem)


def _gather_kernel(idx_ref, x_hbm, out_hbm, sem, *, tile):
    base = pl.program_id(0) * tile

    def issue(j, carry):
        _gather_copy(x_hbm, out_hbm, idx_ref[j], base + j, sem).start()
        return carry

    lax.fori_loop(0, tile, issue, 0, unroll=8)
    pltpu.make_async_copy(x_hbm.at[pl.ds(0, tile)], out_hbm.at[pl.ds(base, tile)], sem).wait()


def _gather_call(x3, src_rows):
    p = src_rows.shape[0]
    tile = GATHER_TILE
    return pl.pallas_call(
        functools.partial(_gather_kernel, tile=tile),
        grid=(p // tile,),
        in_specs=[pl.BlockSpec((tile,), lambda i: (i,), memory_space=pltpu.SMEM),
                  pl.BlockSpec(memory_space=pl.ANY)],
        out_specs=pl.BlockSpec(memory_space=pl.ANY),
        out_shape=jax.ShapeDtypeStruct((p,) + x3.shape[1:], x3.dtype),
        scratch_shapes=[pltpu.SemaphoreType.DMA(())],
        compiler_params=_params("arbitrary"),
        name="moe_gather",
    )(src_rows, x3)


def _moe_kernel(tile_expert_ref, n_tiles_ref, x_ref, g_ref, w1_ref, w3_ref, w2_ref, y_ref,
                h_ref, acc_ref):
    i, f = pl.program_id(0), pl.program_id(1)
    last_f = pl.num_programs(1) - 1
    active = i < n_tiles_ref[0]

    @pl.when(active & (f == 0))
    def _():
        h_ref[...] = _rms_rows(x_ref[...], g_ref[...]).astype(BF16)

    @pl.when(active)
    def _():
        h = h_ref[...]
        a = _dot(h, w1_ref[...])
        z = (a * _sigmoid(a) * _dot(h, w3_ref[...])).astype(BF16)
        part = _dot(z, w2_ref[...])

        @pl.when(f == 0)
        def _():
            acc_ref[...] = part

        @pl.when(f > 0)
        def _():
            acc_ref[...] += part

    @pl.when(active & (f == last_f))
    def _():
        y_ref[...] = acc_ref[...]

    @pl.when(jnp.logical_not(active) & (f == last_f))
    def _():
        y_ref[...] = jnp.zeros_like(y_ref)


def _moe_call(xs, gain, w1, w3, w2, tile_expert, n_tiles):
    p = xs.shape[0]
    d_ff = w1.shape[-1]
    tm, tf = MOE_ROW_TILE, MOE_FF_TILE
    nf = d_ff // tf

    def ff_block(i, f, n_tiles_ref):
        return jnp.where(i < n_tiles_ref[0], f, nf - 1)

    grid_spec = pltpu.PrefetchScalarGridSpec(
        num_scalar_prefetch=2,
        grid=(p // tm, nf),
        in_specs=[
            pl.BlockSpec((tm, D_MODEL), lambda i, f, te, nt: (i, 0)),
            pl.BlockSpec((1, D_MODEL), lambda i, f, te, nt: (0, 0)),
            pl.BlockSpec((None, D_MODEL, tf), lambda i, f, te, nt: (te[i], 0, ff_block(i, f, nt))),
            pl.BlockSpec((None, D_MODEL, tf), lambda i, f, te, nt: (te[i], 0, ff_block(i, f, nt))),
            pl.BlockSpec((None, tf, D_MODEL), lambda i, f, te, nt: (te[i], ff_block(i, f, nt), 0)),
        ],
        out_specs=pl.BlockSpec((tm, D_MODEL), lambda i, f, te, nt: (i, 0)),
        scratch_shapes=[pltpu.VMEM((tm, D_MODEL), BF16), pltpu.VMEM((tm, D_MODEL), F32)],
    )
    return pl.pallas_call(
        _moe_kernel,
        grid_spec=grid_spec,
        out_shape=jax.ShapeDtypeStruct((p, D_MODEL), F32),
        compiler_params=_params("arbitrary", "arbitrary"),
        name="moe_experts",
    )(tile_expert, n_tiles, xs, gain, w1, w3, w2)


def _combine_kernel(pos_ref, x_ref, route_ref, y_hbm, out_ref, buf, sems, *, tile):
    def issue(j, carry):
        _gather_copy(y_hbm, buf.at[0], pos_ref[j], j, sems.at[0]).start()
        _gather_copy(y_hbm, buf.at[1], pos_ref[tile + j], j, sems.at[1]).start()
        return carry

    lax.fori_loop(0, tile, issue, 0, unroll=8)
    for k in range(2):
        pltpu.make_async_copy(y_hbm.at[pl.ds(0, tile)], buf.at[k], sems.at[k]).wait()
    g1 = route_ref[:, :, 0:1]
    g2 = route_ref[:, :, 1:2]
    out_ref[...] = x_ref[...] + g1 * buf[0] + g2 * buf[1]


def _combine_call(x3, route3, y3, pos_tiles):
    n = x3.shape[0]
    tile = TOKEN_TILE
    blk = lambda s: pl.BlockSpec((tile,) + s, lambda i: (i, 0, 0))
    return pl.pallas_call(
        functools.partial(_combine_kernel, tile=tile),
        grid=(n // tile,),
        in_specs=[pl.BlockSpec((2 * tile,), lambda i: (i,), memory_space=pltpu.SMEM),
                  blk((SUBLANES, LANES)), blk((1, LANES)), pl.BlockSpec(memory_space=pl.ANY)],
        out_specs=blk((SUBLANES, LANES)),
        out_shape=jax.ShapeDtypeStruct(x3.shape, F32),
        scratch_shapes=[pltpu.VMEM((2, tile, SUBLANES, LANES), F32),
                        pltpu.SemaphoreType.DMA((2,))],
        compiler_params=_params("arbitrary"),
        name="moe_combine",
    )(pos_tiles, x3, route3, y3)


def _routing_tables(route, n):
    tm = MOE_ROW_TILE
    p = 2 * n + N_EXPERTS * tm
    expert = jnp.concatenate([route[:, 2], route[:, 3]]).astype(jnp.int32)
    onehot = (expert[:, None] == jnp.arange(N_EXPERTS, dtype=jnp.int32)[None, :]).astype(jnp.int32)
    running = jnp.cumsum(onehot, axis=0)
    rank = jnp.sum(running * onehot, axis=1) - 1
    counts = running[-1]
    padded = ((counts + tm - 1) // tm) * tm
    ends = jnp.cumsum(padded)
    starts = ends - padded
    pos = starts[expert] + rank
    n_tiles = (ends[-1] // tm).astype(jnp.int32)
    tile_start = jnp.arange(p // tm, dtype=jnp.int32) * tm
    tile_expert = jnp.sum((tile_start[:, None] >= ends[None, :]).astype(jnp.int32), axis=1)
    last_expert = jnp.sum((ends[-1] - 1 >= ends).astype(jnp.int32))
    tile_expert = jnp.minimum(tile_expert, last_expert).astype(jnp.int32)
    token = jnp.arange(n, dtype=jnp.int32)
    src_rows = jnp.zeros((p,), jnp.int32).at[pos].set(jnp.concatenate([token, token]))
    tile = TOKEN_TILE
    pos_tiles = jnp.concatenate(
        [pos[:n].reshape(n // tile, tile), pos[n:].reshape(n // tile, tile)], axis=1).reshape(-1)
    return src_rows, pos_tiles.astype(jnp.int32), tile_expert, n_tiles.reshape(1)


def _moe_layer(x, gain, router_w, w1, w3, w2):
    n = x.shape[0]
    rw = jnp.pad(router_w, ((0, 0), (0, LANES - N_EXPERTS)))
    rw_hi = rw.astype(BF16)
    rw_lo = (rw - rw_hi.astype(F32)).astype(BF16)
    route = _router_call(x, gain, rw_hi, rw_lo)
    src_rows, pos_tiles, tile_expert, n_tiles = _routing_tables(route, n)
    x3 = x.reshape(n, SUBLANES, LANES)
    xs = _gather_call(x3, src_rows)
    ys = _moe_call(xs.reshape(-1, D_MODEL), gain, w1.astype(BF16), w3.astype(BF16),
                   w2.astype(BF16), tile_expert, n_tiles)
    out = _combine_call(x3, route.reshape(n, 1, LANES), ys.reshape(-1, SUBLANES, LANES), pos_tiles)
    return out.reshape(n, D_MODEL)


def kernel(x, attn_norm, ffn_norm, w_in, q_norm, k_norm, conv_w, hgrn_lower_bounds,
           hgrn_out_norm, w_proj_a, w_proj_b, w_proj_c, w_out, dense_w1, dense_w3,
           dense_w2, router_w, moe_w1, moe_w3, moe_w2):
    batch, seq, d = x.shape
    n = batch * seq
    depth = w_in.shape[0]
    assert d == D_MODEL and seq % (ATTN_GROUPS[-1][1] * BAND) == 0 and seq % TOKEN_TILE == 0
    assert all(window // dilation == BAND for window, dilation in ATTN_GROUPS)

    lbs = jax.nn.softmax(hgrn_lower_bounds.astype(F32), axis=0)
    lower_bound = jnp.cumsum(lbs, axis=0) - lbs[0]
    head_of_col = jnp.arange(ATTN_WIDTH) // HEAD_DIM
    hsum = (head_of_col[:, None] == jnp.arange(LANES)[None, :]).astype(BF16)
    hexp = hsum.T
    a0, a1 = 0, 3 * ATTN_WIDTH
    b1 = a1 + 3 * CONV_WIDTH
    c1 = b1 + 4 * HGRN_WIDTH

    xf = x.reshape(n, d)
    for layer in range(depth):
        w = w_in[layer]
        w_attn = w[:, a0:a1].astype(BF16)
        w_hgrn = w[:, b1:c1].astype(BF16)
        w_conv_gate = jnp.concatenate([w[:, a1:b1], w[:, c1:]], axis=1).astype(BF16)
        gain = attn_norm[layer].reshape(1, d)
        heads = ATTN_WIDTH // HEAD_DIM
        qg = jnp.tile(q_norm[layer] * (HEAD_DIM ** -0.5), heads).reshape(1, ATTN_WIDTH)
        kg = jnp.tile(k_norm[layer], heads).reshape(1, ATTN_WIDTH)

        q, k, v = _qkv_call(xf, gain, w_attn, qg, kg, hsum, hexp)
        oas, las = [], []
        for group, (_, dilation) in enumerate(ATTN_GROUPS):
            o, lse = _attn_call(q, k, v, group, dilation, batch, seq)
            oas.append(o)
            las.append(lse)
        yc = _hgrn_call(xf, gain, w_hgrn, lower_bound[layer].reshape(1, HGRN_WIDTH),
                        hgrn_out_norm[layer].reshape(1, HGRN_HEAD),
                        w_proj_c[layer].astype(BF16), batch, seq)
        xf = _merge_call(xf, gain, w_conv_gate, conv_w[layer], oas, las, yc,
                         w_proj_a[layer].astype(BF16), w_proj_b[layer].astype(BF16),
                         w_out[layer].astype(BF16), seq)

        fgain = ffn_norm[layer].reshape(1, d)
        j = layer // 2
        if layer % 2 == 0:
            xf = _ffn_call(xf, fgain, dense_w1[j].astype(BF16), dense_w3[j].astype(BF16),
                           dense_w2[j].astype(BF16))
        else:
            xf = _moe_layer(xf, fgain, router_w[j], moe_w1[j], moe_w3[j], moe_w2[j])
    return xf.reshape(batch, seq, d)
```

```python
import functools

import jax
import jax.numpy as jnp
from jax import lax
from jax.experimental import pallas as pl
from jax.experimental.pallas import tpu as pltpu

F32 = jnp.float32
BF16 = jnp.bfloat16

D_MODEL = 1024
HEAD_DIM = 64
ATTN_GROUPS = ((128, 1), (512, 4), (2048, 16))
HEADS_PER_GROUP = 4
GROUP_WIDTH = HEADS_PER_GROUP * HEAD_DIM
ATTN_WIDTH = GROUP_WIDTH * len(ATTN_GROUPS)
BAND = 128
CONV_WIDTH = 768
CONV_K = 3
HGRN_WIDTH = 768
HGRN_HEAD = 128
HGRN_HEADS = HGRN_WIDTH // HGRN_HEAD
LB_FLOOR = 1e-30
N_EXPERTS = 8
RMS_EPS = 1e-6
NEG_BIG = -1e30

LANES = 128
SUBLANES = 8

TOKEN_TILE = 512
ATTN_TILE = ATTN_GROUPS[-1][1] * BAND
HGRN_CHUNK = 128
MOE_ROW_TILE = 1024
MOE_FF_TILE = 512
VMEM_LIMIT = 56 * 1024 * 1024


def _dot(a, b):
    return jnp.dot(a, b, preferred_element_type=F32)


def _dot_nt(a, b):
    return lax.dot_general(a, b, (((1,), (1,)), ((), ())), preferred_element_type=F32)


def _dot_tn(a, b):
    return lax.dot_general(a, b, (((0,), (0,)), ((), ())), preferred_element_type=F32)


def _rms_rows(xf, gain):
    ms = jnp.mean(xf * xf, axis=-1, keepdims=True)
    return xf * lax.rsqrt(ms + RMS_EPS) * gain


def _sigmoid(z):
    return 1.0 / (1.0 + jnp.exp(-z))


def _params(*sem):
    return pltpu.CompilerParams(dimension_semantics=sem, vmem_limit_bytes=VMEM_LIMIT)


def _full(shape):
    return pl.BlockSpec(shape, lambda *_: (0,) * len(shape))


def _qkv_kernel(x_ref, g_ref, w_ref, qg_ref, kg_ref, hsum_ref, hexp_ref, *rest, tile):
    out_refs, stage = rest[:9], rest[9]
    h = _rms_rows(x_ref[0], g_ref[...]).astype(BF16)
    p = _dot(h, w_ref[...])

    def head_norm(t, gain):
        ms = _dot((t * t).astype(BF16), hsum_ref[...]) * (1.0 / HEAD_DIM)
        r = lax.rsqrt(ms + RMS_EPS)
        r_hi = r.astype(BF16)
        r_lo = (r - r_hi.astype(F32)).astype(BF16)
        r_cols = _dot(r_hi, hexp_ref[...]) + _dot(r_lo, hexp_ref[...])
        return t * r_cols * gain

    parts = (head_norm(p[:, :ATTN_WIDTH], qg_ref[...]),
             head_norm(p[:, ATTN_WIDTH:2 * ATTN_WIDTH], kg_ref[...]),
             p[:, 2 * ATTN_WIDTH:])
    slot = 0
    for a, part in enumerate(parts):
        for group, (_, dilation) in enumerate(ATTN_GROUPS):
            out = out_refs[3 * group + a]
            c0 = group * GROUP_WIDTH
            if dilation == 1:
                out[0, 0] = part[:, c0:c0 + GROUP_WIDTH].astype(BF16)
                continue
            rows = tile // dilation
            for half in range(GROUP_WIDTH // LANES):
                stage[slot] = part[:, c0 + half * LANES:c0 + (half + 1) * LANES]
                for r in range(dilation):
                    out[0, r, :, half * LANES:(half + 1) * LANES] = (
                        stage[slot, pl.ds(r, rows, stride=dilation), :].astype(BF16))
                slot += 1


def _qkv_call(x, gain, w, qg, kg, hsum, hexp, batch, seq):
    t = TOKEN_TILE
    out_specs, out_shape = [], []
    for _, dilation in ATTN_GROUPS:
        for _ in range(3):
            out_specs.append(pl.BlockSpec((1, dilation, t // dilation, GROUP_WIDTH),
                                          lambda b, i: (b, 0, i, 0)))
            out_shape.append(jax.ShapeDtypeStruct(
                (batch, dilation, seq // dilation, GROUP_WIDTH), BF16))
    n_stage = 3 * (GROUP_WIDTH // LANES) * sum(1 for _, d in ATTN_GROUPS if d > 1)
    return pl.pallas_call(
        functools.partial(_qkv_kernel, tile=t),
        grid=(batch, seq // t),
        in_specs=[pl.BlockSpec((1, t, D_MODEL), lambda b, i: (b, i, 0)), _full((1, D_MODEL)),
                  _full(w.shape), _full((1, ATTN_WIDTH)), _full((1, ATTN_WIDTH)),
                  _full(hsum.shape), _full(hexp.shape)],
        out_specs=out_specs,
        out_shape=out_shape,
        scratch_shapes=[pltpu.VMEM((n_stage, t, LANES), F32)],
        compiler_params=_params("parallel", "parallel"),
        name="qkv_proj",
    )(x.reshape(batch, seq, D_MODEL), gain, w, qg, kg, hsum, hexp)


def _attn_kernel(*refs, n_groups):
    ins, o_ref, scr = refs[:5 * n_groups], refs[5 * n_groups], refs[5 * n_groups + 1:]
    kbufs, vbufs = scr[0:n_groups], scr[n_groups:2 * n_groups]
    o_tok, l_tok, o_rm, l_rm = scr[2 * n_groups:]
    first_tile = pl.program_id(1) == 0
    qi = lax.broadcasted_iota(jnp.int32, (BAND, 2 * BAND), 0)
    kj = lax.broadcasted_iota(jnp.int32, (BAND, 2 * BAND), 1)
    rel = qi + BAND - kj
    in_window = (rel >= 0) & (rel <= BAND)
    halves = GROUP_WIDTH // LANES

    for group, (_, dilation) in enumerate(ATTN_GROUPS):
        q_ref, kc_ref, vc_ref, kp_ref, vp_ref = ins[5 * group:5 * group + 5]
        kbuf, vbuf = kbufs[group], vbufs[group]
        length = ATTN_TILE // dilation
        nj = length // BAND
        kbuf[:, 0:BAND, :] = kp_ref[0]
        kbuf[:, BAND:BAND + length, :] = kc_ref[0]
        vbuf[:, 0:BAND, :] = vp_ref[0]
        vbuf[:, BAND:BAND + length, :] = vc_ref[0]

        def block(idx, carry, q_ref=q_ref, kbuf=kbuf, vbuf=vbuf, nj=nj, length=length,
                  group=group, dilation=dilation):
            r, j = idx // nj, idx % nj
            j0 = pl.multiple_of(j * BAND, BAND)
            kmin = jnp.where(first_tile & (j == 0), BAND, 0)
            valid = in_window & (kj >= kmin)
            q = q_ref[0, r, pl.ds(j0, BAND), :]
            keys = kbuf[r, pl.ds(j0, 2 * BAND), :]
            vals = vbuf[r, pl.ds(j0, 2 * BAND), :]
            o_parts, l_parts = [], []
            for hh in range(HEADS_PER_GROUP):
                cols = slice(hh * HEAD_DIM, (hh + 1) * HEAD_DIM)
                s = _dot_nt(q[:, cols], keys[:, cols])
                s = jnp.where(valid, s, NEG_BIG)
                m = jnp.max(s, axis=-1, keepdims=True)
                p = jnp.exp(s - m)
                den = jnp.sum(p, axis=-1, keepdims=True)
                o_parts.append(_dot(p.astype(BF16), vals[:, cols]) * (1.0 / den))
                l_parts.append(jnp.broadcast_to(m + jnp.log(den), (BAND, HEAD_DIM)))
            o = jnp.concatenate(o_parts, axis=1)
            lse = jnp.concatenate(l_parts, axis=1)
            row0 = pl.multiple_of(r * length + j0, BAND)
            if dilation == 1:
                for half in range(halves):
                    lanes = slice(half * LANES, (half + 1) * LANES)
                    o_tok[group, half, pl.ds(row0, BAND), :] = o[:, lanes]
                    l_tok[group, half, pl.ds(row0, BAND), :] = lse[:, lanes]
            else:
                o_rm[pl.ds(row0, BAND), :] = o
                l_rm[pl.ds(row0, BAND), :] = lse
            return carry

        lax.fori_loop(0, ATTN_TILE // BAND, block, 0)
        if dilation > 1:
            for r in range(dilation):
                rows = slice(r * length, (r + 1) * length)
                for half in range(halves):
                    lanes = slice(half * LANES, (half + 1) * LANES)
                    o_tok[group, half, pl.ds(r, length, stride=dilation), :] = o_rm[rows, lanes]
                    l_tok[group, half, pl.ds(r, length, stride=dilation), :] = l_rm[rows, lanes]

    chunk = 2 * BAND

    def merge(c, carry):
        rows = pl.ds(pl.multiple_of(c * chunk, chunk), chunk)
        for half in range(halves):
            ls = [l_tok[g, half, rows, :] for g in range(n_groups)]
            top = functools.reduce(jnp.maximum, ls)
            es = [jnp.exp(l - top) for l in ls]
            acc = sum(e * o_tok[g, half, rows, :] for g, e in enumerate(es))
            o_ref[0, rows, half * LANES:(half + 1) * LANES] = (acc * (1.0 / sum(es))).astype(BF16)
        return carry

    lax.fori_loop(0, ATTN_TILE // chunk, merge, 0)


def _attn_call(qkv, batch, seq):
    n_groups = len(ATTN_GROUPS)
    in_specs, args, kbufs = [], [], []
    for group, (_, dilation) in enumerate(ATTN_GROUPS):
        length = ATTN_TILE // dilation
        cur = pl.BlockSpec((1, dilation, length, GROUP_WIDTH), lambda b, i: (b, 0, i, 0))
        halo = pl.BlockSpec(
            (1, dilation, BAND, GROUP_WIDTH),
            lambda b, i, nb=length // BAND: (b, 0, jnp.maximum(i * nb - 1, 0), 0))
        q, k, v = qkv[3 * group:3 * group + 3]
        in_specs += [cur, cur, cur, halo, halo]
        args += [q, k, v, k, v]
        kbufs.append(pltpu.VMEM((dilation, BAND + length, GROUP_WIDTH), BF16))
    halves = GROUP_WIDTH // LANES
    o = pl.pallas_call(
        functools.partial(_attn_kernel, n_groups=n_groups),
        grid=(batch, seq // ATTN_TILE),
        in_specs=in_specs,
        out_specs=pl.BlockSpec((1, ATTN_TILE, GROUP_WIDTH), lambda b, i: (b, i, 0)),
        out_shape=jax.ShapeDtypeStruct((batch, seq, GROUP_WIDTH), BF16),
        scratch_shapes=kbufs + kbufs + [
            pltpu.VMEM((n_groups, halves, ATTN_TILE, LANES), F32),
            pltpu.VMEM((n_groups, halves, ATTN_TILE, LANES), F32),
            pltpu.VMEM((ATTN_TILE, GROUP_WIDTH), F32),
            pltpu.VMEM((ATTN_TILE, GROUP_WIDTH), F32)],
        compiler_params=_params("parallel", "arbitrary"),
        name="band_attn",
    )(*args)
    return o.reshape(batch * seq, GROUP_WIDTH)


def _hgrn_kernel(x_ref, g_ref, w_ref, lb_ref, og_ref, wp_ref, tri_ref, y_ref,
                 state_ref, proj_ref, gated_ref, *, tile, chunk):
    @pl.when(pl.program_id(1) == 0)
    def _():
        state_ref[...] = jnp.zeros_like(state_ref)

    h = _rms_rows(x_ref[0], g_ref[...]).astype(BF16)
    proj_ref[...] = _dot(h, w_ref[...])

    lb = lb_ref[...]
    lb_floor = jnp.maximum(lb, LB_FLOOR)
    one_minus_lb = 1.0 - lb
    floor_gap = lb_floor - lb
    row = lax.broadcasted_iota(jnp.int32, (chunk, HGRN_HEAD), 0)
    tt = lax.broadcasted_iota(jnp.int32, (chunk, chunk), 0)
    ss = lax.broadcasted_iota(jnp.int32, (chunk, chunk), 1)
    t_xor_s = tt ^ ss
    w = HGRN_WIDTH

    def chunk_body(c, carry):
        r0 = pl.multiple_of(c * chunk, chunk)
        rows = pl.ds(r0, chunk)
        z = proj_ref[rows, w:2 * w]
        e = jnp.exp(-jnp.abs(z))
        inv = 1.0 / (1.0 + e)
        sig = jnp.where(z >= 0, inv, e * inv)
        nsig = jnp.where(z >= 0, e * inv, inv)
        log_f = jnp.log(lb_floor + one_minus_lb * sig)
        k_all = one_minus_lb * nsig - floor_gap
        b_all = jnp.dot(tri_ref[...], log_f, preferred_element_type=F32,
                        precision=lax.Precision.HIGHEST)
        q_all = proj_ref[rows, 0:w]
        q_all = q_all * _sigmoid(q_all)
        v_all = proj_ref[rows, 2 * w:3 * w]
        g_all = proj_ref[rows, 3 * w:4 * w]
        g_all = g_all * _sigmoid(g_all)
        for hd in range(HGRN_HEADS):
            cols = slice(hd * HGRN_HEAD, (hd + 1) * HGRN_HEAD)
            b, q, k = b_all[:, cols], q_all[:, cols], k_all[:, cols]
            vb = v_all[:, cols].astype(BF16)
            scores = jnp.where(tt == ss, _dot_nt(q.astype(BF16), k.astype(BF16)), 0.0)
            last_of_block = b
            m = 1
            while m < chunk:
                lower = (row & m) == 0
                boundary = jnp.where(lower, last_of_block, pltpu.roll(last_of_block, m, 0))
                side = jnp.where(lower, k, q) * jnp.exp(-jnp.abs(b - boundary))
                side = side.astype(BF16)
                pair = _dot_nt(side, side)
                level = ((t_xor_s & (-m)) == m) & ((tt & m) != 0)
                scores = jnp.where(level, pair, scores)
                if 2 * m < chunk:
                    last_of_block = jnp.where(
                        lower, pltpu.roll(last_of_block, chunk - m, 0), last_of_block)
                m *= 2
            state_t = state_ref[hd]
            q_dec = (q * jnp.exp(b)).astype(BF16)
            o = _dot(scores.astype(BF16), vb) + _dot_nt(q_dec, state_t.astype(BF16))
            b_last = b[chunk - 1:chunk, :]
            k_dec = (k * jnp.exp(b_last - b)).astype(BF16)
            state_ref[hd] = state_t * jnp.exp(b_last) + _dot_tn(vb, k_dec)
            o = _rms_rows(o, og_ref[...]) * g_all[:, cols]
            gated_ref[rows, cols] = o.astype(BF16)
        return carry

    lax.fori_loop(0, tile // chunk, chunk_body, 0)
    y_ref[0] = _dot(gated_ref[...], wp_ref[...]).astype(BF16)


def _hgrn_call(x, gain, w, lb, out_gain, w_proj, batch, seq):
    tile, chunk = TOKEN_TILE, HGRN_CHUNK
    tri = jnp.tril(jnp.ones((chunk, chunk), F32))
    blk = lambda width: pl.BlockSpec((1, tile, width), lambda b, i: (b, i, 0))
    y = pl.pallas_call(
        functools.partial(_hgrn_kernel, tile=tile, chunk=chunk),
        grid=(batch, seq // tile),
        in_specs=[blk(D_MODEL), _full((1, D_MODEL)), _full(w.shape), _full((1, HGRN_WIDTH)),
                  _full((1, HGRN_HEAD)), _full(w_proj.shape), _full((chunk, chunk))],
        out_specs=blk(D_MODEL),
        out_shape=jax.ShapeDtypeStruct((batch, seq, D_MODEL), BF16),
        scratch_shapes=[pltpu.VMEM((HGRN_HEADS, HGRN_HEAD, HGRN_HEAD), F32),
                        pltpu.VMEM((tile, 4 * HGRN_WIDTH), F32),
                        pltpu.VMEM((tile, HGRN_WIDTH), BF16)],
        compiler_params=_params("parallel", "arbitrary"),
        name="hgrn_branch",
    )(x.reshape(batch, seq, D_MODEL), gain, w, lb, out_gain, w_proj, tri)
    return y.reshape(batch * seq, D_MODEL)


def _merge_kernel(x_ref, g_ref, w_ref, cw_ref, oa_ref, yc_ref,
                  wpa_ref, wpb_ref, wo_ref, out_ref, u_ext, *, tile, tiles_per_seq):
    @pl.when(pl.program_id(0) % tiles_per_seq == 0)
    def _():
        u_ext[0:SUBLANES, :] = jnp.zeros((SUBLANES, CONV_WIDTH), F32)

    x = x_ref[...]
    h = _rms_rows(x, g_ref[...]).astype(BF16)
    p = _dot(h, w_ref[...])
    cw = CONV_WIDTH
    u = p[:, 2 * cw:3 * cw] * p[:, 0:cw]
    u_ext[SUBLANES:SUBLANES + tile, :] = u
    conv = (cw_ref[0:1, :] * u
            + cw_ref[1:2, :] * u_ext[SUBLANES - 1:SUBLANES - 1 + tile, :]
            + cw_ref[2:3, :] * u_ext[SUBLANES - 2:SUBLANES - 2 + tile, :])
    u_ext[0:SUBLANES, :] = u[tile - SUBLANES:, :]
    yb = _dot((p[:, cw:2 * cw] * conv).astype(BF16), wpb_ref[...])
    ya = _dot(oa_ref[...], wpa_ref[...])

    d = D_MODEL
    gates = p[:, 3 * cw:]
    merged = (_sigmoid(gates[:, 0:d]) * ya + _sigmoid(gates[:, d:2 * d]) * yb
              + _sigmoid(gates[:, 2 * d:3 * d]) * yc_ref[...].astype(F32))
    out_ref[...] = x + _dot(merged.astype(BF16), wo_ref[...])


def _merge_call(x, gain, w, conv_w, oa, yc, wpa, wpb, wo, seq):
    n = x.shape[0]
    tile = TOKEN_TILE
    row = lambda width: pl.BlockSpec((tile, width), lambda i: (i, 0))
    return pl.pallas_call(
        functools.partial(_merge_kernel, tile=tile, tiles_per_seq=seq // tile),
        grid=(n // tile,),
        in_specs=[row(D_MODEL), _full((1, D_MODEL)), _full(w.shape), _full(conv_w.shape),
                  row(GROUP_WIDTH), row(D_MODEL), _full(wpa.shape), _full(wpb.shape),
                  _full(wo.shape)],
        out_specs=row(D_MODEL),
        out_shape=jax.ShapeDtypeStruct((n, D_MODEL), F32),
        scratch_shapes=[pltpu.VMEM((SUBLANES + tile, CONV_WIDTH), F32)],
        compiler_params=_params("arbitrary"),
        name="merge_out",
    )(x, gain, w, conv_w, oa, yc, wpa, wpb, wo)


def _ffn_kernel(x_ref, g_ref, w1_ref, w3_ref, w2_ref, out_ref):
    x = x_ref[...]
    h = _rms_rows(x, g_ref[...]).astype(BF16)
    a = _dot(h, w1_ref[...])
    z = (a * _sigmoid(a) * _dot(h, w3_ref[...])).astype(BF16)
    out_ref[...] = x + _dot(z, w2_ref[...])


def _ffn_call(x, gain, w1, w3, w2):
    n = x.shape[0]
    tile = TOKEN_TILE
    row = pl.BlockSpec((tile, D_MODEL), lambda i: (i, 0))
    return pl.pallas_call(
        _ffn_kernel,
        grid=(n // tile,),
        in_specs=[row, _full((1, D_MODEL)), _full(w1.shape), _full(w3.shape), _full(w2.shape)],
        out_specs=row,
        out_shape=jax.ShapeDtypeStruct((n, D_MODEL), F32),
        compiler_params=_params("parallel"),
        name="dense_ffn",
    )(x, gain, w1, w3, w2)


def _router_kernel(x_ref, g_ref, rw_hi_ref, rw_lo_ref, route_ref):
    h = _rms_rows(x_ref[...], g_ref[...]).astype(BF16)
    logits = _dot(h, rw_hi_ref[...]) + _dot(h, rw_lo_ref[...])
    lane = lax.broadcasted_iota(jnp.int32, logits.shape, 1)
    logits = jnp.where(lane < N_EXPERTS, logits, -jnp.inf)
    m1 = jnp.max(logits, axis=-1, keepdims=True)
    i1 = jnp.min(jnp.where(logits == m1, lane, LANES), axis=-1, keepdims=True)
    rest = jnp.where(lane == i1, -jnp.inf, logits)
    m2 = jnp.max(rest, axis=-1, keepdims=True)
    i2 = jnp.min(jnp.where(rest == m2, lane, LANES), axis=-1, keepdims=True)
    e = jnp.exp(m2 - m1)
    g1 = 1.0 / (1.0 + e)
    g2 = e * g1
    route_ref[...] = jnp.where(
        lane == 0, g1, jnp.where(lane == 1, g2, jnp.where(
            lane == 2, i1.astype(F32), jnp.where(lane == 3, i2.astype(F32), 0.0))))


def _router_call(x, gain, rw_hi, rw_lo):
    n = x.shape[0]
    tile = TOKEN_TILE
    return pl.pallas_call(
        _router_kernel,
        grid=(n // tile,),
        in_specs=[pl.BlockSpec((tile, D_MODEL), lambda i: (i, 0)), _full((1, D_MODEL)),
                  _full(rw_hi.shape), _full(rw_lo.shape)],
        out_specs=pl.BlockSpec((tile, LANES), lambda i: (i, 0)),
        out_shape=jax.ShapeDtypeStruct((n, LANES), F32),
        compiler_params=_params("parallel"),
        name="router",
    )(x, gain, rw_hi, rw_lo)


def _row_copy(src, dst, src_row, dst_row, sem):
    return pltpu.make_async_copy(src.at[pl.ds(src_row, 1), :], dst.at[pl.ds(dst_row, 1), :], sem)


def _dispatch_kernel(pad_ref, pos_ref, x_ref, out_hbm, zeros, sem, pad_sem, *, tile):
    @pl.when(pl.program_id(0) == 0)
    def _():
        zeros[...] = jnp.zeros_like(zeros)
        for e in range(N_EXPERTS):
            first, count = pad_ref[e], pad_ref[N_EXPERTS + e]

            def fill(j, carry, first=first):
                _row_copy(zeros, out_hbm, 0, first + j, pad_sem).start()
                return carry

            def drain(j, carry, first=first):
                _row_copy(zeros, out_hbm, 0, first + j, pad_sem).wait()
                return carry

            lax.fori_loop(0, count, fill, 0)
            lax.fori_loop(0, count, drain, 0)

        def tail_copy(k):
            start = pl.multiple_of(pad_ref[2 * N_EXPERTS] + k * tile, tile)
            return pltpu.make_async_copy(zeros, out_hbm.at[pl.ds(start, tile), :], pad_sem)

        def tail_fill(k, carry):
            tail_copy(k).start()
            return carry

        def tail_drain(k, carry):
            tail_copy(k).wait()
            return carry

        lax.fori_loop(0, pad_ref[2 * N_EXPERTS + 1], tail_fill, 0)
        lax.fori_loop(0, pad_ref[2 * N_EXPERTS + 1], tail_drain, 0)

    def issue(j, carry):
        _row_copy(x_ref, out_hbm, j, pos_ref[j], sem).start()
        _row_copy(x_ref, out_hbm, j, pos_ref[tile + j], sem).start()
        return carry

    lax.fori_loop(0, tile, issue, 0, unroll=8)
    for _ in range(2):
        pltpu.make_async_copy(x_ref, out_hbm.at[pl.ds(0, tile), :], sem).wait()


def _dispatch_call(x, pos_tiles, pad_info, n_sorted):
    n = x.shape[0]
    tile = TOKEN_TILE
    grid_spec = pltpu.PrefetchScalarGridSpec(
        num_scalar_prefetch=1,
        grid=(n // tile,),
        in_specs=[pl.BlockSpec((2 * tile,), lambda i, pad: (i,), memory_space=pltpu.SMEM),
                  pl.BlockSpec((tile, D_MODEL), lambda i, pad: (i, 0))],
        out_specs=pl.BlockSpec(memory_space=pl.ANY),
        scratch_shapes=[pltpu.VMEM((tile, D_MODEL), F32), pltpu.SemaphoreType.DMA(()),
                        pltpu.SemaphoreType.DMA(())],
    )
    return pl.pallas_call(
        functools.partial(_dispatch_kernel, tile=tile),
        grid_spec=grid_spec,
        out_shape=jax.ShapeDtypeStruct((n_sorted, D_MODEL), F32),
        compiler_params=_params("arbitrary"),
        name="moe_dispatch",
    )(pad_info, pos_tiles, x)


def _moe_kernel(tile_expert_ref, n_tiles_ref, x_ref, g_ref, w1_ref, w3_ref, w2_ref, y_ref,
                h_ref, acc_ref):
    i, f = pl.program_id(0), pl.program_id(1)
    last_f = pl.num_programs(1) - 1
    active = i < n_tiles_ref[0]

    @pl.when(active & (f == 0))
    def _():
        h_ref[...] = _rms_rows(x_ref[...], g_ref[...]).astype(BF16)

    @pl.when(active)
    def _():
        h = h_ref[...]
        a = _dot(h, w1_ref[...])
        z = (a * _sigmoid(a) * _dot(h, w3_ref[...])).astype(BF16)
        part = _dot(z, w2_ref[...])

        @pl.when(f == 0)
        def _():
            acc_ref[...] = part

        @pl.when(f > 0)
        def _():
            acc_ref[...] += part

    @pl.when(active & (f == last_f))
    def _():
        y_ref[...] = acc_ref[...]

    @pl.when(jnp.logical_not(active) & (f == last_f))
    def _():
        y_ref[...] = jnp.zeros_like(y_ref)


def _moe_call(xs, gain, w1, w3, w2, tile_expert, n_tiles):
    p = xs.shape[0]
    d_ff = w1.shape[-1]
    tm, tf = MOE_ROW_TILE, MOE_FF_TILE
    nf = d_ff // tf

    def ff_block(i, f, n_tiles_ref):
        return jnp.where(i < n_tiles_ref[0], f, nf - 1)

    grid_spec = pltpu.PrefetchScalarGridSpec(
        num_scalar_prefetch=2,
        grid=(p // tm, nf),
        in_specs=[
            pl.BlockSpec((tm, D_MODEL), lambda i, f, te, nt: (jnp.minimum(i, nt[0] - 1), 0)),
            pl.BlockSpec((1, D_MODEL), lambda i, f, te, nt: (0, 0)),
            pl.BlockSpec((None, D_MODEL, tf), lambda i, f, te, nt: (te[i], 0, ff_block(i, f, nt))),
            pl.BlockSpec((None, D_MODEL, tf), lambda i, f, te, nt: (te[i], 0, ff_block(i, f, nt))),
            pl.BlockSpec((None, tf, D_MODEL), lambda i, f, te, nt: (te[i], ff_block(i, f, nt), 0)),
        ],
        out_specs=pl.BlockSpec((tm, D_MODEL), lambda i, f, te, nt: (i, 0)),
        scratch_shapes=[pltpu.VMEM((tm, D_MODEL), BF16), pltpu.VMEM((tm, D_MODEL), F32)],
    )
    return pl.pallas_call(
        _moe_kernel,
        grid_spec=grid_spec,
        out_shape=jax.ShapeDtypeStruct((p, D_MODEL), F32),
        compiler_params=_params("arbitrary", "arbitrary"),
        name="moe_experts",
    )(tile_expert, n_tiles, xs, gain, w1, w3, w2)


def _combine_kernel(pos_ref, x_ref, route_ref, y_hbm, out_ref, buf, sems, *, tile):
    def issue(j, carry):
        _row_copy(y_hbm, buf.at[0], pos_ref[j], j, sems.at[0]).start()
        _row_copy(y_hbm, buf.at[1], pos_ref[tile + j], j, sems.at[1]).start()
        return carry

    lax.fori_loop(0, tile, issue, 0, unroll=8)
    for k in range(2):
        pltpu.make_async_copy(y_hbm.at[pl.ds(0, tile), :], buf.at[k], sems.at[k]).wait()
    g1 = route_ref[:, 0:1]
    g2 = route_ref[:, 1:2]
    out_ref[...] = x_ref[...] + g1 * buf[0] + g2 * buf[1]


def _combine_call(x, route, y, pos_tiles):
    n = x.shape[0]
    tile = TOKEN_TILE
    row = lambda width: pl.BlockSpec((tile, width), lambda i: (i, 0))
    return pl.pallas_call(
        functools.partial(_combine_kernel, tile=tile),
        grid=(n // tile,),
        in_specs=[pl.BlockSpec((2 * tile,), lambda i: (i,), memory_space=pltpu.SMEM),
                  row(D_MODEL), row(LANES), pl.BlockSpec(memory_space=pl.ANY)],
        out_specs=row(D_MODEL),
        out_shape=jax.ShapeDtypeStruct(x.shape, F32),
        scratch_shapes=[pltpu.VMEM((2, tile, D_MODEL), F32), pltpu.SemaphoreType.DMA((2,))],
        compiler_params=_params("arbitrary"),
        name="moe_combine",
    )(pos_tiles, x, route, y)


def _routing_tables(route, n):
    tm = MOE_ROW_TILE
    p = 2 * n + N_EXPERTS * tm
    expert = jnp.concatenate([route[:, 2], route[:, 3]]).astype(jnp.int32)
    onehot = (expert[:, None] == jnp.arange(N_EXPERTS, dtype=jnp.int32)[None, :]).astype(jnp.int32)
    running = jnp.cumsum(onehot, axis=0)
    rank = jnp.sum(running * onehot, axis=1) - 1
    counts = running[-1]
    padded = ((counts + tm - 1) // tm) * tm
    ends = jnp.cumsum(padded)
    starts = ends - padded
    pos = jnp.sum(onehot * starts[None, :], axis=1) + rank
    n_tiles = (ends[-1] // tm).astype(jnp.int32)
    tile_start = jnp.arange(p // tm, dtype=jnp.int32) * tm
    tile_expert = jnp.sum((tile_start[:, None] >= ends[None, :]).astype(jnp.int32), axis=1)
    last_expert = jnp.sum((ends[-1] - 1 >= ends).astype(jnp.int32))
    tile_expert = jnp.minimum(tile_expert, last_expert).astype(jnp.int32)
    tile = TOKEN_TILE
    tail = jnp.stack([ends[-1], (p - ends[-1]) // tile])
    pad_info = jnp.concatenate([starts + counts, padded - counts, tail]).astype(jnp.int32)
    pos_tiles = jnp.concatenate(
        [pos[:n].reshape(n // tile, tile), pos[n:].reshape(n // tile, tile)], axis=1).reshape(-1)
    return pos_tiles.astype(jnp.int32), pad_info, tile_expert, n_tiles.reshape(1), p


def _moe_layer(x, gain, router_w, w1, w3, w2):
    n = x.shape[0]
    rw = jnp.pad(router_w, ((0, 0), (0, LANES - N_EXPERTS)))
    rw_hi = rw.astype(BF16)
    rw_lo = (rw - rw_hi.astype(F32)).astype(BF16)
    route = _router_call(x, gain, rw_hi, rw_lo)
    pos_tiles, pad_info, tile_expert, n_tiles, n_sorted = _routing_tables(route, n)
    xs = _dispatch_call(x, pos_tiles, pad_info, n_sorted)
    ys = _moe_call(xs, gain, w1.astype(BF16), w3.astype(BF16), w2.astype(BF16), tile_expert, n_tiles)
    return _combine_call(x, route, ys, pos_tiles)


def kernel(x, attn_norm, ffn_norm, w_in, q_norm, k_norm, conv_w, hgrn_lower_bounds,
           hgrn_out_norm, w_proj_a, w_proj_b, w_proj_c, w_out, dense_w1, dense_w3,
           dense_w2, router_w, moe_w1, moe_w3, moe_w2):
    batch, seq, d = x.shape
    n = batch * seq
    depth = w_in.shape[0]
    assert d == D_MODEL and seq % ATTN_TILE == 0 and seq % TOKEN_TILE == 0
    assert all(window // dilation == BAND for window, dilation in ATTN_GROUPS)

    lbs = jax.nn.softmax(hgrn_lower_bounds.astype(F32), axis=0)
    lower_bound = jnp.cumsum(lbs, axis=0) - lbs[0]
    head_of_col = jnp.arange(ATTN_WIDTH) // HEAD_DIM
    hsum = (head_of_col[:, None] == jnp.arange(LANES)[None, :]).astype(BF16)
    hexp = hsum.T
    a0, a1 = 0, 3 * ATTN_WIDTH
    b1 = a1 + 3 * CONV_WIDTH
    c1 = b1 + 4 * HGRN_WIDTH

    xf = x.reshape(n, d)
    for layer in range(depth):
        w = w_in[layer]
        w_attn = w[:, a0:a1].astype(BF16)
        w_hgrn = w[:, b1:c1].astype(BF16)
        w_conv_gate = jnp.concatenate([w[:, a1:b1], w[:, c1:]], axis=1).astype(BF16)
        gain = attn_norm[layer].reshape(1, d)
        heads = ATTN_WIDTH // HEAD_DIM
        qg = jnp.tile(q_norm[layer] * (HEAD_DIM ** -0.5), heads).reshape(1, ATTN_WIDTH)
        kg = jnp.tile(k_norm[layer], heads).reshape(1, ATTN_WIDTH)

        qkv = _qkv_call(xf, gain, w_attn, qg, kg, hsum, hexp, batch, seq)
        oa = _attn_call(qkv, batch, seq)
        yc = _hgrn_call(xf, gain, w_hgrn, lower_bound[layer].reshape(1, HGRN_WIDTH),
                        hgrn_out_norm[layer].reshape(1, HGRN_HEAD),
                        w_proj_c[layer].astype(BF16), batch, seq)
        xf = _merge_call(xf, gain, w_conv_gate, conv_w[layer], oa, yc,
                         w_proj_a[layer].astype(BF16), w_proj_b[layer].astype(BF16),
                         w_out[layer].astype(BF16), seq)

        fgain = ffn_norm[layer].reshape(1, d)
        j = layer // 2
        if layer % 2 == 0:
            xf = _ffn_call(xf, fgain, dense_w1[j].astype(BF16), dense_w3[j].astype(BF16),
                           dense_w2[j].astype(BF16))
        else:
            xf = _moe_layer(xf, fgain, router_w[j], moe_w1[j], moe_w3[j], moe_w2[j])
    return xf.reshape(batch, seq, d)
```

```python
import functools

import jax
import jax.numpy as jnp
from jax import lax
from jax.experimental import pallas as pl
from jax.experimental.pallas import tpu as pltpu

F32 = jnp.float32
BF16 = jnp.bfloat16

D_MODEL = 1024
HEAD_DIM = 64
ATTN_GROUPS = ((128, 1), (512, 4), (2048, 16))
HEADS_PER_GROUP = 4
GROUP_WIDTH = HEADS_PER_GROUP * HEAD_DIM
ATTN_WIDTH = GROUP_WIDTH * len(ATTN_GROUPS)
BAND = 128
CONV_WIDTH = 768
CONV_K = 3
HGRN_WIDTH = 768
HGRN_HEAD = 128
HGRN_HEADS = HGRN_WIDTH // HGRN_HEAD
LB_FLOOR = 1e-30
N_EXPERTS = 8
RMS_EPS = 1e-6
NEG_BIG = -1e30

LANES = 128
SUBLANES = 8

TOKEN_TILE = 512
ATTN_TILE = ATTN_GROUPS[-1][1] * BAND
HGRN_CHUNK = 128
MOE_ROW_TILE = 512
MOE_FF_TILE = 1792
VMEM_LIMIT = 56 * 1024 * 1024


def _dot(a, b):
    return jnp.dot(a, b, preferred_element_type=F32)


def _dot_nt(a, b):
    return lax.dot_general(a, b, (((1,), (1,)), ((), ())), preferred_element_type=F32)


def _dot_tn(a, b):
    return lax.dot_general(a, b, (((0,), (0,)), ((), ())), preferred_element_type=F32)


def _rms_rows(xf, gain):
    ms = jnp.mean(xf * xf, axis=-1, keepdims=True)
    return xf * lax.rsqrt(ms + RMS_EPS) * gain


def _sigmoid(z):
    return 1.0 / (1.0 + jnp.exp(-z))


def _params(*sem):
    return pltpu.CompilerParams(dimension_semantics=sem, vmem_limit_bytes=VMEM_LIMIT)


def _full(shape):
    return pl.BlockSpec(shape, lambda *_: (0,) * len(shape))


def _qkv_kernel(x_ref, g_ref, w_ref, qg_ref, kg_ref, hsum_ref, hexp_ref, *rest, tile):
    out_refs, stage = rest[:9], rest[9]
    h = _rms_rows(x_ref[0], g_ref[...]).astype(BF16)
    p = _dot(h, w_ref[...])

    def head_norm(t, gain):
        ms = _dot((t * t).astype(BF16), hsum_ref[...]) * (1.0 / HEAD_DIM)
        r = lax.rsqrt(ms + RMS_EPS)
        r_hi = r.astype(BF16)
        r_lo = (r - r_hi.astype(F32)).astype(BF16)
        r_cols = _dot(r_hi, hexp_ref[...]) + _dot(r_lo, hexp_ref[...])
        return t * r_cols * gain

    parts = (head_norm(p[:, :ATTN_WIDTH], qg_ref[...]),
             head_norm(p[:, ATTN_WIDTH:2 * ATTN_WIDTH], kg_ref[...]),
             p[:, 2 * ATTN_WIDTH:])
    slot = 0
    for a, part in enumerate(parts):
        for group, (_, dilation) in enumerate(ATTN_GROUPS):
            out = out_refs[3 * group + a]
            c0 = group * GROUP_WIDTH
            if dilation == 1:
                out[0, 0] = part[:, c0:c0 + GROUP_WIDTH].astype(BF16)
                continue
            rows = tile // dilation
            for half in range(GROUP_WIDTH // LANES):
                stage[slot] = part[:, c0 + half * LANES:c0 + (half + 1) * LANES]
                for r in range(dilation):
                    out[0, r, :, half * LANES:(half + 1) * LANES] = (
                        stage[slot, pl.ds(r, rows, stride=dilation), :].astype(BF16))
                slot += 1


def _qkv_call(x, gain, w, qg, kg, hsum, hexp, batch, seq):
    t = TOKEN_TILE
    out_specs, out_shape = [], []
    for _, dilation in ATTN_GROUPS:
        for _ in range(3):
            out_specs.append(pl.BlockSpec((1, dilation, t // dilation, GROUP_WIDTH),
                                          lambda b, i: (b, 0, i, 0)))
            out_shape.append(jax.ShapeDtypeStruct(
                (batch, dilation, seq // dilation, GROUP_WIDTH), BF16))
    n_stage = 3 * (GROUP_WIDTH // LANES) * sum(1 for _, d in ATTN_GROUPS if d > 1)
    return pl.pallas_call(
        functools.partial(_qkv_kernel, tile=t),
        grid=(batch, seq // t),
        in_specs=[pl.BlockSpec((1, t, D_MODEL), lambda b, i: (b, i, 0)), _full((1, D_MODEL)),
                  _full(w.shape), _full((1, ATTN_WIDTH)), _full((1, ATTN_WIDTH)),
                  _full(hsum.shape), _full(hexp.shape)],
        out_specs=out_specs,
        out_shape=out_shape,
        scratch_shapes=[pltpu.VMEM((n_stage, t, LANES), F32)],
        compiler_params=_params("parallel", "parallel"),
        name="qkv_proj",
    )(x.reshape(batch, seq, D_MODEL), gain, w, qg, kg, hsum, hexp)


def _attn_kernel(*refs, n_groups):
    ins, o_ref, scr = refs[:5 * n_groups], refs[5 * n_groups], refs[5 * n_groups + 1:]
    kbufs, vbufs = scr[0:n_groups], scr[n_groups:2 * n_groups]
    o_tok, l_tok, o_rm, l_rm = scr[2 * n_groups:]
    first_tile = pl.program_id(1) == 0
    qi = lax.broadcasted_iota(jnp.int32, (BAND, 2 * BAND), 0)
    kj = lax.broadcasted_iota(jnp.int32, (BAND, 2 * BAND), 1)
    rel = qi + BAND - kj
    in_window = (rel >= 0) & (rel <= BAND)
    halves = GROUP_WIDTH // LANES

    for group, (_, dilation) in enumerate(ATTN_GROUPS):
        q_ref, kc_ref, vc_ref, kp_ref, vp_ref = ins[5 * group:5 * group + 5]
        kbuf, vbuf = kbufs[group], vbufs[group]
        length = ATTN_TILE // dilation
        nj = length // BAND
        kbuf[:, 0:BAND, :] = kp_ref[0]
        kbuf[:, BAND:BAND + length, :] = kc_ref[0]
        vbuf[:, 0:BAND, :] = vp_ref[0]
        vbuf[:, BAND:BAND + length, :] = vc_ref[0]

        def block(idx, carry, q_ref=q_ref, kbuf=kbuf, vbuf=vbuf, nj=nj, length=length,
                  group=group, dilation=dilation):
            r, j = idx // nj, idx % nj
            j0 = pl.multiple_of(j * BAND, BAND)
            kmin = jnp.where(first_tile & (j == 0), BAND, 0)
            valid = in_window & (kj >= kmin)
            q = q_ref[0, r, pl.ds(j0, BAND), :]
            keys = kbuf[r, pl.ds(j0, 2 * BAND), :]
            vals = vbuf[r, pl.ds(j0, 2 * BAND), :]
            o_parts, l_parts = [], []
            for hh in range(HEADS_PER_GROUP):
                cols = slice(hh * HEAD_DIM, (hh + 1) * HEAD_DIM)
                s = _dot_nt(q[:, cols], keys[:, cols])
                s = jnp.where(valid, s, NEG_BIG)
                m = jnp.max(s, axis=-1, keepdims=True)
                p = jnp.exp(s - m)
                den = jnp.sum(p, axis=-1, keepdims=True)
                o_parts.append(_dot(p.astype(BF16), vals[:, cols]) * (1.0 / den))
                l_parts.append(jnp.broadcast_to(m + jnp.log(den), (BAND, HEAD_DIM)))
            o = jnp.concatenate(o_parts, axis=1)
            lse = jnp.concatenate(l_parts, axis=1)
            row0 = pl.multiple_of(r * length + j0, BAND)
            if dilation == 1:
                for half in range(halves):
                    lanes = slice(half * LANES, (half + 1) * LANES)
                    o_tok[group, half, pl.ds(row0, BAND), :] = o[:, lanes]
                    l_tok[group, half, pl.ds(row0, BAND), :] = lse[:, lanes]
            else:
                o_rm[pl.ds(row0, BAND), :] = o
                l_rm[pl.ds(row0, BAND), :] = lse
            return carry

        lax.fori_loop(0, ATTN_TILE // BAND, block, 0)
        if dilation > 1:
            for r in range(dilation):
                rows = slice(r * length, (r + 1) * length)
                for half in range(halves):
                    lanes = slice(half * LANES, (half + 1) * LANES)
                    o_tok[group, half, pl.ds(r, length, stride=dilation), :] = o_rm[rows, lanes]
                    l_tok[group, half, pl.ds(r, length, stride=dilation), :] = l_rm[rows, lanes]

    chunk = 2 * BAND

    def merge(c, carry):
        rows = pl.ds(pl.multiple_of(c * chunk, chunk), chunk)
        for half in range(halves):
            ls = [l_tok[g, half, rows, :] for g in range(n_groups)]
            top = functools.reduce(jnp.maximum, ls)
            es = [jnp.exp(l - top) for l in ls]
            acc = sum(e * o_tok[g, half, rows, :] for g, e in enumerate(es))
            o_ref[0, rows, half * LANES:(half + 1) * LANES] = (acc * (1.0 / sum(es))).astype(BF16)
        return carry

    lax.fori_loop(0, ATTN_TILE // chunk, merge, 0)


def _attn_call(qkv, batch, seq):
    n_groups = len(ATTN_GROUPS)
    in_specs, args, kbufs = [], [], []
    for group, (_, dilation) in enumerate(ATTN_GROUPS):
        length = ATTN_TILE // dilation
        cur = pl.BlockSpec((1, dilation, length, GROUP_WIDTH), lambda b, i: (b, 0, i, 0))
        halo = pl.BlockSpec(
            (1, dilation, BAND, GROUP_WIDTH),
            lambda b, i, nb=length // BAND: (b, 0, jnp.maximum(i * nb - 1, 0), 0))
        q, k, v = qkv[3 * group:3 * group + 3]
        in_specs += [cur, cur, cur, halo, halo]
        args += [q, k, v, k, v]
        kbufs.append(pltpu.VMEM((dilation, BAND + length, GROUP_WIDTH), BF16))
    halves = GROUP_WIDTH // LANES
    o = pl.pallas_call(
        functools.partial(_attn_kernel, n_groups=n_groups),
        grid=(batch, seq // ATTN_TILE),
        in_specs=in_specs,
        out_specs=pl.BlockSpec((1, ATTN_TILE, GROUP_WIDTH), lambda b, i: (b, i, 0)),
        out_shape=jax.ShapeDtypeStruct((batch, seq, GROUP_WIDTH), BF16),
        scratch_shapes=kbufs + kbufs + [
            pltpu.VMEM((n_groups, halves, ATTN_TILE, LANES), F32),
            pltpu.VMEM((n_groups, halves, ATTN_TILE, LANES), F32),
            pltpu.VMEM((ATTN_TILE, GROUP_WIDTH), F32),
            pltpu.VMEM((ATTN_TILE, GROUP_WIDTH), F32)],
        compiler_params=_params("parallel", "arbitrary"),
        name="band_attn",
    )(*args)
    return o.reshape(batch * seq, GROUP_WIDTH)


def _hgrn_kernel(x_ref, g_ref, w_ref, lb_ref, og_ref, wp_ref, tri_ref, y_ref,
                 state_ref, proj_ref, gated_ref, *, tile, chunk):
    @pl.when(pl.program_id(1) == 0)
    def _():
        state_ref[...] = jnp.zeros_like(state_ref)

    h = _rms_rows(x_ref[0], g_ref[...]).astype(BF16)
    proj_ref[...] = _dot(h, w_ref[...])

    lb = lb_ref[...]
    lb_floor = jnp.maximum(lb, LB_FLOOR)
    one_minus_lb = 1.0 - lb
    floor_gap = lb_floor - lb
    row = lax.broadcasted_iota(jnp.int32, (chunk, HGRN_HEAD), 0)
    tt = lax.broadcasted_iota(jnp.int32, (chunk, chunk), 0)
    ss = lax.broadcasted_iota(jnp.int32, (chunk, chunk), 1)
    t_xor_s = tt ^ ss
    w = HGRN_WIDTH
    sizes = [1 << i for i in range(chunk.bit_length() - 1)]
    lower_rows = [(row & m) == 0 for m in sizes]
    lower_sel = [jnp.where(low, 1.0, 0.0).astype(BF16) for low in lower_rows]
    level_pairs = [((t_xor_s & (-m)) == m) & ((tt & m) != 0) for m in sizes]
    diagonal = tt == ss

    def chunk_body(c, carry):
        r0 = pl.multiple_of(c * chunk, chunk)
        rows = pl.ds(r0, chunk)
        z = proj_ref[rows, w:2 * w]
        e = jnp.exp(-jnp.abs(z))
        inv = 1.0 / (1.0 + e)
        sig = jnp.where(z >= 0, inv, e * inv)
        nsig = jnp.where(z >= 0, e * inv, inv)
        log_f = jnp.log(lb_floor + one_minus_lb * sig)
        k_all = one_minus_lb * nsig - floor_gap
        lf_hi = log_f.astype(BF16)
        rest = log_f - lf_hi.astype(F32)
        lf_mid = rest.astype(BF16)
        lf_lo = (rest - lf_mid.astype(F32)).astype(BF16)
        tri = tri_ref[...]
        b_all = _dot(tri, lf_hi) + _dot(tri, lf_mid) + _dot(tri, lf_lo)
        q_all = proj_ref[rows, 0:w]
        q_all = q_all * _sigmoid(q_all)
        v_all = proj_ref[rows, 2 * w:3 * w]
        g_all = proj_ref[rows, 3 * w:4 * w]
        g_all = g_all * _sigmoid(g_all)
        for hd in range(HGRN_HEADS):
            cols = slice(hd * HGRN_HEAD, (hd + 1) * HGRN_HEAD)
            b, q, k = b_all[:, cols], q_all[:, cols], k_all[:, cols]
            vb = v_all[:, cols].astype(BF16)
            qb, kb = q.astype(BF16), k.astype(BF16)
            scores = jnp.where(diagonal, _dot_nt(qb, kb), 0.0)
            last_of_block = b
            for lvl, m in enumerate(sizes):
                lower = lower_rows[lvl]
                boundary = jnp.where(lower, last_of_block, pltpu.roll(last_of_block, m, 0))
                decay = jnp.exp(-jnp.abs((b - boundary).astype(BF16)))
                side = jnp.where(lower_sel[lvl] > 0, kb, qb) * decay
                scores = jnp.where(level_pairs[lvl], _dot_nt(side, side), scores)
                if 2 * m < chunk:
                    last_of_block = jnp.where(
                        lower, pltpu.roll(last_of_block, chunk - m, 0), last_of_block)
            state_t = state_ref[hd]
            q_dec = (q * jnp.exp(b)).astype(BF16)
            o = _dot(scores.astype(BF16), vb) + _dot_nt(q_dec, state_t.astype(BF16))
            b_last = b[chunk - 1:chunk, :]
            k_dec = (k * jnp.exp(b_last - b)).astype(BF16)
            state_ref[hd] = state_t * jnp.exp(b_last) + _dot_tn(vb, k_dec)
            o = _rms_rows(o, og_ref[...]) * g_all[:, cols]
            gated_ref[rows, cols] = o.astype(BF16)
        return carry

    lax.fori_loop(0, tile // chunk, chunk_body, 0)
    y_ref[0] = _dot(gated_ref[...], wp_ref[...]).astype(BF16)


def _hgrn_call(x, gain, w, lb, out_gain, w_proj, batch, seq):
    tile, chunk = TOKEN_TILE, HGRN_CHUNK
    tri = jnp.tril(jnp.ones((chunk, chunk), BF16))
    blk = lambda width: pl.BlockSpec((1, tile, width), lambda b, i: (b, i, 0))
    y = pl.pallas_call(
        functools.partial(_hgrn_kernel, tile=tile, chunk=chunk),
        grid=(batch, seq // tile),
        in_specs=[blk(D_MODEL), _full((1, D_MODEL)), _full(w.shape), _full((1, HGRN_WIDTH)),
                  _full((1, HGRN_HEAD)), _full(w_proj.shape), _full((chunk, chunk))],
        out_specs=blk(D_MODEL),
        out_shape=jax.ShapeDtypeStruct((batch, seq, D_MODEL), BF16),
        scratch_shapes=[pltpu.VMEM((HGRN_HEADS, HGRN_HEAD, HGRN_HEAD), F32),
                        pltpu.VMEM((tile, 4 * HGRN_WIDTH), F32),
                        pltpu.VMEM((tile, HGRN_WIDTH), BF16)],
        compiler_params=_params("parallel", "arbitrary"),
        name="hgrn_branch",
    )(x.reshape(batch, seq, D_MODEL), gain, w, lb, out_gain, w_proj, tri)
    return y.reshape(batch * seq, D_MODEL)


def _merge_kernel(x_ref, g_ref, w_ref, cw_ref, oa_ref, yc_ref,
                  wpa_ref, wpb_ref, wo_ref, out_ref, u_ext, *, tile, tiles_per_seq):
    @pl.when(pl.program_id(0) % tiles_per_seq == 0)
    def _():
        u_ext[0:SUBLANES, :] = jnp.zeros((SUBLANES, CONV_WIDTH), F32)

    x = x_ref[...]
    h = _rms_rows(x, g_ref[...]).astype(BF16)
    p = _dot(h, w_ref[...])
    cw = CONV_WIDTH
    u = p[:, 2 * cw:3 * cw] * p[:, 0:cw]
    u_ext[SUBLANES:SUBLANES + tile, :] = u
    conv = (cw_ref[0:1, :] * u
            + cw_ref[1:2, :] * u_ext[SUBLANES - 1:SUBLANES - 1 + tile, :]
            + cw_ref[2:3, :] * u_ext[SUBLANES - 2:SUBLANES - 2 + tile, :])
    u_ext[0:SUBLANES, :] = u[tile - SUBLANES:, :]
    yb = _dot((p[:, cw:2 * cw] * conv).astype(BF16), wpb_ref[...])
    ya = _dot(oa_ref[...], wpa_ref[...])

    d = D_MODEL
    gates = p[:, 3 * cw:]
    merged = (_sigmoid(gates[:, 0:d]) * ya + _sigmoid(gates[:, d:2 * d]) * yb
              + _sigmoid(gates[:, 2 * d:3 * d]) * yc_ref[...].astype(F32))
    out_ref[...] = x + _dot(merged.astype(BF16), wo_ref[...])


def _merge_call(x, gain, w, conv_w, oa, yc, wpa, wpb, wo, seq):
    n = x.shape[0]
    tile = TOKEN_TILE
    row = lambda width: pl.BlockSpec((tile, width), lambda i: (i, 0))
    return pl.pallas_call(
        functools.partial(_merge_kernel, tile=tile, tiles_per_seq=seq // tile),
        grid=(n // tile,),
        in_specs=[row(D_MODEL), _full((1, D_MODEL)), _full(w.shape), _full(conv_w.shape),
                  row(GROUP_WIDTH), row(D_MODEL), _full(wpa.shape), _full(wpb.shape),
                  _full(wo.shape)],
        out_specs=row(D_MODEL),
        out_shape=jax.ShapeDtypeStruct((n, D_MODEL), F32),
        scratch_shapes=[pltpu.VMEM((SUBLANES + tile, CONV_WIDTH), F32)],
        compiler_params=_params("arbitrary"),
        name="merge_out",
    )(x, gain, w, conv_w, oa, yc, wpa, wpb, wo)


def _ffn_kernel(x_ref, g_ref, w1_ref, w3_ref, w2_ref, out_ref):
    x = x_ref[...]
    h = _rms_rows(x, g_ref[...]).astype(BF16)
    a = _dot(h, w1_ref[...])
    z = (a * _sigmoid(a) * _dot(h, w3_ref[...])).astype(BF16)
    out_ref[...] = x + _dot(z, w2_ref[...])


def _ffn_call(x, gain, w1, w3, w2):
    n = x.shape[0]
    tile = TOKEN_TILE
    row = pl.BlockSpec((tile, D_MODEL), lambda i: (i, 0))
    return pl.pallas_call(
        _ffn_kernel,
        grid=(n // tile,),
        in_specs=[row, _full((1, D_MODEL)), _full(w1.shape), _full(w3.shape), _full(w2.shape)],
        out_specs=row,
        out_shape=jax.ShapeDtypeStruct((n, D_MODEL), F32),
        compiler_params=_params("parallel"),
        name="dense_ffn",
    )(x, gain, w1, w3, w2)


def _router_kernel(x_ref, g_ref, rw_hi_ref, rw_lo_ref, route_ref):
    h = _rms_rows(x_ref[...], g_ref[...]).astype(BF16)
    logits = _dot(h, rw_hi_ref[...]) + _dot(h, rw_lo_ref[...])
    lane = lax.broadcasted_iota(jnp.int32, logits.shape, 1)
    logits = jnp.where(lane < N_EXPERTS, logits, -jnp.inf)
    m1 = jnp.max(logits, axis=-1, keepdims=True)
    i1 = jnp.min(jnp.where(logits == m1, lane, LANES), axis=-1, keepdims=True)
    rest = jnp.where(lane == i1, -jnp.inf, logits)
    m2 = jnp.max(rest, axis=-1, keepdims=True)
    i2 = jnp.min(jnp.where(rest == m2, lane, LANES), axis=-1, keepdims=True)
    e = jnp.exp(m2 - m1)
    g1 = 1.0 / (1.0 + e)
    g2 = e * g1
    route_ref[...] = jnp.where(
        lane == 0, g1, jnp.where(lane == 1, g2, jnp.where(
            lane == 2, i1.astype(F32), jnp.where(lane == 3, i2.astype(F32), 0.0))))


def _router_call(x, gain, rw_hi, rw_lo):
    n = x.shape[0]
    tile = TOKEN_TILE
    return pl.pallas_call(
        _router_kernel,
        grid=(n // tile,),
        in_specs=[pl.BlockSpec((tile, D_MODEL), lambda i: (i, 0)), _full((1, D_MODEL)),
                  _full(rw_hi.shape), _full(rw_lo.shape)],
        out_specs=pl.BlockSpec((tile, LANES), lambda i: (i, 0)),
        out_shape=jax.ShapeDtypeStruct((n, LANES), F32),
        compiler_params=_params("parallel"),
        name="router",
    )(x, gain, rw_hi, rw_lo)


def _row_copy(src, dst, src_row, dst_row, sem):
    return pltpu.make_async_copy(src.at[pl.ds(src_row, 1), :], dst.at[pl.ds(dst_row, 1), :], sem)


def _dispatch_kernel(pad_ref, pos_ref, x_ref, out_hbm, zeros, sem, pad_sem, *, tile):
    @pl.when(pl.program_id(0) == 0)
    def _():
        zeros[...] = jnp.zeros_like(zeros)
        for e in range(N_EXPERTS):
            first, count = pad_ref[e], pad_ref[N_EXPERTS + e]

            def fill(j, carry, first=first):
                _row_copy(zeros, out_hbm, 0, first + j, pad_sem).start()
                return carry

            def drain(j, carry, first=first):
                _row_copy(zeros, out_hbm, 0, first + j, pad_sem).wait()
                return carry

            lax.fori_loop(0, count, fill, 0)
            lax.fori_loop(0, count, drain, 0)

        def tail_copy(k):
            start = pl.multiple_of(pad_ref[2 * N_EXPERTS] + k * tile, tile)
            return pltpu.make_async_copy(zeros, out_hbm.at[pl.ds(start, tile), :], pad_sem)

        def tail_fill(k, carry):
            tail_copy(k).start()
            return carry

        def tail_drain(k, carry):
            tail_copy(k).wait()
            return carry

        lax.fori_loop(0, pad_ref[2 * N_EXPERTS + 1], tail_fill, 0)
        lax.fori_loop(0, pad_ref[2 * N_EXPERTS + 1], tail_drain, 0)

    def issue(j, carry):
        _row_copy(x_ref, out_hbm, j, pos_ref[j], sem).start()
        _row_copy(x_ref, out_hbm, j, pos_ref[tile + j], sem).start()
        return carry

    lax.fori_loop(0, tile, issue, 0, unroll=8)
    for _ in range(2):
        pltpu.make_async_copy(x_ref, out_hbm.at[pl.ds(0, tile), :], sem).wait()


def _dispatch_call(x, pos_tiles, pad_info, n_sorted):
    n = x.shape[0]
    tile = TOKEN_TILE
    grid_spec = pltpu.PrefetchScalarGridSpec(
        num_scalar_prefetch=1,
        grid=(n // tile,),
        in_specs=[pl.BlockSpec((2 * tile,), lambda i, pad: (i,), memory_space=pltpu.SMEM),
                  pl.BlockSpec((tile, D_MODEL), lambda i, pad: (i, 0))],
        out_specs=pl.BlockSpec(memory_space=pl.ANY),
        scratch_shapes=[pltpu.VMEM((tile, D_MODEL), F32), pltpu.SemaphoreType.DMA(()),
                        pltpu.SemaphoreType.DMA(())],
    )
    return pl.pallas_call(
        functools.partial(_dispatch_kernel, tile=tile),
        grid_spec=grid_spec,
        out_shape=jax.ShapeDtypeStruct((n_sorted, D_MODEL), F32),
        compiler_params=_params("arbitrary"),
        name="moe_dispatch",
    )(pad_info, pos_tiles, x)


def _moe_kernel(tile_expert_ref, n_tiles_ref, x_ref, g_ref, w1_ref, w3_ref, w2_ref, y_ref,
                h_ref, acc_ref):
    i, f = pl.program_id(0), pl.program_id(1)
    last_f = pl.num_programs(1) - 1
    active = i < n_tiles_ref[0]

    @pl.when(active & (f == 0))
    def _():
        h_ref[...] = _rms_rows(x_ref[...], g_ref[...]).astype(BF16)

    @pl.when(active)
    def _():
        h = h_ref[...]
        a = _dot(h, w1_ref[...])
        z = (a * _sigmoid(a) * _dot(h, w3_ref[...])).astype(BF16)
        part = _dot(z, w2_ref[...])

        @pl.when(f == 0)
        def _():
            acc_ref[...] = part

        @pl.when(f > 0)
        def _():
            acc_ref[...] += part

    @pl.when(active & (f == last_f))
    def _():
        y_ref[...] = acc_ref[...]

    @pl.when(jnp.logical_not(active) & (f == last_f))
    def _():
        y_ref[...] = jnp.zeros_like(y_ref)


def _moe_call(xs, gain, w1, w3, w2, tile_expert, n_tiles):
    p = xs.shape[0]
    d_ff = w1.shape[-1]
    tm, tf = MOE_ROW_TILE, MOE_FF_TILE
    nf = d_ff // tf

    def ff_block(i, f, n_tiles_ref):
        return jnp.where(i < n_tiles_ref[0], f, nf - 1)

    grid_spec = pltpu.PrefetchScalarGridSpec(
        num_scalar_prefetch=2,
        grid=(p // tm, nf),
        in_specs=[
            pl.BlockSpec((tm, D_MODEL), lambda i, f, te, nt: (jnp.minimum(i, nt[0] - 1), 0)),
            pl.BlockSpec((1, D_MODEL), lambda i, f, te, nt: (0, 0)),
            pl.BlockSpec((None, D_MODEL, tf), lambda i, f, te, nt: (te[i], 0, ff_block(i, f, nt))),
            pl.BlockSpec((None, D_MODEL, tf), lambda i, f, te, nt: (te[i], 0, ff_block(i, f, nt))),
            pl.BlockSpec((None, tf, D_MODEL), lambda i, f, te, nt: (te[i], ff_block(i, f, nt), 0)),
        ],
        out_specs=pl.BlockSpec((tm, D_MODEL), lambda i, f, te, nt: (i, 0)),
        scratch_shapes=[pltpu.VMEM((tm, D_MODEL), BF16), pltpu.VMEM((tm, D_MODEL), F32)],
    )
    return pl.pallas_call(
        _moe_kernel,
        grid_spec=grid_spec,
        out_shape=jax.ShapeDtypeStruct((p, D_MODEL), F32),
        compiler_params=_params("arbitrary", "arbitrary"),
        name="moe_experts",
    )(tile_expert, n_tiles, xs, gain, w1, w3, w2)


def _combine_kernel(pos_ref, x_ref, route_ref, y_hbm, out_ref, buf, sems, *, tile):
    def issue(j, carry):
        _row_copy(y_hbm, buf.at[0], pos_ref[j], j, sems.at[0]).start()
        _row_copy(y_hbm, buf.at[1], pos_ref[tile + j], j, sems.at[1]).start()
        return carry

    lax.fori_loop(0, tile, issue, 0, unroll=8)
    for k in range(2):
        pltpu.make_async_copy(y_hbm.at[pl.ds(0, tile), :], buf.at[k], sems.at[k]).wait()
    g1 = route_ref[:, 0:1]
    g2 = route_ref[:, 1:2]
    out_ref[...] = x_ref[...] + g1 * buf[0] + g2 * buf[1]


def _combine_call(x, route, y, pos_tiles):
    n = x.shape[0]
    tile = TOKEN_TILE
    row = lambda width: pl.BlockSpec((tile, width), lambda i: (i, 0))
    return pl.pallas_call(
        functools.partial(_combine_kernel, tile=tile),
        grid=(n // tile,),
        in_specs=[pl.BlockSpec((2 * tile,), lambda i: (i,), memory_space=pltpu.SMEM),
                  row(D_MODEL), row(LANES), pl.BlockSpec(memory_space=pl.ANY)],
        out_specs=row(D_MODEL),
        out_shape=jax.ShapeDtypeStruct(x.shape, F32),
        scratch_shapes=[pltpu.VMEM((2, tile, D_MODEL), F32), pltpu.SemaphoreType.DMA((2,))],
        compiler_params=_params("arbitrary"),
        name="moe_combine",
    )(pos_tiles, x, route, y)


def _routing_tables(route, n):
    tm = MOE_ROW_TILE
    p = 2 * n + N_EXPERTS * tm
    expert = jnp.concatenate([route[:, 2], route[:, 3]]).astype(jnp.int32)
    onehot = (expert[:, None] == jnp.arange(N_EXPERTS, dtype=jnp.int32)[None, :]).astype(jnp.int32)
    running = jnp.cumsum(onehot, axis=0)
    rank = jnp.sum(running * onehot, axis=1) - 1
    counts = running[-1]
    padded = ((counts + tm - 1) // tm) * tm
    ends = jnp.cumsum(padded)
    starts = ends - padded
    pos = jnp.sum(onehot * starts[None, :], axis=1) + rank
    n_tiles = (ends[-1] // tm).astype(jnp.int32)
    tile_start = jnp.arange(p // tm, dtype=jnp.int32) * tm
    tile_expert = jnp.sum((tile_start[:, None] >= ends[None, :]).astype(jnp.int32), axis=1)
    last_expert = jnp.sum((ends[-1] - 1 >= ends).astype(jnp.int32))
    tile_expert = jnp.minimum(tile_expert, last_expert).astype(jnp.int32)
    tile = TOKEN_TILE
    tail = jnp.stack([ends[-1], (p - ends[-1]) // tile])
    pad_info = jnp.concatenate([starts + counts, padded - counts, tail]).astype(jnp.int32)
    pos_tiles = jnp.concatenate(
        [pos[:n].reshape(n // tile, tile), pos[n:].reshape(n // tile, tile)], axis=1).reshape(-1)
    return pos_tiles.astype(jnp.int32), pad_info, tile_expert, n_tiles.reshape(1), p


def _moe_layer(x, gain, router_w, w1, w3, w2):
    n = x.shape[0]
    rw = jnp.pad(router_w, ((0, 0), (0, LANES - N_EXPERTS)))
    rw_hi = rw.astype(BF16)
    rw_lo = (rw - rw_hi.astype(F32)).astype(BF16)
    route = _router_call(x, gain, rw_hi, rw_lo)
    pos_tiles, pad_info, tile_expert, n_tiles, n_sorted = _routing_tables(route, n)
    xs = _dispatch_call(x, pos_tiles, pad_info, n_sorted)
    ys = _moe_call(xs, gain, w1.astype(BF16), w3.astype(BF16), w2.astype(BF16), tile_expert, n_tiles)
    return _combine_call(x, route, ys, pos_tiles)


def kernel(x, attn_norm, ffn_norm, w_in, q_norm, k_norm, conv_w, hgrn_lower_bounds,
           hgrn_out_norm, w_proj_a, w_proj_b, w_proj_c, w_out, dense_w1, dense_w3,
           dense_w2, router_w, moe_w1, moe_w3, moe_w2):
    batch, seq, d = x.shape
    n = batch * seq
    depth = w_in.shape[0]
    assert d == D_MODEL and seq % ATTN_TILE == 0 and seq % TOKEN_TILE == 0
    assert all(window // dilation == BAND for window, dilation in ATTN_GROUPS)

    lbs = jax.nn.softmax(hgrn_lower_bounds.astype(F32), axis=0)
    lower_bound = jnp.cumsum(lbs, axis=0) - lbs[0]
    head_of_col = jnp.arange(ATTN_WIDTH) // HEAD_DIM
    hsum = (head_of_col[:, None] == jnp.arange(LANES)[None, :]).astype(BF16)
    hexp = hsum.T
    a0, a1 = 0, 3 * ATTN_WIDTH
    b1 = a1 + 3 * CONV_WIDTH
    c1 = b1 + 4 * HGRN_WIDTH

    xf = x.reshape(n, d)
    for layer in range(depth):
        w = w_in[layer]
        w_attn = w[:, a0:a1].astype(BF16)
        w_hgrn = w[:, b1:c1].astype(BF16)
        w_conv_gate = jnp.concatenate([w[:, a1:b1], w[:, c1:]], axis=1).astype(BF16)
        gain = attn_norm[layer].reshape(1, d)
        heads = ATTN_WIDTH // HEAD_DIM
        qg = jnp.tile(q_norm[layer] * (HEAD_DIM ** -0.5), heads).reshape(1, ATTN_WIDTH)
        kg = jnp.tile(k_norm[layer], heads).reshape(1, ATTN_WIDTH)

        qkv = _qkv_call(xf, gain, w_attn, qg, kg, hsum, hexp, batch, seq)
        oa = _attn_call(qkv, batch, seq)
        yc = _hgrn_call(xf, gain, w_hgrn, lower_bound[layer].reshape(1, HGRN_WIDTH),
                        hgrn_out_norm[layer].reshape(1, HGRN_HEAD),
                        w_proj_c[layer].astype(BF16), batch, seq)
        xf = _merge_call(xf, gain, w_conv_gate, conv_w[layer], oa, yc,
                         w_proj_a[layer].astype(BF16), w_proj_b[layer].astype(BF16),
                         w_out[layer].astype(BF16), seq)

        fgain = ffn_norm[layer].reshape(1, d)
        j = layer // 2
        if layer % 2 == 0:
            xf = _ffn_call(xf, fgain, dense_w1[j].astype(BF16), dense_w3[j].astype(BF16),
                           dense_w2[j].astype(BF16))
        else:
            xf = _moe_layer(xf, fgain, router_w[j], moe_w1[j], moe_w3[j], moe_w2[j])
    return xf.reshape(batch, seq, d)
```

```python
import functools

import jax
import jax.numpy as jnp
from jax import lax
from jax.experimental import pallas as pl
from jax.experimental.pallas import tpu as pltpu

F32 = jnp.float32
BF16 = jnp.bfloat16

D_MODEL = 1024
HEAD_DIM = 64
ATTN_GROUPS = ((128, 1), (512, 4), (2048, 16))
HEADS_PER_GROUP = 4
GROUP_WIDTH = HEADS_PER_GROUP * HEAD_DIM
ATTN_WIDTH = GROUP_WIDTH * len(ATTN_GROUPS)
BAND = 128
CONV_WIDTH = 768
CONV_K = 3
HGRN_WIDTH = 768
HGRN_HEAD = 128
HGRN_HEADS = HGRN_WIDTH // HGRN_HEAD
LB_FLOOR = 1e-30
N_EXPERTS = 8
RMS_EPS = 1e-6
NEG_BIG = -1e30

LANES = 128
SUBLANES = 8

TOKEN_TILE = 512
ATTN_TILE = ATTN_GROUPS[-1][1] * BAND
ATTN_UNROLL = 4
HGRN_CHUNK = 128
MOE_ROW_TILE = 512
MOE_FF_TILE = 1792
VMEM_LIMIT = 56 * 1024 * 1024


def _dot(a, b):
    return jnp.dot(a, b, preferred_element_type=F32)


def _dot_nt(a, b):
    return lax.dot_general(a, b, (((1,), (1,)), ((), ())), preferred_element_type=F32)


def _dot_tn(a, b):
    return lax.dot_general(a, b, (((0,), (0,)), ((), ())), preferred_element_type=F32)


def _rms_rows(xf, gain):
    ms = jnp.mean(xf * xf, axis=-1, keepdims=True)
    return xf * lax.rsqrt(ms + RMS_EPS) * gain


def _sigmoid(z):
    return 1.0 / (1.0 + jnp.exp(-z))


def _params(*sem):
    return pltpu.CompilerParams(dimension_semantics=sem, vmem_limit_bytes=VMEM_LIMIT)


def _full(shape):
    return pl.BlockSpec(shape, lambda *_: (0,) * len(shape))


def _qkv_kernel(x_ref, g_ref, w_ref, qg_ref, kg_ref, hsum_ref, hexp_ref, *rest, tile):
    out_refs, stage = rest[:9], rest[9]
    h = _rms_rows(x_ref[0], g_ref[...]).astype(BF16)
    p = _dot(h, w_ref[...])

    def head_norm(t, gain):
        ms = _dot((t * t).astype(BF16), hsum_ref[...]) * (1.0 / HEAD_DIM)
        r = lax.rsqrt(ms + RMS_EPS)
        r_hi = r.astype(BF16)
        r_lo = (r - r_hi.astype(F32)).astype(BF16)
        r_cols = _dot(r_hi, hexp_ref[...]) + _dot(r_lo, hexp_ref[...])
        return t * r_cols * gain

    parts = (head_norm(p[:, :ATTN_WIDTH], qg_ref[...]),
             head_norm(p[:, ATTN_WIDTH:2 * ATTN_WIDTH], kg_ref[...]),
             p[:, 2 * ATTN_WIDTH:])
    slot = 0
    for a, part in enumerate(parts):
        for group, (_, dilation) in enumerate(ATTN_GROUPS):
            out = out_refs[3 * group + a]
            c0 = group * GROUP_WIDTH
            if dilation == 1:
                out[0, 0] = part[:, c0:c0 + GROUP_WIDTH].astype(BF16)
                continue
            rows = tile // dilation
            for half in range(GROUP_WIDTH // LANES):
                stage[slot] = part[:, c0 + half * LANES:c0 + (half + 1) * LANES]
                for r in range(dilation):
                    out[0, r, :, half * LANES:(half + 1) * LANES] = (
                        stage[slot, pl.ds(r, rows, stride=dilation), :].astype(BF16))
                slot += 1


def _qkv_call(x, gain, w, qg, kg, hsum, hexp, batch, seq):
    t = TOKEN_TILE
    out_specs, out_shape = [], []
    for _, dilation in ATTN_GROUPS:
        for _ in range(3):
            out_specs.append(pl.BlockSpec((1, dilation, t // dilation, GROUP_WIDTH),
                                          lambda b, i: (b, 0, i, 0)))
            out_shape.append(jax.ShapeDtypeStruct(
                (batch, dilation, seq // dilation, GROUP_WIDTH), BF16))
    n_stage = 3 * (GROUP_WIDTH // LANES) * sum(1 for _, d in ATTN_GROUPS if d > 1)
    return pl.pallas_call(
        functools.partial(_qkv_kernel, tile=t),
        grid=(batch, seq // t),
        in_specs=[pl.BlockSpec((1, t, D_MODEL), lambda b, i: (b, i, 0)), _full((1, D_MODEL)),
                  _full(w.shape), _full((1, ATTN_WIDTH)), _full((1, ATTN_WIDTH)),
                  _full(hsum.shape), _full(hexp.shape)],
        out_specs=out_specs,
        out_shape=out_shape,
        scratch_shapes=[pltpu.VMEM((n_stage, t, LANES), F32)],
        compiler_params=_params("parallel", "parallel"),
        name="qkv_proj",
    )(x.reshape(batch, seq, D_MODEL), gain, w, qg, kg, hsum, hexp)


def _attn_kernel(*refs, n_groups):
    ins, o_ref, scr = refs[:5 * n_groups], refs[5 * n_groups], refs[5 * n_groups + 1:]
    kbufs, vbufs = scr[0:n_groups], scr[n_groups:2 * n_groups]
    o_tok, l_tok, o_rm, l_rm = scr[2 * n_groups:]
    first_tile = pl.program_id(1) == 0
    qi = lax.broadcasted_iota(jnp.int32, (BAND, 2 * BAND), 0)
    kj = lax.broadcasted_iota(jnp.int32, (BAND, 2 * BAND), 1)
    rel = qi + BAND - kj
    in_window = (rel >= 0) & (rel <= BAND)
    halves = GROUP_WIDTH // LANES
    lane = lax.broadcasted_iota(jnp.int32, (1, LANES), 1)
    head_lanes = [jnp.where(lane // HEAD_DIM == side, 1.0, 0.0).astype(BF16)
                  for side in range(LANES // HEAD_DIM)]
    first_head = lax.broadcasted_iota(jnp.int32, (BAND, LANES), 1) < HEAD_DIM

    for group, (_, dilation) in enumerate(ATTN_GROUPS):
        q_ref, kc_ref, vc_ref, kp_ref, vp_ref = ins[5 * group:5 * group + 5]
        kbuf, vbuf = kbufs[group], vbufs[group]
        length = ATTN_TILE // dilation
        nj = length // BAND
        kbuf[:, 0:BAND, :] = kp_ref[0]
        kbuf[:, BAND:BAND + length, :] = kc_ref[0]
        vbuf[:, 0:BAND, :] = vp_ref[0]
        vbuf[:, BAND:BAND + length, :] = vc_ref[0]

        def block(idx, carry, q_ref=q_ref, kbuf=kbuf, vbuf=vbuf, nj=nj, length=length,
                  group=group, dilation=dilation):
            r, j = idx // nj, idx % nj
            j0 = pl.multiple_of(j * BAND, BAND)
            kmin = jnp.where(first_tile & (j == 0), BAND, 0)
            valid = in_window & (kj >= kmin)
            row0 = pl.multiple_of(r * length + j0, BAND)
            for half in range(halves):
                lanes = slice(half * LANES, (half + 1) * LANES)
                q2 = q_ref[0, r, pl.ds(j0, BAND), lanes]
                k2 = kbuf[r, pl.ds(j0, 2 * BAND), lanes]
                v2 = vbuf[r, pl.ds(j0, 2 * BAND), lanes]
                o2, l2 = None, None
                for side in range(LANES // HEAD_DIM):
                    s = _dot_nt(q2 * head_lanes[side], k2)
                    s = jnp.where(valid, s, NEG_BIG)
                    m = jnp.max(s, axis=-1, keepdims=True)
                    p = jnp.exp(s - m)
                    den = jnp.sum(p, axis=-1, keepdims=True)
                    o = _dot(p.astype(BF16), v2) * (1.0 / den)
                    lse = jnp.broadcast_to(m + jnp.log(den), (BAND, LANES))
                    o2 = o if side == 0 else jnp.where(first_head, o2, o)
                    l2 = lse if side == 0 else jnp.where(first_head, l2, lse)
                if dilation == 1:
                    o_tok[group, half, pl.ds(row0, BAND), :] = o2
                    l_tok[group, half, pl.ds(row0, BAND), :] = l2
                else:
                    o_rm[pl.ds(row0, BAND), lanes] = o2
                    l_rm[pl.ds(row0, BAND), lanes] = l2
            return carry

        lax.fori_loop(0, ATTN_TILE // BAND, block, 0, unroll=ATTN_UNROLL)
        if dilation > 1:
            for r in range(dilation):
                rows = slice(r * length, (r + 1) * length)
                for half in range(halves):
                    lanes = slice(half * LANES, (half + 1) * LANES)
                    o_tok[group, half, pl.ds(r, length, stride=dilation), :] = o_rm[rows, lanes]
                    l_tok[group, half, pl.ds(r, length, stride=dilation), :] = l_rm[rows, lanes]

    chunk = 2 * BAND

    def merge(c, carry):
        rows = pl.ds(pl.multiple_of(c * chunk, chunk), chunk)
        for half in range(halves):
            ls = [l_tok[g, half, rows, :] for g in range(n_groups)]
            top = functools.reduce(jnp.maximum, ls)
            es = [jnp.exp(l - top) for l in ls]
            acc = sum(e * o_tok[g, half, rows, :] for g, e in enumerate(es))
            o_ref[0, rows, half * LANES:(half + 1) * LANES] = (acc * (1.0 / sum(es))).astype(BF16)
        return carry

    lax.fori_loop(0, ATTN_TILE // chunk, merge, 0)


def _attn_call(qkv, batch, seq):
    n_groups = len(ATTN_GROUPS)
    in_specs, args, kbufs = [], [], []
    for group, (_, dilation) in enumerate(ATTN_GROUPS):
        length = ATTN_TILE // dilation
        cur = pl.BlockSpec((1, dilation, length, GROUP_WIDTH), lambda b, i: (b, 0, i, 0))
        halo = pl.BlockSpec(
            (1, dilation, BAND, GROUP_WIDTH),
            lambda b, i, nb=length // BAND: (b, 0, jnp.maximum(i * nb - 1, 0), 0))
        q, k, v = qkv[3 * group:3 * group + 3]
        in_specs += [cur, cur, cur, halo, halo]
        args += [q, k, v, k, v]
        kbufs.append(pltpu.VMEM((dilation, BAND + length, GROUP_WIDTH), BF16))
    halves = GROUP_WIDTH // LANES
    o = pl.pallas_call(
        functools.partial(_attn_kernel, n_groups=n_groups),
        grid=(batch, seq // ATTN_TILE),
        in_specs=in_specs,
        out_specs=pl.BlockSpec((1, ATTN_TILE, GROUP_WIDTH), lambda b, i: (b, i, 0)),
        out_shape=jax.ShapeDtypeStruct((batch, seq, GROUP_WIDTH), BF16),
        scratch_shapes=kbufs + kbufs + [
            pltpu.VMEM((n_groups, halves, ATTN_TILE, LANES), F32),
            pltpu.VMEM((n_groups, halves, ATTN_TILE, LANES), F32),
            pltpu.VMEM((ATTN_TILE, GROUP_WIDTH), F32),
            pltpu.VMEM((ATTN_TILE, GROUP_WIDTH), F32)],
        compiler_params=_params("parallel", "arbitrary"),
        name="band_attn",
    )(*args)
    return o.reshape(batch * seq, GROUP_WIDTH)


def _hgrn_kernel(x_ref, g_ref, w_ref, lb_ref, og_ref, wp_ref, tri_ref, y_ref,
                 state_ref, proj_ref, gated_ref, *, tile, chunk):
    @pl.when(pl.program_id(1) == 0)
    def _():
        state_ref[...] = jnp.zeros_like(state_ref)

    h = _rms_rows(x_ref[0], g_ref[...]).astype(BF16)
    proj_ref[...] = _dot(h, w_ref[...])

    lb = lb_ref[...]
    lb_floor = jnp.maximum(lb, LB_FLOOR)
    one_minus_lb = 1.0 - lb
    floor_gap = lb_floor - lb
    row = lax.broadcasted_iota(jnp.int32, (chunk, HGRN_HEAD), 0)
    tt = lax.broadcasted_iota(jnp.int32, (chunk, chunk), 0)
    ss = lax.broadcasted_iota(jnp.int32, (chunk, chunk), 1)
    t_xor_s = tt ^ ss
    w = HGRN_WIDTH
    sizes = [1 << i for i in range(chunk.bit_length() - 1)]
    lower_rows = [(row & m) == 0 for m in sizes]
    lower_sel = [jnp.where(low, 1.0, 0.0).astype(BF16) for low in lower_rows]
    level_pairs = [((t_xor_s & (-m)) == m) & ((tt & m) != 0) for m in sizes]
    diagonal = tt == ss

    def chunk_body(c, carry):
        r0 = pl.multiple_of(c * chunk, chunk)
        rows = pl.ds(r0, chunk)
        z = proj_ref[rows, w:2 * w]
        e = jnp.exp(-jnp.abs(z))
        inv = 1.0 / (1.0 + e)
        sig = jnp.where(z >= 0, inv, e * inv)
        nsig = jnp.where(z >= 0, e * inv, inv)
        log_f = jnp.log(lb_floor + one_minus_lb * sig)
        k_all = one_minus_lb * nsig - floor_gap
        lf_hi = log_f.astype(BF16)
        rest = log_f - lf_hi.astype(F32)
        lf_mid = rest.astype(BF16)
        lf_lo = (rest - lf_mid.astype(F32)).astype(BF16)
        tri = tri_ref[...]
        b_all = _dot(tri, lf_hi) + _dot(tri, lf_mid) + _dot(tri, lf_lo)
        q_all = proj_ref[rows, 0:w]
        q_all = q_all * _sigmoid(q_all)
        v_all = proj_ref[rows, 2 * w:3 * w]
        g_all = proj_ref[rows, 3 * w:4 * w]
        g_all = g_all * _sigmoid(g_all)
        for hd in range(HGRN_HEADS):
            cols = slice(hd * HGRN_HEAD, (hd + 1) * HGRN_HEAD)
            b, q, k = b_all[:, cols], q_all[:, cols], k_all[:, cols]
            vb = v_all[:, cols].astype(BF16)
            qb, kb = q.astype(BF16), k.astype(BF16)
            scores = jnp.where(diagonal, _dot_nt(qb, kb), 0.0)
            last_of_block = b
            for lvl, m in enumerate(sizes):
                lower = lower_rows[lvl]
                boundary = jnp.where(lower, last_of_block, pltpu.roll(last_of_block, m, 0))
                decay = jnp.exp(-jnp.abs((b - boundary).astype(BF16)))
                side = jnp.where(lower_sel[lvl] > 0, kb, qb) * decay
                scores = jnp.where(level_pairs[lvl], _dot_nt(side, side), scores)
                if 2 * m < chunk:
                    last_of_block = jnp.where(
                        lower, pltpu.roll(last_of_block, chunk - m, 0), last_of_block)
            state_t = state_ref[hd]
            q_dec = (q * jnp.exp(b)).astype(BF16)
            o = _dot(scores.astype(BF16), vb) + _dot_nt(q_dec, state_t.astype(BF16))
            b_last = b[chunk - 1:chunk, :]
            k_dec = (k * jnp.exp(b_last - b)).astype(BF16)
            state_ref[hd] = state_t * jnp.exp(b_last) + _dot_tn(vb, k_dec)
            o = _rms_rows(o, og_ref[...]) * g_all[:, cols]
            gated_ref[rows, cols] = o.astype(BF16)
        return carry

    lax.fori_loop(0, tile // chunk, chunk_body, 0)
    y_ref[0] = _dot(gated_ref[...], wp_ref[...]).astype(BF16)


def _hgrn_call(x, gain, w, lb, out_gain, w_proj, batch, seq):
    tile, chunk = TOKEN_TILE, HGRN_CHUNK
    tri = jnp.tril(jnp.ones((chunk, chunk), BF16))
    blk = lambda width: pl.BlockSpec((1, tile, width), lambda b, i: (b, i, 0))
    y = pl.pallas_call(
        functools.partial(_hgrn_kernel, tile=tile, chunk=chunk),
        grid=(batch, seq // tile),
        in_specs=[blk(D_MODEL), _full((1, D_MODEL)), _full(w.shape), _full((1, HGRN_WIDTH)),
                  _full((1, HGRN_HEAD)), _full(w_proj.shape), _full((chunk, chunk))],
        out_specs=blk(D_MODEL),
        out_shape=jax.ShapeDtypeStruct((batch, seq, D_MODEL), BF16),
        scratch_shapes=[pltpu.VMEM((HGRN_HEADS, HGRN_HEAD, HGRN_HEAD), F32),
                        pltpu.VMEM((tile, 4 * HGRN_WIDTH), F32),
                        pltpu.VMEM((tile, HGRN_WIDTH), BF16)],
        compiler_params=_params("parallel", "arbitrary"),
        name="hgrn_branch",
    )(x.reshape(batch, seq, D_MODEL), gain, w, lb, out_gain, w_proj, tri)
    return y.reshape(batch * seq, D_MODEL)


def _merge_kernel(x_ref, g_ref, w_ref, cw_ref, oa_ref, yc_ref,
                  wpa_ref, wpb_ref, wo_ref, out_ref, u_ext, *, tile, tiles_per_seq):
    @pl.when(pl.program_id(0) % tiles_per_seq == 0)
    def _():
        u_ext[0:SUBLANES, :] = jnp.zeros((SUBLANES, CONV_WIDTH), F32)

    x = x_ref[...]
    h = _rms_rows(x, g_ref[...]).astype(BF16)
    p = _dot(h, w_ref[...])
    cw = CONV_WIDTH
    u = p[:, 2 * cw:3 * cw] * p[:, 0:cw]
    u_ext[SUBLANES:SUBLANES + tile, :] = u
    conv = (cw_ref[0:1, :] * u
            + cw_ref[1:2, :] * u_ext[SUBLANES - 1:SUBLANES - 1 + tile, :]
            + cw_ref[2:3, :] * u_ext[SUBLANES - 2:SUBLANES - 2 + tile, :])
    u_ext[0:SUBLANES, :] = u[tile - SUBLANES:, :]
    yb = _dot((p[:, cw:2 * cw] * conv).astype(BF16), wpb_ref[...])
    ya = _dot(oa_ref[...], wpa_ref[...])

    d = D_MODEL
    gates = p[:, 3 * cw:]
    merged = (_sigmoid(gates[:, 0:d]) * ya + _sigmoid(gates[:, d:2 * d]) * yb
              + _sigmoid(gates[:, 2 * d:3 * d]) * yc_ref[...].astype(F32))
    out_ref[...] = x + _dot(merged.astype(BF16), wo_ref[...])


def _merge_call(x, gain, w, conv_w, oa, yc, wpa, wpb, wo, seq):
    n = x.shape[0]
    tile = TOKEN_TILE
    row = lambda width: pl.BlockSpec((tile, width), lambda i: (i, 0))
    return pl.pallas_call(
        functools.partial(_merge_kernel, tile=tile, tiles_per_seq=seq // tile),
        grid=(n // tile,),
        in_specs=[row(D_MODEL), _full((1, D_MODEL)), _full(w.shape), _full(conv_w.shape),
                  row(GROUP_WIDTH), row(D_MODEL), _full(wpa.shape), _full(wpb.shape),
                  _full(wo.shape)],
        out_specs=row(D_MODEL),
        out_shape=jax.ShapeDtypeStruct((n, D_MODEL), F32),
        scratch_shapes=[pltpu.VMEM((SUBLANES + tile, CONV_WIDTH), F32)],
        compiler_params=_params("arbitrary"),
        name="merge_out",
    )(x, gain, w, conv_w, oa, yc, wpa, wpb, wo)


def _ffn_kernel(x_ref, g_ref, w1_ref, w3_ref, w2_ref, out_ref):
    x = x_ref[...]
    h = _rms_rows(x, g_ref[...]).astype(BF16)
    a = _dot(h, w1_ref[...])
    z = (a * _sigmoid(a) * _dot(h, w3_ref[...])).astype(BF16)
    out_ref[...] = x + _dot(z, w2_ref[...])


def _ffn_call(x, gain, w1, w3, w2):
    n = x.shape[0]
    tile = TOKEN_TILE
    row = pl.BlockSpec((tile, D_MODEL), lambda i: (i, 0))
    return pl.pallas_call(
        _ffn_kernel,
        grid=(n // tile,),
        in_specs=[row, _full((1, D_MODEL)), _full(w1.shape), _full(w3.shape), _full(w2.shape)],
        out_specs=row,
        out_shape=jax.ShapeDtypeStruct((n, D_MODEL), F32),
        compiler_params=_params("parallel"),
        name="dense_ffn",
    )(x, gain, w1, w3, w2)


def _router_kernel(x_ref, g_ref, rw_hi_ref, rw_lo_ref, route_ref):
    h = _rms_rows(x_ref[...], g_ref[...]).astype(BF16)
    logits = _dot(h, rw_hi_ref[...]) + _dot(h, rw_lo_ref[...])
    lane = lax.broadcasted_iota(jnp.int32, logits.shape, 1)
    logits = jnp.where(lane < N_EXPERTS, logits, -jnp.inf)
    m1 = jnp.max(logits, axis=-1, keepdims=True)
    i1 = jnp.min(jnp.where(logits == m1, lane, LANES), axis=-1, keepdims=True)
    rest = jnp.where(lane == i1, -jnp.inf, logits)
    m2 = jnp.max(rest, axis=-1, keepdims=True)
    i2 = jnp.min(jnp.where(rest == m2, lane, LANES), axis=-1, keepdims=True)
    e = jnp.exp(m2 - m1)
    g1 = 1.0 / (1.0 + e)
    g2 = e * g1
    route_ref[...] = jnp.where(
        lane == 0, g1, jnp.where(lane == 1, g2, jnp.where(
            lane == 2, i1.astype(F32), jnp.where(lane == 3, i2.astype(F32), 0.0))))


def _router_call(x, gain, rw_hi, rw_lo):
    n = x.shape[0]
    tile = TOKEN_TILE
    return pl.pallas_call(
        _router_kernel,
        grid=(n // tile,),
        in_specs=[pl.BlockSpec((tile, D_MODEL), lambda i: (i, 0)), _full((1, D_MODEL)),
                  _full(rw_hi.shape), _full(rw_lo.shape)],
        out_specs=pl.BlockSpec((tile, LANES), lambda i: (i, 0)),
        out_shape=jax.ShapeDtypeStruct((n, LANES), F32),
        compiler_params=_params("parallel"),
        name="router",
    )(x, gain, rw_hi, rw_lo)


def _row_copy(src, dst, src_row, dst_row, sem):
    return pltpu.make_async_copy(src.at[pl.ds(src_row, 1), :], dst.at[pl.ds(dst_row, 1), :], sem)


def _dispatch_kernel(pad_ref, pos_ref, x_ref, out_hbm, zeros, sem, pad_sem, *, tile):
    @pl.when(pl.program_id(0) == 0)
    def _():
        zeros[...] = jnp.zeros_like(zeros)
        for e in range(N_EXPERTS):
            first, count = pad_ref[e], pad_ref[N_EXPERTS + e]

            def fill(j, carry, first=first):
                _row_copy(zeros, out_hbm, 0, first + j, pad_sem).start()
                return carry

            def drain(j, carry, first=first):
                _row_copy(zeros, out_hbm, 0, first + j, pad_sem).wait()
                return carry

            lax.fori_loop(0, count, fill, 0)
            lax.fori_loop(0, count, drain, 0)

        def tail_copy(k):
            start = pl.multiple_of(pad_ref[2 * N_EXPERTS] + k * tile, tile)
            return pltpu.make_async_copy(zeros, out_hbm.at[pl.ds(start, tile), :], pad_sem)

        def tail_fill(k, carry):
            tail_copy(k).start()
            return carry

        def tail_drain(k, carry):
            tail_copy(k).wait()
            return carry

        lax.fori_loop(0, pad_ref[2 * N_EXPERTS + 1], tail_fill, 0)
        lax.fori_loop(0, pad_ref[2 * N_EXPERTS + 1], tail_drain, 0)

    def issue(j, carry):
        _row_copy(x_ref, out_hbm, j, pos_ref[j], sem).start()
        _row_copy(x_ref, out_hbm, j, pos_ref[tile + j], sem).start()
        return carry

    lax.fori_loop(0, tile, issue, 0, unroll=8)
    for _ in range(2):
        pltpu.make_async_copy(x_ref, out_hbm.at[pl.ds(0, tile), :], sem).wait()


def _dispatch_call(x, pos_tiles, pad_info, n_sorted):
    n = x.shape[0]
    tile = TOKEN_TILE
    grid_spec = pltpu.PrefetchScalarGridSpec(
        num_scalar_prefetch=1,
        grid=(n // tile,),
        in_specs=[pl.BlockSpec((2 * tile,), lambda i, pad: (i,), memory_space=pltpu.SMEM),
                  pl.BlockSpec((tile, D_MODEL), lambda i, pad: (i, 0))],
        out_specs=pl.BlockSpec(memory_space=pl.ANY),
        scratch_shapes=[pltpu.VMEM((tile, D_MODEL), F32), pltpu.SemaphoreType.DMA(()),
                        pltpu.SemaphoreType.DMA(())],
    )
    return pl.pallas_call(
        functools.partial(_dispatch_kernel, tile=tile),
        grid_spec=grid_spec,
        out_shape=jax.ShapeDtypeStruct((n_sorted, D_MODEL), F32),
        compiler_params=_params("arbitrary"),
        name="moe_dispatch",
    )(pad_info, pos_tiles, x)


def _moe_kernel(tile_expert_ref, n_tiles_ref, x_ref, g_ref, w1_ref, w3_ref, w2_ref, y_ref,
                h_ref, acc_ref):
    i, f = pl.program_id(0), pl.program_id(1)
    last_f = pl.num_programs(1) - 1
    active = i < n_tiles_ref[0]

    @pl.when(active & (f == 0))
    def _():
        h_ref[...] = _rms_rows(x_ref[...], g_ref[...]).astype(BF16)

    @pl.when(active)
    def _():
        h = h_ref[...]
        a = _dot(h, w1_ref[...])
        z = (a * _sigmoid(a) * _dot(h, w3_ref[...])).astype(BF16)
        part = _dot(z, w2_ref[...])

        @pl.when(f == 0)
        def _():
            acc_ref[...] = part

        @pl.when(f > 0)
        def _():
            acc_ref[...] += part

    @pl.when(active & (f == last_f))
    def _():
        y_ref[...] = acc_ref[...]

    @pl.when(jnp.logical_not(active) & (f == last_f))
    def _():
        y_ref[...] = jnp.zeros_like(y_ref)


def _moe_call(xs, gain, w1, w3, w2, tile_expert, n_tiles):
    p = xs.shape[0]
    d_ff = w1.shape[-1]
    tm, tf = MOE_ROW_TILE, MOE_FF_TILE
    nf = d_ff // tf

    def ff_block(i, f, n_tiles_ref):
        return jnp.where(i < n_tiles_ref[0], f, nf - 1)

    grid_spec = pltpu.PrefetchScalarGridSpec(
        num_scalar_prefetch=2,
        grid=(p // tm, nf),
        in_specs=[
            pl.BlockSpec((tm, D_MODEL), lambda i, f, te, nt: (jnp.minimum(i, nt[0] - 1), 0)),
            pl.BlockSpec((1, D_MODEL), lambda i, f, te, nt: (0, 0)),
            pl.BlockSpec((None, D_MODEL, tf), lambda i, f, te, nt: (te[i], 0, ff_block(i, f, nt))),
            pl.BlockSpec((None, D_MODEL, tf), lambda i, f, te, nt: (te[i], 0, ff_block(i, f, nt))),
            pl.BlockSpec((None, tf, D_MODEL), lambda i, f, te, nt: (te[i], ff_block(i, f, nt), 0)),
        ],
        out_specs=pl.BlockSpec((tm, D_MODEL), lambda i, f, te, nt: (i, 0)),
        scratch_shapes=[pltpu.VMEM((tm, D_MODEL), BF16), pltpu.VMEM((tm, D_MODEL), F32)],
    )
    return pl.pallas_call(
        _moe_kernel,
        grid_spec=grid_spec,
        out_shape=jax.ShapeDtypeStruct((p, D_MODEL), F32),
        compiler_params=_params("arbitrary", "arbitrary"),
        name="moe_experts",
    )(tile_expert, n_tiles, xs, gain, w1, w3, w2)


def _combine_kernel(pos_ref, x_ref, route_ref, y_hbm, out_ref, buf, sems, *, tile):
    def issue(j, carry):
        _row_copy(y_hbm, buf.at[0], pos_ref[j], j, sems.at[0]).start()
        _row_copy(y_hbm, buf.at[1], pos_ref[tile + j], j, sems.at[1]).start()
        return carry

    lax.fori_loop(0, tile, issue, 0, unroll=8)
    for k in range(2):
        pltpu.make_async_copy(y_hbm.at[pl.ds(0, tile), :], buf.at[k], sems.at[k]).wait()
    g1 = route_ref[:, 0:1]
    g2 = route_ref[:, 1:2]
    out_ref[...] = x_ref[...] + g1 * buf[0] + g2 * buf[1]


def _combine_call(x, route, y, pos_tiles):
    n = x.shape[0]
    tile = TOKEN_TILE
    row = lambda width: pl.BlockSpec((tile, width), lambda i: (i, 0))
    return pl.pallas_call(
        functools.partial(_combine_kernel, tile=tile),
        grid=(n // tile,),
        in_specs=[pl.BlockSpec((2 * tile,), lambda i: (i,), memory_space=pltpu.SMEM),
                  row(D_MODEL), row(LANES), pl.BlockSpec(memory_space=pl.ANY)],
        out_specs=row(D_MODEL),
        out_shape=jax.ShapeDtypeStruct(x.shape, F32),
        scratch_shapes=[pltpu.VMEM((2, tile, D_MODEL), F32), pltpu.SemaphoreType.DMA((2,))],
        compiler_params=_params("arbitrary"),
        name="moe_combine",
    )(pos_tiles, x, route, y)


def _routing_tables(route, n):
    tm = MOE_ROW_TILE
    p = 2 * n + N_EXPERTS * tm
    expert = jnp.concatenate([route[:, 2], route[:, 3]]).astype(jnp.int32)
    onehot = (expert[:, None] == jnp.arange(N_EXPERTS, dtype=jnp.int32)[None, :]).astype(jnp.int32)
    running = jnp.cumsum(onehot, axis=0)
    rank = jnp.sum(running * onehot, axis=1) - 1
    counts = running[-1]
    padded = ((counts + tm - 1) // tm) * tm
    ends = jnp.cumsum(padded)
    starts = ends - padded
    pos = jnp.sum(onehot * starts[None, :], axis=1) + rank
    n_tiles = (ends[-1] // tm).astype(jnp.int32)
    tile_start = jnp.arange(p // tm, dtype=jnp.int32) * tm
    tile_expert = jnp.sum((tile_start[:, None] >= ends[None, :]).astype(jnp.int32), axis=1)
    last_expert = jnp.sum((ends[-1] - 1 >= ends).astype(jnp.int32))
    tile_expert = jnp.minimum(tile_expert, last_expert).astype(jnp.int32)
    tile = TOKEN_TILE
    tail = jnp.stack([ends[-1], (p - ends[-1]) // tile])
    pad_info = jnp.concatenate([starts + counts, padded - counts, tail]).astype(jnp.int32)
    pos_tiles = jnp.concatenate(
        [pos[:n].reshape(n // tile, tile), pos[n:].reshape(n // tile, tile)], axis=1).reshape(-1)
    return pos_tiles.astype(jnp.int32), pad_info, tile_expert, n_tiles.reshape(1), p


def _moe_layer(x, gain, router_w, w1, w3, w2):
    n = x.shape[0]
    rw = jnp.pad(router_w, ((0, 0), (0, LANES - N_EXPERTS)))
    rw_hi = rw.astype(BF16)
    rw_lo = (rw - rw_hi.astype(F32)).astype(BF16)
    route = _router_call(x, gain, rw_hi, rw_lo)
    pos_tiles, pad_info, tile_expert, n_tiles, n_sorted = _routing_tables(route, n)
    xs = _dispatch_call(x, pos_tiles, pad_info, n_sorted)
    ys = _moe_call(xs, gain, w1.astype(BF16), w3.astype(BF16), w2.astype(BF16), tile_expert, n_tiles)
    return _combine_call(x, route, ys, pos_tiles)


def kernel(x, attn_norm, ffn_norm, w_in, q_norm, k_norm, conv_w, hgrn_lower_bounds,
           hgrn_out_norm, w_proj_a, w_proj_b, w_proj_c, w_out, dense_w1, dense_w3,
           dense_w2, router_w, moe_w1, moe_w3, moe_w2):
    batch, seq, d = x.shape
    n = batch * seq
    depth = w_in.shape[0]
    assert d == D_MODEL and seq % ATTN_TILE == 0 and seq % TOKEN_TILE == 0
    assert all(window // dilation == BAND for window, dilation in ATTN_GROUPS)

    lbs = jax.nn.softmax(hgrn_lower_bounds.astype(F32), axis=0)
    lower_bound = jnp.cumsum(lbs, axis=0) - lbs[0]
    head_of_col = jnp.arange(ATTN_WIDTH) // HEAD_DIM
    hsum = (head_of_col[:, None] == jnp.arange(LANES)[None, :]).astype(BF16)
    hexp = hsum.T
    a0, a1 = 0, 3 * ATTN_WIDTH
    b1 = a1 + 3 * CONV_WIDTH
    c1 = b1 + 4 * HGRN_WIDTH

    xf = x.reshape(n, d)
    for layer in range(depth):
        w = w_in[layer]
        w_attn = w[:, a0:a1].astype(BF16)
        w_hgrn = w[:, b1:c1].astype(BF16)
        w_conv_gate = jnp.concatenate([w[:, a1:b1], w[:, c1:]], axis=1).astype(BF16)
        gain = attn_norm[layer].reshape(1, d)
        heads = ATTN_WIDTH // HEAD_DIM
        qg = jnp.tile(q_norm[layer] * (HEAD_DIM ** -0.5), heads).reshape(1, ATTN_WIDTH)
        kg = jnp.tile(k_norm[layer], heads).reshape(1, ATTN_WIDTH)

        qkv = _qkv_call(xf, gain, w_attn, qg, kg, hsum, hexp, batch, seq)
        oa = _attn_call(qkv, batch, seq)
        yc = _hgrn_call(xf, gain, w_hgrn, lower_bound[layer].reshape(1, HGRN_WIDTH),
                        hgrn_out_norm[layer].reshape(1, HGRN_HEAD),
                        w_proj_c[layer].astype(BF16), batch, seq)
        xf = _merge_call(xf, gain, w_conv_gate, conv_w[layer], oa, yc,
                         w_proj_a[layer].astype(BF16), w_proj_b[layer].astype(BF16),
                         w_out[layer].astype(BF16), seq)

        fgain = ffn_norm[layer].reshape(1, d)
        j = layer // 2
        if layer % 2 == 0:
            xf = _ffn_call(xf, fgain, dense_w1[j].astype(BF16), dense_w3[j].astype(BF16),
                           dense_w2[j].astype(BF16))
        else:
            xf = _moe_layer(xf, fgain, router_w[j], moe_w1[j], moe_w3[j], moe_w2[j])
    return xf.reshape(batch, seq, d)
```

```python
import functools

import jax
import jax.numpy as jnp
from jax import lax
from jax.experimental import pallas as pl
from jax.experimental.pallas import tpu as pltpu

F32 = jnp.float32
BF16 = jnp.bfloat16

D_MODEL = 1024
HEAD_DIM = 64
ATTN_GROUPS = ((128, 1), (512, 4), (2048, 16))
HEADS_PER_GROUP = 4
GROUP_WIDTH = HEADS_PER_GROUP * HEAD_DIM
ATTN_WIDTH = GROUP_WIDTH * len(ATTN_GROUPS)
BAND = 128
CONV_WIDTH = 768
CONV_K = 3
HGRN_WIDTH = 768
HGRN_HEAD = 128
HGRN_HEADS = HGRN_WIDTH // HGRN_HEAD
LB_FLOOR = 1e-30
N_EXPERTS = 8
RMS_EPS = 1e-6
NEG_BIG = -1e30
LOG2E = 1.4426950408889634

LANES = 128
SUBLANES = 8

TOKEN_TILE = 512
ATTN_TILE = ATTN_GROUPS[-1][1] * BAND
ATTN_UNROLL = 8
HGRN_CHUNK = 256
MOE_ROW_TILE = 512
MOE_FF_TILE = 1792
VMEM_LIMIT = 56 * 1024 * 1024


def _dot(a, b):
    return jnp.dot(a, b, preferred_element_type=F32)


def _dot_nt(a, b):
    return lax.dot_general(a, b, (((1,), (1,)), ((), ())), preferred_element_type=F32)


def _dot_tn(a, b):
    return lax.dot_general(a, b, (((0,), (0,)), ((), ())), preferred_element_type=F32)


def _rms_rows(xf, gain):
    ms = jnp.mean(xf * xf, axis=-1, keepdims=True)
    return xf * lax.rsqrt(ms + RMS_EPS) * gain


def _sigmoid(z):
    return 1.0 / (1.0 + jnp.exp(-z))


def _params(*sem):
    return pltpu.CompilerParams(dimension_semantics=sem, vmem_limit_bytes=VMEM_LIMIT)


def _full(shape):
    return pl.BlockSpec(shape, lambda *_: (0,) * len(shape))


def _qkv_kernel(x_ref, g_ref, w_ref, qg_ref, kg_ref, hsum_ref, hexp_ref, *rest, tile):
    out_refs, stage = rest[:9], rest[9]
    h = _rms_rows(x_ref[0], g_ref[...]).astype(BF16)
    p = _dot(h, w_ref[...])

    def head_norm(t, gain):
        ms = _dot((t * t).astype(BF16), hsum_ref[...]) * (1.0 / HEAD_DIM)
        r = lax.rsqrt(ms + RMS_EPS)
        r_hi = r.astype(BF16)
        r_lo = (r - r_hi.astype(F32)).astype(BF16)
        r_cols = _dot(r_hi, hexp_ref[...]) + _dot(r_lo, hexp_ref[...])
        return t * r_cols * gain

    parts = (head_norm(p[:, :ATTN_WIDTH], qg_ref[...]),
             head_norm(p[:, ATTN_WIDTH:2 * ATTN_WIDTH], kg_ref[...]),
             p[:, 2 * ATTN_WIDTH:])
    slot = 0
    for a, part in enumerate(parts):
        for group, (_, dilation) in enumerate(ATTN_GROUPS):
            out = out_refs[3 * group + a]
            c0 = group * GROUP_WIDTH
            if dilation == 1:
                out[0, 0] = part[:, c0:c0 + GROUP_WIDTH].astype(BF16)
                continue
            rows = tile // dilation
            for half in range(GROUP_WIDTH // LANES):
                stage[slot] = part[:, c0 + half * LANES:c0 + (half + 1) * LANES]
                for r in range(dilation):
                    out[0, r, :, half * LANES:(half + 1) * LANES] = (
                        stage[slot, pl.ds(r, rows, stride=dilation), :].astype(BF16))
                slot += 1


def _qkv_call(x, gain, w, qg, kg, hsum, hexp, batch, seq):
    t = TOKEN_TILE
    out_specs, out_shape = [], []
    for _, dilation in ATTN_GROUPS:
        for _ in range(3):
            out_specs.append(pl.BlockSpec((1, dilation, t // dilation, GROUP_WIDTH),
                                          lambda b, i: (b, 0, i, 0)))
            out_shape.append(jax.ShapeDtypeStruct(
                (batch, dilation, seq // dilation, GROUP_WIDTH), BF16))
    n_stage = 3 * (GROUP_WIDTH // LANES) * sum(1 for _, d in ATTN_GROUPS if d > 1)
    return pl.pallas_call(
        functools.partial(_qkv_kernel, tile=t),
        grid=(batch, seq // t),
        in_specs=[pl.BlockSpec((1, t, D_MODEL), lambda b, i: (b, i, 0)), _full((1, D_MODEL)),
                  _full(w.shape), _full((1, ATTN_WIDTH)), _full((1, ATTN_WIDTH)),
                  _full(hsum.shape), _full(hexp.shape)],
        out_specs=out_specs,
        out_shape=out_shape,
        scratch_shapes=[pltpu.VMEM((n_stage, t, LANES), F32)],
        compiler_params=_params("parallel", "parallel"),
        name="qkv_proj",
    )(x.reshape(batch, seq, D_MODEL), gain, w, qg, kg, hsum, hexp)


def _attn_kernel(*refs, n_groups):
    ins, o_ref, scr = refs[:5 * n_groups], refs[5 * n_groups], refs[5 * n_groups + 1:]
    kbufs, vbufs = scr[0:n_groups], scr[n_groups:2 * n_groups]
    o_tok, l_tok, o_rm, l_rm = scr[2 * n_groups:]
    first_tile = pl.program_id(1) == 0
    qi = lax.broadcasted_iota(jnp.int32, (BAND, 2 * BAND), 0)
    kj = lax.broadcasted_iota(jnp.int32, (BAND, 2 * BAND), 1)
    rel = qi + BAND - kj
    in_window = (rel >= 0) & (rel <= BAND)
    halves = GROUP_WIDTH // LANES
    lane = lax.broadcasted_iota(jnp.int32, (1, LANES), 1)
    head_lanes = [jnp.where(lane // HEAD_DIM == side, 1.0, 0.0).astype(BF16)
                  for side in range(LANES // HEAD_DIM)]
    first_head = lax.broadcasted_iota(jnp.int32, (BAND, LANES), 1) < HEAD_DIM

    for group, (_, dilation) in enumerate(ATTN_GROUPS):
        q_ref, kc_ref, vc_ref, kp_ref, vp_ref = ins[5 * group:5 * group + 5]
        kbuf, vbuf = kbufs[group], vbufs[group]
        length = ATTN_TILE // dilation
        nj = length // BAND
        kbuf[:, 0:BAND, :] = kp_ref[0]
        kbuf[:, BAND:BAND + length, :] = kc_ref[0]
        vbuf[:, 0:BAND, :] = vp_ref[0]
        vbuf[:, BAND:BAND + length, :] = vc_ref[0]

        def block(idx, carry, q_ref=q_ref, kbuf=kbuf, vbuf=vbuf, nj=nj, length=length,
                  group=group, dilation=dilation):
            r, j = idx // nj, idx % nj
            j0 = pl.multiple_of(j * BAND, BAND)
            kmin = jnp.where(first_tile & (j == 0), BAND, 0)
            valid = in_window & (kj >= kmin)
            row0 = pl.multiple_of(r * length + j0, BAND)
            for half in range(halves):
                lanes = slice(half * LANES, (half + 1) * LANES)
                q2 = q_ref[0, r, pl.ds(j0, BAND), lanes]
                k2 = kbuf[r, pl.ds(j0, 2 * BAND), lanes]
                v2 = vbuf[r, pl.ds(j0, 2 * BAND), lanes]
                o2, l2 = None, None
                for side in range(LANES // HEAD_DIM):
                    s = _dot_nt(q2 * head_lanes[side], k2)
                    s = jnp.where(valid, s, NEG_BIG)
                    m = jnp.max(s, axis=-1, keepdims=True)
                    p = jnp.exp(s - m)
                    den = jnp.sum(p, axis=-1, keepdims=True)
                    o = _dot(p.astype(BF16), v2) * (1.0 / den)
                    lse = jnp.broadcast_to(m + jnp.log(den), (BAND, LANES))
                    o2 = o if side == 0 else jnp.where(first_head, o2, o)
                    l2 = lse if side == 0 else jnp.where(first_head, l2, lse)
                if dilation == 1:
                    o_tok[group, half, pl.ds(row0, BAND), :] = o2
                    l_tok[group, half, pl.ds(row0, BAND), :] = l2
                else:
                    o_rm[pl.ds(row0, BAND), lanes] = o2
                    l_rm[pl.ds(row0, BAND), lanes] = l2
            return carry

        lax.fori_loop(0, ATTN_TILE // BAND, block, 0, unroll=ATTN_UNROLL)
        if dilation > 1:
            for r in range(dilation):
                rows = slice(r * length, (r + 1) * length)
                for half in range(halves):
                    lanes = slice(half * LANES, (half + 1) * LANES)
                    o_tok[group, half, pl.ds(r, length, stride=dilation), :] = o_rm[rows, lanes]
                    l_tok[group, half, pl.ds(r, length, stride=dilation), :] = l_rm[rows, lanes]

    chunk = 2 * BAND

    def merge(c, carry):
        rows = pl.ds(pl.multiple_of(c * chunk, chunk), chunk)
        for half in range(halves):
            ls = [l_tok[g, half, rows, :] for g in range(n_groups)]
            top = functools.reduce(jnp.maximum, ls)
            es = [jnp.exp(l - top) for l in ls]
            acc = sum(e * o_tok[g, half, rows, :] for g, e in enumerate(es))
            o_ref[0, rows, half * LANES:(half + 1) * LANES] = (acc * (1.0 / sum(es))).astype(BF16)
        return carry

    lax.fori_loop(0, ATTN_TILE // chunk, merge, 0)


def _attn_call(qkv, batch, seq):
    n_groups = len(ATTN_GROUPS)
    in_specs, args, kbufs = [], [], []
    for group, (_, dilation) in enumerate(ATTN_GROUPS):
        length = ATTN_TILE // dilation
        cur = pl.BlockSpec((1, dilation, length, GROUP_WIDTH), lambda b, i: (b, 0, i, 0))
        halo = pl.BlockSpec(
            (1, dilation, BAND, GROUP_WIDTH),
            lambda b, i, nb=length // BAND: (b, 0, jnp.maximum(i * nb - 1, 0), 0))
        q, k, v = qkv[3 * group:3 * group + 3]
        in_specs += [cur, cur, cur, halo, halo]
        args += [q, k, v, k, v]
        kbufs.append(pltpu.VMEM((dilation, BAND + length, GROUP_WIDTH), BF16))
    halves = GROUP_WIDTH // LANES
    o = pl.pallas_call(
        functools.partial(_attn_kernel, n_groups=n_groups),
        grid=(batch, seq // ATTN_TILE),
        in_specs=in_specs,
        out_specs=pl.BlockSpec((1, ATTN_TILE, GROUP_WIDTH), lambda b, i: (b, i, 0)),
        out_shape=jax.ShapeDtypeStruct((batch, seq, GROUP_WIDTH), BF16),
        scratch_shapes=kbufs + kbufs + [
            pltpu.VMEM((n_groups, halves, ATTN_TILE, LANES), F32),
            pltpu.VMEM((n_groups, halves, ATTN_TILE, LANES), F32),
            pltpu.VMEM((ATTN_TILE, GROUP_WIDTH), F32),
            pltpu.VMEM((ATTN_TILE, GROUP_WIDTH), F32)],
        compiler_params=_params("parallel", "arbitrary"),
        name="band_attn",
    )(*args)
    return o.reshape(batch * seq, GROUP_WIDTH)


def _hgrn_kernel(x_ref, g_ref, w_ref, lb_ref, og_ref, wp_ref, tri_ref, y_ref,
                 state_ref, proj_ref, gated_ref, b_ref, *, tile, chunk):
    @pl.when(pl.program_id(1) == 0)
    def _():
        state_ref[...] = jnp.zeros_like(state_ref)

    h = _rms_rows(x_ref[0], g_ref[...]).astype(BF16)
    proj_ref[...] = _dot(h, w_ref[...])

    lb = lb_ref[...]
    lb_floor = jnp.maximum(lb, LB_FLOOR)
    one_minus_lb = 1.0 - lb
    floor_gap = lb_floor - lb
    row = lax.broadcasted_iota(jnp.int32, (chunk, HGRN_HEAD), 0)
    tt = lax.broadcasted_iota(jnp.int32, (chunk, chunk), 0)
    ss = lax.broadcasted_iota(jnp.int32, (chunk, chunk), 1)
    t_xor_s = tt ^ ss
    w = HGRN_WIDTH
    sizes = [1 << i for i in range(chunk.bit_length() - 1)]
    lower_rows = [(row & m) == 0 for m in sizes]
    lower_sel = [jnp.where(low, 1.0, 0.0).astype(BF16) for low in lower_rows]
    signed_log2e = [jnp.where(low, -LOG2E, LOG2E).astype(BF16) for low in lower_rows]
    level_pairs = [((t_xor_s & (-m)) == m) & ((tt & m) != 0) for m in sizes]
    diagonal = tt == ss

    def chunk_body(c, carry):
        r0 = pl.multiple_of(c * chunk, chunk)
        rows = pl.ds(r0, chunk)
        z = proj_ref[rows, w:2 * w]
        e = jnp.exp(-jnp.abs(z))
        inv = 1.0 / (1.0 + e)
        sig = jnp.where(z >= 0, inv, e * inv)
        nsig = jnp.where(z >= 0, e * inv, inv)
        log_f = jnp.log(lb_floor + one_minus_lb * sig)
        k_all = one_minus_lb * nsig - floor_gap
        lf_hi = log_f.astype(BF16)
        rest = log_f - lf_hi.astype(F32)
        lf_mid = rest.astype(BF16)
        lf_lo = (rest - lf_mid.astype(F32)).astype(BF16)
        tri = tri_ref[...]
        b_all = _dot(tri, lf_hi) + _dot(tri, lf_mid) + _dot(tri, lf_lo)
        b_ref[...] = b_all
        q_all = proj_ref[rows, 0:w].astype(BF16)
        q_all = q_all * _sigmoid(q_all)
        v_all = proj_ref[rows, 2 * w:3 * w]
        g_all = proj_ref[rows, 3 * w:4 * w].astype(BF16)
        g_all = g_all * _sigmoid(g_all)
        for hd in range(HGRN_HEADS):
            cols = slice(hd * HGRN_HEAD, (hd + 1) * HGRN_HEAD)
            b, qb = b_all[:, cols], q_all[:, cols]
            kb = k_all[:, cols].astype(BF16)
            vb = v_all[:, cols].astype(BF16)
            scores = jnp.where(diagonal, _dot_nt(qb, kb), 0.0)
            last_of_block = b
            for lvl, m in enumerate(sizes):
                if m < SUBLANES:
                    lower = lower_rows[lvl]
                    boundary = jnp.where(lower, last_of_block, pltpu.roll(last_of_block, m, 0))
                    if 2 * m < SUBLANES:
                        last_of_block = jnp.where(
                            lower, pltpu.roll(last_of_block, chunk - m, 0), last_of_block)
                else:
                    boundary = jnp.concatenate(
                        [jnp.broadcast_to(b_ref[r:r + 1, cols], (2 * m, HGRN_HEAD))
                         for r in range(m - 1, chunk, 2 * m)], axis=0)
                decay = jnp.exp2((b - boundary).astype(BF16) * signed_log2e[lvl])
                if m < 2 * SUBLANES:
                    source = jnp.where(lower_sel[lvl] > 0, kb, qb)
                else:
                    source = jnp.concatenate(
                        [(kb if (r // m) % 2 == 0 else qb)[r:r + m] for r in range(0, chunk, m)],
                        axis=0)
                side = source * decay
                scores = jnp.where(level_pairs[lvl], _dot_nt(side, side), scores)
            state_t = state_ref[hd]
            q_dec = qb * jnp.exp(b).astype(BF16)
            o = _dot(scores.astype(BF16), vb) + _dot_nt(q_dec, state_t.astype(BF16))
            b_last = b[chunk - 1:chunk, :]
            k_dec = kb * jnp.exp(b_last - b).astype(BF16)
            state_ref[hd] = state_t * jnp.exp(b_last) + _dot_tn(vb, k_dec)
            o = _rms_rows(o, og_ref[...]).astype(BF16) * g_all[:, cols]
            gated_ref[rows, cols] = o
        return carry

    lax.fori_loop(0, tile // chunk, chunk_body, 0, unroll=True)
    y_ref[0] = _dot(gated_ref[...], wp_ref[...]).astype(BF16)


def _hgrn_call(x, gain, w, lb, out_gain, w_proj, batch, seq):
    tile, chunk = TOKEN_TILE, HGRN_CHUNK
    tri = jnp.tril(jnp.ones((chunk, chunk), BF16))
    blk = lambda width: pl.BlockSpec((1, tile, width), lambda b, i: (b, i, 0))
    y = pl.pallas_call(
        functools.partial(_hgrn_kernel, tile=tile, chunk=chunk),
        grid=(batch, seq // tile),
        in_specs=[blk(D_MODEL), _full((1, D_MODEL)), _full(w.shape), _full((1, HGRN_WIDTH)),
                  _full((1, HGRN_HEAD)), _full(w_proj.shape), _full((chunk, chunk))],
        out_specs=blk(D_MODEL),
        out_shape=jax.ShapeDtypeStruct((batch, seq, D_MODEL), BF16),
        scratch_shapes=[pltpu.VMEM((HGRN_HEADS, HGRN_HEAD, HGRN_HEAD), F32),
                        pltpu.VMEM((tile, 4 * HGRN_WIDTH), F32),
                        pltpu.VMEM((tile, HGRN_WIDTH), BF16),
                        pltpu.VMEM((chunk, HGRN_WIDTH), F32)],
        compiler_params=_params("parallel", "arbitrary"),
        name="hgrn_branch",
    )(x.reshape(batch, seq, D_MODEL), gain, w, lb, out_gain, w_proj, tri)
    return y.reshape(batch * seq, D_MODEL)


def _merge_kernel(x_ref, g_ref, w_ref, cw_ref, oa_ref, yc_ref,
                  wpa_ref, wpb_ref, wo_ref, out_ref, u_ext, *, tile, tiles_per_seq):
    @pl.when(pl.program_id(0) % tiles_per_seq == 0)
    def _():
        u_ext[0:SUBLANES, :] = jnp.zeros((SUBLANES, CONV_WIDTH), F32)

    x = x_ref[...]
    h = _rms_rows(x, g_ref[...]).astype(BF16)
    p = _dot(h, w_ref[...])
    cw = CONV_WIDTH
    u = p[:, 2 * cw:3 * cw] * p[:, 0:cw]
    u_ext[SUBLANES:SUBLANES + tile, :] = u
    conv = (cw_ref[0:1, :] * u
            + cw_ref[1:2, :] * u_ext[SUBLANES - 1:SUBLANES - 1 + tile, :]
            + cw_ref[2:3, :] * u_ext[SUBLANES - 2:SUBLANES - 2 + tile, :])
    u_ext[0:SUBLANES, :] = u[tile - SUBLANES:, :]
    yb = _dot((p[:, cw:2 * cw] * conv).astype(BF16), wpb_ref[...])
    ya = _dot(oa_ref[...], wpa_ref[...])

    d = D_MODEL
    gates = p[:, 3 * cw:]
    merged = (_sigmoid(gates[:, 0:d]) * ya + _sigmoid(gates[:, d:2 * d]) * yb
              + _sigmoid(gates[:, 2 * d:3 * d]) * yc_ref[...].astype(F32))
    out_ref[...] = x + _dot(merged.astype(BF16), wo_ref[...])


def _merge_call(x, gain, w, conv_w, oa, yc, wpa, wpb, wo, seq):
    n = x.shape[0]
    tile = TOKEN_TILE
    row = lambda width: pl.BlockSpec((tile, width), lambda i: (i, 0))
    return pl.pallas_call(
        functools.partial(_merge_kernel, tile=tile, tiles_per_seq=seq // tile),
        grid=(n // tile,),
        in_specs=[row(D_MODEL), _full((1, D_MODEL)), _full(w.shape), _full(conv_w.shape),
                  row(GROUP_WIDTH), row(D_MODEL), _full(wpa.shape), _full(wpb.shape),
                  _full(wo.shape)],
        out_specs=row(D_MODEL),
        out_shape=jax.ShapeDtypeStruct((n, D_MODEL), F32),
        scratch_shapes=[pltpu.VMEM((SUBLANES + tile, CONV_WIDTH), F32)],
        compiler_params=_params("arbitrary"),
        name="merge_out",
    )(x, gain, w, conv_w, oa, yc, wpa, wpb, wo)


def _ffn_kernel(x_ref, g_ref, w1_ref, w3_ref, w2_ref, out_ref):
    x = x_ref[...]
    h = _rms_rows(x, g_ref[...]).astype(BF16)
    a = _dot(h, w1_ref[...])
    z = (a * _sigmoid(a) * _dot(h, w3_ref[...])).astype(BF16)
    out_ref[...] = x + _dot(z, w2_ref[...])


def _ffn_call(x, gain, w1, w3, w2):
    n = x.shape[0]
    tile = TOKEN_TILE
    row = pl.BlockSpec((tile, D_MODEL), lambda i: (i, 0))
    return pl.pallas_call(
        _ffn_kernel,
        grid=(n // tile,),
        in_specs=[row, _full((1, D_MODEL)), _full(w1.shape), _full(w3.shape), _full(w2.shape)],
        out_specs=row,
        out_shape=jax.ShapeDtypeStruct((n, D_MODEL), F32),
        compiler_params=_params("parallel"),
        name="dense_ffn",
    )(x, gain, w1, w3, w2)


def _router_kernel(x_ref, g_ref, rw_hi_ref, rw_lo_ref, route_ref):
    h = _rms_rows(x_ref[...], g_ref[...]).astype(BF16)
    logits = _dot(h, rw_hi_ref[...]) + _dot(h, rw_lo_ref[...])
    lane = lax.broadcasted_iota(jnp.int32, logits.shape, 1)
    logits = jnp.where(lane < N_EXPERTS, logits, -jnp.inf)
    m1 = jnp.max(logits, axis=-1, keepdims=True)
    i1 = jnp.min(jnp.where(logits == m1, lane, LANES), axis=-1, keepdims=True)
    rest = jnp.where(lane == i1, -jnp.inf, logits)
    m2 = jnp.max(rest, axis=-1, keepdims=True)
    i2 = jnp.min(jnp.where(rest == m2, lane, LANES), axis=-1, keepdims=True)
    e = jnp.exp(m2 - m1)
    g1 = 1.0 / (1.0 + e)
    g2 = e * g1
    route_ref[...] = jnp.where(
        lane == 0, g1, jnp.where(lane == 1, g2, jnp.where(
            lane == 2, i1.astype(F32), jnp.where(lane == 3, i2.astype(F32), 0.0))))


def _router_call(x, gain, rw_hi, rw_lo):
    n = x.shape[0]
    tile = TOKEN_TILE
    return pl.pallas_call(
        _router_kernel,
        grid=(n // tile,),
        in_specs=[pl.BlockSpec((tile, D_MODEL), lambda i: (i, 0)), _full((1, D_MODEL)),
                  _full(rw_hi.shape), _full(rw_lo.shape)],
        out_specs=pl.BlockSpec((tile, LANES), lambda i: (i, 0)),
        out_shape=jax.ShapeDtypeStruct((n, LANES), F32),
        compiler_params=_params("parallel"),
        name="router",
    )(x, gain, rw_hi, rw_lo)


def _row_copy(src, dst, src_row, dst_row, sem):
    return pltpu.make_async_copy(src.at[pl.ds(src_row, 1), :], dst.at[pl.ds(dst_row, 1), :], sem)


def _dispatch_kernel(pad_ref, pos_ref, x_ref, out_hbm, zeros, sem, pad_sem, *, tile):
    @pl.when(pl.program_id(0) == 0)
    def _():
        zeros[...] = jnp.zeros_like(zeros)
        for e in range(N_EXPERTS):
            first, count = pad_ref[e], pad_ref[N_EXPERTS + e]

            def fill(j, carry, first=first):
                _row_copy(zeros, out_hbm, 0, first + j, pad_sem).start()
                return carry

            def drain(j, carry, first=first):
                _row_copy(zeros, out_hbm, 0, first + j, pad_sem).wait()
                return carry

            lax.fori_loop(0, count, fill, 0)
            lax.fori_loop(0, count, drain, 0)

        def tail_copy(k):
            start = pl.multiple_of(pad_ref[2 * N_EXPERTS] + k * tile, tile)
            return pltpu.make_async_copy(zeros, out_hbm.at[pl.ds(start, tile), :], pad_sem)

        def tail_fill(k, carry):
            tail_copy(k).start()
            return carry

        def tail_drain(k, carry):
            tail_copy(k).wait()
            return carry

        lax.fori_loop(0, pad_ref[2 * N_EXPERTS + 1], tail_fill, 0)
        lax.fori_loop(0, pad_ref[2 * N_EXPERTS + 1], tail_drain, 0)

    def issue(j, carry):
        _row_copy(x_ref, out_hbm, j, pos_ref[j], sem).start()
        _row_copy(x_ref, out_hbm, j, pos_ref[tile + j], sem).start()
        return carry

    lax.fori_loop(0, tile, issue, 0, unroll=8)
    for _ in range(2):
        pltpu.make_async_copy(x_ref, out_hbm.at[pl.ds(0, tile), :], sem).wait()


def _dispatch_call(x, pos_tiles, pad_info, n_sorted):
    n = x.shape[0]
    tile = TOKEN_TILE
    grid_spec = pltpu.PrefetchScalarGridSpec(
        num_scalar_prefetch=1,
        grid=(n // tile,),
        in_specs=[pl.BlockSpec((2 * tile,), lambda i, pad: (i,), memory_space=pltpu.SMEM),
                  pl.BlockSpec((tile, D_MODEL), lambda i, pad: (i, 0))],
        out_specs=pl.BlockSpec(memory_space=pl.ANY),
        scratch_shapes=[pltpu.VMEM((tile, D_MODEL), F32), pltpu.SemaphoreType.DMA(()),
                        pltpu.SemaphoreType.DMA(())],
    )
    return pl.pallas_call(
        functools.partial(_dispatch_kernel, tile=tile),
        grid_spec=grid_spec,
        out_shape=jax.ShapeDtypeStruct((n_sorted, D_MODEL), F32),
        compiler_params=_params("arbitrary"),
        name="moe_dispatch",
    )(pad_info, pos_tiles, x)


def _moe_kernel(tile_expert_ref, n_tiles_ref, x_ref, g_ref, w1_ref, w3_ref, w2_ref, y_ref,
                h_ref, acc_ref):
    i, f = pl.program_id(0), pl.program_id(1)
    last_f = pl.num_programs(1) - 1
    active = i < n_tiles_ref[0]

    @pl.when(active & (f == 0))
    def _():
        h_ref[...] = _rms_rows(x_ref[...], g_ref[...]).astype(BF16)

    @pl.when(active)
    def _():
        h = h_ref[...]
        a = _dot(h, w1_ref[...])
        z = (a * _sigmoid(a) * _dot(h, w3_ref[...])).astype(BF16)
        part = _dot(z, w2_ref[...])

        @pl.when(f == 0)
        def _():
            acc_ref[...] = part

        @pl.when(f > 0)
        def _():
            acc_ref[...] += part

    @pl.when(active & (f == last_f))
    def _():
        y_ref[...] = acc_ref[...]

    @pl.when(jnp.logical_not(active) & (f == last_f))
    def _():
        y_ref[...] = jnp.zeros_like(y_ref)


def _moe_call(xs, gain, w1, w3, w2, tile_expert, n_tiles):
    p = xs.shape[0]
    d_ff = w1.shape[-1]
    tm, tf = MOE_ROW_TILE, MOE_FF_TILE
    nf = d_ff // tf

    def ff_block(i, f, n_tiles_ref):
        return jnp.where(i < n_tiles_ref[0], f, nf - 1)

    grid_spec = pltpu.PrefetchScalarGridSpec(
        num_scalar_prefetch=2,
        grid=(p // tm, nf),
        in_specs=[
            pl.BlockSpec((tm, D_MODEL), lambda i, f, te, nt: (jnp.minimum(i, nt[0] - 1), 0)),
            pl.BlockSpec((1, D_MODEL), lambda i, f, te, nt: (0, 0)),
            pl.BlockSpec((None, D_MODEL, tf), lambda i, f, te, nt: (te[i], 0, ff_block(i, f, nt))),
            pl.BlockSpec((None, D_MODEL, tf), lambda i, f, te, nt: (te[i], 0, ff_block(i, f, nt))),
            pl.BlockSpec((None, tf, D_MODEL), lambda i, f, te, nt: (te[i], ff_block(i, f, nt), 0)),
        ],
        out_specs=pl.BlockSpec((tm, D_MODEL), lambda i, f, te, nt: (i, 0)),
        scratch_shapes=[pltpu.VMEM((tm, D_MODEL), BF16), pltpu.VMEM((tm, D_MODEL), F32)],
    )
    return pl.pallas_call(
        _moe_kernel,
        grid_spec=grid_spec,
        out_shape=jax.ShapeDtypeStruct((p, D_MODEL), F32),
        compiler_params=_params("arbitrary", "arbitrary"),
        name="moe_experts",
    )(tile_expert, n_tiles, xs, gain, w1, w3, w2)


def _combine_kernel(pos_ref, x_ref, route_ref, y_hbm, out_ref, buf, sems, *, tile):
    def issue(j, carry):
        _row_copy(y_hbm, buf.at[0], pos_ref[j], j, sems.at[0]).start()
        _row_copy(y_hbm, buf.at[1], pos_ref[tile + j], j, sems.at[1]).start()
        return carry

    lax.fori_loop(0, tile, issue, 0, unroll=8)
    for k in range(2):
        pltpu.make_async_copy(y_hbm.at[pl.ds(0, tile), :], buf.at[k], sems.at[k]).wait()
    g1 = route_ref[:, 0:1]
    g2 = route_ref[:, 1:2]
    out_ref[...] = x_ref[...] + g1 * buf[0] + g2 * buf[1]


def _combine_call(x, route, y, pos_tiles):
    n = x.shape[0]
    tile = TOKEN_TILE
    row = lambda width: pl.BlockSpec((tile, width), lambda i: (i, 0))
    return pl.pallas_call(
        functools.partial(_combine_kernel, tile=tile),
        grid=(n // tile,),
        in_specs=[pl.BlockSpec((2 * tile,), lambda i: (i,), memory_space=pltpu.SMEM),
                  row(D_MODEL), row(LANES), pl.BlockSpec(memory_space=pl.ANY)],
        out_specs=row(D_MODEL),
        out_shape=jax.ShapeDtypeStruct(x.shape, F32),
        scratch_shapes=[pltpu.VMEM((2, tile, D_MODEL), F32), pltpu.SemaphoreType.DMA((2,))],
        compiler_params=_params("arbitrary"),
        name="moe_combine",
    )(pos_tiles, x, route, y)


def _routing_tables(route, n):
    tm = MOE_ROW_TILE
    p = 2 * n + N_EXPERTS * tm
    expert = jnp.concatenate([route[:, 2], route[:, 3]]).astype(jnp.int32)
    onehot = (expert[:, None] == jnp.arange(N_EXPERTS, dtype=jnp.int32)[None, :]).astype(jnp.int32)
    running = jnp.cumsum(onehot, axis=0)
    rank = jnp.sum(running * onehot, axis=1) - 1
    counts = running[-1]
    padded = ((counts + tm - 1) // tm) * tm
    ends = jnp.cumsum(padded)
    starts = ends - padded
    pos = jnp.sum(onehot * starts[None, :], axis=1) + rank
    n_tiles = (ends[-1] // tm).astype(jnp.int32)
    tile_start = jnp.arange(p // tm, dtype=jnp.int32) * tm
    tile_expert = jnp.sum((tile_start[:, None] >= ends[None, :]).astype(jnp.int32), axis=1)
    last_expert = jnp.sum((ends[-1] - 1 >= ends).astype(jnp.int32))
    tile_expert = jnp.minimum(tile_expert, last_expert).astype(jnp.int32)
    tile = TOKEN_TILE
    tail = jnp.stack([ends[-1], (p - ends[-1]) // tile])
    pad_info = jnp.concatenate([starts + counts, padded - counts, tail]).astype(jnp.int32)
    pos_tiles = jnp.concatenate(
        [pos[:n].reshape(n // tile, tile), pos[n:].reshape(n // tile, tile)], axis=1).reshape(-1)
    return pos_tiles.astype(jnp.int32), pad_info, tile_expert, n_tiles.reshape(1), p


def _moe_layer(x, gain, router_w, w1, w3, w2):
    n = x.shape[0]
    rw = jnp.pad(router_w, ((0, 0), (0, LANES - N_EXPERTS)))
    rw_hi = rw.astype(BF16)
    rw_lo = (rw - rw_hi.astype(F32)).astype(BF16)
    route = _router_call(x, gain, rw_hi, rw_lo)
    pos_tiles, pad_info, tile_expert, n_tiles, n_sorted = _routing_tables(route, n)
    xs = _dispatch_call(x, pos_tiles, pad_info, n_sorted)
    ys = _moe_call(xs, gain, w1.astype(BF16), w3.astype(BF16), w2.astype(BF16), tile_expert, n_tiles)
    return _combine_call(x, route, ys, pos_tiles)


def kernel(x, attn_norm, ffn_norm, w_in, q_norm, k_norm, conv_w, hgrn_lower_bounds,
           hgrn_out_norm, w_proj_a, w_proj_b, w_proj_c, w_out, dense_w1, dense_w3,
           dense_w2, router_w, moe_w1, moe_w3, moe_w2):
    batch, seq, d = x.shape
    n = batch * seq
    depth = w_in.shape[0]
    assert d == D_MODEL and seq % ATTN_TILE == 0 and seq % TOKEN_TILE == 0
    assert all(window // dilation == BAND for window, dilation in ATTN_GROUPS)

    lbs = jax.nn.softmax(hgrn_lower_bounds.astype(F32), axis=0)
    lower_bound = jnp.cumsum(lbs, axis=0) - lbs[0]
    head_of_col = jnp.arange(ATTN_WIDTH) // HEAD_DIM
    hsum = (head_of_col[:, None] == jnp.arange(LANES)[None, :]).astype(BF16)
    hexp = hsum.T
    a0, a1 = 0, 3 * ATTN_WIDTH
    b1 = a1 + 3 * CONV_WIDTH
    c1 = b1 + 4 * HGRN_WIDTH

    xf = x.reshape(n, d)
    for layer in range(depth):
        w = w_in[layer]
        w_attn = w[:, a0:a1].astype(BF16)
        w_hgrn = w[:, b1:c1].astype(BF16)
        w_conv_gate = jnp.concatenate([w[:, a1:b1], w[:, c1:]], axis=1).astype(BF16)
        gain = attn_norm[layer].reshape(1, d)
        heads = ATTN_WIDTH // HEAD_DIM
        qg = jnp.tile(q_norm[layer] * (HEAD_DIM ** -0.5), heads).reshape(1, ATTN_WIDTH)
        kg = jnp.tile(k_norm[layer], heads).reshape(1, ATTN_WIDTH)

        qkv = _qkv_call(xf, gain, w_attn, qg, kg, hsum, hexp, batch, seq)
        oa = _attn_call(qkv, batch, seq)
        yc = _hgrn_call(xf, gain, w_hgrn, lower_bound[layer].reshape(1, HGRN_WIDTH),
                        hgrn_out_norm[layer].reshape(1, HGRN_HEAD),
                        w_proj_c[layer].astype(BF16), batch, seq)
        xf = _merge_call(xf, gain, w_conv_gate, conv_w[layer], oa, yc,
                         w_proj_a[layer].astype(BF16), w_proj_b[layer].astype(BF16),
                         w_out[layer].astype(BF16), seq)

        fgain = ffn_norm[layer].reshape(1, d)
        j = layer // 2
        if layer % 2 == 0:
            xf = _ffn_call(xf, fgain, dense_w1[j].astype(BF16), dense_w3[j].astype(BF16),
                           dense_w2[j].astype(BF16))
        else:
            xf = _moe_layer(xf, fgain, router_w[j], moe_w1[j], moe_w3[j], moe_w2[j])
    return xf.reshape(batch, seq, d)
```

```python
import functools

import jax
import jax.numpy as jnp
from jax import lax
from jax.experimental import pallas as pl
from jax.experimental.pallas import tpu as pltpu

F32 = jnp.float32
BF16 = jnp.bfloat16

D_MODEL = 1024
HEAD_DIM = 64
ATTN_GROUPS = ((128, 1), (512, 4), (2048, 16))
HEADS_PER_GROUP = 4
GROUP_WIDTH = HEADS_PER_GROUP * HEAD_DIM
ATTN_WIDTH = GROUP_WIDTH * len(ATTN_GROUPS)
BAND = 128
CONV_WIDTH = 768
CONV_K = 3
HGRN_WIDTH = 768
HGRN_HEAD = 128
HGRN_HEADS = HGRN_WIDTH // HGRN_HEAD
LB_FLOOR = 1e-30
N_EXPERTS = 8
RMS_EPS = 1e-6
NEG_BIG = -1e30
LOG2E = 1.4426950408889634

LANES = 128
SUBLANES = 8

TOKEN_TILE = 512
ATTN_TILE = ATTN_GROUPS[-1][1] * BAND
ATTN_UNROLL = 8
HGRN_CHUNK = 256
MOE_ROW_TILE = 512
MOE_FF_TILE = 1792
VMEM_LIMIT = 56 * 1024 * 1024


def _dot(a, b):
    return jnp.dot(a, b, preferred_element_type=F32)


def _dot_nt(a, b):
    return lax.dot_general(a, b, (((1,), (1,)), ((), ())), preferred_element_type=F32)


def _dot_tn(a, b):
    return lax.dot_general(a, b, (((0,), (0,)), ((), ())), preferred_element_type=F32)


def _rms_rows(xf, gain):
    ms = jnp.mean(xf * xf, axis=-1, keepdims=True)
    return xf * lax.rsqrt(ms + RMS_EPS) * gain


def _sigmoid(z):
    return 1.0 / (1.0 + jnp.exp(-z))


def _params(*sem):
    return pltpu.CompilerParams(dimension_semantics=sem, vmem_limit_bytes=VMEM_LIMIT)


def _full(shape):
    return pl.BlockSpec(shape, lambda *_: (0,) * len(shape))


def _qkv_kernel(x_ref, g_ref, w_ref, qg_ref, kg_ref, hsum_ref, hexp_ref, *rest, tile):
    out_refs, stage = rest[:9], rest[9]
    h = _rms_rows(x_ref[0], g_ref[...]).astype(BF16)
    p = _dot(h, w_ref[...])

    def head_norm(t, gain):
        ms = _dot((t * t).astype(BF16), hsum_ref[...]) * (1.0 / HEAD_DIM)
        r = lax.rsqrt(ms + RMS_EPS)
        r_hi = r.astype(BF16)
        r_lo = (r - r_hi.astype(F32)).astype(BF16)
        r_cols = _dot(jnp.concatenate([r_hi, r_lo], axis=1), hexp_ref[...])
        return t * r_cols * gain

    parts = (head_norm(p[:, :ATTN_WIDTH], qg_ref[...]),
             head_norm(p[:, ATTN_WIDTH:2 * ATTN_WIDTH], kg_ref[...]),
             p[:, 2 * ATTN_WIDTH:])
    slot = 0
    for a, part in enumerate(parts):
        for group, (_, dilation) in enumerate(ATTN_GROUPS):
            out = out_refs[3 * group + a]
            c0 = group * GROUP_WIDTH
            if dilation == 1:
                out[0, 0] = part[:, c0:c0 + GROUP_WIDTH].astype(BF16)
                continue
            rows = tile // dilation
            for half in range(GROUP_WIDTH // LANES):
                stage[slot] = part[:, c0 + half * LANES:c0 + (half + 1) * LANES]
                for r in range(dilation):
                    out[0, r, :, half * LANES:(half + 1) * LANES] = (
                        stage[slot, pl.ds(r, rows, stride=dilation), :].astype(BF16))
                slot += 1


def _qkv_call(x, gain, w, qg, kg, hsum, hexp, batch, seq):
    t = TOKEN_TILE
    out_specs, out_shape = [], []
    for _, dilation in ATTN_GROUPS:
        for _ in range(3):
            out_specs.append(pl.BlockSpec((1, dilation, t // dilation, GROUP_WIDTH),
                                          lambda b, i: (b, 0, i, 0)))
            out_shape.append(jax.ShapeDtypeStruct(
                (batch, dilation, seq // dilation, GROUP_WIDTH), BF16))
    n_stage = 3 * (GROUP_WIDTH // LANES) * sum(1 for _, d in ATTN_GROUPS if d > 1)
    return pl.pallas_call(
        functools.partial(_qkv_kernel, tile=t),
        grid=(batch, seq // t),
        in_specs=[pl.BlockSpec((1, t, D_MODEL), lambda b, i: (b, i, 0)), _full((1, D_MODEL)),
                  _full(w.shape), _full((1, ATTN_WIDTH)), _full((1, ATTN_WIDTH)),
                  _full(hsum.shape), _full(hexp.shape)],
        out_specs=out_specs,
        out_shape=out_shape,
        scratch_shapes=[pltpu.VMEM((n_stage, t, LANES), F32)],
        compiler_params=_params("parallel", "parallel"),
        name="qkv_proj",
    )(x.reshape(batch, seq, D_MODEL), gain, w, qg, kg, hsum, hexp)


def _attn_kernel(*refs, n_groups):
    ins, o_ref, scr = refs[:5 * n_groups], refs[5 * n_groups], refs[5 * n_groups + 1:]
    kbufs, vbufs = scr[0:n_groups], scr[n_groups:2 * n_groups]
    o_tok, l_tok, o_rm, l_rm = scr[2 * n_groups:]
    first_tile = pl.program_id(1) == 0
    qi = lax.broadcasted_iota(jnp.int32, (BAND, 2 * BAND), 0)
    kj = lax.broadcasted_iota(jnp.int32, (BAND, 2 * BAND), 1)
    rel = qi + BAND - kj
    in_window = (rel >= 0) & (rel <= BAND)
    halves = GROUP_WIDTH // LANES
    lane = lax.broadcasted_iota(jnp.int32, (1, LANES), 1)
    head_lanes = [jnp.where(lane // HEAD_DIM == side, 1.0, 0.0).astype(BF16)
                  for side in range(LANES // HEAD_DIM)]
    first_head = lax.broadcasted_iota(jnp.int32, (BAND, LANES), 1) < HEAD_DIM

    for group, (_, dilation) in enumerate(ATTN_GROUPS):
        q_ref, kc_ref, vc_ref, kp_ref, vp_ref = ins[5 * group:5 * group + 5]
        kbuf, vbuf = kbufs[group], vbufs[group]
        length = ATTN_TILE // dilation
        nj = length // BAND
        kbuf[:, 0:BAND, :] = kp_ref[0]
        kbuf[:, BAND:BAND + length, :] = kc_ref[0]
        vbuf[:, 0:BAND, :] = vp_ref[0]
        vbuf[:, BAND:BAND + length, :] = vc_ref[0]

        def block(idx, carry, q_ref=q_ref, kbuf=kbuf, vbuf=vbuf, nj=nj, length=length,
                  group=group, dilation=dilation):
            r, j = idx // nj, idx % nj
            j0 = pl.multiple_of(j * BAND, BAND)
            kmin = jnp.where(first_tile & (j == 0), BAND, 0)
            valid = in_window & (kj >= kmin)
            row0 = pl.multiple_of(r * length + j0, BAND)
            for half in range(halves):
                lanes = slice(half * LANES, (half + 1) * LANES)
                q2 = q_ref[0, r, pl.ds(j0, BAND), lanes]
                k2 = kbuf[r, pl.ds(j0, 2 * BAND), lanes]
                v2 = vbuf[r, pl.ds(j0, 2 * BAND), lanes]
                o2, l2 = None, None
                for side in range(LANES // HEAD_DIM):
                    s = _dot_nt(q2 * head_lanes[side], k2)
                    s = jnp.where(valid, s, NEG_BIG)
                    m = jnp.max(s, axis=-1, keepdims=True)
                    p = jnp.exp(s - m)
                    den = jnp.sum(p, axis=-1, keepdims=True)
                    o = _dot(p.astype(BF16), v2) * (1.0 / den)
                    lse = jnp.broadcast_to(m + jnp.log(den), (BAND, LANES))
                    o2 = o if side == 0 else jnp.where(first_head, o2, o)
                    l2 = lse if side == 0 else jnp.where(first_head, l2, lse)
                if dilation == 1:
                    o_tok[group, half, pl.ds(row0, BAND), :] = o2
                    l_tok[group, half, pl.ds(row0, BAND), :] = l2
                else:
                    o_rm[pl.ds(row0, BAND), lanes] = o2
                    l_rm[pl.ds(row0, BAND), lanes] = l2
            return carry

        lax.fori_loop(0, ATTN_TILE // BAND, block, 0, unroll=ATTN_UNROLL)
        if dilation > 1:
            for r in range(dilation):
                rows = slice(r * length, (r + 1) * length)
                for half in range(halves):
                    lanes = slice(half * LANES, (half + 1) * LANES)
                    o_tok[group, half, pl.ds(r, length, stride=dilation), :] = o_rm[rows, lanes]
                    l_tok[group, half, pl.ds(r, length, stride=dilation), :] = l_rm[rows, lanes]

    chunk = 2 * BAND

    def merge(c, carry):
        rows = pl.ds(pl.multiple_of(c * chunk, chunk), chunk)
        for half in range(halves):
            ls = [l_tok[g, half, rows, :] for g in range(n_groups)]
            top = functools.reduce(jnp.maximum, ls)
            es = [jnp.exp(l - top) for l in ls]
            acc = sum(e * o_tok[g, half, rows, :] for g, e in enumerate(es))
            o_ref[0, rows, half * LANES:(half + 1) * LANES] = (acc * (1.0 / sum(es))).astype(BF16)
        return carry

    lax.fori_loop(0, ATTN_TILE // chunk, merge, 0)


def _attn_call(qkv, batch, seq):
    n_groups = len(ATTN_GROUPS)
    in_specs, args, kbufs = [], [], []
    for group, (_, dilation) in enumerate(ATTN_GROUPS):
        length = ATTN_TILE // dilation
        cur = pl.BlockSpec((1, dilation, length, GROUP_WIDTH), lambda b, i: (b, 0, i, 0))
        halo = pl.BlockSpec(
            (1, dilation, BAND, GROUP_WIDTH),
            lambda b, i, nb=length // BAND: (b, 0, jnp.maximum(i * nb - 1, 0), 0))
        q, k, v = qkv[3 * group:3 * group + 3]
        in_specs += [cur, cur, cur, halo, halo]
        args += [q, k, v, k, v]
        kbufs.append(pltpu.VMEM((dilation, BAND + length, GROUP_WIDTH), BF16))
    halves = GROUP_WIDTH // LANES
    o = pl.pallas_call(
        functools.partial(_attn_kernel, n_groups=n_groups),
        grid=(batch, seq // ATTN_TILE),
        in_specs=in_specs,
        out_specs=pl.BlockSpec((1, ATTN_TILE, GROUP_WIDTH), lambda b, i: (b, i, 0)),
        out_shape=jax.ShapeDtypeStruct((batch, seq, GROUP_WIDTH), BF16),
        scratch_shapes=kbufs + kbufs + [
            pltpu.VMEM((n_groups, halves, ATTN_TILE, LANES), F32),
            pltpu.VMEM((n_groups, halves, ATTN_TILE, LANES), F32),
            pltpu.VMEM((ATTN_TILE, GROUP_WIDTH), F32),
            pltpu.VMEM((ATTN_TILE, GROUP_WIDTH), F32)],
        compiler_params=_params("parallel", "arbitrary"),
        name="band_attn",
    )(*args)
    return o.reshape(batch * seq, GROUP_WIDTH)


def _hgrn_kernel(x_ref, g_ref, w_ref, lb_ref, og_ref, wp_ref, tri_ref, y_ref,
                 state_ref, proj_ref, gated_ref, b_ref, *, tile, chunk):
    @pl.when(pl.program_id(1) == 0)
    def _():
        state_ref[...] = jnp.zeros_like(state_ref)

    h = _rms_rows(x_ref[0], g_ref[...]).astype(BF16)
    proj_ref[...] = _dot(h, w_ref[...])

    lb = lb_ref[...]
    lb_floor = jnp.maximum(lb, LB_FLOOR)
    one_minus_lb = 1.0 - lb
    floor_gap = lb_floor - lb
    row = lax.broadcasted_iota(jnp.int32, (chunk, HGRN_HEAD), 0)
    tt = lax.broadcasted_iota(jnp.int32, (chunk, chunk), 0)
    ss = lax.broadcasted_iota(jnp.int32, (chunk, chunk), 1)
    t_xor_s = tt ^ ss
    w = HGRN_WIDTH
    sizes = [1 << i for i in range(chunk.bit_length() - 1)]
    lower_rows = [(row & m) == 0 for m in sizes]
    lower_sel = [jnp.where(low, 1.0, 0.0).astype(BF16) for low in lower_rows]
    signed_log2e = [jnp.where(low, -LOG2E, LOG2E).astype(BF16) for low in lower_rows]
    level_pairs = [((t_xor_s & (-m)) == m) & ((tt & m) != 0) for m in sizes]
    diagonal = tt == ss

    def chunk_body(c, carry):
        r0 = pl.multiple_of(c * chunk, chunk)
        rows = pl.ds(r0, chunk)
        z = proj_ref[rows, w:2 * w]
        e = jnp.exp(-jnp.abs(z))
        inv = 1.0 / (1.0 + e)
        sig = jnp.where(z >= 0, inv, e * inv)
        nsig = jnp.where(z >= 0, e * inv, inv)
        log_f = jnp.log(lb_floor + one_minus_lb * sig)
        k_all = one_minus_lb * nsig - floor_gap
        lf_hi = log_f.astype(BF16)
        rest = log_f - lf_hi.astype(F32)
        lf_mid = rest.astype(BF16)
        lf_lo = (rest - lf_mid.astype(F32)).astype(BF16)
        tri = tri_ref[...]
        b_all = _dot(tri, lf_hi) + _dot(tri, lf_mid) + _dot(tri, lf_lo)
        b_ref[...] = b_all
        q_all = proj_ref[rows, 0:w].astype(BF16)
        q_all = q_all * _sigmoid(q_all)
        v_all = proj_ref[rows, 2 * w:3 * w]
        g_all = proj_ref[rows, 3 * w:4 * w].astype(BF16)
        g_all = g_all * _sigmoid(g_all)
        for hd in range(HGRN_HEADS):
            cols = slice(hd * HGRN_HEAD, (hd + 1) * HGRN_HEAD)
            b, qb = b_all[:, cols], q_all[:, cols]
            kb = k_all[:, cols].astype(BF16)
            vb = v_all[:, cols].astype(BF16)
            scores = jnp.where(diagonal, _dot_nt(qb, kb), 0.0)
            last_of_block = b
            for lvl, m in enumerate(sizes):
                if m < SUBLANES:
                    lower = lower_rows[lvl]
                    boundary = jnp.where(lower, last_of_block, pltpu.roll(last_of_block, m, 0))
                    if 2 * m < SUBLANES:
                        last_of_block = jnp.where(
                            lower, pltpu.roll(last_of_block, chunk - m, 0), last_of_block)
                else:
                    boundary = jnp.concatenate(
                        [jnp.broadcast_to(b_ref[r:r + 1, cols], (2 * m, HGRN_HEAD))
                         for r in range(m - 1, chunk, 2 * m)], axis=0)
                decay = jnp.exp2((b - boundary).astype(BF16) * signed_log2e[lvl])
                if m < 2 * SUBLANES:
                    source = jnp.where(lower_sel[lvl] > 0, kb, qb)
                else:
                    source = jnp.concatenate(
                        [(kb if (r // m) % 2 == 0 else qb)[r:r + m] for r in range(0, chunk, m)],
                        axis=0)
                side = source * decay
                scores = jnp.where(level_pairs[lvl], _dot_nt(side, side), scores)
            state_t = state_ref[hd]
            q_dec = qb * jnp.exp(b).astype(BF16)
            o = _dot(scores.astype(BF16), vb) + _dot_nt(q_dec, state_t.astype(BF16))
            b_last = b[chunk - 1:chunk, :]
            k_dec = kb * jnp.exp(b_last - b).astype(BF16)
            state_ref[hd] = state_t * jnp.exp(b_last) + _dot_tn(vb, k_dec)
            o = _rms_rows(o, og_ref[...]).astype(BF16) * g_all[:, cols]
            gated_ref[rows, cols] = o
        return carry

    lax.fori_loop(0, tile // chunk, chunk_body, 0, unroll=True)
    y_ref[0] = _dot(gated_ref[...], wp_ref[...]).astype(BF16)


def _hgrn_call(x, gain, w, lb, out_gain, w_proj, batch, seq):
    tile, chunk = TOKEN_TILE, HGRN_CHUNK
    tri = jnp.tril(jnp.ones((chunk, chunk), BF16))
    blk = lambda width: pl.BlockSpec((1, tile, width), lambda b, i: (b, i, 0))
    y = pl.pallas_call(
        functools.partial(_hgrn_kernel, tile=tile, chunk=chunk),
        grid=(batch, seq // tile),
        in_specs=[blk(D_MODEL), _full((1, D_MODEL)), _full(w.shape), _full((1, HGRN_WIDTH)),
                  _full((1, HGRN_HEAD)), _full(w_proj.shape), _full((chunk, chunk))],
        out_specs=blk(D_MODEL),
        out_shape=jax.ShapeDtypeStruct((batch, seq, D_MODEL), BF16),
        scratch_shapes=[pltpu.VMEM((HGRN_HEADS, HGRN_HEAD, HGRN_HEAD), F32),
                        pltpu.VMEM((tile, 4 * HGRN_WIDTH), F32),
                        pltpu.VMEM((tile, HGRN_WIDTH), BF16),
                        pltpu.VMEM((chunk, HGRN_WIDTH), F32)],
        compiler_params=_params("parallel", "arbitrary"),
        name="hgrn_branch",
    )(x.reshape(batch, seq, D_MODEL), gain, w, lb, out_gain, w_proj, tri)
    return y.reshape(batch * seq, D_MODEL)


def _top2_route(h, rw_hi, rw_lo):
    logits = _dot(h, rw_hi) + _dot(h, rw_lo)
    lane = lax.broadcasted_iota(jnp.int32, logits.shape, 1)
    logits = jnp.where(lane < N_EXPERTS, logits, -jnp.inf)
    m1 = jnp.max(logits, axis=-1, keepdims=True)
    i1 = jnp.min(jnp.where(logits == m1, lane, LANES), axis=-1, keepdims=True)
    rest = jnp.where(lane == i1, -jnp.inf, logits)
    m2 = jnp.max(rest, axis=-1, keepdims=True)
    i2 = jnp.min(jnp.where(rest == m2, lane, LANES), axis=-1, keepdims=True)
    e = jnp.exp(m2 - m1)
    g1 = 1.0 / (1.0 + e)
    g2 = e * g1
    return jnp.where(
        lane == 0, g1, jnp.where(lane == 1, g2, jnp.where(
            lane == 2, i1.astype(F32), jnp.where(lane == 3, i2.astype(F32), 0.0))))


def _merge_kernel(x_ref, g_ref, w_ref, cw_ref, oa_ref, yc_ref, wpa_ref, wpb_ref, wo_ref, *rest,
                  tile, tiles_per_seq, with_router):
    if with_router:
        fg_ref, rw_hi_ref, rw_lo_ref, out_ref, hn_ref, route_ref, u_ext = rest
    else:
        out_ref, u_ext = rest

    @pl.when(pl.program_id(0) % tiles_per_seq == 0)
    def _():
        u_ext[0:SUBLANES, :] = jnp.zeros((SUBLANES, CONV_WIDTH), F32)

    x = x_ref[...]
    h = _rms_rows(x, g_ref[...]).astype(BF16)
    p = _dot(h, w_ref[...])
    cw = CONV_WIDTH
    u = p[:, 2 * cw:3 * cw] * p[:, 0:cw]
    u_ext[SUBLANES:SUBLANES + tile, :] = u
    conv = (cw_ref[0:1, :] * u
            + cw_ref[1:2, :] * u_ext[SUBLANES - 1:SUBLANES - 1 + tile, :]
            + cw_ref[2:3, :] * u_ext[SUBLANES - 2:SUBLANES - 2 + tile, :])
    u_ext[0:SUBLANES, :] = u[tile - SUBLANES:, :]
    yb = _dot((p[:, cw:2 * cw] * conv).astype(BF16), wpb_ref[...])
    ya = _dot(oa_ref[...], wpa_ref[...])

    d = D_MODEL
    gates = p[:, 3 * cw:]
    merged = (_sigmoid(gates[:, 0:d]) * ya + _sigmoid(gates[:, d:2 * d]) * yb
              + _sigmoid(gates[:, 2 * d:3 * d]) * yc_ref[...].astype(F32))
    x_new = x + _dot(merged.astype(BF16), wo_ref[...])
    out_ref[...] = x_new
    if with_router:
        hn = _rms_rows(x_new, fg_ref[...])
        hn_ref[...] = hn
        route_ref[...] = _top2_route(hn.astype(BF16), rw_hi_ref[...], rw_lo_ref[...])


def _merge_call(x, gain, w, conv_w, oa, yc, wpa, wpb, wo, seq, router=None):
    n = x.shape[0]
    tile = TOKEN_TILE
    row = lambda width: pl.BlockSpec((tile, width), lambda i: (i, 0))
    in_specs = [row(D_MODEL), _full((1, D_MODEL)), _full(w.shape), _full(conv_w.shape),
                row(GROUP_WIDTH), row(D_MODEL), _full(wpa.shape), _full(wpb.shape),
                _full(wo.shape)]
    args = [x, gain, w, conv_w, oa, yc, wpa, wpb, wo]
    out_specs, out_shape = [row(D_MODEL)], [jax.ShapeDtypeStruct((n, D_MODEL), F32)]
    if router is not None:
        in_specs += [_full(a.shape) for a in router]
        args += list(router)
        out_specs += [row(D_MODEL), row(LANES)]
        out_shape += [jax.ShapeDtypeStruct((n, D_MODEL), F32),
                      jax.ShapeDtypeStruct((n, LANES), F32)]
    outs = pl.pallas_call(
        functools.partial(_merge_kernel, tile=tile, tiles_per_seq=seq // tile,
                          with_router=router is not None),
        grid=(n // tile,),
        in_specs=in_specs,
        out_specs=out_specs,
        out_shape=out_shape,
        scratch_shapes=[pltpu.VMEM((SUBLANES + tile, CONV_WIDTH), F32)],
        compiler_params=_params("arbitrary"),
        name="merge_out",
    )(*args)
    return outs[0] if router is None else outs


def _ffn_kernel(x_ref, g_ref, w1_ref, w3_ref, w2_ref, out_ref):
    x = x_ref[...]
    h = _rms_rows(x, g_ref[...]).astype(BF16)
    a = _dot(h, w1_ref[...])
    z = (a * _sigmoid(a) * _dot(h, w3_ref[...])).astype(BF16)
    out_ref[...] = x + _dot(z, w2_ref[...])


def _ffn_call(x, gain, w1, w3, w2):
    n = x.shape[0]
    tile = TOKEN_TILE
    row = pl.BlockSpec((tile, D_MODEL), lambda i: (i, 0))
    return pl.pallas_call(
        _ffn_kernel,
        grid=(n // tile,),
        in_specs=[row, _full((1, D_MODEL)), _full(w1.shape), _full(w3.shape), _full(w2.shape)],
        out_specs=row,
        out_shape=jax.ShapeDtypeStruct((n, D_MODEL), F32),
        compiler_params=_params("parallel"),
        name="dense_ffn",
    )(x, gain, w1, w3, w2)


def _row_copy(src, dst, src_row, dst_row, sem):
    return pltpu.make_async_copy(src.at[pl.ds(src_row, 1), :], dst.at[pl.ds(dst_row, 1), :], sem)


def _dispatch_kernel(pad_ref, pos_ref, x_ref, out_hbm, zeros, sem, pad_sem, *, tile):
    @pl.when(pl.program_id(0) == 0)
    def _():
        zeros[...] = jnp.zeros_like(zeros)
        for e in range(N_EXPERTS):
            first, count = pad_ref[e], pad_ref[N_EXPERTS + e]

            def fill(j, carry, first=first):
                _row_copy(zeros, out_hbm, 0, first + j, pad_sem).start()
                return carry

            def drain(j, carry, first=first):
                _row_copy(zeros, out_hbm, 0, first + j, pad_sem).wait()
                return carry

            lax.fori_loop(0, count, fill, 0)
            lax.fori_loop(0, count, drain, 0)

        def tail_copy(k):
            start = pl.multiple_of(pad_ref[2 * N_EXPERTS] + k * tile, tile)
            return pltpu.make_async_copy(zeros, out_hbm.at[pl.ds(start, tile), :], pad_sem)

        def tail_fill(k, carry):
            tail_copy(k).start()
            return carry

        def tail_drain(k, carry):
            tail_copy(k).wait()
            return carry

        lax.fori_loop(0, pad_ref[2 * N_EXPERTS + 1], tail_fill, 0)
        lax.fori_loop(0, pad_ref[2 * N_EXPERTS + 1], tail_drain, 0)

    def issue(j, carry):
        _row_copy(x_ref, out_hbm, j, pos_ref[j], sem).start()
        _row_copy(x_ref, out_hbm, j, pos_ref[tile + j], sem).start()
        return carry

    lax.fori_loop(0, tile, issue, 0, unroll=8)
    for _ in range(2):
        pltpu.make_async_copy(x_ref, out_hbm.at[pl.ds(0, tile), :], sem).wait()


def _dispatch_call(x, pos_tiles, pad_info, n_sorted):
    n = x.shape[0]
    tile = TOKEN_TILE
    grid_spec = pltpu.PrefetchScalarGridSpec(
        num_scalar_prefetch=1,
        grid=(n // tile,),
        in_specs=[pl.BlockSpec((2 * tile,), lambda i, pad: (i,), memory_space=pltpu.SMEM),
                  pl.BlockSpec((tile, D_MODEL), lambda i, pad: (i, 0))],
        out_specs=pl.BlockSpec(memory_space=pl.ANY),
        scratch_shapes=[pltpu.VMEM((tile, D_MODEL), F32), pltpu.SemaphoreType.DMA(()),
                        pltpu.SemaphoreType.DMA(())],
    )
    return pl.pallas_call(
        functools.partial(_dispatch_kernel, tile=tile),
        grid_spec=grid_spec,
        out_shape=jax.ShapeDtypeStruct((n_sorted, D_MODEL), F32),
        compiler_params=_params("arbitrary"),
        name="moe_dispatch",
    )(pad_info, pos_tiles, x)


def _moe_kernel(tile_expert_ref, n_tiles_ref, h_ref, w1_ref, w3_ref, w2_ref, y_ref, acc_ref, *,
                ff_steps):
    i, f = pl.program_id(0), pl.program_id(1)
    last_f = ff_steps - 1
    active = i < n_tiles_ref[0]

    @pl.when(active)
    def _():
        h = h_ref[...].astype(BF16)
        a = _dot(h, w1_ref[...])
        z = (a * _sigmoid(a) * _dot(h, w3_ref[...])).astype(BF16)
        part = _dot(z, w2_ref[...])
        if ff_steps == 1:
            y_ref[...] = part
            return

        @pl.when(f == 0)
        def _():
            acc_ref[...] = part

        @pl.when((f > 0) & (f < last_f))
        def _():
            acc_ref[...] += part

        @pl.when(f == last_f)
        def _():
            y_ref[...] = acc_ref[...] + part

    @pl.when(jnp.logical_not(active) & (f == last_f))
    def _():
        y_ref[...] = jnp.zeros_like(y_ref)


def _moe_call(hs, w1, w3, w2, tile_expert, n_tiles):
    p = hs.shape[0]
    d_ff = w1.shape[-1]
    tm, tf = MOE_ROW_TILE, MOE_FF_TILE
    nf = d_ff // tf

    def ff_block(i, f, n_tiles_ref):
        return jnp.where(i < n_tiles_ref[0], f, nf - 1)

    grid_spec = pltpu.PrefetchScalarGridSpec(
        num_scalar_prefetch=2,
        grid=(p // tm, nf),
        in_specs=[
            pl.BlockSpec((tm, D_MODEL), lambda i, f, te, nt: (jnp.minimum(i, nt[0] - 1), 0)),
            pl.BlockSpec((None, D_MODEL, tf), lambda i, f, te, nt: (te[i], 0, ff_block(i, f, nt))),
            pl.BlockSpec((None, D_MODEL, tf), lambda i, f, te, nt: (te[i], 0, ff_block(i, f, nt))),
            pl.BlockSpec((None, tf, D_MODEL), lambda i, f, te, nt: (te[i], ff_block(i, f, nt), 0)),
        ],
        out_specs=pl.BlockSpec((tm, D_MODEL), lambda i, f, te, nt: (i, 0)),
        scratch_shapes=[pltpu.VMEM((tm, D_MODEL), F32)],
    )
    return pl.pallas_call(
        functools.partial(_moe_kernel, ff_steps=nf),
        grid_spec=grid_spec,
        out_shape=jax.ShapeDtypeStruct((p, D_MODEL), F32),
        compiler_params=_params("arbitrary", "arbitrary"),
        name="moe_experts",
    )(tile_expert, n_tiles, hs, w1, w3, w2)


def _combine_kernel(pos_ref, x_ref, route_ref, y_hbm, out_ref, buf, sems, *, tile):
    def issue(j, carry):
        _row_copy(y_hbm, buf.at[0], pos_ref[j], j, sems.at[0]).start()
        _row_copy(y_hbm, buf.at[1], pos_ref[tile + j], j, sems.at[1]).start()
        return carry

    lax.fori_loop(0, tile, issue, 0, unroll=8)
    for k in range(2):
        pltpu.make_async_copy(y_hbm.at[pl.ds(0, tile), :], buf.at[k], sems.at[k]).wait()
    g1 = route_ref[:, 0:1]
    g2 = route_ref[:, 1:2]
    out_ref[...] = x_ref[...] + g1 * buf[0] + g2 * buf[1]


def _combine_call(x, route, y, pos_tiles):
    n = x.shape[0]
    tile = TOKEN_TILE
    row = lambda width: pl.BlockSpec((tile, width), lambda i: (i, 0))
    return pl.pallas_call(
        functools.partial(_combine_kernel, tile=tile),
        grid=(n // tile,),
        in_specs=[pl.BlockSpec((2 * tile,), lambda i: (i,), memory_space=pltpu.SMEM),
                  row(D_MODEL), row(LANES), pl.BlockSpec(memory_space=pl.ANY)],
        out_specs=row(D_MODEL),
        out_shape=jax.ShapeDtypeStruct(x.shape, F32),
        scratch_shapes=[pltpu.VMEM((2, tile, D_MODEL), F32), pltpu.SemaphoreType.DMA((2,))],
        compiler_params=_params("arbitrary"),
        name="moe_combine",
    )(pos_tiles, x, route, y)


def _routing_tables(route, n):
    tm = MOE_ROW_TILE
    p = 2 * n + N_EXPERTS * tm
    expert = jnp.concatenate([route[:, 2], route[:, 3]]).astype(jnp.int32)
    onehot = (expert[:, None] == jnp.arange(N_EXPERTS, dtype=jnp.int32)[None, :]).astype(jnp.int32)
    running = jnp.cumsum(onehot, axis=0)
    rank = jnp.sum(running * onehot, axis=1) - 1
    counts = running[-1]
    padded = ((counts + tm - 1) // tm) * tm
    ends = jnp.cumsum(padded)
    starts = ends - padded
    pos = jnp.sum(onehot * starts[None, :], axis=1) + rank
    n_tiles = (ends[-1] // tm).astype(jnp.int32)
    tile_start = jnp.arange(p // tm, dtype=jnp.int32) * tm
    tile_expert = jnp.sum((tile_start[:, None] >= ends[None, :]).astype(jnp.int32), axis=1)
    last_expert = jnp.sum((ends[-1] - 1 >= ends).astype(jnp.int32))
    tile_expert = jnp.minimum(tile_expert, last_expert).astype(jnp.int32)
    tile = TOKEN_TILE
    tail = jnp.stack([ends[-1], (p - ends[-1]) // tile])
    pad_info = jnp.concatenate([starts + counts, padded - counts, tail]).astype(jnp.int32)
    pos_tiles = jnp.concatenate(
        [pos[:n].reshape(n // tile, tile), pos[n:].reshape(n // tile, tile)], axis=1).reshape(-1)
    return pos_tiles.astype(jnp.int32), pad_info, tile_expert, n_tiles.reshape(1), p


def _moe_layer(x, hn, route, w1, w3, w2):
    n = x.shape[0]
    pos_tiles, pad_info, tile_expert, n_tiles, n_sorted = _routing_tables(route, n)
    hs = _dispatch_call(hn, pos_tiles, pad_info, n_sorted)
    ys = _moe_call(hs, w1.astype(BF16), w3.astype(BF16), w2.astype(BF16), tile_expert, n_tiles)
    return _combine_call(x, route, ys, pos_tiles)


def kernel(x, attn_norm, ffn_norm, w_in, q_norm, k_norm, conv_w, hgrn_lower_bounds,
           hgrn_out_norm, w_proj_a, w_proj_b, w_proj_c, w_out, dense_w1, dense_w3,
           dense_w2, router_w, moe_w1, moe_w3, moe_w2):
    batch, seq, d = x.shape
    n = batch * seq
    depth = w_in.shape[0]
    assert d == D_MODEL and seq % ATTN_TILE == 0 and seq % TOKEN_TILE == 0
    assert all(window // dilation == BAND for window, dilation in ATTN_GROUPS)

    lbs = jax.nn.softmax(hgrn_lower_bounds.astype(F32), axis=0)
    lower_bound = jnp.cumsum(lbs, axis=0) - lbs[0]
    head_of_col = jnp.arange(ATTN_WIDTH) // HEAD_DIM
    hsum = (head_of_col[:, None] == jnp.arange(LANES)[None, :]).astype(BF16)
    hexp = jnp.concatenate([hsum.T, hsum.T], axis=0)
    a0, a1 = 0, 3 * ATTN_WIDTH
    b1 = a1 + 3 * CONV_WIDTH
    c1 = b1 + 4 * HGRN_WIDTH

    xf = x.reshape(n, d)
    for layer in range(depth):
        w = w_in[layer]
        w_attn = w[:, a0:a1].astype(BF16)
        w_hgrn = w[:, b1:c1].astype(BF16)
        w_conv_gate = jnp.concatenate([w[:, a1:b1], w[:, c1:]], axis=1).astype(BF16)
        gain = attn_norm[layer].reshape(1, d)
        heads = ATTN_WIDTH // HEAD_DIM
        qg = jnp.tile(q_norm[layer] * (HEAD_DIM ** -0.5), heads).reshape(1, ATTN_WIDTH)
        kg = jnp.tile(k_norm[layer], heads).reshape(1, ATTN_WIDTH)

        qkv = _qkv_call(xf, gain, w_attn, qg, kg, hsum, hexp, batch, seq)
        oa = _attn_call(qkv, batch, seq)
        yc = _hgrn_call(xf, gain, w_hgrn, lower_bound[layer].reshape(1, HGRN_WIDTH),
                        hgrn_out_norm[layer].reshape(1, HGRN_HEAD),
                        w_proj_c[layer].astype(BF16), batch, seq)
        fgain = ffn_norm[layer].reshape(1, d)
        j = layer // 2
        routed = layer % 2 == 1
        router = None
        if routed:
            rw = jnp.pad(router_w[j], ((0, 0), (0, LANES - N_EXPERTS)))
            rw_hi = rw.astype(BF16)
            router = (fgain, rw_hi, (rw - rw_hi.astype(F32)).astype(BF16))
        merged = _merge_call(xf, gain, w_conv_gate, conv_w[layer], oa, yc,
                             w_proj_a[layer].astype(BF16), w_proj_b[layer].astype(BF16),
                             w_out[layer].astype(BF16), seq, router)
        if routed:
            xf, hn, route = merged
            xf = _moe_layer(xf, hn, route, moe_w1[j], moe_w3[j], moe_w2[j])
        else:
            xf = _ffn_call(merged, fgain, dense_w1[j].astype(BF16), dense_w3[j].astype(BF16),
                           dense_w2[j].astype(BF16))
    return xf.reshape(batch, seq, d)
```

```python
import functools

import jax
import jax.numpy as jnp
from jax import lax
from jax.experimental import pallas as pl
from jax.experimental.pallas import tpu as pltpu

F32 = jnp.float32
BF16 = jnp.bfloat16

D_MODEL = 1024
HEAD_DIM = 64
ATTN_GROUPS = ((128, 1), (512, 4), (2048, 16))
HEADS_PER_GROUP = 4
GROUP_WIDTH = HEADS_PER_GROUP * HEAD_DIM
ATTN_WIDTH = GROUP_WIDTH * len(ATTN_GROUPS)
BAND = 128
CONV_WIDTH = 768
CONV_K = 3
HGRN_WIDTH = 768
HGRN_HEAD = 128
HGRN_HEADS = HGRN_WIDTH // HGRN_HEAD
LB_FLOOR = 1e-30
N_EXPERTS = 8
RMS_EPS = 1e-6
NEG_BIG = -1e30
LOG2E = 1.4426950408889634

LANES = 128
SUBLANES = 8

TOKEN_TILE = 512
ATTN_TILE = ATTN_GROUPS[-1][1] * BAND
ATTN_UNROLL = 8
HGRN_CHUNK = 256
MOE_ROW_TILE = 512
MOE_FF_TILE = 1792
VMEM_LIMIT = 56 * 1024 * 1024


def _dot(a, b):
    return jnp.dot(a, b, preferred_element_type=F32)


def _dot_nt(a, b):
    return lax.dot_general(a, b, (((1,), (1,)), ((), ())), preferred_element_type=F32)


def _dot_tn(a, b):
    return lax.dot_general(a, b, (((0,), (0,)), ((), ())), preferred_element_type=F32)


def _rms_rows(xf, gain):
    ms = jnp.mean(xf * xf, axis=-1, keepdims=True)
    return xf * lax.rsqrt(ms + RMS_EPS) * gain


def _sigmoid(z):
    return 1.0 / (1.0 + jnp.exp(-z))


def _params(*sem):
    return pltpu.CompilerParams(dimension_semantics=sem, vmem_limit_bytes=VMEM_LIMIT)


def _full(shape):
    return pl.BlockSpec(shape, lambda *_: (0,) * len(shape))


def _qkv_kernel(x_ref, g_ref, w_ref, qg_ref, kg_ref, hsum_ref, hexp_ref, *rest, tile):
    out_refs, stage = rest[:9], rest[9]
    h = _rms_rows(x_ref[0], g_ref[...]).astype(BF16)
    p = _dot(h, w_ref[...])

    def head_norm(t, gain):
        ms = _dot((t * t).astype(BF16), hsum_ref[...]) * (1.0 / HEAD_DIM)
        r = lax.rsqrt(ms + RMS_EPS)
        r_hi = r.astype(BF16)
        r_lo = (r - r_hi.astype(F32)).astype(BF16)
        r_cols = _dot(jnp.concatenate([r_hi, r_lo], axis=1), hexp_ref[...])
        return t * r_cols * gain

    parts = (head_norm(p[:, :ATTN_WIDTH], qg_ref[...]),
             head_norm(p[:, ATTN_WIDTH:2 * ATTN_WIDTH], kg_ref[...]),
             p[:, 2 * ATTN_WIDTH:])
    slot = 0
    for a, part in enumerate(parts):
        for group, (_, dilation) in enumerate(ATTN_GROUPS):
            out = out_refs[3 * group + a]
            c0 = group * GROUP_WIDTH
            if dilation == 1:
                out[0, 0] = part[:, c0:c0 + GROUP_WIDTH].astype(BF16)
                continue
            rows = tile // dilation
            for half in range(GROUP_WIDTH // LANES):
                stage[slot] = part[:, c0 + half * LANES:c0 + (half + 1) * LANES]
                for r in range(dilation):
                    out[0, r, :, half * LANES:(half + 1) * LANES] = (
                        stage[slot, pl.ds(r, rows, stride=dilation), :].astype(BF16))
                slot += 1


def _qkv_call(x, gain, w, qg, kg, hsum, hexp, batch, seq):
    t = TOKEN_TILE
    out_specs, out_shape = [], []
    for _, dilation in ATTN_GROUPS:
        for _ in range(3):
            out_specs.append(pl.BlockSpec((1, dilation, t // dilation, GROUP_WIDTH),
                                          lambda b, i: (b, 0, i, 0)))
            out_shape.append(jax.ShapeDtypeStruct(
                (batch, dilation, seq // dilation, GROUP_WIDTH), BF16))
    n_stage = 3 * (GROUP_WIDTH // LANES) * sum(1 for _, d in ATTN_GROUPS if d > 1)
    return pl.pallas_call(
        functools.partial(_qkv_kernel, tile=t),
        grid=(batch, seq // t),
        in_specs=[pl.BlockSpec((1, t, D_MODEL), lambda b, i: (b, i, 0)), _full((1, D_MODEL)),
                  _full(w.shape), _full((1, ATTN_WIDTH)), _full((1, ATTN_WIDTH)),
                  _full(hsum.shape), _full(hexp.shape)],
        out_specs=out_specs,
        out_shape=out_shape,
        scratch_shapes=[pltpu.VMEM((n_stage, t, LANES), F32)],
        compiler_params=_params("parallel", "parallel"),
        name="qkv_proj",
    )(x.reshape(batch, seq, D_MODEL), gain, w, qg, kg, hsum, hexp)


def _attn_kernel(*refs, n_groups):
    ins, o_ref, scr = refs[:5 * n_groups], refs[5 * n_groups], refs[5 * n_groups + 1:]
    kbufs, vbufs = scr[0:n_groups], scr[n_groups:2 * n_groups]
    o_tok, l_tok, o_rm, l_rm = scr[2 * n_groups:]
    first_tile = pl.program_id(1) == 0
    qi = lax.broadcasted_iota(jnp.int32, (BAND, 2 * BAND), 0)
    kj = lax.broadcasted_iota(jnp.int32, (BAND, 2 * BAND), 1)
    rel = qi + BAND - kj
    in_window = (rel >= 0) & (rel <= BAND)
    halves = GROUP_WIDTH // LANES
    lane = lax.broadcasted_iota(jnp.int32, (1, LANES), 1)
    head_lanes = [jnp.where(lane // HEAD_DIM == side, 1.0, 0.0).astype(BF16)
                  for side in range(LANES // HEAD_DIM)]
    first_head = lax.broadcasted_iota(jnp.int32, (BAND, LANES), 1) < HEAD_DIM

    for group, (_, dilation) in enumerate(ATTN_GROUPS):
        q_ref, kc_ref, vc_ref, kp_ref, vp_ref = ins[5 * group:5 * group + 5]
        kbuf, vbuf = kbufs[group], vbufs[group]
        length = ATTN_TILE // dilation
        nj = length // BAND
        kbuf[:, 0:BAND, :] = kp_ref[0]
        kbuf[:, BAND:BAND + length, :] = kc_ref[0]
        vbuf[:, 0:BAND, :] = vp_ref[0]
        vbuf[:, BAND:BAND + length, :] = vc_ref[0]

        def block(idx, carry, q_ref=q_ref, kbuf=kbuf, vbuf=vbuf, nj=nj, length=length,
                  group=group, dilation=dilation):
            r, j = idx // nj, idx % nj
            j0 = pl.multiple_of(j * BAND, BAND)
            kmin = jnp.where(first_tile & (j == 0), BAND, 0)
            valid = in_window & (kj >= kmin)
            row0 = pl.multiple_of(r * length + j0, BAND)
            for half in range(halves):
                lanes = slice(half * LANES, (half + 1) * LANES)
                q2 = q_ref[0, r, pl.ds(j0, BAND), lanes]
                k2 = kbuf[r, pl.ds(j0, 2 * BAND), lanes]
                v2 = vbuf[r, pl.ds(j0, 2 * BAND), lanes]
                o2, l2 = None, None
                for side in range(LANES // HEAD_DIM):
                    s = _dot_nt(q2 * head_lanes[side], k2)
                    s = jnp.where(valid, s, NEG_BIG)
                    m = jnp.max(s, axis=-1, keepdims=True)
                    p = jnp.exp(s - m)
                    den = jnp.sum(p, axis=-1, keepdims=True)
                    o = _dot(p.astype(BF16), v2) * (1.0 / den)
                    lse = jnp.broadcast_to(m + jnp.log(den), (BAND, LANES))
                    o2 = o if side == 0 else jnp.where(first_head, o2, o)
                    l2 = lse if side == 0 else jnp.where(first_head, l2, lse)
                if dilation == 1:
                    o_tok[group, half, pl.ds(row0, BAND), :] = o2
                    l_tok[group, half, pl.ds(row0, BAND), :] = l2
                else:
                    o_rm[pl.ds(row0, BAND), lanes] = o2
                    l_rm[pl.ds(row0, BAND), lanes] = l2
            return carry

        lax.fori_loop(0, ATTN_TILE // BAND, block, 0, unroll=ATTN_UNROLL)
        if dilation > 1:
            for r in range(dilation):
                rows = slice(r * length, (r + 1) * length)
                for half in range(halves):
                    lanes = slice(half * LANES, (half + 1) * LANES)
                    o_tok[group, half, pl.ds(r, length, stride=dilation), :] = o_rm[rows, lanes]
                    l_tok[group, half, pl.ds(r, length, stride=dilation), :] = l_rm[rows, lanes]

    chunk = 2 * BAND

    def merge(c, carry):
        rows = pl.ds(pl.multiple_of(c * chunk, chunk), chunk)
        for half in range(halves):
            ls = [l_tok[g, half, rows, :] for g in range(n_groups)]
            top = functools.reduce(jnp.maximum, ls)
            es = [jnp.exp(l - top) for l in ls]
            acc = sum(e * o_tok[g, half, rows, :] for g, e in enumerate(es))
            o_ref[0, rows, half * LANES:(half + 1) * LANES] = (acc * (1.0 / sum(es))).astype(BF16)
        return carry

    lax.fori_loop(0, ATTN_TILE // chunk, merge, 0)


def _attn_call(qkv, batch, seq):
    n_groups = len(ATTN_GROUPS)
    in_specs, args, kbufs = [], [], []
    for group, (_, dilation) in enumerate(ATTN_GROUPS):
        length = ATTN_TILE // dilation
        cur = pl.BlockSpec((1, dilation, length, GROUP_WIDTH), lambda b, i: (b, 0, i, 0))
        halo = pl.BlockSpec(
            (1, dilation, BAND, GROUP_WIDTH),
            lambda b, i, nb=length // BAND: (b, 0, jnp.maximum(i * nb - 1, 0), 0))
        q, k, v = qkv[3 * group:3 * group + 3]
        in_specs += [cur, cur, cur, halo, halo]
        args += [q, k, v, k, v]
        kbufs.append(pltpu.VMEM((dilation, BAND + length, GROUP_WIDTH), BF16))
    halves = GROUP_WIDTH // LANES
    o = pl.pallas_call(
        functools.partial(_attn_kernel, n_groups=n_groups),
        grid=(batch, seq // ATTN_TILE),
        in_specs=in_specs,
        out_specs=pl.BlockSpec((1, ATTN_TILE, GROUP_WIDTH), lambda b, i: (b, i, 0)),
        out_shape=jax.ShapeDtypeStruct((batch, seq, GROUP_WIDTH), BF16),
        scratch_shapes=kbufs + kbufs + [
            pltpu.VMEM((n_groups, halves, ATTN_TILE, LANES), F32),
            pltpu.VMEM((n_groups, halves, ATTN_TILE, LANES), F32),
            pltpu.VMEM((ATTN_TILE, GROUP_WIDTH), F32),
            pltpu.VMEM((ATTN_TILE, GROUP_WIDTH), F32)],
        compiler_params=_params("parallel", "arbitrary"),
        name="band_attn",
    )(*args)
    return o.reshape(batch * seq, GROUP_WIDTH)


def _hgrn_kernel(x_ref, g_ref, w_ref, lb_ref, og_ref, wp_ref, tri_ref, y_ref,
                 state_ref, proj_ref, gated_ref, b_ref, *, tile, chunk):
    @pl.when(pl.program_id(1) == 0)
    def _():
        state_ref[...] = jnp.zeros_like(state_ref)

    h = _rms_rows(x_ref[0], g_ref[...]).astype(BF16)
    proj_ref[...] = _dot(h, w_ref[...])

    lb = lb_ref[...]
    lb_floor = jnp.maximum(lb, LB_FLOOR)
    one_minus_lb = 1.0 - lb
    floor_gap = lb_floor - lb
    row = lax.broadcasted_iota(jnp.int32, (chunk, HGRN_HEAD), 0)
    tt = lax.broadcasted_iota(jnp.int32, (chunk, chunk), 0)
    ss = lax.broadcasted_iota(jnp.int32, (chunk, chunk), 1)
    t_xor_s = tt ^ ss
    w = HGRN_WIDTH
    sizes = [1 << i for i in range(chunk.bit_length() - 1)]
    lower_rows = [(row & m) == 0 for m in sizes]
    lower_sel = [jnp.where(low, 1.0, 0.0).astype(BF16) for low in lower_rows]
    signed_log2e = [jnp.where(low, -LOG2E, LOG2E).astype(BF16) for low in lower_rows]
    level_pairs = [((t_xor_s & (-m)) == m) & ((tt & m) != 0) for m in sizes]
    diagonal = tt == ss

    def chunk_body(c, carry):
        r0 = pl.multiple_of(c * chunk, chunk)
        rows = pl.ds(r0, chunk)
        z = proj_ref[rows, w:2 * w]
        e = jnp.exp(-jnp.abs(z))
        inv = 1.0 / (1.0 + e)
        sig = jnp.where(z >= 0, inv, e * inv)
        nsig = jnp.where(z >= 0, e * inv, inv)
        log_f = jnp.log(lb_floor + one_minus_lb * sig)
        k_all = one_minus_lb * nsig - floor_gap
        lf_hi = log_f.astype(BF16)
        rest = log_f - lf_hi.astype(F32)
        lf_mid = rest.astype(BF16)
        lf_lo = (rest - lf_mid.astype(F32)).astype(BF16)
        tri = tri_ref[...]
        b_all = _dot(tri, lf_hi) + _dot(tri, lf_mid) + _dot(tri, lf_lo)
        b_ref[...] = b_all
        q_all = proj_ref[rows, 0:w].astype(BF16)
        q_all = q_all * _sigmoid(q_all)
        v_all = proj_ref[rows, 2 * w:3 * w]
        g_all = proj_ref[rows, 3 * w:4 * w].astype(BF16)
        g_all = g_all * _sigmoid(g_all)
        for hd in range(HGRN_HEADS):
            cols = slice(hd * HGRN_HEAD, (hd + 1) * HGRN_HEAD)
            b, qb = b_all[:, cols], q_all[:, cols]
            kb = k_all[:, cols].astype(BF16)
            vb = v_all[:, cols].astype(BF16)
            scores = jnp.where(diagonal, _dot_nt(qb, kb), 0.0)
            last_of_block = b
            for lvl, m in enumerate(sizes):
                if m < SUBLANES:
                    lower = lower_rows[lvl]
                    boundary = jnp.where(lower, last_of_block, pltpu.roll(last_of_block, m, 0))
                    if 2 * m < SUBLANES:
                        last_of_block = jnp.where(
                            lower, pltpu.roll(last_of_block, chunk - m, 0), last_of_block)
                else:
                    boundary = jnp.concatenate(
                        [jnp.broadcast_to(b_ref[r:r + 1, cols], (2 * m, HGRN_HEAD))
                         for r in range(m - 1, chunk, 2 * m)], axis=0)
                decay = jnp.exp2((b - boundary).astype(BF16) * signed_log2e[lvl])
                if m < 2 * SUBLANES:
                    source = jnp.where(lower_sel[lvl] > 0, kb, qb)
                else:
                    source = jnp.concatenate(
                        [(kb if (r // m) % 2 == 0 else qb)[r:r + m] for r in range(0, chunk, m)],
                        axis=0)
                side = source * decay
                scores = jnp.where(level_pairs[lvl], _dot_nt(side, side), scores)
            state_t = state_ref[hd]
            q_dec = qb * jnp.exp(b).astype(BF16)
            o = _dot(scores.astype(BF16), vb) + _dot_nt(q_dec, state_t.astype(BF16))
            b_last = b[chunk - 1:chunk, :]
            k_dec = kb * jnp.exp(b_last - b).astype(BF16)
            state_ref[hd] = state_t * jnp.exp(b_last) + _dot_tn(vb, k_dec)
            o = _rms_rows(o, og_ref[...]).astype(BF16) * g_all[:, cols]
            gated_ref[rows, cols] = o
        return carry

    lax.fori_loop(0, tile // chunk, chunk_body, 0, unroll=True)
    y_ref[0] = _dot(gated_ref[...], wp_ref[...]).astype(BF16)


def _hgrn_call(x, gain, w, lb, out_gain, w_proj, batch, seq):
    tile, chunk = TOKEN_TILE, HGRN_CHUNK
    tri = jnp.tril(jnp.ones((chunk, chunk), BF16))
    blk = lambda width: pl.BlockSpec((1, tile, width), lambda b, i: (b, i, 0))
    y = pl.pallas_call(
        functools.partial(_hgrn_kernel, tile=tile, chunk=chunk),
        grid=(batch, seq // tile),
        in_specs=[blk(D_MODEL), _full((1, D_MODEL)), _full(w.shape), _full((1, HGRN_WIDTH)),
                  _full((1, HGRN_HEAD)), _full(w_proj.shape), _full((chunk, chunk))],
        out_specs=blk(D_MODEL),
        out_shape=jax.ShapeDtypeStruct((batch, seq, D_MODEL), BF16),
        scratch_shapes=[pltpu.VMEM((HGRN_HEADS, HGRN_HEAD, HGRN_HEAD), F32),
                        pltpu.VMEM((tile, 4 * HGRN_WIDTH), F32),
                        pltpu.VMEM((tile, HGRN_WIDTH), BF16),
                        pltpu.VMEM((chunk, HGRN_WIDTH), F32)],
        compiler_params=_params("parallel", "arbitrary"),
        name="hgrn_branch",
    )(x.reshape(batch, seq, D_MODEL), gain, w, lb, out_gain, w_proj, tri)
    return y.reshape(batch * seq, D_MODEL)


def _top2_route(h, rw_hi, rw_lo):
    logits = _dot(h, rw_hi) + _dot(h, rw_lo)
    lane = lax.broadcasted_iota(jnp.int32, logits.shape, 1)
    logits = jnp.where(lane < N_EXPERTS, logits, -jnp.inf)
    m1 = jnp.max(logits, axis=-1, keepdims=True)
    i1 = jnp.min(jnp.where(logits == m1, lane, LANES), axis=-1, keepdims=True)
    rest = jnp.where(lane == i1, -jnp.inf, logits)
    m2 = jnp.max(rest, axis=-1, keepdims=True)
    i2 = jnp.min(jnp.where(rest == m2, lane, LANES), axis=-1, keepdims=True)
    e = jnp.exp(m2 - m1)
    g1 = 1.0 / (1.0 + e)
    g2 = e * g1
    return jnp.where(
        lane == 0, g1, jnp.where(lane == 1, g2, jnp.where(
            lane == 2, i1.astype(F32), jnp.where(lane == 3, i2.astype(F32), 0.0))))


def _merge_kernel(x_ref, g_ref, w_ref, cw_ref, oa_ref, yc_ref, wpa_ref, wpb_ref, wo_ref, *rest,
                  tile, tiles_per_seq, with_router):
    if with_router:
        fg_ref, rw_hi_ref, rw_lo_ref, out_ref, hn_ref, route_ref, u_ext = rest
    else:
        out_ref, u_ext = rest

    @pl.when(pl.program_id(0) % tiles_per_seq == 0)
    def _():
        u_ext[0:SUBLANES, :] = jnp.zeros((SUBLANES, CONV_WIDTH), F32)

    x = x_ref[...]
    h = _rms_rows(x, g_ref[...]).astype(BF16)
    p = _dot(h, w_ref[...])
    cw = CONV_WIDTH
    u = p[:, 2 * cw:3 * cw] * p[:, 0:cw]
    u_ext[SUBLANES:SUBLANES + tile, :] = u
    conv = (cw_ref[0:1, :] * u
            + cw_ref[1:2, :] * u_ext[SUBLANES - 1:SUBLANES - 1 + tile, :]
            + cw_ref[2:3, :] * u_ext[SUBLANES - 2:SUBLANES - 2 + tile, :])
    u_ext[0:SUBLANES, :] = u[tile - SUBLANES:, :]
    yb = _dot((p[:, cw:2 * cw] * conv).astype(BF16), wpb_ref[...])
    ya = _dot(oa_ref[...], wpa_ref[...])

    d = D_MODEL
    gates = p[:, 3 * cw:]
    merged = (_sigmoid(gates[:, 0:d]) * ya + _sigmoid(gates[:, d:2 * d]) * yb
              + _sigmoid(gates[:, 2 * d:3 * d]) * yc_ref[...].astype(F32))
    x_new = x + _dot(merged.astype(BF16), wo_ref[...])
    out_ref[...] = x_new
    if with_router:
        hn = _rms_rows(x_new, fg_ref[...])
        hn_ref[...] = hn
        route_ref[...] = _top2_route(hn.astype(BF16), rw_hi_ref[...], rw_lo_ref[...])


def _merge_call(x, gain, w, conv_w, oa, yc, wpa, wpb, wo, seq, router=None):
    n = x.shape[0]
    tile = TOKEN_TILE
    row = lambda width: pl.BlockSpec((tile, width), lambda i: (i, 0))
    in_specs = [row(D_MODEL), _full((1, D_MODEL)), _full(w.shape), _full(conv_w.shape),
                row(GROUP_WIDTH), row(D_MODEL), _full(wpa.shape), _full(wpb.shape),
                _full(wo.shape)]
    args = [x, gain, w, conv_w, oa, yc, wpa, wpb, wo]
    out_specs, out_shape = [row(D_MODEL)], [jax.ShapeDtypeStruct((n, D_MODEL), F32)]
    if router is not None:
        in_specs += [_full(a.shape) for a in router]
        args += list(router)
        out_specs += [row(D_MODEL), row(LANES)]
        out_shape += [jax.ShapeDtypeStruct((n, D_MODEL), F32),
                      jax.ShapeDtypeStruct((n, LANES), F32)]
    outs = pl.pallas_call(
        functools.partial(_merge_kernel, tile=tile, tiles_per_seq=seq // tile,
                          with_router=router is not None),
        grid=(n // tile,),
        in_specs=in_specs,
        out_specs=out_specs,
        out_shape=out_shape,
        scratch_shapes=[pltpu.VMEM((SUBLANES + tile, CONV_WIDTH), F32)],
        compiler_params=_params("arbitrary"),
        name="merge_out",
    )(*args)
    return outs[0] if router is None else outs


def _ffn_kernel(x_ref, g_ref, w1_ref, w3_ref, w2_ref, out_ref):
    x = x_ref[...]
    h = _rms_rows(x, g_ref[...]).astype(BF16)
    a = _dot(h, w1_ref[...])
    z = (a * _sigmoid(a) * _dot(h, w3_ref[...])).astype(BF16)
    out_ref[...] = x + _dot(z, w2_ref[...])


def _ffn_call(x, gain, w1, w3, w2):
    n = x.shape[0]
    tile = TOKEN_TILE
    row = pl.BlockSpec((tile, D_MODEL), lambda i: (i, 0))
    return pl.pallas_call(
        _ffn_kernel,
        grid=(n // tile,),
        in_specs=[row, _full((1, D_MODEL)), _full(w1.shape), _full(w3.shape), _full(w2.shape)],
        out_specs=row,
        out_shape=jax.ShapeDtypeStruct((n, D_MODEL), F32),
        compiler_params=_params("parallel"),
        name="dense_ffn",
    )(x, gain, w1, w3, w2)


def _row_copy(src, dst, src_row, dst_row, sem):
    return pltpu.make_async_copy(src.at[pl.ds(src_row, 1), :], dst.at[pl.ds(dst_row, 1), :], sem)


def _dispatch_kernel(pad_ref, pos_ref, x_ref, out_hbm, zeros, sem, pad_sem, *, tile):
    @pl.when(pl.program_id(0) == 0)
    def _():
        zeros[...] = jnp.zeros_like(zeros)
        for e in range(N_EXPERTS):
            first, count = pad_ref[e], pad_ref[N_EXPERTS + e]

            def fill(j, carry, first=first):
                _row_copy(zeros, out_hbm, 0, first + j, pad_sem).start()
                return carry

            def drain(j, carry, first=first):
                _row_copy(zeros, out_hbm, 0, first + j, pad_sem).wait()
                return carry

            lax.fori_loop(0, count, fill, 0)
            lax.fori_loop(0, count, drain, 0)

        def tail_copy(k):
            start = pl.multiple_of(pad_ref[2 * N_EXPERTS] + k * tile, tile)
            return pltpu.make_async_copy(zeros, out_hbm.at[pl.ds(start, tile), :], pad_sem)

        def tail_fill(k, carry):
            tail_copy(k).start()
            return carry

        def tail_drain(k, carry):
            tail_copy(k).wait()
            return carry

        lax.fori_loop(0, pad_ref[2 * N_EXPERTS + 1], tail_fill, 0)
        lax.fori_loop(0, pad_ref[2 * N_EXPERTS + 1], tail_drain, 0)

    def issue(j, carry):
        _row_copy(x_ref, out_hbm, j, pos_ref[j], sem).start()
        _row_copy(x_ref, out_hbm, j, pos_ref[tile + j], sem).start(priority=1)
        return carry

    lax.fori_loop(0, tile, issue, 0, unroll=8)
    for _ in range(2):
        pltpu.make_async_copy(x_ref, out_hbm.at[pl.ds(0, tile), :], sem).wait()


def _dispatch_call(x, pos_tiles, pad_info, n_sorted):
    n = x.shape[0]
    tile = TOKEN_TILE
    grid_spec = pltpu.PrefetchScalarGridSpec(
        num_scalar_prefetch=1,
        grid=(n // tile,),
        in_specs=[pl.BlockSpec((2 * tile,), lambda i, pad: (i,), memory_space=pltpu.SMEM),
                  pl.BlockSpec((tile, D_MODEL), lambda i, pad: (i, 0))],
        out_specs=pl.BlockSpec(memory_space=pl.ANY),
        scratch_shapes=[pltpu.VMEM((tile, D_MODEL), F32), pltpu.SemaphoreType.DMA(()),
                        pltpu.SemaphoreType.DMA(())],
    )
    return pl.pallas_call(
        functools.partial(_dispatch_kernel, tile=tile),
        grid_spec=grid_spec,
        out_shape=jax.ShapeDtypeStruct((n_sorted, D_MODEL), F32),
        compiler_params=_params("arbitrary"),
        name="moe_dispatch",
    )(pad_info, pos_tiles, x)


def _moe_kernel(tile_expert_ref, n_tiles_ref, h_ref, w1_ref, w3_ref, w2_ref, y_ref, acc_ref, *,
                ff_steps):
    i, f = pl.program_id(0), pl.program_id(1)
    last_f = ff_steps - 1
    active = i < n_tiles_ref[0]

    @pl.when(active)
    def _():
        h = h_ref[...].astype(BF16)
        a = _dot(h, w1_ref[...])
        z = (a * _sigmoid(a) * _dot(h, w3_ref[...])).astype(BF16)
        part = _dot(z, w2_ref[...])
        if ff_steps == 1:
            y_ref[...] = part
            return

        @pl.when(f == 0)
        def _():
            acc_ref[...] = part

        @pl.when((f > 0) & (f < last_f))
        def _():
            acc_ref[...] += part

        @pl.when(f == last_f)
        def _():
            y_ref[...] = acc_ref[...] + part

    @pl.when(jnp.logical_not(active) & (f == last_f))
    def _():
        y_ref[...] = jnp.zeros_like(y_ref)


def _moe_call(hs, w1, w3, w2, tile_expert, n_tiles):
    p = hs.shape[0]
    d_ff = w1.shape[-1]
    tm, tf = MOE_ROW_TILE, MOE_FF_TILE
    nf = d_ff // tf

    def ff_block(i, f, n_tiles_ref):
        return jnp.where(i < n_tiles_ref[0], f, nf - 1)

    grid_spec = pltpu.PrefetchScalarGridSpec(
        num_scalar_prefetch=2,
        grid=(p // tm, nf),
        in_specs=[
            pl.BlockSpec((tm, D_MODEL), lambda i, f, te, nt: (jnp.minimum(i, nt[0] - 1), 0)),
            pl.BlockSpec((None, D_MODEL, tf), lambda i, f, te, nt: (te[i], 0, ff_block(i, f, nt))),
            pl.BlockSpec((None, D_MODEL, tf), lambda i, f, te, nt: (te[i], 0, ff_block(i, f, nt))),
            pl.BlockSpec((None, tf, D_MODEL), lambda i, f, te, nt: (te[i], ff_block(i, f, nt), 0)),
        ],
        out_specs=pl.BlockSpec((tm, D_MODEL), lambda i, f, te, nt: (i, 0)),
        scratch_shapes=[pltpu.VMEM((tm, D_MODEL), F32)],
    )
    return pl.pallas_call(
        functools.partial(_moe_kernel, ff_steps=nf),
        grid_spec=grid_spec,
        out_shape=jax.ShapeDtypeStruct((p, D_MODEL), F32),
        compiler_params=_params("arbitrary", "arbitrary"),
        name="moe_experts",
    )(tile_expert, n_tiles, hs, w1, w3, w2)


def _combine_kernel(pos_ref, x_ref, route_ref, y_hbm, out_ref, buf, sems, *, tile):
    def issue(j, carry):
        _row_copy(y_hbm, buf.at[0], pos_ref[j], j, sems.at[0]).start()
        _row_copy(y_hbm, buf.at[1], pos_ref[tile + j], j, sems.at[1]).start(priority=1)
        return carry

    lax.fori_loop(0, tile, issue, 0, unroll=8)
    for k in range(2):
        pltpu.make_async_copy(y_hbm.at[pl.ds(0, tile), :], buf.at[k], sems.at[k]).wait()
    g1 = route_ref[:, 0:1]
    g2 = route_ref[:, 1:2]
    out_ref[...] = x_ref[...] + g1 * buf[0] + g2 * buf[1]


def _combine_call(x, route, y, pos_tiles):
    n = x.shape[0]
    tile = TOKEN_TILE
    row = lambda width: pl.BlockSpec((tile, width), lambda i: (i, 0))
    return pl.pallas_call(
        functools.partial(_combine_kernel, tile=tile),
        grid=(n // tile,),
        in_specs=[pl.BlockSpec((2 * tile,), lambda i: (i,), memory_space=pltpu.SMEM),
                  row(D_MODEL), row(LANES), pl.BlockSpec(memory_space=pl.ANY)],
        out_specs=row(D_MODEL),
        out_shape=jax.ShapeDtypeStruct(x.shape, F32),
        scratch_shapes=[pltpu.VMEM((2, tile, D_MODEL), F32), pltpu.SemaphoreType.DMA((2,))],
        compiler_params=_params("arbitrary"),
        name="moe_combine",
    )(pos_tiles, x, route, y)


def _routing_tables(route, n):
    tm = MOE_ROW_TILE
    p = 2 * n + N_EXPERTS * tm
    expert = jnp.concatenate([route[:, 2], route[:, 3]]).astype(jnp.int32)
    onehot = (expert[:, None] == jnp.arange(N_EXPERTS, dtype=jnp.int32)[None, :]).astype(jnp.int32)
    running = jnp.cumsum(onehot, axis=0)
    rank = jnp.sum(running * onehot, axis=1) - 1
    counts = running[-1]
    padded = ((counts + tm - 1) // tm) * tm
    ends = jnp.cumsum(padded)
    starts = ends - padded
    pos = jnp.sum(onehot * starts[None, :], axis=1) + rank
    n_tiles = (ends[-1] // tm).astype(jnp.int32)
    tile_start = jnp.arange(p // tm, dtype=jnp.int32) * tm
    tile_expert = jnp.sum((tile_start[:, None] >= ends[None, :]).astype(jnp.int32), axis=1)
    last_expert = jnp.sum((ends[-1] - 1 >= ends).astype(jnp.int32))
    tile_expert = jnp.minimum(tile_expert, last_expert).astype(jnp.int32)
    tile = TOKEN_TILE
    tail = jnp.stack([ends[-1], (p - ends[-1]) // tile])
    pad_info = jnp.concatenate([starts + counts, padded - counts, tail]).astype(jnp.int32)
    pos_tiles = jnp.concatenate(
        [pos[:n].reshape(n // tile, tile), pos[n:].reshape(n // tile, tile)], axis=1).reshape(-1)
    return pos_tiles.astype(jnp.int32), pad_info, tile_expert, n_tiles.reshape(1), p


def _moe_layer(x, hn, route, w1, w3, w2):
    n = x.shape[0]
    pos_tiles, pad_info, tile_expert, n_tiles, n_sorted = _routing_tables(route, n)
    hs = _dispatch_call(hn, pos_tiles, pad_info, n_sorted)
    ys = _moe_call(hs, w1.astype(BF16), w3.astype(BF16), w2.astype(BF16), tile_expert, n_tiles)
    return _combine_call(x, route, ys, pos_tiles)


def kernel(x, attn_norm, ffn_norm, w_in, q_norm, k_norm, conv_w, hgrn_lower_bounds,
           hgrn_out_norm, w_proj_a, w_proj_b, w_proj_c, w_out, dense_w1, dense_w3,
           dense_w2, router_w, moe_w1, moe_w3, moe_w2):
    batch, seq, d = x.shape
    n = batch * seq
    depth = w_in.shape[0]
    assert d == D_MODEL and seq % ATTN_TILE == 0 and seq % TOKEN_TILE == 0
    assert all(window // dilation == BAND for window, dilation in ATTN_GROUPS)

    lbs = jax.nn.softmax(hgrn_lower_bounds.astype(F32), axis=0)
    lower_bound = jnp.cumsum(lbs, axis=0) - lbs[0]
    head_of_col = jnp.arange(ATTN_WIDTH) // HEAD_DIM
    hsum = (head_of_col[:, None] == jnp.arange(LANES)[None, :]).astype(BF16)
    hexp = jnp.concatenate([hsum.T, hsum.T], axis=0)
    a0, a1 = 0, 3 * ATTN_WIDTH
    b1 = a1 + 3 * CONV_WIDTH
    c1 = b1 + 4 * HGRN_WIDTH

    xf = x.reshape(n, d)
    for layer in range(depth):
        w = w_in[layer]
        w_attn = w[:, a0:a1].astype(BF16)
        w_hgrn = w[:, b1:c1].astype(BF16)
        w_conv_gate = jnp.concatenate([w[:, a1:b1], w[:, c1:]], axis=1).astype(BF16)
        gain = attn_norm[layer].reshape(1, d)
        heads = ATTN_WIDTH // HEAD_DIM
        qg = jnp.tile(q_norm[layer] * (HEAD_DIM ** -0.5), heads).reshape(1, ATTN_WIDTH)
        kg = jnp.tile(k_norm[layer], heads).reshape(1, ATTN_WIDTH)

        qkv = _qkv_call(xf, gain, w_attn, qg, kg, hsum, hexp, batch, seq)
        oa = _attn_call(qkv, batch, seq)
        yc = _hgrn_call(xf, gain, w_hgrn, lower_bound[layer].reshape(1, HGRN_WIDTH),
                        hgrn_out_norm[layer].reshape(1, HGRN_HEAD),
                        w_proj_c[layer].astype(BF16), batch, seq)
        fgain = ffn_norm[layer].reshape(1, d)
        j = layer // 2
        routed = layer % 2 == 1
        router = None
        if routed:
            rw = jnp.pad(router_w[j], ((0, 0), (0, LANES - N_EXPERTS)))
            rw_hi = rw.astype(BF16)
            router = (fgain, rw_hi, (rw - rw_hi.astype(F32)).astype(BF16))
        merged = _merge_call(xf, gain, w_conv_gate, conv_w[layer], oa, yc,
                             w_proj_a[layer].astype(BF16), w_proj_b[layer].astype(BF16),
                             w_out[layer].astype(BF16), seq, router)
        if routed:
            xf, hn, route = merged
            xf = _moe_layer(xf, hn, route, moe_w1[j], moe_w3[j], moe_w2[j])
        else:
            xf = _ffn_call(merged, fgain, dense_w1[j].astype(BF16), dense_w3[j].astype(BF16),
                           dense_w2[j].astype(BF16))
    return xf.reshape(batch, seq, d)
```

```python
import functools

import jax
import jax.numpy as jnp
from jax import lax
from jax.experimental import pallas as pl
from jax.experimental.pallas import tpu as pltpu

F32 = jnp.float32
BF16 = jnp.bfloat16

D_MODEL = 1024
HEAD_DIM = 64
ATTN_GROUPS = ((128, 1), (512, 4), (2048, 16))
HEADS_PER_GROUP = 4
GROUP_WIDTH = HEADS_PER_GROUP * HEAD_DIM
ATTN_WIDTH = GROUP_WIDTH * len(ATTN_GROUPS)
BAND = 128
CONV_WIDTH = 768
CONV_K = 3
HGRN_WIDTH = 768
HGRN_HEAD = 128
HGRN_HEADS = HGRN_WIDTH // HGRN_HEAD
LB_FLOOR = 1e-30
N_EXPERTS = 8
RMS_EPS = 1e-6
NEG_BIG = -1e30
LOG2E = 1.4426950408889634

LANES = 128
SUBLANES = 8

TOKEN_TILE = 512
ATTN_TILE = ATTN_GROUPS[-1][1] * BAND
ATTN_UNROLL = 8
HGRN_CHUNK = 256
MOE_ROW_TILE = 512
MOE_FF_TILE = 1792
VMEM_LIMIT = 56 * 1024 * 1024


def _dot(a, b):
    return jnp.dot(a, b, preferred_element_type=F32)


def _dot_nt(a, b):
    return lax.dot_general(a, b, (((1,), (1,)), ((), ())), preferred_element_type=F32)


def _dot_tn(a, b):
    return lax.dot_general(a, b, (((0,), (0,)), ((), ())), preferred_element_type=F32)


def _rms_rows(xf, gain):
    ms = jnp.mean(xf * xf, axis=-1, keepdims=True)
    return xf * lax.rsqrt(ms + RMS_EPS) * gain


def _sigmoid(z):
    return 1.0 / (1.0 + jnp.exp(-z))


def _params(*sem):
    return pltpu.CompilerParams(dimension_semantics=sem, vmem_limit_bytes=VMEM_LIMIT)


def _full(shape):
    return pl.BlockSpec(shape, lambda *_: (0,) * len(shape))


def _qkv_kernel(x_ref, g_ref, w_ref, qg_ref, kg_ref, hsum_ref, hexp_ref, *rest, tile):
    out_refs, stage = rest[:9], rest[9]
    h = _rms_rows(x_ref[0], g_ref[...]).astype(BF16)
    p = _dot(h, w_ref[...])

    def head_norm(t, gain):
        ms = _dot((t * t).astype(BF16), hsum_ref[...]) * (1.0 / HEAD_DIM)
        r = lax.rsqrt(ms + RMS_EPS)
        r_hi = r.astype(BF16)
        r_lo = (r - r_hi.astype(F32)).astype(BF16)
        r_cols = _dot(jnp.concatenate([r_hi, r_lo], axis=1), hexp_ref[...])
        return t * r_cols * gain

    parts = (head_norm(p[:, :ATTN_WIDTH], qg_ref[...]),
             head_norm(p[:, ATTN_WIDTH:2 * ATTN_WIDTH], kg_ref[...]),
             p[:, 2 * ATTN_WIDTH:])
    slot = 0
    for a, part in enumerate(parts):
        for group, (_, dilation) in enumerate(ATTN_GROUPS):
            out = out_refs[3 * group + a]
            c0 = group * GROUP_WIDTH
            if dilation == 1:
                out[0, 0] = part[:, c0:c0 + GROUP_WIDTH].astype(BF16)
                continue
            rows = tile // dilation
            for half in range(GROUP_WIDTH // LANES):
                stage[slot] = part[:, c0 + half * LANES:c0 + (half + 1) * LANES]
                for r in range(dilation):
                    out[0, r, :, half * LANES:(half + 1) * LANES] = (
                        stage[slot, pl.ds(r, rows, stride=dilation), :].astype(BF16))
                slot += 1


def _qkv_call(x, gain, w, qg, kg, hsum, hexp, batch, seq):
    t = TOKEN_TILE
    out_specs, out_shape = [], []
    for _, dilation in ATTN_GROUPS:
        for _ in range(3):
            out_specs.append(pl.BlockSpec((1, dilation, t // dilation, GROUP_WIDTH),
                                          lambda b, i: (b, 0, i, 0)))
            out_shape.append(jax.ShapeDtypeStruct(
                (batch, dilation, seq // dilation, GROUP_WIDTH), BF16))
    n_stage = 3 * (GROUP_WIDTH // LANES) * sum(1 for _, d in ATTN_GROUPS if d > 1)
    return pl.pallas_call(
        functools.partial(_qkv_kernel, tile=t),
        grid=(batch, seq // t),
        in_specs=[pl.BlockSpec((1, t, D_MODEL), lambda b, i: (b, i, 0)), _full((1, D_MODEL)),
                  _full(w.shape), _full((1, ATTN_WIDTH)), _full((1, ATTN_WIDTH)),
                  _full(hsum.shape), _full(hexp.shape)],
        out_specs=out_specs,
        out_shape=out_shape,
        scratch_shapes=[pltpu.VMEM((n_stage, t, LANES), F32)],
        compiler_params=_params("parallel", "parallel"),
        name="qkv_proj",
    )(x.reshape(batch, seq, D_MODEL), gain, w, qg, kg, hsum, hexp)


def _attn_kernel(*refs, n_groups):
    ins, o_ref, scr = refs[:5 * n_groups], refs[5 * n_groups], refs[5 * n_groups + 1:]
    kbufs, vbufs = scr[0:n_groups], scr[n_groups:2 * n_groups]
    o_tok, l_tok, o_rm, l_rm = scr[2 * n_groups:]
    first_tile = pl.program_id(1) == 0
    qi = lax.broadcasted_iota(jnp.int32, (BAND, 2 * BAND), 0)
    kj = lax.broadcasted_iota(jnp.int32, (BAND, 2 * BAND), 1)
    rel = qi + BAND - kj
    in_window = (rel >= 0) & (rel <= BAND)
    halves = GROUP_WIDTH // LANES
    lane = lax.broadcasted_iota(jnp.int32, (1, LANES), 1)
    head_lanes = [jnp.where(lane // HEAD_DIM == side, 1.0, 0.0).astype(BF16)
                  for side in range(LANES // HEAD_DIM)]
    first_head = lax.broadcasted_iota(jnp.int32, (BAND, LANES), 1) < HEAD_DIM

    for group, (_, dilation) in enumerate(ATTN_GROUPS):
        q_ref, kc_ref, vc_ref, kp_ref, vp_ref = ins[5 * group:5 * group + 5]
        kbuf, vbuf = kbufs[group], vbufs[group]
        length = ATTN_TILE // dilation
        nj = length // BAND
        kbuf[:, 0:BAND, :] = kp_ref[0]
        kbuf[:, BAND:BAND + length, :] = kc_ref[0]
        vbuf[:, 0:BAND, :] = vp_ref[0]
        vbuf[:, BAND:BAND + length, :] = vc_ref[0]

        def block(idx, carry, q_ref=q_ref, kbuf=kbuf, vbuf=vbuf, nj=nj, length=length,
                  group=group, dilation=dilation):
            r, j = idx // nj, idx % nj
            j0 = pl.multiple_of(j * BAND, BAND)
            kmin = jnp.where(first_tile & (j == 0), BAND, 0)
            valid = in_window & (kj >= kmin)
            row0 = pl.multiple_of(r * length + j0, BAND)
            for half in range(halves):
                lanes = slice(half * LANES, (half + 1) * LANES)
                q2 = q_ref[0, r, pl.ds(j0, BAND), lanes]
                k2 = kbuf[r, pl.ds(j0, 2 * BAND), lanes]
                v2 = vbuf[r, pl.ds(j0, 2 * BAND), lanes]
                o2, l2 = None, None
                for side in range(LANES // HEAD_DIM):
                    s = _dot_nt(q2 * head_lanes[side], k2)
                    s = jnp.where(valid, s, NEG_BIG)
                    m = jnp.max(s, axis=-1, keepdims=True)
                    p = jnp.exp(s - m)
                    den = jnp.sum(p, axis=-1, keepdims=True)
                    o = _dot(p.astype(BF16), v2) * (1.0 / den)
                    lse = jnp.broadcast_to(m + jnp.log(den), (BAND, LANES))
                    o2 = o if side == 0 else jnp.where(first_head, o2, o)
                    l2 = lse if side == 0 else jnp.where(first_head, l2, lse)
                if dilation == 1:
                    o_tok[group, half, pl.ds(row0, BAND), :] = o2
                    l_tok[group, half, pl.ds(row0, BAND), :] = l2
                else:
                    o_rm[pl.ds(row0, BAND), lanes] = o2
                    l_rm[pl.ds(row0, BAND), lanes] = l2
            return carry

        lax.fori_loop(0, ATTN_TILE // BAND, block, 0, unroll=ATTN_UNROLL)
        if dilation > 1:
            for r in range(dilation):
                rows = slice(r * length, (r + 1) * length)
                for half in range(halves):
                    lanes = slice(half * LANES, (half + 1) * LANES)
                    o_tok[group, half, pl.ds(r, length, stride=dilation), :] = o_rm[rows, lanes]
                    l_tok[group, half, pl.ds(r, length, stride=dilation), :] = l_rm[rows, lanes]

    chunk = 2 * BAND

    def merge(c, carry):
        rows = pl.ds(pl.multiple_of(c * chunk, chunk), chunk)
        for half in range(halves):
            ls = [l_tok[g, half, rows, :] for g in range(n_groups)]
            top = functools.reduce(jnp.maximum, ls)
            es = [jnp.exp(l - top) for l in ls]
            acc = sum(e * o_tok[g, half, rows, :] for g, e in enumerate(es))
            o_ref[0, rows, half * LANES:(half + 1) * LANES] = (acc * (1.0 / sum(es))).astype(BF16)
        return carry

    lax.fori_loop(0, ATTN_TILE // chunk, merge, 0)


def _attn_call(qkv, batch, seq):
    n_groups = len(ATTN_GROUPS)
    in_specs, args, kbufs = [], [], []
    for group, (_, dilation) in enumerate(ATTN_GROUPS):
        length = ATTN_TILE // dilation
        cur = pl.BlockSpec((1, dilation, length, GROUP_WIDTH), lambda b, i: (b, 0, i, 0))
        halo = pl.BlockSpec(
            (1, dilation, BAND, GROUP_WIDTH),
            lambda b, i, nb=length // BAND: (b, 0, jnp.maximum(i * nb - 1, 0), 0))
        q, k, v = qkv[3 * group:3 * group + 3]
        in_specs += [cur, cur, cur, halo, halo]
        args += [q, k, v, k, v]
        kbufs.append(pltpu.VMEM((dilation, BAND + length, GROUP_WIDTH), BF16))
    halves = GROUP_WIDTH // LANES
    o = pl.pallas_call(
        functools.partial(_attn_kernel, n_groups=n_groups),
        grid=(batch, seq // ATTN_TILE),
        in_specs=in_specs,
        out_specs=pl.BlockSpec((1, ATTN_TILE, GROUP_WIDTH), lambda b, i: (b, i, 0)),
        out_shape=jax.ShapeDtypeStruct((batch, seq, GROUP_WIDTH), BF16),
        scratch_shapes=kbufs + kbufs + [
            pltpu.VMEM((n_groups, halves, ATTN_TILE, LANES), F32),
            pltpu.VMEM((n_groups, halves, ATTN_TILE, LANES), F32),
            pltpu.VMEM((ATTN_TILE, GROUP_WIDTH), F32),
            pltpu.VMEM((ATTN_TILE, GROUP_WIDTH), F32)],
        compiler_params=_params("parallel", "arbitrary"),
        name="band_attn",
    )(*args)
    return o.reshape(batch * seq, GROUP_WIDTH)


def _hgrn_kernel(x_ref, g_ref, w_ref, lb_ref, og_ref, wp_ref, tri_ref, y_ref,
                 state_ref, proj_ref, gated_ref, b_ref, *, tile, chunk):
    @pl.when(pl.program_id(1) == 0)
    def _():
        state_ref[...] = jnp.zeros_like(state_ref)

    h = _rms_rows(x_ref[0], g_ref[...]).astype(BF16)
    proj_ref[...] = _dot(h, w_ref[...])

    lb = lb_ref[...]
    lb_floor = jnp.maximum(lb, LB_FLOOR)
    one_minus_lb = 1.0 - lb
    floor_gap = lb_floor - lb
    row = lax.broadcasted_iota(jnp.int32, (chunk, HGRN_HEAD), 0)
    tt = lax.broadcasted_iota(jnp.int32, (chunk, chunk), 0)
    ss = lax.broadcasted_iota(jnp.int32, (chunk, chunk), 1)
    t_xor_s = tt ^ ss
    w = HGRN_WIDTH
    sizes = [1 << i for i in range(chunk.bit_length() - 1)]
    lower_rows = [(row & m) == 0 for m in sizes]
    lower_sel = [jnp.where(low, 1.0, 0.0).astype(BF16) for low in lower_rows]
    signed_log2e = [jnp.where(low, -LOG2E, LOG2E).astype(BF16) for low in lower_rows]
    level_pairs = [((t_xor_s & (-m)) == m) & ((tt & m) != 0) for m in sizes]
    diagonal = tt == ss

    def chunk_body(c, carry):
        r0 = pl.multiple_of(c * chunk, chunk)
        rows = pl.ds(r0, chunk)
        z = proj_ref[rows, w:2 * w]
        e = jnp.exp(-jnp.abs(z))
        inv = 1.0 / (1.0 + e)
        sig = jnp.where(z >= 0, inv, e * inv)
        nsig = jnp.where(z >= 0, e * inv, inv)
        log_f = jnp.log(lb_floor + one_minus_lb * sig)
        k_all = one_minus_lb * nsig - floor_gap
        lf_hi = log_f.astype(BF16)
        rest = log_f - lf_hi.astype(F32)
        lf_mid = rest.astype(BF16)
        lf_lo = (rest - lf_mid.astype(F32)).astype(BF16)
        tri = tri_ref[...]
        b_all = _dot(tri, lf_hi) + _dot(tri, lf_mid) + _dot(tri, lf_lo)
        b_ref[...] = b_all
        q_all = proj_ref[rows, 0:w].astype(BF16)
        q_all = q_all * _sigmoid(q_all)
        v_all = proj_ref[rows, 2 * w:3 * w]
        g_all = proj_ref[rows, 3 * w:4 * w].astype(BF16)
        g_all = g_all * _sigmoid(g_all)
        for hd in range(HGRN_HEADS):
            cols = slice(hd * HGRN_HEAD, (hd + 1) * HGRN_HEAD)
            b, qb = b_all[:, cols], q_all[:, cols]
            kb = k_all[:, cols].astype(BF16)
            vb = v_all[:, cols].astype(BF16)
            scores = jnp.where(diagonal, _dot_nt(qb, kb), 0.0)
            last_of_block = b
            for lvl, m in enumerate(sizes):
                if m < SUBLANES:
                    lower = lower_rows[lvl]
                    boundary = jnp.where(lower, last_of_block, pltpu.roll(last_of_block, m, 0))
                    if 2 * m < SUBLANES:
                        last_of_block = jnp.where(
                            lower, pltpu.roll(last_of_block, chunk - m, 0), last_of_block)
                else:
                    boundary = jnp.concatenate(
                        [jnp.broadcast_to(b_ref[r:r + 1, cols], (2 * m, HGRN_HEAD))
                         for r in range(m - 1, chunk, 2 * m)], axis=0)
                decay = jnp.exp2((b - boundary).astype(BF16) * signed_log2e[lvl])
                if m < 2 * SUBLANES:
                    source = jnp.where(lower_sel[lvl] > 0, kb, qb)
                else:
                    source = jnp.concatenate(
                        [(kb if (r // m) % 2 == 0 else qb)[r:r + m] for r in range(0, chunk, m)],
                        axis=0)
                side = source * decay
                scores = jnp.where(level_pairs[lvl], _dot_nt(side, side), scores)
            state_t = state_ref[hd]
            q_dec = qb * jnp.exp(b).astype(BF16)
            o = _dot(scores.astype(BF16), vb) + _dot_nt(q_dec, state_t.astype(BF16))
            b_last = b[chunk - 1:chunk, :]
            k_dec = kb * jnp.exp(b_last - b).astype(BF16)
            state_ref[hd] = state_t * jnp.exp(b_last) + _dot_tn(vb, k_dec)
            o = _rms_rows(o, og_ref[...]).astype(BF16) * g_all[:, cols]
            gated_ref[rows, cols] = o
        return carry

    lax.fori_loop(0, tile // chunk, chunk_body, 0, unroll=True)
    y_ref[0] = _dot(gated_ref[...], wp_ref[...]).astype(BF16)


def _hgrn_call(x, gain, w, lb, out_gain, w_proj, batch, seq):
    tile, chunk = TOKEN_TILE, HGRN_CHUNK
    tri = jnp.tril(jnp.ones((chunk, chunk), BF16))
    blk = lambda width: pl.BlockSpec((1, tile, width), lambda b, i: (b, i, 0))
    y = pl.pallas_call(
        functools.partial(_hgrn_kernel, tile=tile, chunk=chunk),
        grid=(batch, seq // tile),
        in_specs=[blk(D_MODEL), _full((1, D_MODEL)), _full(w.shape), _full((1, HGRN_WIDTH)),
                  _full((1, HGRN_HEAD)), _full(w_proj.shape), _full((chunk, chunk))],
        out_specs=blk(D_MODEL),
        out_shape=jax.ShapeDtypeStruct((batch, seq, D_MODEL), BF16),
        scratch_shapes=[pltpu.VMEM((HGRN_HEADS, HGRN_HEAD, HGRN_HEAD), F32),
                        pltpu.VMEM((tile, 4 * HGRN_WIDTH), F32),
                        pltpu.VMEM((tile, HGRN_WIDTH), BF16),
                        pltpu.VMEM((chunk, HGRN_WIDTH), F32)],
        compiler_params=_params("parallel", "arbitrary"),
        name="hgrn_branch",
    )(x.reshape(batch, seq, D_MODEL), gain, w, lb, out_gain, w_proj, tri)
    return y.reshape(batch * seq, D_MODEL)


def _top2_route(h, rw_hi, rw_lo):
    logits = _dot(h, rw_hi) + _dot(h, rw_lo)
    lane = lax.broadcasted_iota(jnp.int32, logits.shape, 1)
    logits = jnp.where(lane < N_EXPERTS, logits, -jnp.inf)
    m1 = jnp.max(logits, axis=-1, keepdims=True)
    i1 = jnp.min(jnp.where(logits == m1, lane, LANES), axis=-1, keepdims=True)
    rest = jnp.where(lane == i1, -jnp.inf, logits)
    m2 = jnp.max(rest, axis=-1, keepdims=True)
    i2 = jnp.min(jnp.where(rest == m2, lane, LANES), axis=-1, keepdims=True)
    e = jnp.exp(m2 - m1)
    g1 = 1.0 / (1.0 + e)
    g2 = e * g1
    return jnp.where(
        lane == 0, g1, jnp.where(lane == 1, g2, jnp.where(
            lane == 2, i1.astype(F32), jnp.where(lane == 3, i2.astype(F32), 0.0))))


def _merge_kernel(x_ref, g_ref, w_ref, cw_ref, oa_ref, yc_ref, wpa_ref, wpb_ref, wo_ref, *rest,
                  tile, tiles_per_seq, steps, with_router):
    if with_router:
        fg_ref, rw_hi_ref, rw_lo_ref, out_ref, hn_ref, route_ref, u_ext, prev_ref = rest
    else:
        out_ref, u_ext = rest
    step = pl.program_id(0)

    @pl.when(step % tiles_per_seq == 0)
    def _():
        u_ext[0:SUBLANES, :] = jnp.zeros((SUBLANES, CONV_WIDTH), F32)

    def route_previous_tile():
        hn = _rms_rows(prev_ref[...], fg_ref[...])
        hn_ref[...] = hn
        route_ref[...] = _top2_route(hn.astype(BF16), rw_hi_ref[...], rw_lo_ref[...])

    def merge_tile():
        x = x_ref[...]
        h = _rms_rows(x, g_ref[...]).astype(BF16)
        p = _dot(h, w_ref[...])
        cw = CONV_WIDTH
        u = p[:, 2 * cw:3 * cw] * p[:, 0:cw]
        u_ext[SUBLANES:SUBLANES + tile, :] = u
        conv = cw_ref[0:1, :] * u
        for tap in range(1, CONV_K):
            conv = conv + cw_ref[tap:tap + 1, :] * u_ext[SUBLANES - tap:SUBLANES - tap + tile, :]
        u_ext[0:SUBLANES, :] = u[tile - SUBLANES:, :]
        yb = _dot((p[:, cw:2 * cw] * conv).astype(BF16), wpb_ref[...])
        ya = _dot(oa_ref[...], wpa_ref[...])

        d = D_MODEL
        gates = p[:, 3 * cw:]
        merged = (_sigmoid(gates[:, 0:d]) * ya + _sigmoid(gates[:, d:2 * d]) * yb
                  + _sigmoid(gates[:, 2 * d:3 * d]) * yc_ref[...].astype(F32))
        x_new = x + _dot(merged.astype(BF16), wo_ref[...])
        out_ref[...] = x_new
        return x_new

    if not with_router:
        merge_tile()
        return

    @pl.when(step == 0)
    def _():
        prev_ref[...] = jnp.zeros_like(prev_ref)

    @pl.when(step < steps)
    def _():
        route_previous_tile()
        prev_ref[...] = merge_tile()

    @pl.when(step == steps)
    def _():
        route_previous_tile()


def _merge_call(x, gain, w, conv_w, oa, yc, wpa, wpb, wo, seq, router=None):
    n = x.shape[0]
    tile = TOKEN_TILE
    steps = n // tile
    row = lambda width: pl.BlockSpec((tile, width), lambda i: (jnp.minimum(i, steps - 1), 0))
    late = lambda width: pl.BlockSpec((tile, width), lambda i: (jnp.maximum(i - 1, 0), 0))
    in_specs = [row(D_MODEL), _full((1, D_MODEL)), _full(w.shape), _full(conv_w.shape),
                row(GROUP_WIDTH), row(D_MODEL), _full(wpa.shape), _full(wpb.shape),
                _full(wo.shape)]
    args = [x, gain, w, conv_w, oa, yc, wpa, wpb, wo]
    out_specs, out_shape = [row(D_MODEL)], [jax.ShapeDtypeStruct((n, D_MODEL), F32)]
    scratch = [pltpu.VMEM((SUBLANES + tile, CONV_WIDTH), F32)]
    if router is not None:
        in_specs += [_full(a.shape) for a in router]
        args += list(router)
        out_specs += [late(D_MODEL), late(LANES)]
        out_shape += [jax.ShapeDtypeStruct((n, D_MODEL), F32),
                      jax.ShapeDtypeStruct((n, LANES), F32)]
        scratch.append(pltpu.VMEM((tile, D_MODEL), F32))
    outs = pl.pallas_call(
        functools.partial(_merge_kernel, tile=tile, tiles_per_seq=seq // tile, steps=steps,
                          with_router=router is not None),
        grid=(steps + (router is not None),),
        in_specs=in_specs,
        out_specs=out_specs,
        out_shape=out_shape,
        scratch_shapes=scratch,
        compiler_params=_params("arbitrary"),
        name="merge_out",
    )(*args)
    return outs[0] if router is None else outs


def _ffn_kernel(x_ref, g_ref, w1_ref, w3_ref, w2_ref, out_ref):
    x = x_ref[...]
    h = _rms_rows(x, g_ref[...]).astype(BF16)
    a = _dot(h, w1_ref[...])
    z = (a * _sigmoid(a) * _dot(h, w3_ref[...])).astype(BF16)
    out_ref[...] = x + _dot(z, w2_ref[...])


def _ffn_call(x, gain, w1, w3, w2):
    n = x.shape[0]
    tile = TOKEN_TILE
    row = pl.BlockSpec((tile, D_MODEL), lambda i: (i, 0))
    return pl.pallas_call(
        _ffn_kernel,
        grid=(n // tile,),
        in_specs=[row, _full((1, D_MODEL)), _full(w1.shape), _full(w3.shape), _full(w2.shape)],
        out_specs=row,
        out_shape=jax.ShapeDtypeStruct((n, D_MODEL), F32),
        compiler_params=_params("parallel"),
        name="dense_ffn",
    )(x, gain, w1, w3, w2)


def _row_copy(src, dst, src_row, dst_row, sem):
    return pltpu.make_async_copy(src.at[pl.ds(src_row, 1), :], dst.at[pl.ds(dst_row, 1), :], sem)


def _dispatch_kernel(pad_ref, pos_ref, x_ref, out_hbm, zeros, sem, pad_sem, *, tile):
    @pl.when(pl.program_id(0) == 0)
    def _():
        zeros[...] = jnp.zeros_like(zeros)
        for e in range(N_EXPERTS):
            first, count = pad_ref[e], pad_ref[N_EXPERTS + e]

            def fill(j, carry, first=first):
                _row_copy(zeros, out_hbm, 0, first + j, pad_sem).start()
                return carry

            def drain(j, carry, first=first):
                _row_copy(zeros, out_hbm, 0, first + j, pad_sem).wait()
                return carry

            lax.fori_loop(0, count, fill, 0)
            lax.fori_loop(0, count, drain, 0)

        def tail_copy(k):
            start = pl.multiple_of(pad_ref[2 * N_EXPERTS] + k * tile, tile)
            return pltpu.make_async_copy(zeros, out_hbm.at[pl.ds(start, tile), :], pad_sem)

        def tail_fill(k, carry):
            tail_copy(k).start()
            return carry

        def tail_drain(k, carry):
            tail_copy(k).wait()
            return carry

        lax.fori_loop(0, pad_ref[2 * N_EXPERTS + 1], tail_fill, 0)
        lax.fori_loop(0, pad_ref[2 * N_EXPERTS + 1], tail_drain, 0)

    def issue(j, carry):
        _row_copy(x_ref, out_hbm, j, pos_ref[j], sem).start()
        _row_copy(x_ref, out_hbm, j, pos_ref[tile + j], sem).start(priority=1)
        return carry

    lax.fori_loop(0, tile, issue, 0, unroll=8)
    for _ in range(2):
        pltpu.make_async_copy(x_ref, out_hbm.at[pl.ds(0, tile), :], sem).wait()


def _dispatch_call(x, pos_tiles, pad_info, n_sorted):
    n = x.shape[0]
    tile = TOKEN_TILE
    grid_spec = pltpu.PrefetchScalarGridSpec(
        num_scalar_prefetch=1,
        grid=(n // tile,),
        in_specs=[pl.BlockSpec((2 * tile,), lambda i, pad: (i,), memory_space=pltpu.SMEM),
                  pl.BlockSpec((tile, D_MODEL), lambda i, pad: (i, 0))],
        out_specs=pl.BlockSpec(memory_space=pl.ANY),
        scratch_shapes=[pltpu.VMEM((tile, D_MODEL), F32), pltpu.SemaphoreType.DMA(()),
                        pltpu.SemaphoreType.DMA(())],
    )
    return pl.pallas_call(
        functools.partial(_dispatch_kernel, tile=tile),
        grid_spec=grid_spec,
        out_shape=jax.ShapeDtypeStruct((n_sorted, D_MODEL), F32),
        compiler_params=_params("arbitrary"),
        name="moe_dispatch",
    )(pad_info, pos_tiles, x)


def _moe_kernel(tile_expert_ref, n_tiles_ref, h_ref, w1_ref, w3_ref, w2_ref, y_ref, acc_ref, *,
                ff_steps):
    i, f = pl.program_id(0), pl.program_id(1)
    last_f = ff_steps - 1
    active = i < n_tiles_ref[0]

    @pl.when(active)
    def _():
        h = h_ref[...].astype(BF16)
        a = _dot(h, w1_ref[...])
        z = (a * _sigmoid(a) * _dot(h, w3_ref[...])).astype(BF16)
        part = _dot(z, w2_ref[...])
        if ff_steps == 1:
            y_ref[...] = part
            return

        @pl.when(f == 0)
        def _():
            acc_ref[...] = part

        @pl.when((f > 0) & (f < last_f))
        def _():
            acc_ref[...] += part

        @pl.when(f == last_f)
        def _():
            y_ref[...] = acc_ref[...] + part

    @pl.when(jnp.logical_not(active) & (f == last_f))
    def _():
        y_ref[...] = jnp.zeros_like(y_ref)


def _moe_call(hs, w1, w3, w2, tile_expert, n_tiles):
    p = hs.shape[0]
    d_ff = w1.shape[-1]
    tm, tf = MOE_ROW_TILE, MOE_FF_TILE
    nf = d_ff // tf

    def ff_block(i, f, n_tiles_ref):
        return jnp.where(i < n_tiles_ref[0], f, nf - 1)

    grid_spec = pltpu.PrefetchScalarGridSpec(
        num_scalar_prefetch=2,
        grid=(p // tm, nf),
        in_specs=[
            pl.BlockSpec((tm, D_MODEL), lambda i, f, te, nt: (jnp.minimum(i, nt[0] - 1), 0)),
            pl.BlockSpec((None, D_MODEL, tf), lambda i, f, te, nt: (te[i], 0, ff_block(i, f, nt))),
            pl.BlockSpec((None, D_MODEL, tf), lambda i, f, te, nt: (te[i], 0, ff_block(i, f, nt))),
            pl.BlockSpec((None, tf, D_MODEL), lambda i, f, te, nt: (te[i], ff_block(i, f, nt), 0)),
        ],
        out_specs=pl.BlockSpec((tm, D_MODEL), lambda i, f, te, nt: (i, 0)),
        scratch_shapes=[pltpu.VMEM((tm, D_MODEL), F32)],
    )
    return pl.pallas_call(
        functools.partial(_moe_kernel, ff_steps=nf),
        grid_spec=grid_spec,
        out_shape=jax.ShapeDtypeStruct((p, D_MODEL), F32),
        compiler_params=_params("arbitrary", "arbitrary"),
        name="moe_experts",
    )(tile_expert, n_tiles, hs, w1, w3, w2)


def _combine_kernel(pos_ref, x_ref, route_ref, y_hbm, out_ref, buf, sems, *, tile):
    def issue(j, carry):
        _row_copy(y_hbm, buf.at[0], pos_ref[j], j, sems.at[0]).start()
        _row_copy(y_hbm, buf.at[1], pos_ref[tile + j], j, sems.at[1]).start(priority=1)
        return carry

    lax.fori_loop(0, tile, issue, 0, unroll=8)
    for k in range(2):
        pltpu.make_async_copy(y_hbm.at[pl.ds(0, tile), :], buf.at[k], sems.at[k]).wait()
    g1 = route_ref[:, 0:1]
    g2 = route_ref[:, 1:2]
    out_ref[...] = x_ref[...] + g1 * buf[0] + g2 * buf[1]


def _combine_call(x, route, y, pos_tiles):
    n = x.shape[0]
    tile = TOKEN_TILE
    row = lambda width: pl.BlockSpec((tile, width), lambda i: (i, 0))
    return pl.pallas_call(
        functools.partial(_combine_kernel, tile=tile),
        grid=(n // tile,),
        in_specs=[pl.BlockSpec((2 * tile,), lambda i: (i,), memory_space=pltpu.SMEM),
                  row(D_MODEL), row(LANES), pl.BlockSpec(memory_space=pl.ANY)],
        out_specs=row(D_MODEL),
        out_shape=jax.ShapeDtypeStruct(x.shape, F32),
        scratch_shapes=[pltpu.VMEM((2, tile, D_MODEL), F32), pltpu.SemaphoreType.DMA((2,))],
        compiler_params=_params("arbitrary"),
        name="moe_combine",
    )(pos_tiles, x, route, y)


def _routing_tables(route, n):
    tm = MOE_ROW_TILE
    p = 2 * n + N_EXPERTS * tm
    expert = jnp.concatenate([route[:, 2], route[:, 3]]).astype(jnp.int32)
    onehot = (expert[:, None] == jnp.arange(N_EXPERTS, dtype=jnp.int32)[None, :]).astype(jnp.int32)
    running = jnp.cumsum(onehot, axis=0)
    rank = jnp.sum(running * onehot, axis=1) - 1
    counts = running[-1]
    padded = ((counts + tm - 1) // tm) * tm
    ends = jnp.cumsum(padded)
    starts = ends - padded
    pos = jnp.sum(onehot * starts[None, :], axis=1) + rank
    n_tiles = (ends[-1] // tm).astype(jnp.int32)
    tile_start = jnp.arange(p // tm, dtype=jnp.int32) * tm
    tile_expert = jnp.sum((tile_start[:, None] >= ends[None, :]).astype(jnp.int32), axis=1)
    last_expert = jnp.sum((ends[-1] - 1 >= ends).astype(jnp.int32))
    tile_expert = jnp.minimum(tile_expert, last_expert).astype(jnp.int32)
    tile = TOKEN_TILE
    tail = jnp.stack([ends[-1], (p - ends[-1]) // tile])
    pad_info = jnp.concatenate([starts + counts, padded - counts, tail]).astype(jnp.int32)
    pos_tiles = jnp.concatenate(
        [pos[:n].reshape(n // tile, tile), pos[n:].reshape(n // tile, tile)], axis=1).reshape(-1)
    return pos_tiles.astype(jnp.int32), pad_info, tile_expert, n_tiles.reshape(1), p


def _moe_layer(x, hn, route, w1, w3, w2):
    n = x.shape[0]
    pos_tiles, pad_info, tile_expert, n_tiles, n_sorted = _routing_tables(route, n)
    hs = _dispatch_call(hn, pos_tiles, pad_info, n_sorted)
    ys = _moe_call(hs, w1.astype(BF16), w3.astype(BF16), w2.astype(BF16), tile_expert, n_tiles)
    return _combine_call(x, route, ys, pos_tiles)


def kernel(x, attn_norm, ffn_norm, w_in, q_norm, k_norm, conv_w, hgrn_lower_bounds,
           hgrn_out_norm, w_proj_a, w_proj_b, w_proj_c, w_out, dense_w1, dense_w3,
           dense_w2, router_w, moe_w1, moe_w3, moe_w2):
    batch, seq, d = x.shape
    n = batch * seq
    depth = w_in.shape[0]
    assert d == D_MODEL and seq % ATTN_TILE == 0 and seq % TOKEN_TILE == 0
    assert all(window // dilation == BAND for window, dilation in ATTN_GROUPS)

    lbs = jax.nn.softmax(hgrn_lower_bounds.astype(F32), axis=0)
    lower_bound = jnp.cumsum(lbs, axis=0) - lbs[0]
    head_of_col = jnp.arange(ATTN_WIDTH) // HEAD_DIM
    hsum = (head_of_col[:, None] == jnp.arange(LANES)[None, :]).astype(BF16)
    hexp = jnp.concatenate([hsum.T, hsum.T], axis=0)
    a0, a1 = 0, 3 * ATTN_WIDTH
    b1 = a1 + 3 * CONV_WIDTH
    c1 = b1 + 4 * HGRN_WIDTH

    xf = x.reshape(n, d)
    for layer in range(depth):
        w = w_in[layer]
        w_attn = w[:, a0:a1].astype(BF16)
        w_hgrn = w[:, b1:c1].astype(BF16)
        w_conv_gate = jnp.concatenate([w[:, a1:b1], w[:, c1:]], axis=1).astype(BF16)
        gain = attn_norm[layer].reshape(1, d)
        heads = ATTN_WIDTH // HEAD_DIM
        qg = jnp.tile(q_norm[layer] * (HEAD_DIM ** -0.5), heads).reshape(1, ATTN_WIDTH)
        kg = jnp.tile(k_norm[layer], heads).reshape(1, ATTN_WIDTH)

        qkv = _qkv_call(xf, gain, w_attn, qg, kg, hsum, hexp, batch, seq)
        oa = _attn_call(qkv, batch, seq)
        yc = _hgrn_call(xf, gain, w_hgrn, lower_bound[layer].reshape(1, HGRN_WIDTH),
                        hgrn_out_norm[layer].reshape(1, HGRN_HEAD),
                        w_proj_c[layer].astype(BF16), batch, seq)
        fgain = ffn_norm[layer].reshape(1, d)
        j = layer // 2
        routed = layer % 2 == 1
        router = None
        if routed:
            rw = jnp.pad(router_w[j], ((0, 0), (0, LANES - N_EXPERTS)))
            rw_hi = rw.astype(BF16)
            router = (fgain, rw_hi, (rw - rw_hi.astype(F32)).astype(BF16))
        merged = _merge_call(xf, gain, w_conv_gate, conv_w[layer], oa, yc,
                             w_proj_a[layer].astype(BF16), w_proj_b[layer].astype(BF16),
                             w_out[layer].astype(BF16), seq, router)
        if routed:
            xf, hn, route = merged
            xf = _moe_layer(xf, hn, route, moe_w1[j], moe_w3[j], moe_w2[j])
        else:
            xf = _ffn_call(merged, fgain, dense_w1[j].astype(BF16), dense_w3[j].astype(BF16),
                           dense_w2[j].astype(BF16))
    return xf.reshape(batch, seq, d)
```

```python
import functools

import jax
import jax.numpy as jnp
from jax import lax
from jax.experimental import pallas as pl
from jax.experimental.pallas import tpu as pltpu

F32 = jnp.float32
BF16 = jnp.bfloat16

D_MODEL = 1024
HEAD_DIM = 64
ATTN_GROUPS = ((128, 1), (512, 4), (2048, 16))
HEADS_PER_GROUP = 4
GROUP_WIDTH = HEADS_PER_GROUP * HEAD_DIM
ATTN_WIDTH = GROUP_WIDTH * len(ATTN_GROUPS)
BAND = 128
CONV_WIDTH = 768
CONV_K = 3
HGRN_WIDTH = 768
HGRN_HEAD = 128
HGRN_HEADS = HGRN_WIDTH // HGRN_HEAD
LB_FLOOR = 1e-30
N_EXPERTS = 8
RMS_EPS = 1e-6
NEG_BIG = -1e30
LOG2E = 1.4426950408889634

LANES = 128
SUBLANES = 8

TOKEN_TILE = 512
ATTN_TILE = ATTN_GROUPS[-1][1] * BAND
ATTN_UNROLL = 8
HGRN_CHUNK = 256
MOE_ROW_TILE = 512
MOE_FF_TILE = 1792
LOCAL_ROWS = -(-(2 * TOKEN_TILE + N_EXPERTS * SUBLANES) // LANES) * LANES
GROUP_BLOCKS = tuple(1 << b for b in reversed(range(8)))
FILL_BLOCKS = tuple(b for b in GROUP_BLOCKS if b * SUBLANES < MOE_ROW_TILE)
VMEM_LIMIT = 56 * 1024 * 1024


def _dot(a, b):
    return jnp.dot(a, b, preferred_element_type=F32)


def _dot_nt(a, b):
    return lax.dot_general(a, b, (((1,), (1,)), ((), ())), preferred_element_type=F32)


def _dot_tn(a, b):
    return lax.dot_general(a, b, (((0,), (0,)), ((), ())), preferred_element_type=F32)


def _rms_rows(xf, gain):
    ms = jnp.mean(xf * xf, axis=-1, keepdims=True)
    return xf * lax.rsqrt(ms + RMS_EPS) * gain


def _sigmoid(z):
    return 1.0 / (1.0 + jnp.exp(-z))


def _params(*sem):
    return pltpu.CompilerParams(dimension_semantics=sem, vmem_limit_bytes=VMEM_LIMIT)


def _full(shape):
    return pl.BlockSpec(shape, lambda *_: (0,) * len(shape))


def _qkv_kernel(x_ref, g_ref, w_ref, qg_ref, kg_ref, hsum_ref, hexp_ref, *rest, tile):
    out_refs, stage = rest[:9], rest[9]
    h = _rms_rows(x_ref[0], g_ref[...]).astype(BF16)
    p = _dot(h, w_ref[...])

    def head_norm(t, gain):
        ms = _dot((t * t).astype(BF16), hsum_ref[...]) * (1.0 / HEAD_DIM)
        r = lax.rsqrt(ms + RMS_EPS)
        r_hi = r.astype(BF16)
        r_lo = (r - r_hi.astype(F32)).astype(BF16)
        r_cols = _dot(jnp.concatenate([r_hi, r_lo], axis=1), hexp_ref[...])
        return t * r_cols * gain

    parts = (head_norm(p[:, :ATTN_WIDTH], qg_ref[...]),
             head_norm(p[:, ATTN_WIDTH:2 * ATTN_WIDTH], kg_ref[...]),
             p[:, 2 * ATTN_WIDTH:])
    slot = 0
    for a, part in enumerate(parts):
        for group, (_, dilation) in enumerate(ATTN_GROUPS):
            out = out_refs[3 * group + a]
            c0 = group * GROUP_WIDTH
            if dilation == 1:
                out[0, 0] = part[:, c0:c0 + GROUP_WIDTH].astype(BF16)
                continue
            rows = tile // dilation
            for half in range(GROUP_WIDTH // LANES):
                stage[slot] = part[:, c0 + half * LANES:c0 + (half + 1) * LANES]
                for r in range(dilation):
                    out[0, r, :, half * LANES:(half + 1) * LANES] = (
                        stage[slot, pl.ds(r, rows, stride=dilation), :].astype(BF16))
                slot += 1


def _qkv_call(x, gain, w, qg, kg, hsum, hexp, batch, seq):
    t = TOKEN_TILE
    out_specs, out_shape = [], []
    for _, dilation in ATTN_GROUPS:
        for _ in range(3):
            out_specs.append(pl.BlockSpec((1, dilation, t // dilation, GROUP_WIDTH),
                                          lambda b, i: (b, 0, i, 0)))
            out_shape.append(jax.ShapeDtypeStruct(
                (batch, dilation, seq // dilation, GROUP_WIDTH), BF16))
    n_stage = 3 * (GROUP_WIDTH // LANES) * sum(1 for _, d in ATTN_GROUPS if d > 1)
    return pl.pallas_call(
        functools.partial(_qkv_kernel, tile=t),
        grid=(batch, seq // t),
        in_specs=[pl.BlockSpec((1, t, D_MODEL), lambda b, i: (b, i, 0)), _full((1, D_MODEL)),
                  _full(w.shape), _full((1, ATTN_WIDTH)), _full((1, ATTN_WIDTH)),
                  _full(hsum.shape), _full(hexp.shape)],
        out_specs=out_specs,
        out_shape=out_shape,
        scratch_shapes=[pltpu.VMEM((n_stage, t, LANES), F32)],
        compiler_params=_params("parallel", "parallel"),
        name="qkv_proj",
    )(x.reshape(batch, seq, D_MODEL), gain, w, qg, kg, hsum, hexp)


def _attn_kernel(*refs, n_groups):
    ins, o_ref, scr = refs[:5 * n_groups], refs[5 * n_groups], refs[5 * n_groups + 1:]
    kbufs, vbufs = scr[0:n_groups], scr[n_groups:2 * n_groups]
    o_tok, l_tok, o_rm, l_rm = scr[2 * n_groups:]
    first_tile = pl.program_id(1) == 0
    qi = lax.broadcasted_iota(jnp.int32, (BAND, 2 * BAND), 0)
    kj = lax.broadcasted_iota(jnp.int32, (BAND, 2 * BAND), 1)
    rel = qi + BAND - kj
    in_window = (rel >= 0) & (rel <= BAND)
    halves = GROUP_WIDTH // LANES
    lane = lax.broadcasted_iota(jnp.int32, (1, LANES), 1)
    head_lanes = [jnp.where(lane // HEAD_DIM == side, 1.0, 0.0).astype(BF16)
                  for side in range(LANES // HEAD_DIM)]
    first_head = lax.broadcasted_iota(jnp.int32, (BAND, LANES), 1) < HEAD_DIM

    for group, (_, dilation) in enumerate(ATTN_GROUPS):
        q_ref, kc_ref, vc_ref, kp_ref, vp_ref = ins[5 * group:5 * group + 5]
        kbuf, vbuf = kbufs[group], vbufs[group]
        length = ATTN_TILE // dilation
        nj = length // BAND
        kbuf[:, 0:BAND, :] = kp_ref[0]
        kbuf[:, BAND:BAND + length, :] = kc_ref[0]
        vbuf[:, 0:BAND, :] = vp_ref[0]
        vbuf[:, BAND:BAND + length, :] = vc_ref[0]

        def block(idx, carry, q_ref=q_ref, kbuf=kbuf, vbuf=vbuf, nj=nj, length=length,
                  group=group, dilation=dilation):
            r, j = idx // nj, idx % nj
            j0 = pl.multiple_of(j * BAND, BAND)
            kmin = jnp.where(first_tile & (j == 0), BAND, 0)
            valid = in_window & (kj >= kmin)
            row0 = pl.multiple_of(r * length + j0, BAND)
            for half in range(halves):
                lanes = slice(half * LANES, (half + 1) * LANES)
                q2 = q_ref[0, r, pl.ds(j0, BAND), lanes]
                k2 = kbuf[r, pl.ds(j0, 2 * BAND), lanes]
                v2 = vbuf[r, pl.ds(j0, 2 * BAND), lanes]
                o2, l2 = None, None
                for side in range(LANES // HEAD_DIM):
                    s = _dot_nt(q2 * head_lanes[side], k2)
                    s = jnp.where(valid, s, NEG_BIG)
                    m = jnp.max(s, axis=-1, keepdims=True)
                    p = jnp.exp(s - m)
                    den = jnp.sum(p, axis=-1, keepdims=True)
                    o = _dot(p.astype(BF16), v2) * (1.0 / den)
                    lse = jnp.broadcast_to(m + jnp.log(den), (BAND, LANES))
                    o2 = o if side == 0 else jnp.where(first_head, o2, o)
                    l2 = lse if side == 0 else jnp.where(first_head, l2, lse)
                if dilation == 1:
                    o_tok[group, half, pl.ds(row0, BAND), :] = o2
                    l_tok[group, half, pl.ds(row0, BAND), :] = l2
                else:
                    o_rm[pl.ds(row0, BAND), lanes] = o2
                    l_rm[pl.ds(row0, BAND), lanes] = l2
            return carry

        lax.fori_loop(0, ATTN_TILE // BAND, block, 0, unroll=ATTN_UNROLL)
        if dilation > 1:
            for r in range(dilation):
                rows = slice(r * length, (r + 1) * length)
                for half in range(halves):
                    lanes = slice(half * LANES, (half + 1) * LANES)
                    o_tok[group, half, pl.ds(r, length, stride=dilation), :] = o_rm[rows, lanes]
                    l_tok[group, half, pl.ds(r, length, stride=dilation), :] = l_rm[rows, lanes]

    chunk = 2 * BAND

    def merge(c, carry):
        rows = pl.ds(pl.multiple_of(c * chunk, chunk), chunk)
        for half in range(halves):
            ls = [l_tok[g, half, rows, :] for g in range(n_groups)]
            top = functools.reduce(jnp.maximum, ls)
            es = [jnp.exp(l - top) for l in ls]
            acc = sum(e * o_tok[g, half, rows, :] for g, e in enumerate(es))
            o_ref[0, rows, half * LANES:(half + 1) * LANES] = (acc * (1.0 / sum(es))).astype(BF16)
        return carry

    lax.fori_loop(0, ATTN_TILE // chunk, merge, 0)


def _attn_call(qkv, batch, seq):
    n_groups = len(ATTN_GROUPS)
    in_specs, args, kbufs = [], [], []
    for group, (_, dilation) in enumerate(ATTN_GROUPS):
        length = ATTN_TILE // dilation
        cur = pl.BlockSpec((1, dilation, length, GROUP_WIDTH), lambda b, i: (b, 0, i, 0))
        halo = pl.BlockSpec(
            (1, dilation, BAND, GROUP_WIDTH),
            lambda b, i, nb=length // BAND: (b, 0, jnp.maximum(i * nb - 1, 0), 0))
        q, k, v = qkv[3 * group:3 * group + 3]
        in_specs += [cur, cur, cur, halo, halo]
        args += [q, k, v, k, v]
        kbufs.append(pltpu.VMEM((dilation, BAND + length, GROUP_WIDTH), BF16))
    halves = GROUP_WIDTH // LANES
    o = pl.pallas_call(
        functools.partial(_attn_kernel, n_groups=n_groups),
        grid=(batch, seq // ATTN_TILE),
        in_specs=in_specs,
        out_specs=pl.BlockSpec((1, ATTN_TILE, GROUP_WIDTH), lambda b, i: (b, i, 0)),
        out_shape=jax.ShapeDtypeStruct((batch, seq, GROUP_WIDTH), BF16),
        scratch_shapes=kbufs + kbufs + [
            pltpu.VMEM((n_groups, halves, ATTN_TILE, LANES), F32),
            pltpu.VMEM((n_groups, halves, ATTN_TILE, LANES), F32),
            pltpu.VMEM((ATTN_TILE, GROUP_WIDTH), F32),
            pltpu.VMEM((ATTN_TILE, GROUP_WIDTH), F32)],
        compiler_params=_params("parallel", "arbitrary"),
        name="band_attn",
    )(*args)
    return o.reshape(batch * seq, GROUP_WIDTH)


def _hgrn_kernel(x_ref, g_ref, w_ref, lb_ref, og_ref, wp_ref, tri_ref, y_ref,
                 state_ref, proj_ref, gated_ref, b_ref, *, tile, chunk):
    @pl.when(pl.program_id(1) == 0)
    def _():
        state_ref[...] = jnp.zeros_like(state_ref)

    h = _rms_rows(x_ref[0], g_ref[...]).astype(BF16)
    proj_ref[...] = _dot(h, w_ref[...])

    lb = lb_ref[...]
    lb_floor = jnp.maximum(lb, LB_FLOOR)
    one_minus_lb = 1.0 - lb
    floor_gap = lb_floor - lb
    row = lax.broadcasted_iota(jnp.int32, (chunk, HGRN_HEAD), 0)
    tt = lax.broadcasted_iota(jnp.int32, (chunk, chunk), 0)
    ss = lax.broadcasted_iota(jnp.int32, (chunk, chunk), 1)
    t_xor_s = tt ^ ss
    w = HGRN_WIDTH
    sizes = [1 << i for i in range(chunk.bit_length() - 1)]
    lower_rows = [(row & m) == 0 for m in sizes]
    lower_sel = [jnp.where(low, 1.0, 0.0).astype(BF16) for low in lower_rows]
    signed_log2e = [jnp.where(low, -LOG2E, LOG2E).astype(BF16) for low in lower_rows]
    level_pairs = [((t_xor_s & (-m)) == m) & ((tt & m) != 0) for m in sizes]
    diagonal = tt == ss

    def chunk_body(c, carry):
        r0 = pl.multiple_of(c * chunk, chunk)
        rows = pl.ds(r0, chunk)
        z = proj_ref[rows, w:2 * w]
        e = jnp.exp(-jnp.abs(z))
        inv = 1.0 / (1.0 + e)
        sig = jnp.where(z >= 0, inv, e * inv)
        nsig = jnp.where(z >= 0, e * inv, inv)
        log_f = jnp.log(lb_floor + one_minus_lb * sig)
        k_all = one_minus_lb * nsig - floor_gap
        lf_hi = log_f.astype(BF16)
        rest = log_f - lf_hi.astype(F32)
        lf_mid = rest.astype(BF16)
        lf_lo = (rest - lf_mid.astype(F32)).astype(BF16)
        tri = tri_ref[...]
        b_all = _dot(tri, lf_hi) + _dot(tri, lf_mid) + _dot(tri, lf_lo)
        b_ref[...] = b_all
        q_all = proj_ref[rows, 0:w].astype(BF16)
        q_all = q_all * _sigmoid(q_all)
        v_all = proj_ref[rows, 2 * w:3 * w]
        g_all = proj_ref[rows, 3 * w:4 * w].astype(BF16)
        g_all = g_all * _sigmoid(g_all)
        for hd in range(HGRN_HEADS):
            cols = slice(hd * HGRN_HEAD, (hd + 1) * HGRN_HEAD)
            b, qb = b_all[:, cols], q_all[:, cols]
            kb = k_all[:, cols].astype(BF16)
            vb = v_all[:, cols].astype(BF16)
            scores = jnp.where(diagonal, _dot_nt(qb, kb), 0.0)
            last_of_block = b
            for lvl, m in enumerate(sizes):
                if m < SUBLANES:
                    lower = lower_rows[lvl]
                    boundary = jnp.where(lower, last_of_block, pltpu.roll(last_of_block, m, 0))
                    if 2 * m < SUBLANES:
                        last_of_block = jnp.where(
                            lower, pltpu.roll(last_of_block, chunk - m, 0), last_of_block)
                else:
                    boundary = jnp.concatenate(
                        [jnp.broadcast_to(b_ref[r:r + 1, cols], (2 * m, HGRN_HEAD))
                         for r in range(m - 1, chunk, 2 * m)], axis=0)
                decay = jnp.exp2((b - boundary).astype(BF16) * signed_log2e[lvl])
                if m < 2 * SUBLANES:
                    source = jnp.where(lower_sel[lvl] > 0, kb, qb)
                else:
                    source = jnp.concatenate(
                        [(kb if (r // m) % 2 == 0 else qb)[r:r + m] for r in range(0, chunk, m)],
                        axis=0)
                side = source * decay
                scores = jnp.where(level_pairs[lvl], _dot_nt(side, side), scores)
            state_t = state_ref[hd]
            q_dec = qb * jnp.exp(b).astype(BF16)
            o = _dot(scores.astype(BF16), vb) + _dot_nt(q_dec, state_t.astype(BF16))
            b_last = b[chunk - 1:chunk, :]
            k_dec = kb * jnp.exp(b_last - b).astype(BF16)
            state_ref[hd] = state_t * jnp.exp(b_last) + _dot_tn(vb, k_dec)
            o = _rms_rows(o, og_ref[...]).astype(BF16) * g_all[:, cols]
            gated_ref[rows, cols] = o
        return carry

    lax.fori_loop(0, tile // chunk, chunk_body, 0, unroll=True)
    y_ref[0] = _dot(gated_ref[...], wp_ref[...]).astype(BF16)


def _hgrn_call(x, gain, w, lb, out_gain, w_proj, batch, seq):
    tile, chunk = TOKEN_TILE, HGRN_CHUNK
    tri = jnp.tril(jnp.ones((chunk, chunk), BF16))
    blk = lambda width: pl.BlockSpec((1, tile, width), lambda b, i: (b, i, 0))
    y = pl.pallas_call(
        functools.partial(_hgrn_kernel, tile=tile, chunk=chunk),
        grid=(batch, seq // tile),
        in_specs=[blk(D_MODEL), _full((1, D_MODEL)), _full(w.shape), _full((1, HGRN_WIDTH)),
                  _full((1, HGRN_HEAD)), _full(w_proj.shape), _full((chunk, chunk))],
        out_specs=blk(D_MODEL),
        out_shape=jax.ShapeDtypeStruct((batch, seq, D_MODEL), BF16),
        scratch_shapes=[pltpu.VMEM((HGRN_HEADS, HGRN_HEAD, HGRN_HEAD), F32),
                        pltpu.VMEM((tile, 4 * HGRN_WIDTH), F32),
                        pltpu.VMEM((tile, HGRN_WIDTH), BF16),
                        pltpu.VMEM((chunk, HGRN_WIDTH), F32)],
        compiler_params=_params("parallel", "arbitrary"),
        name="hgrn_branch",
    )(x.reshape(batch, seq, D_MODEL), gain, w, lb, out_gain, w_proj, tri)
    return y.reshape(batch * seq, D_MODEL)


def _top2_route(h, rw_hi, rw_lo):
    logits = _dot(h, rw_hi) + _dot(h, rw_lo)
    lane = lax.broadcasted_iota(jnp.int32, logits.shape, 1)
    logits = jnp.where(lane < N_EXPERTS, logits, -jnp.inf)
    m1 = jnp.max(logits, axis=-1, keepdims=True)
    i1 = jnp.min(jnp.where(logits == m1, lane, LANES), axis=-1, keepdims=True)
    rest = jnp.where(lane == i1, -jnp.inf, logits)
    m2 = jnp.max(rest, axis=-1, keepdims=True)
    i2 = jnp.min(jnp.where(rest == m2, lane, LANES), axis=-1, keepdims=True)
    e = jnp.exp(m2 - m1)
    g1 = 1.0 / (1.0 + e)
    g2 = e * g1
    return jnp.where(
        lane == 0, g1, jnp.where(lane == 1, g2, jnp.where(
            lane == 2, i1.astype(F32), jnp.where(lane == 3, i2.astype(F32), 0.0))))


def _merge_kernel(x_ref, g_ref, w_ref, cw_ref, oa_ref, yc_ref, wpa_ref, wpb_ref, wo_ref, *rest,
                  tile, tiles_per_seq, steps, with_router):
    if with_router:
        fg_ref, rw_hi_ref, rw_lo_ref, out_ref, hn_ref, route_ref, u_ext, prev_ref = rest
    else:
        out_ref, u_ext = rest
    step = pl.program_id(0)

    @pl.when(step % tiles_per_seq == 0)
    def _():
        u_ext[0:SUBLANES, :] = jnp.zeros((SUBLANES, CONV_WIDTH), F32)

    def route_previous_tile():
        hn = _rms_rows(prev_ref[...], fg_ref[...])
        hn_ref[...] = hn
        route_ref[...] = _top2_route(hn.astype(BF16), rw_hi_ref[...], rw_lo_ref[...])

    def merge_tile():
        x = x_ref[...]
        h = _rms_rows(x, g_ref[...]).astype(BF16)
        p = _dot(h, w_ref[...])
        cw = CONV_WIDTH
        u = p[:, 2 * cw:3 * cw] * p[:, 0:cw]
        u_ext[SUBLANES:SUBLANES + tile, :] = u
        conv = cw_ref[0:1, :] * u
        for tap in range(1, CONV_K):
            conv = conv + cw_ref[tap:tap + 1, :] * u_ext[SUBLANES - tap:SUBLANES - tap + tile, :]
        u_ext[0:SUBLANES, :] = u[tile - SUBLANES:, :]
        yb = _dot((p[:, cw:2 * cw] * conv).astype(BF16), wpb_ref[...])
        ya = _dot(oa_ref[...], wpa_ref[...])

        d = D_MODEL
        gates = p[:, 3 * cw:]
        merged = (_sigmoid(gates[:, 0:d]) * ya + _sigmoid(gates[:, d:2 * d]) * yb
                  + _sigmoid(gates[:, 2 * d:3 * d]) * yc_ref[...].astype(F32))
        x_new = x + _dot(merged.astype(BF16), wo_ref[...])
        out_ref[...] = x_new
        return x_new

    if not with_router:
        merge_tile()
        return

    @pl.when(step == 0)
    def _():
        prev_ref[...] = jnp.zeros_like(prev_ref)

    @pl.when(step < steps)
    def _():
        route_previous_tile()
        prev_ref[...] = merge_tile()

    @pl.when(step == steps)
    def _():
        route_previous_tile()


def _merge_call(x, gain, w, conv_w, oa, yc, wpa, wpb, wo, seq, router=None):
    n = x.shape[0]
    tile = TOKEN_TILE
    steps = n // tile
    row = lambda width: pl.BlockSpec((tile, width), lambda i: (jnp.minimum(i, steps - 1), 0))
    late = lambda width: pl.BlockSpec((tile, width), lambda i: (jnp.maximum(i - 1, 0), 0))
    in_specs = [row(D_MODEL), _full((1, D_MODEL)), _full(w.shape), _full(conv_w.shape),
                row(GROUP_WIDTH), row(D_MODEL), _full(wpa.shape), _full(wpb.shape),
                _full(wo.shape)]
    args = [x, gain, w, conv_w, oa, yc, wpa, wpb, wo]
    out_specs, out_shape = [row(D_MODEL)], [jax.ShapeDtypeStruct((n, D_MODEL), F32)]
    scratch = [pltpu.VMEM((SUBLANES + tile, CONV_WIDTH), F32)]
    if router is not None:
        in_specs += [_full(a.shape) for a in router]
        args += list(router)
        out_specs += [late(D_MODEL), late(LANES)]
        out_shape += [jax.ShapeDtypeStruct((n, D_MODEL), F32),
                      jax.ShapeDtypeStruct((n, LANES), F32)]
        scratch.append(pltpu.VMEM((tile, D_MODEL), F32))
    outs = pl.pallas_call(
        functools.partial(_merge_kernel, tile=tile, tiles_per_seq=seq // tile, steps=steps,
                          with_router=router is not None),
        grid=(steps + (router is not None),),
        in_specs=in_specs,
        out_specs=out_specs,
        out_shape=out_shape,
        scratch_shapes=scratch,
        compiler_params=_params("arbitrary"),
        name="merge_out",
    )(*args)
    return outs[0] if router is None else outs


def _ffn_kernel(x_ref, g_ref, w1_ref, w3_ref, w2_ref, out_ref):
    x = x_ref[...]
    h = _rms_rows(x, g_ref[...]).astype(BF16)
    a = _dot(h, w1_ref[...])
    z = (a * _sigmoid(a) * _dot(h, w3_ref[...])).astype(BF16)
    out_ref[...] = x + _dot(z, w2_ref[...])


def _ffn_call(x, gain, w1, w3, w2):
    n = x.shape[0]
    tile = TOKEN_TILE
    row = pl.BlockSpec((tile, D_MODEL), lambda i: (i, 0))
    return pl.pallas_call(
        _ffn_kernel,
        grid=(n // tile,),
        in_specs=[row, _full((1, D_MODEL)), _full(w1.shape), _full(w3.shape), _full(w2.shape)],
        out_specs=row,
        out_shape=jax.ShapeDtypeStruct((n, D_MODEL), F32),
        compiler_params=_params("parallel"),
        name="dense_ffn",
    )(x, gain, w1, w3, w2)


def _block_copies(n8, issue, blocks=GROUP_BLOCKS):
    done = jnp.int32(0)
    for bit in blocks:
        take = (n8 & bit) != 0

        @pl.when(take)
        def _(done=done, bit=bit):
            issue(pl.multiple_of(done * SUBLANES, SUBLANES), bit * SUBLANES)

        done = done + jnp.where(take, bit, 0)


def _slot_positions(route, tri, local_base):
    expert = jnp.concatenate([route[:, 2:3], route[:, 3:4]], axis=0)
    lane = lax.broadcasted_iota(jnp.int32, (expert.shape[0], LANES), 1).astype(F32)
    onehot = expert == lane
    seen = _dot(tri, jnp.where(onehot, 1.0, 0.0).astype(BF16))
    pos = jnp.sum(jnp.where(onehot, seen - 1.0 + local_base, 0.0), axis=1, keepdims=True)
    t = route.shape[0]
    return pos[:t], pos[t:]


def _local_base_row(plan_ref, tile_index):
    lane = lax.broadcasted_iota(jnp.int32, (1, LANES), 1)
    row = jnp.zeros((1, LANES), F32)
    for e in range(N_EXPERTS):
        row = jnp.where(lane == e, plan_ref[tile_index * N_EXPERTS + e].astype(F32), row)
    return row


def _dispatch_kernel(plan_ref, hn_ref, route_ref, tri_ref, out_hbm, lpos_ref, sorted_ref, zeros,
                     sem, fill_sem, *, tile, n_groups):
    i = pl.program_id(0)
    fill0 = 3 * n_groups

    def fills(wait):
        for e in range(N_EXPERTS):
            first = pl.multiple_of(plan_ref[fill0 + e], SUBLANES)

            def issue(off, rows, first=first):
                cp = pltpu.make_async_copy(zeros.at[pl.ds(0, rows), :],
                                           out_hbm.at[pl.ds(first + off, rows), :], fill_sem)
                cp.wait() if wait else cp.start()

            _block_copies(plan_ref[fill0 + N_EXPERTS + e], issue, FILL_BLOCKS)

        def tail(k, carry):
            start = pl.multiple_of(plan_ref[fill0 + 2 * N_EXPERTS] + k * tile, tile)
            cp = pltpu.make_async_copy(zeros, out_hbm.at[pl.ds(start, tile), :], fill_sem)
            cp.wait() if wait else cp.start()
            return carry

        lax.fori_loop(0, plan_ref[fill0 + 2 * N_EXPERTS + 1], tail, 0)

    @pl.when(i == 0)
    def _():
        zeros[...] = jnp.zeros_like(zeros)
        fills(wait=False)
        fills(wait=True)

    pos1, pos2 = _slot_positions(route_ref[...], tri_ref[...], _local_base_row(plan_ref, i))
    lane = lax.broadcasted_iota(jnp.int32, (tile, LANES), 1)
    lpos_ref[...] = jnp.where(lane == 0, pos1, jnp.where(lane == 1, pos2, 0.0))
    col = lax.broadcasted_iota(jnp.int32, (tile, LOCAL_ROWS), 1).astype(F32)
    select = jnp.where((pos1 == col) | (pos2 == col), 1.0, 0.0).astype(BF16)
    sorted_rows = _dot_tn(select, hn_ref[...].astype(BF16))

    def groups(step, wait):
        for e in range(N_EXPERTS):
            g = step * N_EXPERTS + e
            local = pl.multiple_of(plan_ref[g], SUBLANES)
            dest = pl.multiple_of(plan_ref[2 * n_groups + g], SUBLANES)

            def issue(off, rows, local=local, dest=dest):
                cp = pltpu.make_async_copy(sorted_ref.at[pl.ds(local + off, rows), :],
                                           out_hbm.at[pl.ds(dest + off, rows), :], sem)
                cp.wait() if wait else cp.start()

            _block_copies(plan_ref[n_groups + g], issue)

    @pl.when(i > 0)
    def _():
        groups(i - 1, wait=True)

    sorted_ref[...] = sorted_rows
    groups(i, wait=False)

    @pl.when(i == pl.num_programs(0) - 1)
    def _():
        groups(i, wait=True)


def _dispatch_call(hn, route, plan, tri, n_sorted):
    n = hn.shape[0]
    tile = TOKEN_TILE
    n_groups = (n // tile) * N_EXPERTS
    row = lambda width: pl.BlockSpec((tile, width), lambda i, plan: (i, 0))
    grid_spec = pltpu.PrefetchScalarGridSpec(
        num_scalar_prefetch=1,
        grid=(n // tile,),
        in_specs=[row(D_MODEL), row(LANES),
                  pl.BlockSpec(tri.shape, lambda i, plan: (0, 0))],
        out_specs=[pl.BlockSpec(memory_space=pl.ANY), row(LANES)],
        scratch_shapes=[pltpu.VMEM((LOCAL_ROWS, D_MODEL), F32), pltpu.VMEM((tile, D_MODEL), F32),
                        pltpu.SemaphoreType.DMA(()), pltpu.SemaphoreType.DMA(())],
    )
    return pl.pallas_call(
        functools.partial(_dispatch_kernel, tile=tile, n_groups=n_groups),
        grid_spec=grid_spec,
        out_shape=[jax.ShapeDtypeStruct((n_sorted, D_MODEL), F32),
                   jax.ShapeDtypeStruct((n, LANES), F32)],
        compiler_params=_params("arbitrary"),
        name="moe_dispatch",
    )(plan, hn, route, tri)


def _moe_kernel(tile_expert_ref, n_tiles_ref, h_ref, w1_ref, w3_ref, w2_ref, y_ref, acc_ref, *,
                ff_steps):
    i, f = pl.program_id(0), pl.program_id(1)
    last_f = ff_steps - 1
    active = i < n_tiles_ref[0]

    @pl.when(active)
    def _():
        h = h_ref[...].astype(BF16)
        a = _dot(h, w1_ref[...])
        z = (a * _sigmoid(a) * _dot(h, w3_ref[...])).astype(BF16)
        part = _dot(z, w2_ref[...])
        if ff_steps == 1:
            y_ref[...] = part
            return

        @pl.when(f == 0)
        def _():
            acc_ref[...] = part

        @pl.when((f > 0) & (f < last_f))
        def _():
            acc_ref[...] += part

        @pl.when(f == last_f)
        def _():
            y_ref[...] = acc_ref[...] + part

    @pl.when(jnp.logical_not(active) & (f == last_f))
    def _():
        y_ref[...] = jnp.zeros_like(y_ref)


def _moe_call(hs, w1, w3, w2, tile_expert, n_tiles):
    p = hs.shape[0]
    d_ff = w1.shape[-1]
    tm, tf = MOE_ROW_TILE, MOE_FF_TILE
    nf = d_ff // tf

    def ff_block(i, f, n_tiles_ref):
        return jnp.where(i < n_tiles_ref[0], f, nf - 1)

    grid_spec = pltpu.PrefetchScalarGridSpec(
        num_scalar_prefetch=2,
        grid=(p // tm, nf),
        in_specs=[
            pl.BlockSpec((tm, D_MODEL), lambda i, f, te, nt: (jnp.minimum(i, nt[0] - 1), 0)),
            pl.BlockSpec((None, D_MODEL, tf), lambda i, f, te, nt: (te[i], 0, ff_block(i, f, nt))),
            pl.BlockSpec((None, D_MODEL, tf), lambda i, f, te, nt: (te[i], 0, ff_block(i, f, nt))),
            pl.BlockSpec((None, tf, D_MODEL), lambda i, f, te, nt: (te[i], ff_block(i, f, nt), 0)),
        ],
        out_specs=pl.BlockSpec((tm, D_MODEL), lambda i, f, te, nt: (i, 0)),
        scratch_shapes=[pltpu.VMEM((tm, D_MODEL), F32)],
    )
    return pl.pallas_call(
        functools.partial(_moe_kernel, ff_steps=nf),
        grid_spec=grid_spec,
        out_shape=jax.ShapeDtypeStruct((p, D_MODEL), F32),
        compiler_params=_params("arbitrary", "arbitrary"),
        name="moe_experts",
    )(tile_expert, n_tiles, hs, w1, w3, w2)


def _combine_kernel(plan_ref, x_ref, route_ref, lpos_ref, y_hbm, out_ref, ybuf, sem, *,
                    tile, n_groups):
    i = pl.program_id(0)

    @pl.when(i == 0)
    def _():
        ybuf[...] = jnp.zeros_like(ybuf)

    def groups(wait):
        for e in range(N_EXPERTS):
            g = i * N_EXPERTS + e
            local = pl.multiple_of(plan_ref[g], SUBLANES)
            src = pl.multiple_of(plan_ref[2 * n_groups + g], SUBLANES)

            def issue(off, rows, local=local, src=src):
                cp = pltpu.make_async_copy(y_hbm.at[pl.ds(src + off, rows), :],
                                           ybuf.at[pl.ds(local + off, rows), :], sem)
                cp.wait() if wait else cp.start()

            _block_copies(plan_ref[n_groups + g], issue)

    groups(wait=False)
    col = lax.broadcasted_iota(jnp.int32, (tile, LOCAL_ROWS), 1).astype(F32)
    pick1 = jnp.where(lpos_ref[:, 0:1] == col, 1.0, 0.0).astype(BF16)
    pick2 = jnp.where(lpos_ref[:, 1:2] == col, 1.0, 0.0).astype(BF16)
    groups(wait=True)
    last = i * N_EXPERTS + N_EXPERTS - 1
    used = plan_ref[last] + plan_ref[n_groups + last] * SUBLANES
    row = lax.broadcasted_iota(jnp.int32, (LOCAL_ROWS, 1), 0)
    y = jnp.where(row < used, ybuf[...], 0.0).astype(BF16)
    out_ref[...] = (x_ref[...] + route_ref[:, 0:1] * _dot(pick1, y)
                    + route_ref[:, 1:2] * _dot(pick2, y))


def _combine_call(x, route, lpos, y, plan):
    n = x.shape[0]
    tile = TOKEN_TILE
    n_groups = (n // tile) * N_EXPERTS
    row = lambda width: pl.BlockSpec((tile, width), lambda i, plan: (i, 0))
    grid_spec = pltpu.PrefetchScalarGridSpec(
        num_scalar_prefetch=1,
        grid=(n // tile,),
        in_specs=[row(D_MODEL), row(LANES), row(LANES), pl.BlockSpec(memory_space=pl.ANY)],
        out_specs=row(D_MODEL),
        scratch_shapes=[pltpu.VMEM((LOCAL_ROWS, D_MODEL), F32), pltpu.SemaphoreType.DMA(())],
    )
    return pl.pallas_call(
        functools.partial(_combine_kernel, tile=tile, n_groups=n_groups),
        grid_spec=grid_spec,
        out_shape=jax.ShapeDtypeStruct(x.shape, F32),
        compiler_params=_params("arbitrary"),
        name="moe_combine",
    )(plan, x, route, lpos, y)


def _routing_plan(route, n):
    tile, tm = TOKEN_TILE, MOE_ROW_TILE
    n_tiles_tok = n // tile
    slack = -(-(n_tiles_tok * N_EXPERTS * SUBLANES) // tm) * tm
    p = 2 * n + slack + N_EXPERTS * tm
    experts = jnp.arange(N_EXPERTS, dtype=jnp.int32)
    choice = route[:, 2:4].astype(jnp.int32).reshape(n_tiles_tok, tile, 2)
    counts = jnp.sum((choice[..., None] == experts).astype(jnp.int32), axis=(1, 2))
    size8 = (counts + SUBLANES - 1) // SUBLANES
    size = size8 * SUBLANES
    local = jnp.cumsum(size, axis=1) - size
    total = jnp.sum(size, axis=0)
    padded = ((total + tm - 1) // tm) * tm
    ends = jnp.cumsum(padded)
    starts = ends - padded
    dest = starts[None, :] + jnp.cumsum(size, axis=0) - size
    n_tiles = (ends[-1] // tm).astype(jnp.int32)
    tile_start = jnp.arange(p // tm, dtype=jnp.int32) * tm
    tile_expert = jnp.sum((tile_start[:, None] >= ends[None, :]).astype(jnp.int32), axis=1)
    last_expert = jnp.sum((ends[-1] - 1 >= ends).astype(jnp.int32))
    tile_expert = jnp.minimum(tile_expert, last_expert).astype(jnp.int32)
    tail = jnp.stack([ends[-1], (p - ends[-1]) // tile])
    plan = jnp.concatenate([local.reshape(-1), size8.reshape(-1), dest.reshape(-1),
                            starts + total, (padded - total) // SUBLANES, tail]).astype(jnp.int32)
    return plan, tile_expert, n_tiles.reshape(1), p


def _moe_layer(x, hn, route, w1, w3, w2):
    n = x.shape[0]
    plan, tile_expert, n_tiles, n_sorted = _routing_plan(route, n)
    tri = jnp.tril(jnp.ones((2 * TOKEN_TILE, 2 * TOKEN_TILE), BF16))
    hs, lpos = _dispatch_call(hn, route, plan, tri, n_sorted)
    ys = _moe_call(hs, w1.astype(BF16), w3.astype(BF16), w2.astype(BF16), tile_expert, n_tiles)
    return _combine_call(x, route, lpos, ys, plan)


def kernel(x, attn_norm, ffn_norm, w_in, q_norm, k_norm, conv_w, hgrn_lower_bounds,
           hgrn_out_norm, w_proj_a, w_proj_b, w_proj_c, w_out, dense_w1, dense_w3,
           dense_w2, router_w, moe_w1, moe_w3, moe_w2):
    batch, seq, d = x.shape
    n = batch * seq
    depth = w_in.shape[0]
    assert d == D_MODEL and seq % ATTN_TILE == 0 and seq % TOKEN_TILE == 0
    assert all(window // dilation == BAND for window, dilation in ATTN_GROUPS)

    lbs = jax.nn.softmax(hgrn_lower_bounds.astype(F32), axis=0)
    lower_bound = jnp.cumsum(lbs, axis=0) - lbs[0]
    head_of_col = jnp.arange(ATTN_WIDTH) // HEAD_DIM
    hsum = (head_of_col[:, None] == jnp.arange(LANES)[None, :]).astype(BF16)
    hexp = jnp.concatenate([hsum.T, hsum.T], axis=0)
    a0, a1 = 0, 3 * ATTN_WIDTH
    b1 = a1 + 3 * CONV_WIDTH
    c1 = b1 + 4 * HGRN_WIDTH

    xf = x.reshape(n, d)
    for layer in range(depth):
        w = w_in[layer]
        w_attn = w[:, a0:a1].astype(BF16)
        w_hgrn = w[:, b1:c1].astype(BF16)
        w_conv_gate = jnp.concatenate([w[:, a1:b1], w[:, c1:]], axis=1).astype(BF16)
        gain = attn_norm[layer].reshape(1, d)
        heads = ATTN_WIDTH // HEAD_DIM
        qg = jnp.tile(q_norm[layer] * (HEAD_DIM ** -0.5), heads).reshape(1, ATTN_WIDTH)
        kg = jnp.tile(k_norm[layer], heads).reshape(1, ATTN_WIDTH)

        qkv = _qkv_call(xf, gain, w_attn, qg, kg, hsum, hexp, batch, seq)
        oa = _attn_call(qkv, batch, seq)
        yc = _hgrn_call(xf, gain, w_hgrn, lower_bound[layer].reshape(1, HGRN_WIDTH),
                        hgrn_out_norm[layer].reshape(1, HGRN_HEAD),
                        w_proj_c[layer].astype(BF16), batch, seq)
        fgain = ffn_norm[layer].reshape(1, d)
        j = layer // 2
        routed = layer % 2 == 1
        router = None
        if routed:
            rw = jnp.pad(router_w[j], ((0, 0), (0, LANES - N_EXPERTS)))
            rw_hi = rw.astype(BF16)
            router = (fgain, rw_hi, (rw - rw_hi.astype(F32)).astype(BF16))
        merged = _merge_call(xf, gain, w_conv_gate, conv_w[layer], oa, yc,
                             w_proj_a[layer].astype(BF16), w_proj_b[layer].astype(BF16),
                             w_out[layer].astype(BF16), seq, router)
        if routed:
            xf, hn, route = merged
            xf = _moe_layer(xf, hn, route, moe_w1[j], moe_w3[j], moe_w2[j])
        else:
            xf = _ffn_call(merged, fgain, dense_w1[j].astype(BF16), dense_w3[j].astype(BF16),
                           dense_w2[j].astype(BF16))
    return xf.reshape(batch, seq, d)
```

```python
import functools

import jax
import jax.numpy as jnp
from jax import lax
from jax.experimental import pallas as pl
from jax.experimental.pallas import tpu as pltpu

F32 = jnp.float32
BF16 = jnp.bfloat16

D_MODEL = 1024
HEAD_DIM = 64
ATTN_GROUPS = ((128, 1), (512, 4), (2048, 16))
HEADS_PER_GROUP = 4
GROUP_WIDTH = HEADS_PER_GROUP * HEAD_DIM
ATTN_WIDTH = GROUP_WIDTH * len(ATTN_GROUPS)
BAND = 128
CONV_WIDTH = 768
CONV_K = 3
HGRN_WIDTH = 768
HGRN_HEAD = 128
HGRN_HEADS = HGRN_WIDTH // HGRN_HEAD
LB_FLOOR = 1e-30
N_EXPERTS = 8
RMS_EPS = 1e-6
NEG_BIG = -1e30
LOG2E = 1.4426950408889634

LANES = 128
SUBLANES = 8

TOKEN_TILE = 512
ATTN_TILE = ATTN_GROUPS[-1][1] * BAND
ATTN_UNROLL = 8
HGRN_CHUNK = 256
MOE_ROW_TILE = 512
MOE_FF_TILE = 1792
LOCAL_ROWS = -(-(2 * TOKEN_TILE + N_EXPERTS * SUBLANES) // LANES) * LANES
GROUP_BLOCKS = tuple(1 << b for b in reversed(range(8)))
FILL_BLOCKS = tuple(b for b in GROUP_BLOCKS if b * SUBLANES < MOE_ROW_TILE)
VMEM_LIMIT = 56 * 1024 * 1024


def _dot(a, b):
    return jnp.dot(a, b, preferred_element_type=F32)


def _dot_nt(a, b):
    return lax.dot_general(a, b, (((1,), (1,)), ((), ())), preferred_element_type=F32)


def _dot_tn(a, b):
    return lax.dot_general(a, b, (((0,), (0,)), ((), ())), preferred_element_type=F32)


def _rms_rows(xf, gain):
    ms = jnp.mean(xf * xf, axis=-1, keepdims=True)
    return xf * lax.rsqrt(ms + RMS_EPS) * gain


def _sigmoid(z):
    return 1.0 / (1.0 + jnp.exp(-z))


def _params(*sem):
    return pltpu.CompilerParams(dimension_semantics=sem, vmem_limit_bytes=VMEM_LIMIT)


def _full(shape):
    return pl.BlockSpec(shape, lambda *_: (0,) * len(shape))


def _qkv_kernel(x_ref, g_ref, w_ref, qg_ref, kg_ref, hsum_ref, hexp_ref, *rest, tile):
    out_refs, stage = rest[:9], rest[9]
    h = _rms_rows(x_ref[0], g_ref[...]).astype(BF16)
    p = _dot(h, w_ref[...])

    def head_norm(t, gain):
        ms = _dot((t * t).astype(BF16), hsum_ref[...]) * (1.0 / HEAD_DIM)
        r = lax.rsqrt(ms + RMS_EPS)
        r_hi = r.astype(BF16)
        r_lo = (r - r_hi.astype(F32)).astype(BF16)
        r_cols = _dot(jnp.concatenate([r_hi, r_lo], axis=1), hexp_ref[...])
        return t * r_cols * gain

    parts = (head_norm(p[:, :ATTN_WIDTH], qg_ref[...]),
             head_norm(p[:, ATTN_WIDTH:2 * ATTN_WIDTH], kg_ref[...]),
             p[:, 2 * ATTN_WIDTH:])
    slot = 0
    for a, part in enumerate(parts):
        for group, (_, dilation) in enumerate(ATTN_GROUPS):
            out = out_refs[3 * group + a]
            c0 = group * GROUP_WIDTH
            if dilation == 1:
                out[0, 0] = part[:, c0:c0 + GROUP_WIDTH].astype(BF16)
                continue
            rows = tile // dilation
            for half in range(GROUP_WIDTH // LANES):
                stage[slot] = part[:, c0 + half * LANES:c0 + (half + 1) * LANES]
                for r in range(dilation):
                    out[0, r, :, half * LANES:(half + 1) * LANES] = (
                        stage[slot, pl.ds(r, rows, stride=dilation), :].astype(BF16))
                slot += 1


def _qkv_call(x, gain, w, qg, kg, hsum, hexp, batch, seq):
    t = TOKEN_TILE
    out_specs, out_shape = [], []
    for _, dilation in ATTN_GROUPS:
        for _ in range(3):
            out_specs.append(pl.BlockSpec((1, dilation, t // dilation, GROUP_WIDTH),
                                          lambda b, i: (b, 0, i, 0)))
            out_shape.append(jax.ShapeDtypeStruct(
                (batch, dilation, seq // dilation, GROUP_WIDTH), BF16))
    n_stage = 3 * (GROUP_WIDTH // LANES) * sum(1 for _, d in ATTN_GROUPS if d > 1)
    return pl.pallas_call(
        functools.partial(_qkv_kernel, tile=t),
        grid=(batch, seq // t),
        in_specs=[pl.BlockSpec((1, t, D_MODEL), lambda b, i: (b, i, 0)), _full((1, D_MODEL)),
                  _full(w.shape), _full((1, ATTN_WIDTH)), _full((1, ATTN_WIDTH)),
                  _full(hsum.shape), _full(hexp.shape)],
        out_specs=out_specs,
        out_shape=out_shape,
        scratch_shapes=[pltpu.VMEM((n_stage, t, LANES), F32)],
        compiler_params=_params("parallel", "parallel"),
        name="qkv_proj",
    )(x.reshape(batch, seq, D_MODEL), gain, w, qg, kg, hsum, hexp)


def _attn_kernel(*refs, n_groups):
    ins, o_ref, scr = refs[:5 * n_groups], refs[5 * n_groups], refs[5 * n_groups + 1:]
    kbufs, vbufs = scr[0:n_groups], scr[n_groups:2 * n_groups]
    o_tok, l_tok, o_rm, l_rm = scr[2 * n_groups:]
    first_tile = pl.program_id(1) == 0
    qi = lax.broadcasted_iota(jnp.int32, (BAND, 2 * BAND), 0)
    kj = lax.broadcasted_iota(jnp.int32, (BAND, 2 * BAND), 1)
    rel = qi + BAND - kj
    in_window = (rel >= 0) & (rel <= BAND)
    halves = GROUP_WIDTH // LANES
    lane = lax.broadcasted_iota(jnp.int32, (1, LANES), 1)
    head_lanes = [jnp.where(lane // HEAD_DIM == side, 1.0, 0.0).astype(BF16)
                  for side in range(LANES // HEAD_DIM)]
    first_head = lax.broadcasted_iota(jnp.int32, (BAND, LANES), 1) < HEAD_DIM

    for group, (_, dilation) in enumerate(ATTN_GROUPS):
        q_ref, kc_ref, vc_ref, kp_ref, vp_ref = ins[5 * group:5 * group + 5]
        kbuf, vbuf = kbufs[group], vbufs[group]
        length = ATTN_TILE // dilation
        nj = length // BAND
        kbuf[:, 0:BAND, :] = kp_ref[0]
        kbuf[:, BAND:BAND + length, :] = kc_ref[0]
        vbuf[:, 0:BAND, :] = vp_ref[0]
        vbuf[:, BAND:BAND + length, :] = vc_ref[0]

        def block(idx, carry, q_ref=q_ref, kbuf=kbuf, vbuf=vbuf, nj=nj, length=length,
                  group=group, dilation=dilation):
            r, j = idx // nj, idx % nj
            j0 = pl.multiple_of(j * BAND, BAND)
            kmin = jnp.where(first_tile & (j == 0), BAND, 0)
            valid = in_window & (kj >= kmin)
            row0 = pl.multiple_of(r * length + j0, BAND)
            for half in range(halves):
                lanes = slice(half * LANES, (half + 1) * LANES)
                q2 = q_ref[0, r, pl.ds(j0, BAND), lanes]
                k2 = kbuf[r, pl.ds(j0, 2 * BAND), lanes]
                v2 = vbuf[r, pl.ds(j0, 2 * BAND), lanes]
                o2, l2 = None, None
                for side in range(LANES // HEAD_DIM):
                    s = _dot_nt(q2 * head_lanes[side], k2)
                    s = jnp.where(valid, s, NEG_BIG)
                    m = jnp.max(s, axis=-1, keepdims=True)
                    p = jnp.exp(s - m)
                    den = jnp.sum(p, axis=-1, keepdims=True)
                    o = _dot(p.astype(BF16), v2) * (1.0 / den)
                    lse = jnp.broadcast_to(m + jnp.log(den), (BAND, LANES))
                    o2 = o if side == 0 else jnp.where(first_head, o2, o)
                    l2 = lse if side == 0 else jnp.where(first_head, l2, lse)
                if dilation == 1:
                    o_tok[group, half, pl.ds(row0, BAND), :] = o2
                    l_tok[group, half, pl.ds(row0, BAND), :] = l2
                else:
                    o_rm[pl.ds(row0, BAND), lanes] = o2
                    l_rm[pl.ds(row0, BAND), lanes] = l2
            return carry

        lax.fori_loop(0, ATTN_TILE // BAND, block, 0, unroll=ATTN_UNROLL)
        if dilation > 1:
            for r in range(dilation):
                rows = slice(r * length, (r + 1) * length)
                for half in range(halves):
                    lanes = slice(half * LANES, (half + 1) * LANES)
                    o_tok[group, half, pl.ds(r, length, stride=dilation), :] = o_rm[rows, lanes]
                    l_tok[group, half, pl.ds(r, length, stride=dilation), :] = l_rm[rows, lanes]

    chunk = 2 * BAND

    def merge(c, carry):
        rows = pl.ds(pl.multiple_of(c * chunk, chunk), chunk)
        for half in range(halves):
            ls = [l_tok[g, half, rows, :] for g in range(n_groups)]
            top = functools.reduce(jnp.maximum, ls)
            es = [jnp.exp(l - top) for l in ls]
            acc = sum(e * o_tok[g, half, rows, :] for g, e in enumerate(es))
            o_ref[0, rows, half * LANES:(half + 1) * LANES] = (acc * (1.0 / sum(es))).astype(BF16)
        return carry

    lax.fori_loop(0, ATTN_TILE // chunk, merge, 0)


def _attn_call(qkv, batch, seq):
    n_groups = len(ATTN_GROUPS)
    in_specs, args, kbufs = [], [], []
    for group, (_, dilation) in enumerate(ATTN_GROUPS):
        length = ATTN_TILE // dilation
        cur = pl.BlockSpec((1, dilation, length, GROUP_WIDTH), lambda b, i: (b, 0, i, 0))
        halo = pl.BlockSpec(
            (1, dilation, BAND, GROUP_WIDTH),
            lambda b, i, nb=length // BAND: (b, 0, jnp.maximum(i * nb - 1, 0), 0))
        q, k, v = qkv[3 * group:3 * group + 3]
        in_specs += [cur, cur, cur, halo, halo]
        args += [q, k, v, k, v]
        kbufs.append(pltpu.VMEM((dilation, BAND + length, GROUP_WIDTH), BF16))
    halves = GROUP_WIDTH // LANES
    o = pl.pallas_call(
        functools.partial(_attn_kernel, n_groups=n_groups),
        grid=(batch, seq // ATTN_TILE),
        in_specs=in_specs,
        out_specs=pl.BlockSpec((1, ATTN_TILE, GROUP_WIDTH), lambda b, i: (b, i, 0)),
        out_shape=jax.ShapeDtypeStruct((batch, seq, GROUP_WIDTH), BF16),
        scratch_shapes=kbufs + kbufs + [
            pltpu.VMEM((n_groups, halves, ATTN_TILE, LANES), F32),
            pltpu.VMEM((n_groups, halves, ATTN_TILE, LANES), F32),
            pltpu.VMEM((ATTN_TILE, GROUP_WIDTH), F32),
            pltpu.VMEM((ATTN_TILE, GROUP_WIDTH), F32)],
        compiler_params=_params("parallel", "arbitrary"),
        name="band_attn",
    )(*args)
    return o.reshape(batch * seq, GROUP_WIDTH)


def _hgrn_kernel(x_ref, g_ref, w_ref, lb_ref, og_ref, wp_ref, tri_ref, y_ref,
                 state_ref, proj_ref, gated_ref, b_ref, *, tile, chunk):
    @pl.when(pl.program_id(1) == 0)
    def _():
        state_ref[...] = jnp.zeros_like(state_ref)

    h = _rms_rows(x_ref[0], g_ref[...]).astype(BF16)
    proj_ref[...] = _dot(h, w_ref[...])

    lb = lb_ref[...]
    lb_floor = jnp.maximum(lb, LB_FLOOR)
    one_minus_lb = 1.0 - lb
    floor_gap = lb_floor - lb
    row = lax.broadcasted_iota(jnp.int32, (chunk, HGRN_HEAD), 0)
    tt = lax.broadcasted_iota(jnp.int32, (chunk, chunk), 0)
    ss = lax.broadcasted_iota(jnp.int32, (chunk, chunk), 1)
    t_xor_s = tt ^ ss
    w = HGRN_WIDTH
    sizes = [1 << i for i in range(chunk.bit_length() - 1)]
    lower_rows = [(row & m) == 0 for m in sizes]
    lower_sel = [jnp.where(low, 1.0, 0.0).astype(BF16) for low in lower_rows]
    signed_log2e = [jnp.where(low, -LOG2E, LOG2E).astype(BF16) for low in lower_rows]
    level_pairs = [((t_xor_s & (-m)) == m) & ((tt & m) != 0) for m in sizes]
    diagonal = tt == ss

    def chunk_body(c, carry):
        r0 = pl.multiple_of(c * chunk, chunk)
        rows = pl.ds(r0, chunk)
        z = proj_ref[rows, w:2 * w]
        e = jnp.exp(-jnp.abs(z))
        inv = 1.0 / (1.0 + e)
        sig = jnp.where(z >= 0, inv, e * inv)
        nsig = jnp.where(z >= 0, e * inv, inv)
        log_f = jnp.log(lb_floor + one_minus_lb * sig)
        k_all = one_minus_lb * nsig - floor_gap
        lf_hi = log_f.astype(BF16)
        rest = log_f - lf_hi.astype(F32)
        lf_mid = rest.astype(BF16)
        lf_lo = (rest - lf_mid.astype(F32)).astype(BF16)
        tri = tri_ref[...]
        b_all = _dot(tri, lf_hi) + _dot(tri, lf_mid) + _dot(tri, lf_lo)
        b_ref[...] = b_all
        q_all = proj_ref[rows, 0:w].astype(BF16)
        q_all = q_all * _sigmoid(q_all)
        v_all = proj_ref[rows, 2 * w:3 * w]
        g_all = proj_ref[rows, 3 * w:4 * w].astype(BF16)
        g_all = g_all * _sigmoid(g_all)
        for hd in range(HGRN_HEADS):
            cols = slice(hd * HGRN_HEAD, (hd + 1) * HGRN_HEAD)
            b, qb = b_all[:, cols], q_all[:, cols]
            kb = k_all[:, cols].astype(BF16)
            vb = v_all[:, cols].astype(BF16)
            scores = jnp.where(diagonal, _dot_nt(qb, kb), 0.0)
            last_of_block = b
            for lvl, m in enumerate(sizes):
                if m < SUBLANES:
                    lower = lower_rows[lvl]
                    boundary = jnp.where(lower, last_of_block, pltpu.roll(last_of_block, m, 0))
                    if 2 * m < SUBLANES:
                        last_of_block = jnp.where(
                            lower, pltpu.roll(last_of_block, chunk - m, 0), last_of_block)
                else:
                    boundary = jnp.concatenate(
                        [jnp.broadcast_to(b_ref[r:r + 1, cols], (2 * m, HGRN_HEAD))
                         for r in range(m - 1, chunk, 2 * m)], axis=0)
                decay = jnp.exp2((b - boundary).astype(BF16) * signed_log2e[lvl])
                if m < 2 * SUBLANES:
                    source = jnp.where(lower_sel[lvl] > 0, kb, qb)
                else:
                    source = jnp.concatenate(
                        [(kb if (r // m) % 2 == 0 else qb)[r:r + m] for r in range(0, chunk, m)],
                        axis=0)
                side = source * decay
                scores = jnp.where(level_pairs[lvl], _dot_nt(side, side), scores)
            state_t = state_ref[hd]
            q_dec = qb * jnp.exp(b).astype(BF16)
            o = _dot(scores.astype(BF16), vb) + _dot_nt(q_dec, state_t.astype(BF16))
            b_last = b[chunk - 1:chunk, :]
            k_dec = kb * jnp.exp(b_last - b).astype(BF16)
            state_ref[hd] = state_t * jnp.exp(b_last) + _dot_tn(vb, k_dec)
            o = _rms_rows(o, og_ref[...]).astype(BF16) * g_all[:, cols]
            gated_ref[rows, cols] = o
        return carry

    lax.fori_loop(0, tile // chunk, chunk_body, 0, unroll=True)
    y_ref[0] = _dot(gated_ref[...], wp_ref[...]).astype(BF16)


def _hgrn_call(x, gain, w, lb, out_gain, w_proj, batch, seq):
    tile, chunk = TOKEN_TILE, HGRN_CHUNK
    tri = jnp.tril(jnp.ones((chunk, chunk), BF16))
    blk = lambda width: pl.BlockSpec((1, tile, width), lambda b, i: (b, i, 0))
    y = pl.pallas_call(
        functools.partial(_hgrn_kernel, tile=tile, chunk=chunk),
        grid=(batch, seq // tile),
        in_specs=[blk(D_MODEL), _full((1, D_MODEL)), _full(w.shape), _full((1, HGRN_WIDTH)),
                  _full((1, HGRN_HEAD)), _full(w_proj.shape), _full((chunk, chunk))],
        out_specs=blk(D_MODEL),
        out_shape=jax.ShapeDtypeStruct((batch, seq, D_MODEL), BF16),
        scratch_shapes=[pltpu.VMEM((HGRN_HEADS, HGRN_HEAD, HGRN_HEAD), F32),
                        pltpu.VMEM((tile, 4 * HGRN_WIDTH), F32),
                        pltpu.VMEM((tile, HGRN_WIDTH), BF16),
                        pltpu.VMEM((chunk, HGRN_WIDTH), F32)],
        compiler_params=_params("parallel", "arbitrary"),
        name="hgrn_branch",
    )(x.reshape(batch, seq, D_MODEL), gain, w, lb, out_gain, w_proj, tri)
    return y.reshape(batch * seq, D_MODEL)


def _top2_route(h, rw_hi, rw_lo):
    logits = _dot(h, rw_hi) + _dot(h, rw_lo)
    lane = lax.broadcasted_iota(jnp.int32, logits.shape, 1)
    logits = jnp.where(lane < N_EXPERTS, logits, -jnp.inf)
    m1 = jnp.max(logits, axis=-1, keepdims=True)
    i1 = jnp.min(jnp.where(logits == m1, lane, LANES), axis=-1, keepdims=True)
    rest = jnp.where(lane == i1, -jnp.inf, logits)
    m2 = jnp.max(rest, axis=-1, keepdims=True)
    i2 = jnp.min(jnp.where(rest == m2, lane, LANES), axis=-1, keepdims=True)
    e = jnp.exp(m2 - m1)
    g1 = 1.0 / (1.0 + e)
    g2 = e * g1
    return jnp.where(
        lane == 0, g1, jnp.where(lane == 1, g2, jnp.where(
            lane == 2, i1.astype(F32), jnp.where(lane == 3, i2.astype(F32), 0.0))))


def _merge_kernel(x_ref, g_ref, w_ref, cw_ref, oa_ref, yc_ref, wpa_ref, wpb_ref, wo_ref, *rest,
                  tile, tiles_per_seq, steps, with_router):
    if with_router:
        fg_ref, rw_hi_ref, rw_lo_ref, out_ref, hn_ref, route_ref, u_ext, prev_ref = rest
    else:
        out_ref, u_ext = rest
    step = pl.program_id(0)

    @pl.when(step % tiles_per_seq == 0)
    def _():
        u_ext[0:SUBLANES, :] = jnp.zeros((SUBLANES, CONV_WIDTH), F32)

    def route_previous_tile():
        hn = _rms_rows(prev_ref[...], fg_ref[...])
        hn_ref[...] = hn
        route_ref[...] = _top2_route(hn.astype(BF16), rw_hi_ref[...], rw_lo_ref[...])

    def merge_tile():
        x = x_ref[...]
        h = _rms_rows(x, g_ref[...]).astype(BF16)
        p = _dot(h, w_ref[...])
        cw = CONV_WIDTH
        u = p[:, 2 * cw:3 * cw] * p[:, 0:cw]
        u_ext[SUBLANES:SUBLANES + tile, :] = u
        conv = cw_ref[0:1, :] * u
        for tap in range(1, CONV_K):
            conv = conv + cw_ref[tap:tap + 1, :] * u_ext[SUBLANES - tap:SUBLANES - tap + tile, :]
        u_ext[0:SUBLANES, :] = u[tile - SUBLANES:, :]
        yb = _dot((p[:, cw:2 * cw] * conv).astype(BF16), wpb_ref[...])
        ya = _dot(oa_ref[...], wpa_ref[...])

        d = D_MODEL
        gates = p[:, 3 * cw:]
        merged = (_sigmoid(gates[:, 0:d]) * ya + _sigmoid(gates[:, d:2 * d]) * yb
                  + _sigmoid(gates[:, 2 * d:3 * d]) * yc_ref[...].astype(F32))
        x_new = x + _dot(merged.astype(BF16), wo_ref[...])
        out_ref[...] = x_new
        return x_new

    if not with_router:
        merge_tile()
        return

    @pl.when(step == 0)
    def _():
        prev_ref[...] = jnp.zeros_like(prev_ref)

    @pl.when(step < steps)
    def _():
        route_previous_tile()
        prev_ref[...] = merge_tile()

    @pl.when(step == steps)
    def _():
        route_previous_tile()


def _merge_call(x, gain, w, conv_w, oa, yc, wpa, wpb, wo, seq, router=None):
    n = x.shape[0]
    tile = TOKEN_TILE
    steps = n // tile
    row = lambda width: pl.BlockSpec((tile, width), lambda i: (jnp.minimum(i, steps - 1), 0))
    late = lambda width: pl.BlockSpec((tile, width), lambda i: (jnp.maximum(i - 1, 0), 0))
    in_specs = [row(D_MODEL), _full((1, D_MODEL)), _full(w.shape), _full(conv_w.shape),
                row(GROUP_WIDTH), row(D_MODEL), _full(wpa.shape), _full(wpb.shape),
                _full(wo.shape)]
    args = [x, gain, w, conv_w, oa, yc, wpa, wpb, wo]
    out_specs, out_shape = [row(D_MODEL)], [jax.ShapeDtypeStruct((n, D_MODEL), F32)]
    scratch = [pltpu.VMEM((SUBLANES + tile, CONV_WIDTH), F32)]
    if router is not None:
        in_specs += [_full(a.shape) for a in router]
        args += list(router)
        out_specs += [late(D_MODEL), late(LANES)]
        out_shape += [jax.ShapeDtypeStruct((n, D_MODEL), F32),
                      jax.ShapeDtypeStruct((n, LANES), F32)]
        scratch.append(pltpu.VMEM((tile, D_MODEL), F32))
    outs = pl.pallas_call(
        functools.partial(_merge_kernel, tile=tile, tiles_per_seq=seq // tile, steps=steps,
                          with_router=router is not None),
        grid=(steps + (router is not None),),
        in_specs=in_specs,
        out_specs=out_specs,
        out_shape=out_shape,
        scratch_shapes=scratch,
        compiler_params=_params("arbitrary"),
        name="merge_out",
    )(*args)
    return outs[0] if router is None else outs


def _ffn_kernel(x_ref, g_ref, w1_ref, w3_ref, w2_ref, out_ref):
    x = x_ref[...]
    h = _rms_rows(x, g_ref[...]).astype(BF16)
    a = _dot(h, w1_ref[...])
    z = (a * _sigmoid(a) * _dot(h, w3_ref[...])).astype(BF16)
    out_ref[...] = x + _dot(z, w2_ref[...])


def _ffn_call(x, gain, w1, w3, w2):
    n = x.shape[0]
    tile = TOKEN_TILE
    row = pl.BlockSpec((tile, D_MODEL), lambda i: (i, 0))
    return pl.pallas_call(
        _ffn_kernel,
        grid=(n // tile,),
        in_specs=[row, _full((1, D_MODEL)), _full(w1.shape), _full(w3.shape), _full(w2.shape)],
        out_specs=row,
        out_shape=jax.ShapeDtypeStruct((n, D_MODEL), F32),
        compiler_params=_params("parallel"),
        name="dense_ffn",
    )(x, gain, w1, w3, w2)


def _block_copies(n8, issue, blocks=GROUP_BLOCKS):
    done = jnp.int32(0)
    for bit in blocks:
        take = (n8 & bit) != 0

        @pl.when(take)
        def _(done=done, bit=bit):
            issue(pl.multiple_of(done * SUBLANES, SUBLANES), bit * SUBLANES)

        done = done + jnp.where(take, bit, 0)


def _slot_positions(route, tri, local_base):
    expert = jnp.concatenate([route[:, 2:3], route[:, 3:4]], axis=0)
    lane = lax.broadcasted_iota(jnp.int32, (expert.shape[0], LANES), 1).astype(F32)
    onehot = expert == lane
    seen = _dot(tri, jnp.where(onehot, 1.0, 0.0).astype(BF16))
    pos = jnp.sum(jnp.where(onehot, seen - 1.0 + local_base, 0.0), axis=1, keepdims=True)
    t = route.shape[0]
    return pos[:t], pos[t:]


def _local_base_row(plan_ref, tile_index):
    lane = lax.broadcasted_iota(jnp.int32, (1, LANES), 1)
    row = jnp.zeros((1, LANES), F32)
    for e in range(N_EXPERTS):
        row = jnp.where(lane == e, plan_ref[tile_index * N_EXPERTS + e].astype(F32), row)
    return row


def _dispatch_kernel(plan_ref, hn_ref, route_ref, tri_ref, out_hbm, lpos_ref, sorted_ref, zeros,
                     sem, fill_sem, *, tile, n_groups):
    i = pl.program_id(0)
    fill0 = 3 * n_groups

    def fills(wait):
        for e in range(N_EXPERTS):
            first = pl.multiple_of(plan_ref[fill0 + e], SUBLANES)

            def issue(off, rows, first=first):
                cp = pltpu.make_async_copy(zeros.at[pl.ds(0, rows), :],
                                           out_hbm.at[pl.ds(first + off, rows), :], fill_sem)
                cp.wait() if wait else cp.start()

            _block_copies(plan_ref[fill0 + N_EXPERTS + e], issue, FILL_BLOCKS)

        def tail(k, carry):
            start = pl.multiple_of(plan_ref[fill0 + 2 * N_EXPERTS] + k * tile, tile)
            cp = pltpu.make_async_copy(zeros, out_hbm.at[pl.ds(start, tile), :], fill_sem)
            cp.wait() if wait else cp.start()
            return carry

        lax.fori_loop(0, plan_ref[fill0 + 2 * N_EXPERTS + 1], tail, 0)

    @pl.when(i == 0)
    def _():
        zeros[...] = jnp.zeros_like(zeros)
        fills(wait=False)
        fills(wait=True)

    pos1, pos2 = _slot_positions(route_ref[...], tri_ref[...], _local_base_row(plan_ref, i))
    lane = lax.broadcasted_iota(jnp.int32, (tile, LANES), 1)
    lpos_ref[...] = jnp.where(lane == 0, pos1, jnp.where(lane == 1, pos2, 0.0))
    col = lax.broadcasted_iota(jnp.int32, (tile, LOCAL_ROWS), 1).astype(F32)
    select = jnp.where((pos1 == col) | (pos2 == col), 1.0, 0.0).astype(BF16)
    sorted_rows = _dot_tn(select, hn_ref[...].astype(BF16))

    def groups(step, wait):
        for e in range(N_EXPERTS):
            g = step * N_EXPERTS + e
            local = pl.multiple_of(plan_ref[g], SUBLANES)
            dest = pl.multiple_of(plan_ref[2 * n_groups + g], SUBLANES)

            def issue(off, rows, local=local, dest=dest):
                cp = pltpu.make_async_copy(sorted_ref.at[pl.ds(local + off, rows), :],
                                           out_hbm.at[pl.ds(dest + off, rows), :], sem)
                cp.wait() if wait else cp.start()

            _block_copies(plan_ref[n_groups + g], issue)

    @pl.when(i > 0)
    def _():
        groups(i - 1, wait=True)

    sorted_ref[...] = sorted_rows
    groups(i, wait=False)

    @pl.when(i == pl.num_programs(0) - 1)
    def _():
        groups(i, wait=True)


def _dispatch_call(hn, route, plan, tri, n_sorted):
    n = hn.shape[0]
    tile = TOKEN_TILE
    n_groups = (n // tile) * N_EXPERTS
    row = lambda width: pl.BlockSpec((tile, width), lambda i, plan: (i, 0))
    grid_spec = pltpu.PrefetchScalarGridSpec(
        num_scalar_prefetch=1,
        grid=(n // tile,),
        in_specs=[row(D_MODEL), row(LANES),
                  pl.BlockSpec(tri.shape, lambda i, plan: (0, 0))],
        out_specs=[pl.BlockSpec(memory_space=pl.ANY), row(LANES)],
        scratch_shapes=[pltpu.VMEM((LOCAL_ROWS, D_MODEL), F32), pltpu.VMEM((tile, D_MODEL), F32),
                        pltpu.SemaphoreType.DMA(()), pltpu.SemaphoreType.DMA(())],
    )
    return pl.pallas_call(
        functools.partial(_dispatch_kernel, tile=tile, n_groups=n_groups),
        grid_spec=grid_spec,
        out_shape=[jax.ShapeDtypeStruct((n_sorted, D_MODEL), F32),
                   jax.ShapeDtypeStruct((n, LANES), F32)],
        compiler_params=_params("arbitrary"),
        name="moe_dispatch",
    )(plan, hn, route, tri)


def _moe_kernel(tile_expert_ref, n_tiles_ref, h_ref, w1_ref, w3_ref, w2_ref, y_ref, acc_ref, *,
                ff_steps):
    i, f = pl.program_id(0), pl.program_id(1)
    last_f = ff_steps - 1
    active = i < n_tiles_ref[0]

    @pl.when(active)
    def _():
        h = h_ref[...].astype(BF16)
        a = _dot(h, w1_ref[...])
        z = (a * _sigmoid(a) * _dot(h, w3_ref[...])).astype(BF16)
        part = _dot(z, w2_ref[...])
        if ff_steps == 1:
            y_ref[...] = part
            return

        @pl.when(f == 0)
        def _():
            acc_ref[...] = part

        @pl.when((f > 0) & (f < last_f))
        def _():
            acc_ref[...] += part

        @pl.when(f == last_f)
        def _():
            y_ref[...] = acc_ref[...] + part

    @pl.when(jnp.logical_not(active) & (f == last_f))
    def _():
        y_ref[...] = jnp.zeros_like(y_ref)


def _moe_call(hs, w1, w3, w2, tile_expert, n_tiles):
    p = hs.shape[0]
    d_ff = w1.shape[-1]
    tm, tf = MOE_ROW_TILE, MOE_FF_TILE
    nf = d_ff // tf

    def ff_block(i, f, n_tiles_ref):
        return jnp.where(i < n_tiles_ref[0], f, nf - 1)

    grid_spec = pltpu.PrefetchScalarGridSpec(
        num_scalar_prefetch=2,
        grid=(p // tm, nf),
        in_specs=[
            pl.BlockSpec((tm, D_MODEL), lambda i, f, te, nt: (jnp.minimum(i, nt[0] - 1), 0)),
            pl.BlockSpec((None, D_MODEL, tf), lambda i, f, te, nt: (te[i], 0, ff_block(i, f, nt))),
            pl.BlockSpec((None, D_MODEL, tf), lambda i, f, te, nt: (te[i], 0, ff_block(i, f, nt))),
            pl.BlockSpec((None, tf, D_MODEL), lambda i, f, te, nt: (te[i], ff_block(i, f, nt), 0)),
        ],
        out_specs=pl.BlockSpec((tm, D_MODEL), lambda i, f, te, nt: (i, 0)),
        scratch_shapes=[pltpu.VMEM((tm, D_MODEL), F32)],
    )
    return pl.pallas_call(
        functools.partial(_moe_kernel, ff_steps=nf),
        grid_spec=grid_spec,
        out_shape=jax.ShapeDtypeStruct((p, D_MODEL), F32),
        compiler_params=_params("arbitrary", "arbitrary"),
        name="moe_experts",
    )(tile_expert, n_tiles, hs, w1, w3, w2)


def _combine_kernel(plan_ref, x_ref, route_ref, lpos_ref, y_hbm, out_ref, ybuf, sems, *,
                    tile, n_groups):
    i = pl.program_id(0)

    def groups(step, wait):
        slot = step % 2
        for e in range(N_EXPERTS):
            g = step * N_EXPERTS + e
            local = pl.multiple_of(plan_ref[g], SUBLANES)
            src = pl.multiple_of(plan_ref[2 * n_groups + g], SUBLANES)

            def issue(off, rows, local=local, src=src):
                cp = pltpu.make_async_copy(y_hbm.at[pl.ds(src + off, rows), :],
                                           ybuf.at[slot, pl.ds(local + off, rows), :], sems.at[slot])
                cp.wait() if wait else cp.start()

            _block_copies(plan_ref[n_groups + g], issue)

    @pl.when(i == 0)
    def _():
        ybuf[...] = jnp.zeros_like(ybuf)
        groups(i, wait=False)

    @pl.when(i + 1 < pl.num_programs(0))
    def _():
        groups(i + 1, wait=False)

    col = lax.broadcasted_iota(jnp.int32, (tile, LOCAL_ROWS), 1).astype(F32)
    pick1 = jnp.where(lpos_ref[:, 0:1] == col, 1.0, 0.0).astype(BF16)
    pick2 = jnp.where(lpos_ref[:, 1:2] == col, 1.0, 0.0).astype(BF16)
    groups(i, wait=True)
    last = i * N_EXPERTS + N_EXPERTS - 1
    used = plan_ref[last] + plan_ref[n_groups + last] * SUBLANES
    row = lax.broadcasted_iota(jnp.int32, (LOCAL_ROWS, 1), 0)
    y = jnp.where(row < used, ybuf[i % 2], 0.0).astype(BF16)
    out_ref[...] = (x_ref[...] + route_ref[:, 0:1] * _dot(pick1, y)
                    + route_ref[:, 1:2] * _dot(pick2, y))


def _combine_call(x, route, lpos, y, plan):
    n = x.shape[0]
    tile = TOKEN_TILE
    n_groups = (n // tile) * N_EXPERTS
    row = lambda width: pl.BlockSpec((tile, width), lambda i, plan: (i, 0))
    grid_spec = pltpu.PrefetchScalarGridSpec(
        num_scalar_prefetch=1,
        grid=(n // tile,),
        in_specs=[row(D_MODEL), row(LANES), row(LANES), pl.BlockSpec(memory_space=pl.ANY)],
        out_specs=row(D_MODEL),
        scratch_shapes=[pltpu.VMEM((2, LOCAL_ROWS, D_MODEL), F32), pltpu.SemaphoreType.DMA((2,))],
    )
    return pl.pallas_call(
        functools.partial(_combine_kernel, tile=tile, n_groups=n_groups),
        grid_spec=grid_spec,
        out_shape=jax.ShapeDtypeStruct(x.shape, F32),
        compiler_params=_params("arbitrary"),
        name="moe_combine",
    )(plan, x, route, lpos, y)


def _routing_plan(route, n):
    tile, tm = TOKEN_TILE, MOE_ROW_TILE
    n_tiles_tok = n // tile
    slack = -(-(n_tiles_tok * N_EXPERTS * SUBLANES) // tm) * tm
    p = 2 * n + slack + N_EXPERTS * tm
    experts = jnp.arange(N_EXPERTS, dtype=jnp.int32)
    choice = route[:, 2:4].astype(jnp.int32).reshape(n_tiles_tok, tile, 2)
    counts = jnp.sum((choice[..., None] == experts).astype(jnp.int32), axis=(1, 2))
    size8 = (counts + SUBLANES - 1) // SUBLANES
    size = size8 * SUBLANES
    local = jnp.cumsum(size, axis=1) - size
    total = jnp.sum(size, axis=0)
    padded = ((total + tm - 1) // tm) * tm
    ends = jnp.cumsum(padded)
    starts = ends - padded
    dest = starts[None, :] + jnp.cumsum(size, axis=0) - size
    n_tiles = (ends[-1] // tm).astype(jnp.int32)
    tile_start = jnp.arange(p // tm, dtype=jnp.int32) * tm
    tile_expert = jnp.sum((tile_start[:, None] >= ends[None, :]).astype(jnp.int32), axis=1)
    last_expert = jnp.sum((ends[-1] - 1 >= ends).astype(jnp.int32))
    tile_expert = jnp.minimum(tile_expert, last_expert).astype(jnp.int32)
    tail = jnp.stack([ends[-1], (p - ends[-1]) // tile])
    plan = jnp.concatenate([local.reshape(-1), size8.reshape(-1), dest.reshape(-1),
                            starts + total, (padded - total) // SUBLANES, tail]).astype(jnp.int32)
    return plan, tile_expert, n_tiles.reshape(1), p


def _moe_layer(x, hn, route, w1, w3, w2):
    n = x.shape[0]
    plan, tile_expert, n_tiles, n_sorted = _routing_plan(route, n)
    tri = jnp.tril(jnp.ones((2 * TOKEN_TILE, 2 * TOKEN_TILE), BF16))
    hs, lpos = _dispatch_call(hn, route, plan, tri, n_sorted)
    ys = _moe_call(hs, w1.astype(BF16), w3.astype(BF16), w2.astype(BF16), tile_expert, n_tiles)
    return _combine_call(x, route, lpos, ys, plan)


def kernel(x, attn_norm, ffn_norm, w_in, q_norm, k_norm, conv_w, hgrn_lower_bounds,
           hgrn_out_norm, w_proj_a, w_proj_b, w_proj_c, w_out, dense_w1, dense_w3,
           dense_w2, router_w, moe_w1, moe_w3, moe_w2):
    batch, seq, d = x.shape
    n = batch * seq
    depth = w_in.shape[0]
    assert d == D_MODEL and seq % ATTN_TILE == 0 and seq % TOKEN_TILE == 0
    assert all(window // dilation == BAND for window, dilation in ATTN_GROUPS)

    lbs = jax.nn.softmax(hgrn_lower_bounds.astype(F32), axis=0)
    lower_bound = jnp.cumsum(lbs, axis=0) - lbs[0]
    head_of_col = jnp.arange(ATTN_WIDTH) // HEAD_DIM
    hsum = (head_of_col[:, None] == jnp.arange(LANES)[None, :]).astype(BF16)
    hexp = jnp.concatenate([hsum.T, hsum.T], axis=0)
    a0, a1 = 0, 3 * ATTN_WIDTH
    b1 = a1 + 3 * CONV_WIDTH
    c1 = b1 + 4 * HGRN_WIDTH

    xf = x.reshape(n, d)
    for layer in range(depth):
        w = w_in[layer]
        w_attn = w[:, a0:a1].astype(BF16)
        w_hgrn = w[:, b1:c1].astype(BF16)
        w_conv_gate = jnp.concatenate([w[:, a1:b1], w[:, c1:]], axis=1).astype(BF16)
        gain = attn_norm[layer].reshape(1, d)
        heads = ATTN_WIDTH // HEAD_DIM
        qg = jnp.tile(q_norm[layer] * (HEAD_DIM ** -0.5), heads).reshape(1, ATTN_WIDTH)
        kg = jnp.tile(k_norm[layer], heads).reshape(1, ATTN_WIDTH)

        qkv = _qkv_call(xf, gain, w_attn, qg, kg, hsum, hexp, batch, seq)
        oa = _attn_call(qkv, batch, seq)
        yc = _hgrn_call(xf, gain, w_hgrn, lower_bound[layer].reshape(1, HGRN_WIDTH),
                        hgrn_out_norm[layer].reshape(1, HGRN_HEAD),
                        w_proj_c[layer].astype(BF16), batch, seq)
        fgain = ffn_norm[layer].reshape(1, d)
        j = layer // 2
        routed = layer % 2 == 1
        router = None
        if routed:
            rw = jnp.pad(router_w[j], ((0, 0), (0, LANES - N_EXPERTS)))
            rw_hi = rw.astype(BF16)
            router = (fgain, rw_hi, (rw - rw_hi.astype(F32)).astype(BF16))
        merged = _merge_call(xf, gain, w_conv_gate, conv_w[layer], oa, yc,
                             w_proj_a[layer].astype(BF16), w_proj_b[layer].astype(BF16),
                             w_out[layer].astype(BF16), seq, router)
        if routed:
            xf, hn, route = merged
            xf = _moe_layer(xf, hn, route, moe_w1[j], moe_w3[j], moe_w2[j])
        else:
            xf = _ffn_call(merged, fgain, dense_w1[j].astype(BF16), dense_w3[j].astype(BF16),
                           dense_w2[j].astype(BF16))
    return xf.reshape(batch, seq, d)
```

```python
import functools

import jax
import jax.numpy as jnp
from jax import lax
from jax.experimental import pallas as pl
from jax.experimental.pallas import tpu as pltpu

F32 = jnp.float32
BF16 = jnp.bfloat16

D_MODEL = 1024
HEAD_DIM = 64
ATTN_GROUPS = ((128, 1), (512, 4), (2048, 16))
HEADS_PER_GROUP = 4
GROUP_WIDTH = HEADS_PER_GROUP * HEAD_DIM
ATTN_WIDTH = GROUP_WIDTH * len(ATTN_GROUPS)
BAND = 128
CONV_WIDTH = 768
CONV_K = 3
HGRN_WIDTH = 768
HGRN_HEAD = 128
HGRN_HEADS = HGRN_WIDTH // HGRN_HEAD
LB_FLOOR = 1e-30
N_EXPERTS = 8
RMS_EPS = 1e-6
NEG_BIG = -1e30
LOG2E = 1.4426950408889634

LANES = 128
SUBLANES = 8

TOKEN_TILE = 512
ATTN_TILE = ATTN_GROUPS[-1][1] * BAND
ATTN_UNROLL = 8
HGRN_CHUNK = 256
MOE_ROW_TILE = 512
MOE_FF_TILE = 1792
LOCAL_ROWS = -(-(2 * TOKEN_TILE + N_EXPERTS * SUBLANES) // LANES) * LANES
GROUP_BLOCKS = tuple(1 << b for b in reversed(range(8)))
FILL_BLOCKS = tuple(b for b in GROUP_BLOCKS if b * SUBLANES < MOE_ROW_TILE)
VMEM_LIMIT = 56 * 1024 * 1024


def _dot(a, b):
    return jnp.dot(a, b, preferred_element_type=F32)


def _dot_nt(a, b):
    return lax.dot_general(a, b, (((1,), (1,)), ((), ())), preferred_element_type=F32)


def _dot_tn(a, b):
    return lax.dot_general(a, b, (((0,), (0,)), ((), ())), preferred_element_type=F32)


def _rms_rows(xf, gain):
    ms = jnp.mean(xf * xf, axis=-1, keepdims=True)
    return xf * lax.rsqrt(ms + RMS_EPS) * gain


def _sigmoid(z):
    return 1.0 / (1.0 + jnp.exp(-z))


def _params(*sem):
    return pltpu.CompilerParams(dimension_semantics=sem, vmem_limit_bytes=VMEM_LIMIT)


def _full(shape):
    return pl.BlockSpec(shape, lambda *_: (0,) * len(shape))


def _qkv_kernel(x_ref, g_ref, w_ref, qg_ref, kg_ref, hsum_ref, hexp_ref, *rest, tile):
    out_refs, stage = rest[:9], rest[9]
    h = _rms_rows(x_ref[0], g_ref[...]).astype(BF16)
    p = _dot(h, w_ref[...])

    def head_norm(t, gain):
        ms = _dot((t * t).astype(BF16), hsum_ref[...]) * (1.0 / HEAD_DIM)
        r = lax.rsqrt(ms + RMS_EPS)
        r_hi = r.astype(BF16)
        r_lo = (r - r_hi.astype(F32)).astype(BF16)
        r_cols = _dot(jnp.concatenate([r_hi, r_lo], axis=1), hexp_ref[...])
        return t * r_cols * gain

    parts = (head_norm(p[:, :ATTN_WIDTH], qg_ref[...]),
             head_norm(p[:, ATTN_WIDTH:2 * ATTN_WIDTH], kg_ref[...]),
             p[:, 2 * ATTN_WIDTH:])
    slot = 0
    for a, part in enumerate(parts):
        for group, (_, dilation) in enumerate(ATTN_GROUPS):
            out = out_refs[3 * group + a]
            c0 = group * GROUP_WIDTH
            if dilation == 1:
                out[0, 0] = part[:, c0:c0 + GROUP_WIDTH].astype(BF16)
                continue
            rows = tile // dilation
            for half in range(GROUP_WIDTH // LANES):
                stage[slot] = part[:, c0 + half * LANES:c0 + (half + 1) * LANES]
                for r in range(dilation):
                    out[0, r, :, half * LANES:(half + 1) * LANES] = (
                        stage[slot, pl.ds(r, rows, stride=dilation), :].astype(BF16))
                slot += 1


def _qkv_call(x, gain, w, qg, kg, hsum, hexp, batch, seq):
    t = TOKEN_TILE
    out_specs, out_shape = [], []
    for _, dilation in ATTN_GROUPS:
        for _ in range(3):
            out_specs.append(pl.BlockSpec((1, dilation, t // dilation, GROUP_WIDTH),
                                          lambda b, i: (b, 0, i, 0)))
            out_shape.append(jax.ShapeDtypeStruct(
                (batch, dilation, seq // dilation, GROUP_WIDTH), BF16))
    n_stage = 3 * (GROUP_WIDTH // LANES) * sum(1 for _, d in ATTN_GROUPS if d > 1)
    return pl.pallas_call(
        functools.partial(_qkv_kernel, tile=t),
        grid=(batch, seq // t),
        in_specs=[pl.BlockSpec((1, t, D_MODEL), lambda b, i: (b, i, 0)), _full((1, D_MODEL)),
                  _full(w.shape), _full((1, ATTN_WIDTH)), _full((1, ATTN_WIDTH)),
                  _full(hsum.shape), _full(hexp.shape)],
        out_specs=out_specs,
        out_shape=out_shape,
        scratch_shapes=[pltpu.VMEM((n_stage, t, LANES), F32)],
        compiler_params=_params("parallel", "parallel"),
        name="qkv_proj",
    )(x.reshape(batch, seq, D_MODEL), gain, w, qg, kg, hsum, hexp)


def _attn_kernel(*refs, n_groups):
    ins, o_ref, scr = refs[:5 * n_groups], refs[5 * n_groups], refs[5 * n_groups + 1:]
    kbufs, vbufs = scr[0:n_groups], scr[n_groups:2 * n_groups]
    o_tok, l_tok, o_rm, l_rm = scr[2 * n_groups:]
    first_tile = pl.program_id(1) == 0
    qi = lax.broadcasted_iota(jnp.int32, (BAND, 2 * BAND), 0)
    kj = lax.broadcasted_iota(jnp.int32, (BAND, 2 * BAND), 1)
    rel = qi + BAND - kj
    in_window = (rel >= 0) & (rel <= BAND)
    halves = GROUP_WIDTH // LANES
    lane = lax.broadcasted_iota(jnp.int32, (1, LANES), 1)
    head_lanes = [jnp.where(lane // HEAD_DIM == side, 1.0, 0.0).astype(BF16)
                  for side in range(LANES // HEAD_DIM)]
    first_head = lax.broadcasted_iota(jnp.int32, (BAND, LANES), 1) < HEAD_DIM

    for group, (_, dilation) in enumerate(ATTN_GROUPS):
        q_ref, kc_ref, vc_ref, kp_ref, vp_ref = ins[5 * group:5 * group + 5]
        kbuf, vbuf = kbufs[group], vbufs[group]
        length = ATTN_TILE // dilation
        nj = length // BAND
        kbuf[:, 0:BAND, :] = kp_ref[0]
        kbuf[:, BAND:BAND + length, :] = kc_ref[0]
        vbuf[:, 0:BAND, :] = vp_ref[0]
        vbuf[:, BAND:BAND + length, :] = vc_ref[0]

        def block(idx, carry, q_ref=q_ref, kbuf=kbuf, vbuf=vbuf, nj=nj, length=length,
                  group=group, dilation=dilation):
            r, j = idx // nj, idx % nj
            j0 = pl.multiple_of(j * BAND, BAND)
            kmin = jnp.where(first_tile & (j == 0), BAND, 0)
            valid = in_window & (kj >= kmin)
            row0 = pl.multiple_of(r * length + j0, BAND)
            for half in range(halves):
                lanes = slice(half * LANES, (half + 1) * LANES)
                q2 = q_ref[0, r, pl.ds(j0, BAND), lanes]
                k2 = kbuf[r, pl.ds(j0, 2 * BAND), lanes]
                v2 = vbuf[r, pl.ds(j0, 2 * BAND), lanes]
                o2, l2 = None, None
                for side in range(LANES // HEAD_DIM):
                    s = _dot_nt(q2 * head_lanes[side], k2)
                    s = jnp.where(valid, s, NEG_BIG)
                    m = jnp.max(s, axis=-1, keepdims=True)
                    p = jnp.exp(s - m)
                    den = jnp.sum(p, axis=-1, keepdims=True)
                    o = _dot(p.astype(BF16), v2) * (1.0 / den)
                    lse = jnp.broadcast_to(m + jnp.log(den), (BAND, LANES))
                    o2 = o if side == 0 else jnp.where(first_head, o2, o)
                    l2 = lse if side == 0 else jnp.where(first_head, l2, lse)
                if dilation == 1:
                    o_tok[group, half, pl.ds(row0, BAND), :] = o2
                    l_tok[group, half, pl.ds(row0, BAND), :] = l2
                else:
                    o_rm[pl.ds(row0, BAND), lanes] = o2
                    l_rm[pl.ds(row0, BAND), lanes] = l2
            return carry

        lax.fori_loop(0, ATTN_TILE // BAND, block, 0, unroll=ATTN_UNROLL)
        if dilation > 1:
            for r in range(dilation):
                rows = slice(r * length, (r + 1) * length)
                for half in range(halves):
                    lanes = slice(half * LANES, (half + 1) * LANES)
                    o_tok[group, half, pl.ds(r, length, stride=dilation), :] = o_rm[rows, lanes]
                    l_tok[group, half, pl.ds(r, length, stride=dilation), :] = l_rm[rows, lanes]

    chunk = 2 * BAND

    def merge(c, carry):
        rows = pl.ds(pl.multiple_of(c * chunk, chunk), chunk)
        for half in range(halves):
            ls = [l_tok[g, half, rows, :] for g in range(n_groups)]
            top = functools.reduce(jnp.maximum, ls)
            es = [jnp.exp(l - top) for l in ls]
            acc = sum(e * o_tok[g, half, rows, :] for g, e in enumerate(es))
            o_ref[0, rows, half * LANES:(half + 1) * LANES] = (acc * (1.0 / sum(es))).astype(BF16)
        return carry

    lax.fori_loop(0, ATTN_TILE // chunk, merge, 0)


def _attn_call(qkv, batch, seq):
    n_groups = len(ATTN_GROUPS)
    in_specs, args, kbufs = [], [], []
    for group, (_, dilation) in enumerate(ATTN_GROUPS):
        length = ATTN_TILE // dilation
        cur = pl.BlockSpec((1, dilation, length, GROUP_WIDTH), lambda b, i: (b, 0, i, 0))
        halo = pl.BlockSpec(
            (1, dilation, BAND, GROUP_WIDTH),
            lambda b, i, nb=length // BAND: (b, 0, jnp.maximum(i * nb - 1, 0), 0))
        q, k, v = qkv[3 * group:3 * group + 3]
        in_specs += [cur, cur, cur, halo, halo]
        args += [q, k, v, k, v]
        kbufs.append(pltpu.VMEM((dilation, BAND + length, GROUP_WIDTH), BF16))
    halves = GROUP_WIDTH // LANES
    o = pl.pallas_call(
        functools.partial(_attn_kernel, n_groups=n_groups),
        grid=(batch, seq // ATTN_TILE),
        in_specs=in_specs,
        out_specs=pl.BlockSpec((1, ATTN_TILE, GROUP_WIDTH), lambda b, i: (b, i, 0)),
        out_shape=jax.ShapeDtypeStruct((batch, seq, GROUP_WIDTH), BF16),
        scratch_shapes=kbufs + kbufs + [
            pltpu.VMEM((n_groups, halves, ATTN_TILE, LANES), F32),
            pltpu.VMEM((n_groups, halves, ATTN_TILE, LANES), F32),
            pltpu.VMEM((ATTN_TILE, GROUP_WIDTH), F32),
            pltpu.VMEM((ATTN_TILE, GROUP_WIDTH), F32)],
        compiler_params=_params("parallel", "arbitrary"),
        name="band_attn",
    )(*args)
    return o.reshape(batch * seq, GROUP_WIDTH)


def _hgrn_kernel(x_ref, g_ref, w_ref, lb_ref, og_ref, wp_ref, tri_ref, y_ref,
                 state_ref, proj_ref, gated_ref, b_ref, *, tile, chunk):
    @pl.when(pl.program_id(1) == 0)
    def _():
        state_ref[...] = jnp.zeros_like(state_ref)

    h = _rms_rows(x_ref[0], g_ref[...]).astype(BF16)
    proj_ref[...] = _dot(h, w_ref[...])

    lb = lb_ref[...]
    lb_floor = jnp.maximum(lb, LB_FLOOR)
    one_minus_lb = 1.0 - lb
    floor_gap = lb_floor - lb
    row = lax.broadcasted_iota(jnp.int32, (chunk, HGRN_HEAD), 0)
    tt = lax.broadcasted_iota(jnp.int32, (chunk, chunk), 0)
    ss = lax.broadcasted_iota(jnp.int32, (chunk, chunk), 1)
    t_xor_s = tt ^ ss
    w = HGRN_WIDTH
    sizes = [1 << i for i in range(chunk.bit_length() - 1)]
    lower_rows = [(row & m) == 0 for m in sizes]
    lower_sel = [jnp.where(low, 1.0, 0.0).astype(BF16) for low in lower_rows]
    signed_log2e = [jnp.where(low, -LOG2E, LOG2E).astype(BF16) for low in lower_rows]
    level_pairs = [((t_xor_s & (-m)) == m) & ((tt & m) != 0) for m in sizes]
    diagonal = tt == ss

    def chunk_body(c, carry):
        r0 = pl.multiple_of(c * chunk, chunk)
        rows = pl.ds(r0, chunk)
        z = proj_ref[rows, w:2 * w]
        e = jnp.exp(-jnp.abs(z))
        inv = 1.0 / (1.0 + e)
        sig = jnp.where(z >= 0, inv, e * inv)
        nsig = jnp.where(z >= 0, e * inv, inv)
        log_f = jnp.log(lb_floor + one_minus_lb * sig)
        k_all = one_minus_lb * nsig - floor_gap
        lf_hi = log_f.astype(BF16)
        rest = log_f - lf_hi.astype(F32)
        lf_mid = rest.astype(BF16)
        lf_lo = (rest - lf_mid.astype(F32)).astype(BF16)
        tri = tri_ref[...]
        b_all = _dot(tri, lf_hi) + _dot(tri, lf_mid) + _dot(tri, lf_lo)
        b_ref[...] = b_all
        q_all = proj_ref[rows, 0:w].astype(BF16)
        q_all = q_all * _sigmoid(q_all)
        v_all = proj_ref[rows, 2 * w:3 * w]
        g_all = proj_ref[rows, 3 * w:4 * w].astype(BF16)
        g_all = g_all * _sigmoid(g_all)
        for hd in range(HGRN_HEADS):
            cols = slice(hd * HGRN_HEAD, (hd + 1) * HGRN_HEAD)
            b, qb = b_all[:, cols], q_all[:, cols]
            kb = k_all[:, cols].astype(BF16)
            vb = v_all[:, cols].astype(BF16)
            scores = jnp.where(diagonal, _dot_nt(qb, kb), 0.0)
            last_of_block = b
            for lvl, m in enumerate(sizes):
                if m < SUBLANES:
                    lower = lower_rows[lvl]
                    boundary = jnp.where(lower, last_of_block, pltpu.roll(last_of_block, m, 0))
                    if 2 * m < SUBLANES:
                        last_of_block = jnp.where(
                            lower, pltpu.roll(last_of_block, chunk - m, 0), last_of_block)
                else:
                    boundary = jnp.concatenate(
                        [jnp.broadcast_to(b_ref[r:r + 1, cols], (2 * m, HGRN_HEAD))
                         for r in range(m - 1, chunk, 2 * m)], axis=0)
                decay = jnp.exp2((b - boundary).astype(BF16) * signed_log2e[lvl])
                if m < 2 * SUBLANES:
                    source = jnp.where(lower_sel[lvl] > 0, kb, qb)
                else:
                    source = jnp.concatenate(
                        [(kb if (r // m) % 2 == 0 else qb)[r:r + m] for r in range(0, chunk, m)],
                        axis=0)
                side = source * decay
                scores = jnp.where(level_pairs[lvl], _dot_nt(side, side), scores)
            state_t = state_ref[hd]
            q_dec = qb * jnp.exp(b).astype(BF16)
            o = _dot(scores.astype(BF16), vb) + _dot_nt(q_dec, state_t.astype(BF16))
            b_last = b[chunk - 1:chunk, :]
            k_dec = kb * jnp.exp(b_last - b).astype(BF16)
            state_ref[hd] = state_t * jnp.exp(b_last) + _dot_tn(vb, k_dec)
            o = _rms_rows(o, og_ref[...]).astype(BF16) * g_all[:, cols]
            gated_ref[rows, cols] = o
        return carry

    lax.fori_loop(0, tile // chunk, chunk_body, 0, unroll=True)
    y_ref[0] = _dot(gated_ref[...], wp_ref[...]).astype(BF16)


def _hgrn_call(x, gain, w, lb, out_gain, w_proj, batch, seq):
    tile, chunk = TOKEN_TILE, HGRN_CHUNK
    tri = jnp.tril(jnp.ones((chunk, chunk), BF16))
    blk = lambda width: pl.BlockSpec((1, tile, width), lambda b, i: (b, i, 0))
    y = pl.pallas_call(
        functools.partial(_hgrn_kernel, tile=tile, chunk=chunk),
        grid=(batch, seq // tile),
        in_specs=[blk(D_MODEL), _full((1, D_MODEL)), _full(w.shape), _full((1, HGRN_WIDTH)),
                  _full((1, HGRN_HEAD)), _full(w_proj.shape), _full((chunk, chunk))],
        out_specs=blk(D_MODEL),
        out_shape=jax.ShapeDtypeStruct((batch, seq, D_MODEL), BF16),
        scratch_shapes=[pltpu.VMEM((HGRN_HEADS, HGRN_HEAD, HGRN_HEAD), F32),
                        pltpu.VMEM((tile, 4 * HGRN_WIDTH), F32),
                        pltpu.VMEM((tile, HGRN_WIDTH), BF16),
                        pltpu.VMEM((chunk, HGRN_WIDTH), F32)],
        compiler_params=_params("parallel", "arbitrary"),
        name="hgrn_branch",
    )(x.reshape(batch, seq, D_MODEL), gain, w, lb, out_gain, w_proj, tri)
    return y.reshape(batch * seq, D_MODEL)


def _top2_route(h, rw_hi, rw_lo):
    logits = _dot(h, rw_hi) + _dot(h, rw_lo)
    lane = lax.broadcasted_iota(jnp.int32, logits.shape, 1)
    logits = jnp.where(lane < N_EXPERTS, logits, -jnp.inf)
    m1 = jnp.max(logits, axis=-1, keepdims=True)
    i1 = jnp.min(jnp.where(logits == m1, lane, LANES), axis=-1, keepdims=True)
    rest = jnp.where(lane == i1, -jnp.inf, logits)
    m2 = jnp.max(rest, axis=-1, keepdims=True)
    i2 = jnp.min(jnp.where(rest == m2, lane, LANES), axis=-1, keepdims=True)
    e = jnp.exp(m2 - m1)
    g1 = 1.0 / (1.0 + e)
    g2 = e * g1
    return jnp.where(
        lane == 0, g1, jnp.where(lane == 1, g2, jnp.where(
            lane == 2, i1.astype(F32), jnp.where(lane == 3, i2.astype(F32), 0.0))))


def _merge_kernel(x_ref, g_ref, w_ref, cw_ref, oa_ref, yc_ref, wpa_ref, wpb_ref, wo_ref, *rest,
                  tile, tiles_per_seq, steps, with_router):
    if with_router:
        fg_ref, rw_hi_ref, rw_lo_ref, out_ref, hn_ref, route_ref, u_ext, prev_ref = rest
    else:
        out_ref, u_ext = rest
    step = pl.program_id(0)

    @pl.when(step % tiles_per_seq == 0)
    def _():
        u_ext[0:SUBLANES, :] = jnp.zeros((SUBLANES, CONV_WIDTH), F32)

    def route_previous_tile():
        hn = _rms_rows(prev_ref[...], fg_ref[...])
        hn_ref[...] = hn
        route_ref[...] = _top2_route(hn.astype(BF16), rw_hi_ref[...], rw_lo_ref[...])

    def merge_tile():
        x = x_ref[...]
        h = _rms_rows(x, g_ref[...]).astype(BF16)
        p = _dot(h, w_ref[...])
        cw = CONV_WIDTH
        u = p[:, 2 * cw:3 * cw] * p[:, 0:cw]
        u_ext[SUBLANES:SUBLANES + tile, :] = u
        conv = cw_ref[0:1, :] * u
        for tap in range(1, CONV_K):
            conv = conv + cw_ref[tap:tap + 1, :] * u_ext[SUBLANES - tap:SUBLANES - tap + tile, :]
        u_ext[0:SUBLANES, :] = u[tile - SUBLANES:, :]
        yb = _dot((p[:, cw:2 * cw] * conv).astype(BF16), wpb_ref[...])
        ya = _dot(oa_ref[...], wpa_ref[...])

        d = D_MODEL
        gates = p[:, 3 * cw:]
        merged = (_sigmoid(gates[:, 0:d]) * ya + _sigmoid(gates[:, d:2 * d]) * yb
                  + _sigmoid(gates[:, 2 * d:3 * d]) * yc_ref[...].astype(F32))
        x_new = x + _dot(merged.astype(BF16), wo_ref[...])
        out_ref[...] = x_new
        return x_new

    if not with_router:
        merge_tile()
        return

    @pl.when(step == 0)
    def _():
        prev_ref[...] = jnp.zeros_like(prev_ref)

    @pl.when(step < steps)
    def _():
        route_previous_tile()
        prev_ref[...] = merge_tile()

    @pl.when(step == steps)
    def _():
        route_previous_tile()


def _merge_call(x, gain, w, conv_w, oa, yc, wpa, wpb, wo, seq, router=None):
    n = x.shape[0]
    tile = TOKEN_TILE
    steps = n // tile
    row = lambda width: pl.BlockSpec((tile, width), lambda i: (jnp.minimum(i, steps - 1), 0))
    late = lambda width: pl.BlockSpec((tile, width), lambda i: (jnp.maximum(i - 1, 0), 0))
    in_specs = [row(D_MODEL), _full((1, D_MODEL)), _full(w.shape), _full(conv_w.shape),
                row(GROUP_WIDTH), row(D_MODEL), _full(wpa.shape), _full(wpb.shape),
                _full(wo.shape)]
    args = [x, gain, w, conv_w, oa, yc, wpa, wpb, wo]
    out_specs, out_shape = [row(D_MODEL)], [jax.ShapeDtypeStruct((n, D_MODEL), F32)]
    scratch = [pltpu.VMEM((SUBLANES + tile, CONV_WIDTH), F32)]
    if router is not None:
        in_specs += [_full(a.shape) for a in router]
        args += list(router)
        out_specs += [late(D_MODEL), late(LANES)]
        out_shape += [jax.ShapeDtypeStruct((n, D_MODEL), F32),
                      jax.ShapeDtypeStruct((n, LANES), F32)]
        scratch.append(pltpu.VMEM((tile, D_MODEL), F32))
    outs = pl.pallas_call(
        functools.partial(_merge_kernel, tile=tile, tiles_per_seq=seq // tile, steps=steps,
                          with_router=router is not None),
        grid=(steps + (router is not None),),
        in_specs=in_specs,
        out_specs=out_specs,
        out_shape=out_shape,
        scratch_shapes=scratch,
        compiler_params=_params("arbitrary"),
        name="merge_out",
    )(*args)
    return outs[0] if router is None else outs


def _ffn_kernel(x_ref, g_ref, w1_ref, w3_ref, w2_ref, out_ref):
    x = x_ref[...]
    h = _rms_rows(x, g_ref[...]).astype(BF16)
    a = _dot(h, w1_ref[...])
    z = (a * _sigmoid(a) * _dot(h, w3_ref[...])).astype(BF16)
    out_ref[...] = x + _dot(z, w2_ref[...])


def _ffn_call(x, gain, w1, w3, w2):
    n = x.shape[0]
    tile = TOKEN_TILE
    row = pl.BlockSpec((tile, D_MODEL), lambda i: (i, 0))
    return pl.pallas_call(
        _ffn_kernel,
        grid=(n // tile,),
        in_specs=[row, _full((1, D_MODEL)), _full(w1.shape), _full(w3.shape), _full(w2.shape)],
        out_specs=row,
        out_shape=jax.ShapeDtypeStruct((n, D_MODEL), F32),
        compiler_params=_params("parallel"),
        name="dense_ffn",
    )(x, gain, w1, w3, w2)


def _block_copies(n8, issue, blocks=GROUP_BLOCKS):
    done = jnp.int32(0)
    for bit in blocks:
        take = (n8 & bit) != 0

        @pl.when(take)
        def _(done=done, bit=bit):
            issue(pl.multiple_of(done * SUBLANES, SUBLANES), bit * SUBLANES)

        done = done + jnp.where(take, bit, 0)


def _slot_positions(route, tri, local_base):
    expert = jnp.concatenate([route[:, 2:3], route[:, 3:4]], axis=0)
    lane = lax.broadcasted_iota(jnp.int32, (expert.shape[0], LANES), 1).astype(F32)
    onehot = expert == lane
    seen = _dot(tri, jnp.where(onehot, 1.0, 0.0).astype(BF16))
    pos = jnp.sum(jnp.where(onehot, seen - 1.0 + local_base, 0.0), axis=1, keepdims=True)
    t = route.shape[0]
    return pos[:t], pos[t:]


def _local_base_row(plan_ref, tile_index):
    lane = lax.broadcasted_iota(jnp.int32, (1, LANES), 1)
    row = jnp.zeros((1, LANES), F32)
    for e in range(N_EXPERTS):
        row = jnp.where(lane == e, plan_ref[tile_index * N_EXPERTS + e].astype(F32), row)
    return row


def _dispatch_kernel(plan_ref, hn_ref, route_ref, tri_ref, out_hbm, lpos_ref, sorted_ref, zeros,
                     sems, fill_sem, *, tile, n_groups):
    i = pl.program_id(0)
    fill0 = 3 * n_groups

    def fills(wait):
        for e in range(N_EXPERTS):
            first = pl.multiple_of(plan_ref[fill0 + e], SUBLANES)

            def issue(off, rows, first=first):
                cp = pltpu.make_async_copy(zeros.at[pl.ds(0, rows), :],
                                           out_hbm.at[pl.ds(first + off, rows), :], fill_sem)
                cp.wait() if wait else cp.start()

            _block_copies(plan_ref[fill0 + N_EXPERTS + e], issue, FILL_BLOCKS)

        def tail(k, carry):
            start = pl.multiple_of(plan_ref[fill0 + 2 * N_EXPERTS] + k * tile, tile)
            cp = pltpu.make_async_copy(zeros, out_hbm.at[pl.ds(start, tile), :], fill_sem)
            cp.wait() if wait else cp.start()
            return carry

        lax.fori_loop(0, plan_ref[fill0 + 2 * N_EXPERTS + 1], tail, 0)

    @pl.when(i == 0)
    def _():
        zeros[...] = jnp.zeros_like(zeros)
        fills(wait=False)
        fills(wait=True)

    pos1, pos2 = _slot_positions(route_ref[...], tri_ref[...], _local_base_row(plan_ref, i))
    lane = lax.broadcasted_iota(jnp.int32, (tile, LANES), 1)
    lpos_ref[...] = jnp.where(lane == 0, pos1, jnp.where(lane == 1, pos2, 0.0))
    col = lax.broadcasted_iota(jnp.int32, (tile, LOCAL_ROWS), 1).astype(F32)
    select = jnp.where((pos1 == col) | (pos2 == col), 1.0, 0.0).astype(BF16)

    def groups(step, wait):
        slot = step % 2
        for e in range(N_EXPERTS):
            g = step * N_EXPERTS + e
            local = pl.multiple_of(plan_ref[g], SUBLANES)
            dest = pl.multiple_of(plan_ref[2 * n_groups + g], SUBLANES)

            def issue(off, rows, local=local, dest=dest):
                cp = pltpu.make_async_copy(sorted_ref.at[slot, pl.ds(local + off, rows), :],
                                           out_hbm.at[pl.ds(dest + off, rows), :], sems.at[slot])
                cp.wait() if wait else cp.start()

            _block_copies(plan_ref[n_groups + g], issue)

    @pl.when(i > 1)
    def _():
        groups(i - 2, wait=True)

    sorted_ref[i % 2] = _dot_tn(select, hn_ref[...].astype(BF16))
    groups(i, wait=False)

    @pl.when(i == pl.num_programs(0) - 1)
    def _():
        @pl.when(i > 0)
        def _():
            groups(i - 1, wait=True)

        groups(i, wait=True)


def _dispatch_call(hn, route, plan, tri, n_sorted):
    n = hn.shape[0]
    tile = TOKEN_TILE
    n_groups = (n // tile) * N_EXPERTS
    row = lambda width: pl.BlockSpec((tile, width), lambda i, plan: (i, 0))
    grid_spec = pltpu.PrefetchScalarGridSpec(
        num_scalar_prefetch=1,
        grid=(n // tile,),
        in_specs=[row(D_MODEL), row(LANES),
                  pl.BlockSpec(tri.shape, lambda i, plan: (0, 0))],
        out_specs=[pl.BlockSpec(memory_space=pl.ANY), row(LANES)],
        scratch_shapes=[pltpu.VMEM((2, LOCAL_ROWS, D_MODEL), F32), pltpu.VMEM((tile, D_MODEL), F32),
                        pltpu.SemaphoreType.DMA((2,)), pltpu.SemaphoreType.DMA(())],
    )
    return pl.pallas_call(
        functools.partial(_dispatch_kernel, tile=tile, n_groups=n_groups),
        grid_spec=grid_spec,
        out_shape=[jax.ShapeDtypeStruct((n_sorted, D_MODEL), F32),
                   jax.ShapeDtypeStruct((n, LANES), F32)],
        compiler_params=_params("arbitrary"),
        name="moe_dispatch",
    )(plan, hn, route, tri)


def _moe_kernel(tile_expert_ref, n_tiles_ref, h_ref, w1_ref, w3_ref, w2_ref, y_ref):
    i, f = pl.program_id(0), pl.program_id(1)
    active = i < n_tiles_ref[0]

    @pl.when((i < 2) & (f == 0))
    def _():
        y_ref[...] = jnp.zeros_like(y_ref)

    @pl.when(active)
    def _():
        h = h_ref[...].astype(BF16)
        a = _dot(h, w1_ref[...])
        z = (a * _sigmoid(a) * _dot(h, w3_ref[...])).astype(BF16)
        y_ref[...] = jnp.where(f == 0, 0.0, y_ref[...]) + _dot(z, w2_ref[...])

    @pl.when(jnp.logical_not(active) & (f == 0))
    def _():
        y_ref[...] = jnp.zeros_like(y_ref)


def _moe_call(hs, w1, w3, w2, tile_expert, n_tiles):
    p = hs.shape[0]
    d_ff = w1.shape[-1]
    tm, tf = MOE_ROW_TILE, MOE_FF_TILE
    nf = d_ff // tf

    def ff_block(i, f, n_tiles_ref):
        return jnp.where(i < n_tiles_ref[0], f, nf - 1)

    grid_spec = pltpu.PrefetchScalarGridSpec(
        num_scalar_prefetch=2,
        grid=(p // tm, nf),
        in_specs=[
            pl.BlockSpec((tm, D_MODEL), lambda i, f, te, nt: (jnp.minimum(i, nt[0] - 1), 0)),
            pl.BlockSpec((None, D_MODEL, tf), lambda i, f, te, nt: (te[i], 0, ff_block(i, f, nt))),
            pl.BlockSpec((None, D_MODEL, tf), lambda i, f, te, nt: (te[i], 0, ff_block(i, f, nt))),
            pl.BlockSpec((None, tf, D_MODEL), lambda i, f, te, nt: (te[i], ff_block(i, f, nt), 0)),
        ],
        out_specs=pl.BlockSpec((tm, D_MODEL), lambda i, f, te, nt: (i, 0)),
    )
    return pl.pallas_call(
        _moe_kernel,
        grid_spec=grid_spec,
        out_shape=jax.ShapeDtypeStruct((p, D_MODEL), F32),
        compiler_params=_params("arbitrary", "arbitrary"),
        name="moe_experts",
    )(tile_expert, n_tiles, hs, w1, w3, w2)


def _combine_kernel(plan_ref, x_ref, route_ref, lpos_ref, y_hbm, out_ref, ybuf, sems, *,
                    tile, n_groups):
    i = pl.program_id(0)

    def groups(step, wait):
        slot = step % 2
        for e in range(N_EXPERTS):
            g = step * N_EXPERTS + e
            local = pl.multiple_of(plan_ref[g], SUBLANES)
            src = pl.multiple_of(plan_ref[2 * n_groups + g], SUBLANES)

            def issue(off, rows, local=local, src=src):
                cp = pltpu.make_async_copy(y_hbm.at[pl.ds(src + off, rows), :],
                                           ybuf.at[slot, pl.ds(local + off, rows), :], sems.at[slot])
                cp.wait() if wait else cp.start()

            _block_copies(plan_ref[n_groups + g], issue)

    @pl.when(i == 0)
    def _():
        ybuf[...] = jnp.zeros_like(ybuf)
        groups(i, wait=False)

    @pl.when(i + 1 < pl.num_programs(0))
    def _():
        groups(i + 1, wait=False)

    col = lax.broadcasted_iota(jnp.int32, (tile, LOCAL_ROWS), 1).astype(F32)
    pick1 = jnp.where(lpos_ref[:, 0:1] == col, 1.0, 0.0).astype(BF16)
    pick2 = jnp.where(lpos_ref[:, 1:2] == col, 1.0, 0.0).astype(BF16)
    groups(i, wait=True)
    last = i * N_EXPERTS + N_EXPERTS - 1
    used = plan_ref[last] + plan_ref[n_groups + last] * SUBLANES
    row = lax.broadcasted_iota(jnp.int32, (LOCAL_ROWS, 1), 0)
    y = jnp.where(row < used, ybuf[i % 2], 0.0).astype(BF16)
    out_ref[...] = (x_ref[...] + route_ref[:, 0:1] * _dot(pick1, y)
                    + route_ref[:, 1:2] * _dot(pick2, y))


def _combine_call(x, route, lpos, y, plan):
    n = x.shape[0]
    tile = TOKEN_TILE
    n_groups = (n // tile) * N_EXPERTS
    row = lambda width: pl.BlockSpec((tile, width), lambda i, plan: (i, 0))
    grid_spec = pltpu.PrefetchScalarGridSpec(
        num_scalar_prefetch=1,
        grid=(n // tile,),
        in_specs=[row(D_MODEL), row(LANES), row(LANES), pl.BlockSpec(memory_space=pl.ANY)],
        out_specs=row(D_MODEL),
        scratch_shapes=[pltpu.VMEM((2, LOCAL_ROWS, D_MODEL), F32), pltpu.SemaphoreType.DMA((2,))],
    )
    return pl.pallas_call(
        functools.partial(_combine_kernel, tile=tile, n_groups=n_groups),
        grid_spec=grid_spec,
        out_shape=jax.ShapeDtypeStruct(x.shape, F32),
        compiler_params=_params("arbitrary"),
        name="moe_combine",
    )(plan, x, route, lpos, y)


def _routing_plan(route, n):
    tile, tm = TOKEN_TILE, MOE_ROW_TILE
    n_tiles_tok = n // tile
    slack = -(-(n_tiles_tok * N_EXPERTS * SUBLANES) // tm) * tm
    p = 2 * n + slack + N_EXPERTS * tm
    experts = jnp.arange(N_EXPERTS, dtype=jnp.int32)
    choice = route[:, 2:4].astype(jnp.int32).reshape(n_tiles_tok, tile, 2)
    counts = jnp.sum((choice[..., None] == experts).astype(jnp.int32), axis=(1, 2))
    size8 = (counts + SUBLANES - 1) // SUBLANES
    size = size8 * SUBLANES
    local = jnp.cumsum(size, axis=1) - size
    total = jnp.sum(size, axis=0)
    padded = ((total + tm - 1) // tm) * tm
    ends = jnp.cumsum(padded)
    starts = ends - padded
    dest = starts[None, :] + jnp.cumsum(size, axis=0) - size
    n_tiles = (ends[-1] // tm).astype(jnp.int32)
    tile_start = jnp.arange(p // tm, dtype=jnp.int32) * tm
    tile_expert = jnp.sum((tile_start[:, None] >= ends[None, :]).astype(jnp.int32), axis=1)
    last_expert = jnp.sum((ends[-1] - 1 >= ends).astype(jnp.int32))
    tile_expert = jnp.minimum(tile_expert, last_expert).astype(jnp.int32)
    tail = jnp.stack([ends[-1], (p - ends[-1]) // tile])
    plan = jnp.concatenate([local.reshape(-1), size8.reshape(-1), dest.reshape(-1),
                            starts + total, (padded - total) // SUBLANES, tail]).astype(jnp.int32)
    return plan, tile_expert, n_tiles.reshape(1), p


def _moe_layer(x, hn, route, w1, w3, w2):
    n = x.shape[0]
    plan, tile_expert, n_tiles, n_sorted = _routing_plan(route, n)
    tri = jnp.tril(jnp.ones((2 * TOKEN_TILE, 2 * TOKEN_TILE), BF16))
    hs, lpos = _dispatch_call(hn, route, plan, tri, n_sorted)
    ys = _moe_call(hs, w1.astype(BF16), w3.astype(BF16), w2.astype(BF16), tile_expert, n_tiles)
    return _combine_call(x, route, lpos, ys, plan)


def kernel(x, attn_norm, ffn_norm, w_in, q_norm, k_norm, conv_w, hgrn_lower_bounds,
           hgrn_out_norm, w_proj_a, w_proj_b, w_proj_c, w_out, dense_w1, dense_w3,
           dense_w2, router_w, moe_w1, moe_w3, moe_w2):
    batch, seq, d = x.shape
    n = batch * seq
    depth = w_in.shape[0]
    assert d == D_MODEL and seq % ATTN_TILE == 0 and seq % TOKEN_TILE == 0
    assert all(window // dilation == BAND for window, dilation in ATTN_GROUPS)

    lbs = jax.nn.softmax(hgrn_lower_bounds.astype(F32), axis=0)
    lower_bound = jnp.cumsum(lbs, axis=0) - lbs[0]
    head_of_col = jnp.arange(ATTN_WIDTH) // HEAD_DIM
    hsum = (head_of_col[:, None] == jnp.arange(LANES)[None, :]).astype(BF16)
    hexp = jnp.concatenate([hsum.T, hsum.T], axis=0)
    a0, a1 = 0, 3 * ATTN_WIDTH
    b1 = a1 + 3 * CONV_WIDTH
    c1 = b1 + 4 * HGRN_WIDTH

    xf = x.reshape(n, d)
    for layer in range(depth):
        w = w_in[layer]
        w_attn = w[:, a0:a1].astype(BF16)
        w_hgrn = w[:, b1:c1].astype(BF16)
        w_conv_gate = jnp.concatenate([w[:, a1:b1], w[:, c1:]], axis=1).astype(BF16)
        gain = attn_norm[layer].reshape(1, d)
        heads = ATTN_WIDTH // HEAD_DIM
        qg = jnp.tile(q_norm[layer] * (HEAD_DIM ** -0.5), heads).reshape(1, ATTN_WIDTH)
        kg = jnp.tile(k_norm[layer], heads).reshape(1, ATTN_WIDTH)

        qkv = _qkv_call(xf, gain, w_attn, qg, kg, hsum, hexp, batch, seq)
        oa = _attn_call(qkv, batch, seq)
        yc = _hgrn_call(xf, gain, w_hgrn, lower_bound[layer].reshape(1, HGRN_WIDTH),
                        hgrn_out_norm[layer].reshape(1, HGRN_HEAD),
                        w_proj_c[layer].astype(BF16), batch, seq)
        fgain = ffn_norm[layer].reshape(1, d)
        j = layer // 2
        routed = layer % 2 == 1
        router = None
        if routed:
            rw = jnp.pad(router_w[j], ((0, 0), (0, LANES - N_EXPERTS)))
            rw_hi = rw.astype(BF16)
            router = (fgain, rw_hi, (rw - rw_hi.astype(F32)).astype(BF16))
        merged = _merge_call(xf, gain, w_conv_gate, conv_w[layer], oa, yc,
                             w_proj_a[layer].astype(BF16), w_proj_b[layer].astype(BF16),
                             w_out[layer].astype(BF16), seq, router)
        if routed:
            xf, hn, route = merged
            xf = _moe_layer(xf, hn, route, moe_w1[j], moe_w3[j], moe_w2[j])
        else:
            xf = _ffn_call(merged, fgain, dense_w1[j].astype(BF16), dense_w3[j].astype(BF16),
                           dense_w2[j].astype(BF16))
    return xf.reshape(batch, seq, d)
```

```python
import functools

import jax
import jax.numpy as jnp
from jax import lax
from jax.experimental import pallas as pl
from jax.experimental.pallas import tpu as pltpu

F32 = jnp.float32
BF16 = jnp.bfloat16

D_MODEL = 1024
HEAD_DIM = 64
ATTN_GROUPS = ((128, 1), (512, 4), (2048, 16))
HEADS_PER_GROUP = 4
GROUP_WIDTH = HEADS_PER_GROUP * HEAD_DIM
ATTN_WIDTH = GROUP_WIDTH * len(ATTN_GROUPS)
BAND = 128
CONV_WIDTH = 768
CONV_K = 3
HGRN_WIDTH = 768
HGRN_HEAD = 128
HGRN_HEADS = HGRN_WIDTH // HGRN_HEAD
LB_FLOOR = 1e-30
N_EXPERTS = 8
RMS_EPS = 1e-6
NEG_BIG = -1e30
LOG2E = 1.4426950408889634

LANES = 128
SUBLANES = 8

TOKEN_TILE = 512
ATTN_TILE = ATTN_GROUPS[-1][1] * BAND
ATTN_UNROLL = 8
HGRN_CHUNK = 256
MOE_ROW_TILE = 512
MOE_FF_TILE = 1792
LOCAL_ROWS = -(-(2 * TOKEN_TILE + N_EXPERTS * SUBLANES) // LANES) * LANES
GROUP_BLOCKS = tuple(1 << b for b in reversed(range(8)))
FILL_BLOCKS = tuple(b for b in GROUP_BLOCKS if b * SUBLANES < MOE_ROW_TILE)
VMEM_LIMIT = 56 * 1024 * 1024


def _dot(a, b):
    return jnp.dot(a, b, preferred_element_type=F32)


def _dot_nt(a, b):
    return lax.dot_general(a, b, (((1,), (1,)), ((), ())), preferred_element_type=F32)


def _dot_tn(a, b):
    return lax.dot_general(a, b, (((0,), (0,)), ((), ())), preferred_element_type=F32)


def _rms_rows(xf, gain):
    ms = jnp.mean(xf * xf, axis=-1, keepdims=True)
    return xf * lax.rsqrt(ms + RMS_EPS) * gain


def _sigmoid(z):
    return 1.0 / (1.0 + jnp.exp(-z))


def _params(*sem):
    return pltpu.CompilerParams(dimension_semantics=sem, vmem_limit_bytes=VMEM_LIMIT)


def _full(shape):
    return pl.BlockSpec(shape, lambda *_: (0,) * len(shape))


def _qkv_kernel(x_ref, g_ref, w_ref, qg_ref, kg_ref, hsum_ref, hexp_ref, *rest, tile):
    out_refs, stage = rest[:9], rest[9]
    h = _rms_rows(x_ref[0], g_ref[...]).astype(BF16)
    p = _dot(h, w_ref[...])

    def head_norm(t, gain):
        ms = _dot((t * t).astype(BF16), hsum_ref[...]) * (1.0 / HEAD_DIM)
        r = lax.rsqrt(ms + RMS_EPS)
        r_hi = r.astype(BF16)
        r_lo = (r - r_hi.astype(F32)).astype(BF16)
        r_cols = _dot(jnp.concatenate([r_hi, r_lo], axis=1), hexp_ref[...])
        return t * r_cols * gain

    parts = (head_norm(p[:, :ATTN_WIDTH], qg_ref[...]),
             head_norm(p[:, ATTN_WIDTH:2 * ATTN_WIDTH], kg_ref[...]),
             p[:, 2 * ATTN_WIDTH:])
    slot = 0
    for a, part in enumerate(parts):
        for group, (_, dilation) in enumerate(ATTN_GROUPS):
            out = out_refs[3 * group + a]
            c0 = group * GROUP_WIDTH
            if dilation == 1:
                out[0, 0] = part[:, c0:c0 + GROUP_WIDTH].astype(BF16)
                continue
            rows = tile // dilation
            for half in range(GROUP_WIDTH // LANES):
                stage[slot] = part[:, c0 + half * LANES:c0 + (half + 1) * LANES]
                for r in range(dilation):
                    out[0, r, :, half * LANES:(half + 1) * LANES] = (
                        stage[slot, pl.ds(r, rows, stride=dilation), :].astype(BF16))
                slot += 1


def _qkv_call(x, gain, w, qg, kg, hsum, hexp, batch, seq):
    t = TOKEN_TILE
    out_specs, out_shape = [], []
    for _, dilation in ATTN_GROUPS:
        for _ in range(3):
            out_specs.append(pl.BlockSpec((1, dilation, t // dilation, GROUP_WIDTH),
                                          lambda b, i: (b, 0, i, 0)))
            out_shape.append(jax.ShapeDtypeStruct(
                (batch, dilation, seq // dilation, GROUP_WIDTH), BF16))
    n_stage = 3 * (GROUP_WIDTH // LANES) * sum(1 for _, d in ATTN_GROUPS if d > 1)
    return pl.pallas_call(
        functools.partial(_qkv_kernel, tile=t),
        grid=(batch, seq // t),
        in_specs=[pl.BlockSpec((1, t, D_MODEL), lambda b, i: (b, i, 0)), _full((1, D_MODEL)),
                  _full(w.shape), _full((1, ATTN_WIDTH)), _full((1, ATTN_WIDTH)),
                  _full(hsum.shape), _full(hexp.shape)],
        out_specs=out_specs,
        out_shape=out_shape,
        scratch_shapes=[pltpu.VMEM((n_stage, t, LANES), F32)],
        compiler_params=_params("parallel", "parallel"),
        name="qkv_proj",
    )(x.reshape(batch, seq, D_MODEL), gain, w, qg, kg, hsum, hexp)


def _attn_kernel(*refs, n_groups):
    ins, o_ref, scr = refs[:5 * n_groups], refs[5 * n_groups], refs[5 * n_groups + 1:]
    kbufs, vbufs = scr[0:n_groups], scr[n_groups:2 * n_groups]
    o_tok, l_tok, o_rm, l_rm = scr[2 * n_groups:]
    first_tile = pl.program_id(1) == 0
    qi = lax.broadcasted_iota(jnp.int32, (BAND, 2 * BAND), 0)
    kj = lax.broadcasted_iota(jnp.int32, (BAND, 2 * BAND), 1)
    rel = qi + BAND - kj
    in_window = (rel >= 0) & (rel <= BAND)
    halves = GROUP_WIDTH // LANES
    lane = lax.broadcasted_iota(jnp.int32, (1, LANES), 1)
    head_lanes = [jnp.where(lane // HEAD_DIM == side, 1.0, 0.0).astype(BF16)
                  for side in range(LANES // HEAD_DIM)]
    first_head = lax.broadcasted_iota(jnp.int32, (BAND, LANES), 1) < HEAD_DIM

    for group, (_, dilation) in enumerate(ATTN_GROUPS):
        q_ref, kc_ref, vc_ref, kp_ref, vp_ref = ins[5 * group:5 * group + 5]
        kbuf, vbuf = kbufs[group], vbufs[group]
        length = ATTN_TILE // dilation
        nj = length // BAND
        kbuf[:, 0:BAND, :] = kp_ref[0]
        kbuf[:, BAND:BAND + length, :] = kc_ref[0]
        vbuf[:, 0:BAND, :] = vp_ref[0]
        vbuf[:, BAND:BAND + length, :] = vc_ref[0]

        def block(idx, carry, q_ref=q_ref, kbuf=kbuf, vbuf=vbuf, nj=nj, length=length,
                  group=group, dilation=dilation):
            r, j = idx // nj, idx % nj
            j0 = pl.multiple_of(j * BAND, BAND)
            kmin = jnp.where(first_tile & (j == 0), BAND, 0)
            valid = in_window & (kj >= kmin)
            row0 = pl.multiple_of(r * length + j0, BAND)
            for half in range(halves):
                lanes = slice(half * LANES, (half + 1) * LANES)
                q2 = q_ref[0, r, pl.ds(j0, BAND), lanes]
                k2 = kbuf[r, pl.ds(j0, 2 * BAND), lanes]
                v2 = vbuf[r, pl.ds(j0, 2 * BAND), lanes]
                o2, l2 = None, None
                for side in range(LANES // HEAD_DIM):
                    s = _dot_nt(q2 * head_lanes[side], k2)
                    s = jnp.where(valid, s, NEG_BIG)
                    m = jnp.max(s, axis=-1, keepdims=True)
                    p = jnp.exp(s - m)
                    den = jnp.sum(p, axis=-1, keepdims=True)
                    o = _dot(p.astype(BF16), v2) * (1.0 / den)
                    lse = jnp.broadcast_to(m + jnp.log(den), (BAND, LANES))
                    o2 = o if side == 0 else jnp.where(first_head, o2, o)
                    l2 = lse if side == 0 else jnp.where(first_head, l2, lse)
                if dilation == 1:
                    o_tok[group, half, pl.ds(row0, BAND), :] = o2
                    l_tok[group, half, pl.ds(row0, BAND), :] = l2
                else:
                    o_rm[pl.ds(row0, BAND), lanes] = o2
                    l_rm[pl.ds(row0, BAND), lanes] = l2
            return carry

        lax.fori_loop(0, ATTN_TILE // BAND, block, 0, unroll=ATTN_UNROLL)
        if dilation > 1:
            for r in range(dilation):
                rows = slice(r * length, (r + 1) * length)
                for half in range(halves):
                    lanes = slice(half * LANES, (half + 1) * LANES)
                    o_tok[group, half, pl.ds(r, length, stride=dilation), :] = o_rm[rows, lanes]
                    l_tok[group, half, pl.ds(r, length, stride=dilation), :] = l_rm[rows, lanes]

    chunk = 2 * BAND

    def merge(c, carry):
        rows = pl.ds(pl.multiple_of(c * chunk, chunk), chunk)
        for half in range(halves):
            ls = [l_tok[g, half, rows, :] for g in range(n_groups)]
            top = functools.reduce(jnp.maximum, ls)
            es = [jnp.exp(l - top) for l in ls]
            acc = sum(e * o_tok[g, half, rows, :] for g, e in enumerate(es))
            o_ref[0, rows, half * LANES:(half + 1) * LANES] = (acc * (1.0 / sum(es))).astype(BF16)
        return carry

    lax.fori_loop(0, ATTN_TILE // chunk, merge, 0)


def _attn_call(qkv, batch, seq):
    n_groups = len(ATTN_GROUPS)
    in_specs, args, kbufs = [], [], []
    for group, (_, dilation) in enumerate(ATTN_GROUPS):
        length = ATTN_TILE // dilation
        cur = pl.BlockSpec((1, dilation, length, GROUP_WIDTH), lambda b, i: (b, 0, i, 0))
        halo = pl.BlockSpec(
            (1, dilation, BAND, GROUP_WIDTH),
            lambda b, i, nb=length // BAND: (b, 0, jnp.maximum(i * nb - 1, 0), 0))
        q, k, v = qkv[3 * group:3 * group + 3]
        in_specs += [cur, cur, cur, halo, halo]
        args += [q, k, v, k, v]
        kbufs.append(pltpu.VMEM((dilation, BAND + length, GROUP_WIDTH), BF16))
    halves = GROUP_WIDTH // LANES
    o = pl.pallas_call(
        functools.partial(_attn_kernel, n_groups=n_groups),
        grid=(batch, seq // ATTN_TILE),
        in_specs=in_specs,
        out_specs=pl.BlockSpec((1, ATTN_TILE, GROUP_WIDTH), lambda b, i: (b, i, 0)),
        out_shape=jax.ShapeDtypeStruct((batch, seq, GROUP_WIDTH), BF16),
        scratch_shapes=kbufs + kbufs + [
            pltpu.VMEM((n_groups, halves, ATTN_TILE, LANES), F32),
            pltpu.VMEM((n_groups, halves, ATTN_TILE, LANES), F32),
            pltpu.VMEM((ATTN_TILE, GROUP_WIDTH), F32),
            pltpu.VMEM((ATTN_TILE, GROUP_WIDTH), F32)],
        compiler_params=_params("parallel", "arbitrary"),
        name="band_attn",
    )(*args)
    return o.reshape(batch * seq, GROUP_WIDTH)


def _hgrn_kernel(x_ref, g_ref, w_ref, lb_ref, og_ref, wp_ref, tri_ref, y_ref,
                 state_ref, proj_ref, gated_ref, b_ref, *, tile, chunk):
    @pl.when(pl.program_id(1) == 0)
    def _():
        state_ref[...] = jnp.zeros_like(state_ref)

    h = _rms_rows(x_ref[0], g_ref[...]).astype(BF16)
    proj_ref[...] = _dot(h, w_ref[...])

    lb = lb_ref[...]
    lb_floor = jnp.maximum(lb, LB_FLOOR)
    one_minus_lb = 1.0 - lb
    floor_gap = lb_floor - lb
    row = lax.broadcasted_iota(jnp.int32, (chunk, HGRN_HEAD), 0)
    tt = lax.broadcasted_iota(jnp.int32, (chunk, chunk), 0)
    ss = lax.broadcasted_iota(jnp.int32, (chunk, chunk), 1)
    t_xor_s = tt ^ ss
    w = HGRN_WIDTH
    sizes = [1 << i for i in range(chunk.bit_length() - 1)]
    lower_rows = [(row & m) == 0 for m in sizes]
    lower_sel = [jnp.where(low, 1.0, 0.0).astype(BF16) for low in lower_rows]
    signed_log2e = [jnp.where(low, -LOG2E, LOG2E).astype(BF16) for low in lower_rows]
    level_pairs = [((t_xor_s & (-m)) == m) & ((tt & m) != 0) for m in sizes]
    diagonal = tt == ss

    def chunk_body(c, carry):
        r0 = pl.multiple_of(c * chunk, chunk)
        rows = pl.ds(r0, chunk)
        z = proj_ref[rows, w:2 * w]
        e = jnp.exp(-jnp.abs(z))
        inv = 1.0 / (1.0 + e)
        sig = jnp.where(z >= 0, inv, e * inv)
        nsig = jnp.where(z >= 0, e * inv, inv)
        log_f = jnp.log(lb_floor + one_minus_lb * sig)
        k_all = one_minus_lb * nsig - floor_gap
        lf_hi = log_f.astype(BF16)
        rest = log_f - lf_hi.astype(F32)
        lf_mid = rest.astype(BF16)
        lf_lo = (rest - lf_mid.astype(F32)).astype(BF16)
        tri = tri_ref[...]
        b_all = _dot(tri, lf_hi) + _dot(tri, lf_mid) + _dot(tri, lf_lo)
        b_ref[...] = b_all
        q_all = proj_ref[rows, 0:w].astype(BF16)
        q_all = q_all * _sigmoid(q_all)
        v_all = proj_ref[rows, 2 * w:3 * w]
        g_all = proj_ref[rows, 3 * w:4 * w].astype(BF16)
        g_all = g_all * _sigmoid(g_all)
        for hd in range(HGRN_HEADS):
            cols = slice(hd * HGRN_HEAD, (hd + 1) * HGRN_HEAD)
            b, qb = b_all[:, cols], q_all[:, cols]
            kb = k_all[:, cols].astype(BF16)
            vb = v_all[:, cols].astype(BF16)
            scores = jnp.where(diagonal, _dot_nt(qb, kb), 0.0)
            last_of_block = b
            for lvl, m in enumerate(sizes):
                if m < SUBLANES:
                    lower = lower_rows[lvl]
                    boundary = jnp.where(lower, last_of_block, pltpu.roll(last_of_block, m, 0))
                    if 2 * m < SUBLANES:
                        last_of_block = jnp.where(
                            lower, pltpu.roll(last_of_block, chunk - m, 0), last_of_block)
                else:
                    boundary = jnp.concatenate(
                        [jnp.broadcast_to(b_ref[r:r + 1, cols], (2 * m, HGRN_HEAD))
                         for r in range(m - 1, chunk, 2 * m)], axis=0)
                decay = jnp.exp2((b - boundary).astype(BF16) * signed_log2e[lvl])
                if m < 2 * SUBLANES:
                    source = jnp.where(lower_sel[lvl] > 0, kb, qb)
                else:
                    source = jnp.concatenate(
                        [(kb if (r // m) % 2 == 0 else qb)[r:r + m] for r in range(0, chunk, m)],
                        axis=0)
                side = source * decay
                scores = jnp.where(level_pairs[lvl], _dot_nt(side, side), scores)
            state_t = state_ref[hd]
            q_dec = qb * jnp.exp(b).astype(BF16)
            o = _dot(scores.astype(BF16), vb) + _dot_nt(q_dec, state_t.astype(BF16))
            b_last = b[chunk - 1:chunk, :]
            k_dec = kb * jnp.exp(b_last - b).astype(BF16)
            state_ref[hd] = state_t * jnp.exp(b_last) + _dot_tn(vb, k_dec)
            o = _rms_rows(o, og_ref[...]).astype(BF16) * g_all[:, cols]
            gated_ref[rows, cols] = o
        return carry

    lax.fori_loop(0, tile // chunk, chunk_body, 0, unroll=True)
    y_ref[0] = _dot(gated_ref[...], wp_ref[...]).astype(BF16)


def _hgrn_call(x, gain, w, lb, out_gain, w_proj, batch, seq):
    tile, chunk = TOKEN_TILE, HGRN_CHUNK
    tri = jnp.tril(jnp.ones((chunk, chunk), BF16))
    blk = lambda width: pl.BlockSpec((1, tile, width), lambda b, i: (b, i, 0))
    y = pl.pallas_call(
        functools.partial(_hgrn_kernel, tile=tile, chunk=chunk),
        grid=(batch, seq // tile),
        in_specs=[blk(D_MODEL), _full((1, D_MODEL)), _full(w.shape), _full((1, HGRN_WIDTH)),
                  _full((1, HGRN_HEAD)), _full(w_proj.shape), _full((chunk, chunk))],
        out_specs=blk(D_MODEL),
        out_shape=jax.ShapeDtypeStruct((batch, seq, D_MODEL), BF16),
        scratch_shapes=[pltpu.VMEM((HGRN_HEADS, HGRN_HEAD, HGRN_HEAD), F32),
                        pltpu.VMEM((tile, 4 * HGRN_WIDTH), F32),
                        pltpu.VMEM((tile, HGRN_WIDTH), BF16),
                        pltpu.VMEM((chunk, HGRN_WIDTH), F32)],
        compiler_params=_params("parallel", "arbitrary"),
        name="hgrn_branch",
    )(x.reshape(batch, seq, D_MODEL), gain, w, lb, out_gain, w_proj, tri)
    return y.reshape(batch * seq, D_MODEL)


def _top2_route(h, rw_hi, rw_lo):
    logits = _dot(h, rw_hi) + _dot(h, rw_lo)
    lane = lax.broadcasted_iota(jnp.int32, logits.shape, 1)
    logits = jnp.where(lane < N_EXPERTS, logits, -jnp.inf)
    m1 = jnp.max(logits, axis=-1, keepdims=True)
    i1 = jnp.min(jnp.where(logits == m1, lane, LANES), axis=-1, keepdims=True)
    rest = jnp.where(lane == i1, -jnp.inf, logits)
    m2 = jnp.max(rest, axis=-1, keepdims=True)
    i2 = jnp.min(jnp.where(rest == m2, lane, LANES), axis=-1, keepdims=True)
    e = jnp.exp(m2 - m1)
    g1 = 1.0 / (1.0 + e)
    g2 = e * g1
    return jnp.where(
        lane == 0, g1, jnp.where(lane == 1, g2, jnp.where(
            lane == 2, i1.astype(F32), jnp.where(lane == 3, i2.astype(F32), 0.0))))


def _merge_kernel(x_ref, g_ref, w_ref, cw_ref, oa_ref, yc_ref, wpa_ref, wpb_ref, wo_ref, *rest,
                  tile, tiles_per_seq, steps, with_router):
    if with_router:
        fg_ref, rw_hi_ref, rw_lo_ref, out_ref, hn_ref, route_ref, u_ext, prev_ref = rest
    else:
        out_ref, u_ext = rest
    step = pl.program_id(0)

    @pl.when(step % tiles_per_seq == 0)
    def _():
        u_ext[0:SUBLANES, :] = jnp.zeros((SUBLANES, CONV_WIDTH), F32)

    def route_previous_tile():
        hn = _rms_rows(prev_ref[...], fg_ref[...])
        hn_ref[...] = hn
        route_ref[...] = _top2_route(hn.astype(BF16), rw_hi_ref[...], rw_lo_ref[...])

    def merge_tile():
        x = x_ref[...]
        h = _rms_rows(x, g_ref[...]).astype(BF16)
        p = _dot(h, w_ref[...])
        cw = CONV_WIDTH
        u = p[:, 2 * cw:3 * cw] * p[:, 0:cw]
        u_ext[SUBLANES:SUBLANES + tile, :] = u
        conv = cw_ref[0:1, :] * u
        for tap in range(1, CONV_K):
            conv = conv + cw_ref[tap:tap + 1, :] * u_ext[SUBLANES - tap:SUBLANES - tap + tile, :]
        u_ext[0:SUBLANES, :] = u[tile - SUBLANES:, :]
        yb = _dot((p[:, cw:2 * cw] * conv).astype(BF16), wpb_ref[...])
        ya = _dot(oa_ref[...], wpa_ref[...])

        d = D_MODEL
        gates = p[:, 3 * cw:]
        merged = (_sigmoid(gates[:, 0:d]) * ya + _sigmoid(gates[:, d:2 * d]) * yb
                  + _sigmoid(gates[:, 2 * d:3 * d]) * yc_ref[...].astype(F32))
        x_new = x + _dot(merged.astype(BF16), wo_ref[...])
        out_ref[...] = x_new
        return x_new

    if not with_router:
        merge_tile()
        return

    @pl.when(step == 0)
    def _():
        prev_ref[...] = jnp.zeros_like(prev_ref)

    @pl.when(step < steps)
    def _():
        route_previous_tile()
        prev_ref[...] = merge_tile()

    @pl.when(step == steps)
    def _():
        route_previous_tile()


def _merge_call(x, gain, w, conv_w, oa, yc, wpa, wpb, wo, seq, router=None):
    n = x.shape[0]
    tile = TOKEN_TILE
    steps = n // tile
    row = lambda width: pl.BlockSpec((tile, width), lambda i: (jnp.minimum(i, steps - 1), 0))
    late = lambda width: pl.BlockSpec((tile, width), lambda i: (jnp.maximum(i - 1, 0), 0))
    in_specs = [row(D_MODEL), _full((1, D_MODEL)), _full(w.shape), _full(conv_w.shape),
                row(GROUP_WIDTH), row(D_MODEL), _full(wpa.shape), _full(wpb.shape),
                _full(wo.shape)]
    args = [x, gain, w, conv_w, oa, yc, wpa, wpb, wo]
    out_specs, out_shape = [row(D_MODEL)], [jax.ShapeDtypeStruct((n, D_MODEL), F32)]
    scratch = [pltpu.VMEM((SUBLANES + tile, CONV_WIDTH), F32)]
    if router is not None:
        in_specs += [_full(a.shape) for a in router]
        args += list(router)
        out_specs += [late(D_MODEL), late(LANES)]
        out_shape += [jax.ShapeDtypeStruct((n, D_MODEL), F32),
                      jax.ShapeDtypeStruct((n, LANES), F32)]
        scratch.append(pltpu.VMEM((tile, D_MODEL), F32))
    outs = pl.pallas_call(
        functools.partial(_merge_kernel, tile=tile, tiles_per_seq=seq // tile, steps=steps,
                          with_router=router is not None),
        grid=(steps + (router is not None),),
        in_specs=in_specs,
        out_specs=out_specs,
        out_shape=out_shape,
        scratch_shapes=scratch,
        compiler_params=_params("arbitrary"),
        name="merge_out",
    )(*args)
    return outs[0] if router is None else outs


def _ffn_kernel(x_ref, g_ref, w1_ref, w3_ref, w2_ref, out_ref):
    x = x_ref[...]
    h = _rms_rows(x, g_ref[...]).astype(BF16)
    a = _dot(h, w1_ref[...])
    z = (a * _sigmoid(a) * _dot(h, w3_ref[...])).astype(BF16)
    out_ref[...] = x + _dot(z, w2_ref[...])


def _ffn_call(x, gain, w1, w3, w2):
    n = x.shape[0]
    tile = TOKEN_TILE
    row = pl.BlockSpec((tile, D_MODEL), lambda i: (i, 0))
    return pl.pallas_call(
        _ffn_kernel,
        grid=(n // tile,),
        in_specs=[row, _full((1, D_MODEL)), _full(w1.shape), _full(w3.shape), _full(w2.shape)],
        out_specs=row,
        out_shape=jax.ShapeDtypeStruct((n, D_MODEL), F32),
        compiler_params=_params("parallel"),
        name="dense_ffn",
    )(x, gain, w1, w3, w2)


def _block_copies(n8, issue, blocks=GROUP_BLOCKS):
    done = jnp.int32(0)
    for bit in blocks:
        take = (n8 & bit) != 0

        @pl.when(take)
        def _(done=done, bit=bit):
            issue(pl.multiple_of(done * SUBLANES, SUBLANES), bit * SUBLANES)

        done = done + jnp.where(take, bit, 0)


def _slot_positions(route, tri, local_base):
    expert = jnp.concatenate([route[:, 2:3], route[:, 3:4]], axis=0)
    lane = lax.broadcasted_iota(jnp.int32, (expert.shape[0], LANES), 1).astype(F32)
    onehot = expert == lane
    seen = _dot(tri, jnp.where(onehot, 1.0, 0.0).astype(BF16))
    pos = jnp.sum(jnp.where(onehot, seen - 1.0 + local_base, 0.0), axis=1, keepdims=True)
    t = route.shape[0]
    return pos[:t], pos[t:]


def _local_base_row(plan_ref, tile_index):
    lane = lax.broadcasted_iota(jnp.int32, (1, LANES), 1)
    row = jnp.zeros((1, LANES), F32)
    for e in range(N_EXPERTS):
        row = jnp.where(lane == e, plan_ref[tile_index * N_EXPERTS + e].astype(F32), row)
    return row


def _dispatch_kernel(plan_ref, hn_ref, route_ref, tri_ref, w1_ref, w3_ref, w2_ref,
                     out_hbm, lpos_ref, w1_bf_ref, w3_bf_ref, w2_bf_ref,
                     sorted_ref, zeros, sems, fill_sem, *, tile, n_groups):
    i = pl.program_id(0)
    for src, dst in ((w1_ref, w1_bf_ref), (w3_ref, w3_bf_ref), (w2_ref, w2_bf_ref)):
        dst[...] = src[...].astype(BF16)
    fill0 = 3 * n_groups

    def fills(wait):
        for e in range(N_EXPERTS):
            first = pl.multiple_of(plan_ref[fill0 + e], SUBLANES)

            def issue(off, rows, first=first):
                cp = pltpu.make_async_copy(zeros.at[pl.ds(0, rows), :],
                                           out_hbm.at[pl.ds(first + off, rows), :], fill_sem)
                cp.wait() if wait else cp.start()

            _block_copies(plan_ref[fill0 + N_EXPERTS + e], issue, FILL_BLOCKS)

        def tail(k, carry):
            start = pl.multiple_of(plan_ref[fill0 + 2 * N_EXPERTS] + k * tile, tile)
            cp = pltpu.make_async_copy(zeros, out_hbm.at[pl.ds(start, tile), :], fill_sem)
            cp.wait() if wait else cp.start()
            return carry

        lax.fori_loop(0, plan_ref[fill0 + 2 * N_EXPERTS + 1], tail, 0)

    @pl.when(i == 0)
    def _():
        zeros[...] = jnp.zeros_like(zeros)
        fills(wait=False)
        fills(wait=True)

    pos1, pos2 = _slot_positions(route_ref[...], tri_ref[...], _local_base_row(plan_ref, i))
    lane = lax.broadcasted_iota(jnp.int32, (tile, LANES), 1)
    lpos_ref[...] = jnp.where(lane == 0, pos1, jnp.where(lane == 1, pos2, 0.0))
    col = lax.broadcasted_iota(jnp.int32, (tile, LOCAL_ROWS), 1).astype(F32)
    select = jnp.where((pos1 == col) | (pos2 == col), 1.0, 0.0).astype(BF16)

    def groups(step, wait):
        slot = step % 2
        for e in range(N_EXPERTS):
            g = step * N_EXPERTS + e
            local = pl.multiple_of(plan_ref[g], SUBLANES)
            dest = pl.multiple_of(plan_ref[2 * n_groups + g], SUBLANES)

            def issue(off, rows, local=local, dest=dest):
                cp = pltpu.make_async_copy(sorted_ref.at[slot, pl.ds(local + off, rows), :],
                                           out_hbm.at[pl.ds(dest + off, rows), :], sems.at[slot])
                cp.wait() if wait else cp.start()

            _block_copies(plan_ref[n_groups + g], issue)

    @pl.when(i > 1)
    def _():
        groups(i - 2, wait=True)

    sorted_ref[i % 2] = _dot_tn(select, hn_ref[...].astype(BF16))
    groups(i, wait=False)

    @pl.when(i == pl.num_programs(0) - 1)
    def _():
        @pl.when(i > 0)
        def _():
            groups(i - 1, wait=True)

        groups(i, wait=True)


def _dispatch_call(hn, route, plan, tri, n_sorted, weights):
    n = hn.shape[0]
    tile = TOKEN_TILE
    steps = n // tile
    n_groups = steps * N_EXPERTS
    row = lambda width: pl.BlockSpec((tile, width), lambda i, plan: (i, 0))
    flat = [w.reshape(-1, w.shape[-1]) for w in weights]
    assert all(w.shape[0] % (steps * 2 * SUBLANES) == 0 for w in flat)
    slab = lambda w: pl.BlockSpec((w.shape[0] // steps, w.shape[1]), lambda i, plan: (i, 0))
    grid_spec = pltpu.PrefetchScalarGridSpec(
        num_scalar_prefetch=1,
        grid=(steps,),
        in_specs=[row(D_MODEL), row(LANES), pl.BlockSpec(tri.shape, lambda i, plan: (0, 0))]
                 + [slab(w) for w in flat],
        out_specs=[pl.BlockSpec(memory_space=pl.ANY), row(LANES)] + [slab(w) for w in flat],
        scratch_shapes=[pltpu.VMEM((2, LOCAL_ROWS, D_MODEL), F32), pltpu.VMEM((tile, D_MODEL), F32),
                        pltpu.SemaphoreType.DMA((2,)), pltpu.SemaphoreType.DMA(())],
    )
    hs, lpos, *converted = pl.pallas_call(
        functools.partial(_dispatch_kernel, tile=tile, n_groups=n_groups),
        grid_spec=grid_spec,
        out_shape=[jax.ShapeDtypeStruct((n_sorted, D_MODEL), F32),
                   jax.ShapeDtypeStruct((n, LANES), F32)]
                  + [jax.ShapeDtypeStruct(w.shape, BF16) for w in flat],
        compiler_params=_params("arbitrary"),
        name="moe_dispatch",
    )(plan, hn, route, tri, *flat)
    return hs, lpos, [c.reshape(w.shape) for c, w in zip(converted, weights)]


def _moe_kernel(tile_expert_ref, n_tiles_ref, h_ref, w1_ref, w3_ref, w2_ref, y_ref):
    i, f = pl.program_id(0), pl.program_id(1)
    active = i < n_tiles_ref[0]

    @pl.when((i < 2) & (f == 0))
    def _():
        y_ref[...] = jnp.zeros_like(y_ref)

    @pl.when(active)
    def _():
        h = h_ref[...].astype(BF16)
        a = _dot(h, w1_ref[...])
        z = (a * _sigmoid(a) * _dot(h, w3_ref[...])).astype(BF16)
        y_ref[...] = jnp.where(f == 0, 0.0, y_ref[...]) + _dot(z, w2_ref[...])

    @pl.when(jnp.logical_not(active) & (f == 0))
    def _():
        y_ref[...] = jnp.zeros_like(y_ref)


def _moe_call(hs, w1, w3, w2, tile_expert, n_tiles):
    p = hs.shape[0]
    d_ff = w1.shape[-1]
    tm, tf = MOE_ROW_TILE, MOE_FF_TILE
    nf = d_ff // tf

    def ff_block(i, f, n_tiles_ref):
        return jnp.where(i < n_tiles_ref[0], f, nf - 1)

    grid_spec = pltpu.PrefetchScalarGridSpec(
        num_scalar_prefetch=2,
        grid=(p // tm, nf),
        in_specs=[
            pl.BlockSpec((tm, D_MODEL), lambda i, f, te, nt: (jnp.minimum(i, nt[0] - 1), 0)),
            pl.BlockSpec((None, D_MODEL, tf), lambda i, f, te, nt: (te[i], 0, ff_block(i, f, nt))),
            pl.BlockSpec((None, D_MODEL, tf), lambda i, f, te, nt: (te[i], 0, ff_block(i, f, nt))),
            pl.BlockSpec((None, tf, D_MODEL), lambda i, f, te, nt: (te[i], ff_block(i, f, nt), 0)),
        ],
        out_specs=pl.BlockSpec((tm, D_MODEL), lambda i, f, te, nt: (i, 0)),
    )
    return pl.pallas_call(
        _moe_kernel,
        grid_spec=grid_spec,
        out_shape=jax.ShapeDtypeStruct((p, D_MODEL), F32),
        compiler_params=_params("arbitrary", "arbitrary"),
        name="moe_experts",
    )(tile_expert, n_tiles, hs, w1, w3, w2)


def _combine_kernel(plan_ref, x_ref, route_ref, lpos_ref, y_hbm, out_ref, ybuf, sems, *,
                    tile, n_groups):
    i = pl.program_id(0)

    def groups(step, wait):
        slot = step % 2
        for e in range(N_EXPERTS):
            g = step * N_EXPERTS + e
            local = pl.multiple_of(plan_ref[g], SUBLANES)
            src = pl.multiple_of(plan_ref[2 * n_groups + g], SUBLANES)

            def issue(off, rows, local=local, src=src):
                cp = pltpu.make_async_copy(y_hbm.at[pl.ds(src + off, rows), :],
                                           ybuf.at[slot, pl.ds(local + off, rows), :], sems.at[slot])
                cp.wait() if wait else cp.start()

            _block_copies(plan_ref[n_groups + g], issue)

    @pl.when(i == 0)
    def _():
        ybuf[...] = jnp.zeros_like(ybuf)
        groups(i, wait=False)

    @pl.when(i + 1 < pl.num_programs(0))
    def _():
        groups(i + 1, wait=False)

    col = lax.broadcasted_iota(jnp.int32, (tile, LOCAL_ROWS), 1).astype(F32)
    pick1 = jnp.where(lpos_ref[:, 0:1] == col, 1.0, 0.0).astype(BF16)
    pick2 = jnp.where(lpos_ref[:, 1:2] == col, 1.0, 0.0).astype(BF16)
    groups(i, wait=True)
    last = i * N_EXPERTS + N_EXPERTS - 1
    used = plan_ref[last] + plan_ref[n_groups + last] * SUBLANES
    row = lax.broadcasted_iota(jnp.int32, (LOCAL_ROWS, 1), 0)
    y = jnp.where(row < used, ybuf[i % 2], 0.0).astype(BF16)
    out_ref[...] = (x_ref[...] + route_ref[:, 0:1] * _dot(pick1, y)
                    + route_ref[:, 1:2] * _dot(pick2, y))


def _combine_call(x, route, lpos, y, plan):
    n = x.shape[0]
    tile = TOKEN_TILE
    n_groups = (n // tile) * N_EXPERTS
    row = lambda width: pl.BlockSpec((tile, width), lambda i, plan: (i, 0))
    grid_spec = pltpu.PrefetchScalarGridSpec(
        num_scalar_prefetch=1,
        grid=(n // tile,),
        in_specs=[row(D_MODEL), row(LANES), row(LANES), pl.BlockSpec(memory_space=pl.ANY)],
        out_specs=row(D_MODEL),
        scratch_shapes=[pltpu.VMEM((2, LOCAL_ROWS, D_MODEL), F32), pltpu.SemaphoreType.DMA((2,))],
    )
    return pl.pallas_call(
        functools.partial(_combine_kernel, tile=tile, n_groups=n_groups),
        grid_spec=grid_spec,
        out_shape=jax.ShapeDtypeStruct(x.shape, F32),
        compiler_params=_params("arbitrary"),
        name="moe_combine",
    )(plan, x, route, lpos, y)


def _routing_plan(route, n):
    tile, tm = TOKEN_TILE, MOE_ROW_TILE
    n_tiles_tok = n // tile
    slack = -(-(n_tiles_tok * N_EXPERTS * SUBLANES) // tm) * tm
    p = 2 * n + slack + N_EXPERTS * tm
    experts = jnp.arange(N_EXPERTS, dtype=jnp.int32)
    choice = route[:, 2:4].astype(jnp.int32).reshape(n_tiles_tok, tile, 2)
    counts = jnp.sum((choice[..., None] == experts).astype(jnp.int32), axis=(1, 2))
    size8 = (counts + SUBLANES - 1) // SUBLANES
    size = size8 * SUBLANES
    local = jnp.cumsum(size, axis=1) - size
    total = jnp.sum(size, axis=0)
    padded = ((total + tm - 1) // tm) * tm
    ends = jnp.cumsum(padded)
    starts = ends - padded
    dest = starts[None, :] + jnp.cumsum(size, axis=0) - size
    n_tiles = (ends[-1] // tm).astype(jnp.int32)
    tile_start = jnp.arange(p // tm, dtype=jnp.int32) * tm
    tile_expert = jnp.sum((tile_start[:, None] >= ends[None, :]).astype(jnp.int32), axis=1)
    last_expert = jnp.sum((ends[-1] - 1 >= ends).astype(jnp.int32))
    tile_expert = jnp.minimum(tile_expert, last_expert).astype(jnp.int32)
    tail = jnp.stack([ends[-1], (p - ends[-1]) // tile])
    plan = jnp.concatenate([local.reshape(-1), size8.reshape(-1), dest.reshape(-1),
                            starts + total, (padded - total) // SUBLANES, tail]).astype(jnp.int32)
    return plan, tile_expert, n_tiles.reshape(1), p


def _moe_layer(x, hn, route, w1, w3, w2):
    n = x.shape[0]
    plan, tile_expert, n_tiles, n_sorted = _routing_plan(route, n)
    tri = jnp.tril(jnp.ones((2 * TOKEN_TILE, 2 * TOKEN_TILE), BF16))
    hs, lpos, (w1, w3, w2) = _dispatch_call(hn, route, plan, tri, n_sorted, (w1, w3, w2))
    ys = _moe_call(hs, w1, w3, w2, tile_expert, n_tiles)
    return _combine_call(x, route, lpos, ys, plan)


def kernel(x, attn_norm, ffn_norm, w_in, q_norm, k_norm, conv_w, hgrn_lower_bounds,
           hgrn_out_norm, w_proj_a, w_proj_b, w_proj_c, w_out, dense_w1, dense_w3,
           dense_w2, router_w, moe_w1, moe_w3, moe_w2):
    batch, seq, d = x.shape
    n = batch * seq
    depth = w_in.shape[0]
    assert d == D_MODEL and seq % ATTN_TILE == 0 and seq % TOKEN_TILE == 0
    assert all(window // dilation == BAND for window, dilation in ATTN_GROUPS)

    lbs = jax.nn.softmax(hgrn_lower_bounds.astype(F32), axis=0)
    lower_bound = jnp.cumsum(lbs, axis=0) - lbs[0]
    head_of_col = jnp.arange(ATTN_WIDTH) // HEAD_DIM
    hsum = (head_of_col[:, None] == jnp.arange(LANES)[None, :]).astype(BF16)
    hexp = jnp.concatenate([hsum.T, hsum.T], axis=0)
    a0, a1 = 0, 3 * ATTN_WIDTH
    b1 = a1 + 3 * CONV_WIDTH
    c1 = b1 + 4 * HGRN_WIDTH

    xf = x.reshape(n, d)
    for layer in range(depth):
        w = w_in[layer]
        w_attn = w[:, a0:a1].astype(BF16)
        w_hgrn = w[:, b1:c1].astype(BF16)
        w_conv_gate = jnp.concatenate([w[:, a1:b1], w[:, c1:]], axis=1).astype(BF16)
        gain = attn_norm[layer].reshape(1, d)
        heads = ATTN_WIDTH // HEAD_DIM
        qg = jnp.tile(q_norm[layer] * (HEAD_DIM ** -0.5), heads).reshape(1, ATTN_WIDTH)
        kg = jnp.tile(k_norm[layer], heads).reshape(1, ATTN_WIDTH)

        qkv = _qkv_call(xf, gain, w_attn, qg, kg, hsum, hexp, batch, seq)
        oa = _attn_call(qkv, batch, seq)
        yc = _hgrn_call(xf, gain, w_hgrn, lower_bound[layer].reshape(1, HGRN_WIDTH),
                        hgrn_out_norm[layer].reshape(1, HGRN_HEAD),
                        w_proj_c[layer].astype(BF16), batch, seq)
        fgain = ffn_norm[layer].reshape(1, d)
        j = layer // 2
        routed = layer % 2 == 1
        router = None
        if routed:
            rw = jnp.pad(router_w[j], ((0, 0), (0, LANES - N_EXPERTS)))
            rw_hi = rw.astype(BF16)
            router = (fgain, rw_hi, (rw - rw_hi.astype(F32)).astype(BF16))
        merged = _merge_call(xf, gain, w_conv_gate, conv_w[layer], oa, yc,
                             w_proj_a[layer].astype(BF16), w_proj_b[layer].astype(BF16),
                             w_out[layer].astype(BF16), seq, router)
        if routed:
            xf, hn, route = merged
            xf = _moe_layer(xf, hn, route, moe_w1[j], moe_w3[j], moe_w2[j])
        else:
            xf = _ffn_call(merged, fgain, dense_w1[j].astype(BF16), dense_w3[j].astype(BF16),
                           dense_w2[j].astype(BF16))
    return xf.reshape(batch, seq, d)
```

```python
import functools

import jax
import jax.numpy as jnp
from jax import lax
from jax.experimental import pallas as pl
from jax.experimental.pallas import tpu as pltpu

F32 = jnp.float32
BF16 = jnp.bfloat16

D_MODEL = 1024
HEAD_DIM = 64
ATTN_GROUPS = ((128, 1), (512, 4), (2048, 16))
HEADS_PER_GROUP = 4
GROUP_WIDTH = HEADS_PER_GROUP * HEAD_DIM
ATTN_WIDTH = GROUP_WIDTH * len(ATTN_GROUPS)
BAND = 128
CONV_WIDTH = 768
CONV_K = 3
HGRN_WIDTH = 768
HGRN_HEAD = 128
HGRN_HEADS = HGRN_WIDTH // HGRN_HEAD
LB_FLOOR = 1e-30
N_EXPERTS = 8
RMS_EPS = 1e-6
NEG_BIG = -1e30
LOG2E = 1.4426950408889634

LANES = 128
SUBLANES = 8

TOKEN_TILE = 512
ATTN_TILE = ATTN_GROUPS[-1][1] * BAND
ATTN_UNROLL = 8
HGRN_CHUNK = 256
MOE_ROW_TILE = 512
MOE_FF_TILE = 1792
LOCAL_ROWS = -(-(2 * TOKEN_TILE + N_EXPERTS * SUBLANES) // LANES) * LANES
GROUP_BLOCKS = tuple(1 << b for b in reversed(range(8)))
FILL_BLOCKS = tuple(b for b in GROUP_BLOCKS if b * SUBLANES < MOE_ROW_TILE)
VMEM_LIMIT = 56 * 1024 * 1024


def _dot(a, b):
    return jnp.dot(a, b, preferred_element_type=F32)


def _dot_nt(a, b):
    return lax.dot_general(a, b, (((1,), (1,)), ((), ())), preferred_element_type=F32)


def _dot_tn(a, b):
    return lax.dot_general(a, b, (((0,), (0,)), ((), ())), preferred_element_type=F32)


def _rms_rows(xf, gain):
    ms = jnp.mean(xf * xf, axis=-1, keepdims=True)
    return xf * lax.rsqrt(ms + RMS_EPS) * gain


def _sigmoid(z):
    return 1.0 / (1.0 + jnp.exp(-z))


def _params(*sem):
    return pltpu.CompilerParams(dimension_semantics=sem, vmem_limit_bytes=VMEM_LIMIT)


def _full(shape):
    return pl.BlockSpec(shape, lambda *_: (0,) * len(shape))


def _qkv_kernel(x_ref, g_ref, w_ref, qg_ref, kg_ref, hsum_ref, hexp_ref, *rest, tile):
    out_refs, stage = rest[:9], rest[9]
    h = _rms_rows(x_ref[0], g_ref[...]).astype(BF16)
    p = _dot(h, w_ref[...])

    def head_norm(t, gain):
        ms = _dot((t * t).astype(BF16), hsum_ref[...]) * (1.0 / HEAD_DIM)
        r = lax.rsqrt(ms + RMS_EPS)
        r_hi = r.astype(BF16)
        r_lo = (r - r_hi.astype(F32)).astype(BF16)
        r_cols = _dot(jnp.concatenate([r_hi, r_lo], axis=1), hexp_ref[...])
        return t * r_cols * gain

    parts = (head_norm(p[:, :ATTN_WIDTH], qg_ref[...]),
             head_norm(p[:, ATTN_WIDTH:2 * ATTN_WIDTH], kg_ref[...]),
             p[:, 2 * ATTN_WIDTH:])
    slot = 0
    for a, part in enumerate(parts):
        for group, (_, dilation) in enumerate(ATTN_GROUPS):
            out = out_refs[3 * group + a]
            c0 = group * GROUP_WIDTH
            if dilation == 1:
                out[0, 0] = part[:, c0:c0 + GROUP_WIDTH].astype(BF16)
                continue
            rows = tile // dilation
            for half in range(GROUP_WIDTH // LANES):
                stage[slot] = part[:, c0 + half * LANES:c0 + (half + 1) * LANES]
                for r in range(dilation):
                    out[0, r, :, half * LANES:(half + 1) * LANES] = (
                        stage[slot, pl.ds(r, rows, stride=dilation), :].astype(BF16))
                slot += 1


def _qkv_call(x, gain, w, qg, kg, hsum, hexp, batch, seq):
    t = TOKEN_TILE
    out_specs, out_shape = [], []
    for _, dilation in ATTN_GROUPS:
        for _ in range(3):
            out_specs.append(pl.BlockSpec((1, dilation, t // dilation, GROUP_WIDTH),
                                          lambda b, i: (b, 0, i, 0)))
            out_shape.append(jax.ShapeDtypeStruct(
                (batch, dilation, seq // dilation, GROUP_WIDTH), BF16))
    n_stage = 3 * (GROUP_WIDTH // LANES) * sum(1 for _, d in ATTN_GROUPS if d > 1)
    return pl.pallas_call(
        functools.partial(_qkv_kernel, tile=t),
        grid=(batch, seq // t),
        in_specs=[pl.BlockSpec((1, t, D_MODEL), lambda b, i: (b, i, 0)), _full((1, D_MODEL)),
                  _full(w.shape), _full((1, ATTN_WIDTH)), _full((1, ATTN_WIDTH)),
                  _full(hsum.shape), _full(hexp.shape)],
        out_specs=out_specs,
        out_shape=out_shape,
        scratch_shapes=[pltpu.VMEM((n_stage, t, LANES), F32)],
        compiler_params=_params("parallel", "parallel"),
        name="qkv_proj",
    )(x.reshape(batch, seq, D_MODEL), gain, w, qg, kg, hsum, hexp)


def _attn_kernel(*refs, n_groups):
    ins, o_ref, scr = refs[:5 * n_groups], refs[5 * n_groups], refs[5 * n_groups + 1:]
    kbufs, vbufs = scr[0:n_groups], scr[n_groups:2 * n_groups]
    o_tok, l_tok, o_rm, l_rm = scr[2 * n_groups:]
    first_tile = pl.program_id(1) == 0
    qi = lax.broadcasted_iota(jnp.int32, (BAND, 2 * BAND), 0)
    kj = lax.broadcasted_iota(jnp.int32, (BAND, 2 * BAND), 1)
    rel = qi + BAND - kj
    in_window = (rel >= 0) & (rel <= BAND)
    halves = GROUP_WIDTH // LANES
    lane = lax.broadcasted_iota(jnp.int32, (1, LANES), 1)
    head_lanes = [jnp.where(lane // HEAD_DIM == side, 1.0, 0.0).astype(BF16)
                  for side in range(LANES // HEAD_DIM)]
    first_head = lax.broadcasted_iota(jnp.int32, (BAND, LANES), 1) < HEAD_DIM

    for group, (_, dilation) in enumerate(ATTN_GROUPS):
        q_ref, kc_ref, vc_ref, kp_ref, vp_ref = ins[5 * group:5 * group + 5]
        kbuf, vbuf = kbufs[group], vbufs[group]
        length = ATTN_TILE // dilation
        nj = length // BAND
        kbuf[:, 0:BAND, :] = kp_ref[0]
        kbuf[:, BAND:BAND + length, :] = kc_ref[0]
        vbuf[:, 0:BAND, :] = vp_ref[0]
        vbuf[:, BAND:BAND + length, :] = vc_ref[0]

        def block(idx, carry, q_ref=q_ref, kbuf=kbuf, vbuf=vbuf, nj=nj, length=length,
                  group=group, dilation=dilation):
            r, j = idx // nj, idx % nj
            j0 = pl.multiple_of(j * BAND, BAND)
            kmin = jnp.where(first_tile & (j == 0), BAND, 0)
            valid = in_window & (kj >= kmin)
            row0 = pl.multiple_of(r * length + j0, BAND)
            for half in range(halves):
                lanes = slice(half * LANES, (half + 1) * LANES)
                q2 = q_ref[0, r, pl.ds(j0, BAND), lanes]
                k2 = kbuf[r, pl.ds(j0, 2 * BAND), lanes]
                v2 = vbuf[r, pl.ds(j0, 2 * BAND), lanes]
                o2, l2 = None, None
                for side in range(LANES // HEAD_DIM):
                    s = _dot_nt(q2 * head_lanes[side], k2)
                    s = jnp.where(valid, s, NEG_BIG)
                    m = jnp.max(s, axis=-1, keepdims=True)
                    p = jnp.exp(s - m)
                    den = jnp.sum(p, axis=-1, keepdims=True)
                    o = _dot(p.astype(BF16), v2) * (1.0 / den)
                    lse = jnp.broadcast_to(m + jnp.log(den), (BAND, LANES))
                    o2 = o if side == 0 else jnp.where(first_head, o2, o)
                    l2 = lse if side == 0 else jnp.where(first_head, l2, lse)
                if dilation == 1:
                    o_tok[group, half, pl.ds(row0, BAND), :] = o2
                    l_tok[group, half, pl.ds(row0, BAND), :] = l2
                else:
                    o_rm[pl.ds(row0, BAND), lanes] = o2
                    l_rm[pl.ds(row0, BAND), lanes] = l2
            return carry

        lax.fori_loop(0, ATTN_TILE // BAND, block, 0, unroll=ATTN_UNROLL)
        if dilation > 1:
            for r in range(dilation):
                rows = slice(r * length, (r + 1) * length)
                for half in range(halves):
                    lanes = slice(half * LANES, (half + 1) * LANES)
                    o_tok[group, half, pl.ds(r, length, stride=dilation), :] = o_rm[rows, lanes]
                    l_tok[group, half, pl.ds(r, length, stride=dilation), :] = l_rm[rows, lanes]

    chunk = 2 * BAND

    def merge(c, carry):
        rows = pl.ds(pl.multiple_of(c * chunk, chunk), chunk)
        for half in range(halves):
            ls = [l_tok[g, half, rows, :] for g in range(n_groups)]
            top = functools.reduce(jnp.maximum, ls)
            es = [jnp.exp(l - top) for l in ls]
            acc = sum(e * o_tok[g, half, rows, :] for g, e in enumerate(es))
            o_ref[0, rows, half * LANES:(half + 1) * LANES] = (acc * (1.0 / sum(es))).astype(BF16)
        return carry

    lax.fori_loop(0, ATTN_TILE // chunk, merge, 0)


def _attn_call(qkv, batch, seq):
    n_groups = len(ATTN_GROUPS)
    in_specs, args, kbufs = [], [], []
    for group, (_, dilation) in enumerate(ATTN_GROUPS):
        length = ATTN_TILE // dilation
        cur = pl.BlockSpec((1, dilation, length, GROUP_WIDTH), lambda b, i: (b, 0, i, 0))
        halo = pl.BlockSpec(
            (1, dilation, BAND, GROUP_WIDTH),
            lambda b, i, nb=length // BAND: (b, 0, jnp.maximum(i * nb - 1, 0), 0))
        q, k, v = qkv[3 * group:3 * group + 3]
        in_specs += [cur, cur, cur, halo, halo]
        args += [q, k, v, k, v]
        kbufs.append(pltpu.VMEM((dilation, BAND + length, GROUP_WIDTH), BF16))
    halves = GROUP_WIDTH // LANES
    o = pl.pallas_call(
        functools.partial(_attn_kernel, n_groups=n_groups),
        grid=(batch, seq // ATTN_TILE),
        in_specs=in_specs,
        out_specs=pl.BlockSpec((1, ATTN_TILE, GROUP_WIDTH), lambda b, i: (b, i, 0)),
        out_shape=jax.ShapeDtypeStruct((batch, seq, GROUP_WIDTH), BF16),
        scratch_shapes=kbufs + kbufs + [
            pltpu.VMEM((n_groups, halves, ATTN_TILE, LANES), F32),
            pltpu.VMEM((n_groups, halves, ATTN_TILE, LANES), F32),
            pltpu.VMEM((ATTN_TILE, GROUP_WIDTH), F32),
            pltpu.VMEM((ATTN_TILE, GROUP_WIDTH), F32)],
        compiler_params=_params("parallel", "arbitrary"),
        name="band_attn",
    )(*args)
    return o.reshape(batch * seq, GROUP_WIDTH)


def _hgrn_kernel(x_ref, g_ref, w_ref, lb_ref, og_ref, wp_ref, tri_ref, *rest,
                 tile, chunk, n_convert):
    to_convert, rest = rest[:n_convert], rest[n_convert:]
    y_ref, converted = rest[0], rest[1:1 + n_convert]
    state_ref, proj_ref, gated_ref, b_ref = rest[1 + n_convert:]
    for src, dst in zip(to_convert, converted):
        dst[...] = src[...].astype(BF16)

    @pl.when(pl.program_id(1) == 0)
    def _():
        state_ref[...] = jnp.zeros_like(state_ref)

    h = _rms_rows(x_ref[0], g_ref[...]).astype(BF16)
    proj_ref[...] = _dot(h, w_ref[...])

    lb = lb_ref[...]
    lb_floor = jnp.maximum(lb, LB_FLOOR)
    one_minus_lb = 1.0 - lb
    floor_gap = lb_floor - lb
    row = lax.broadcasted_iota(jnp.int32, (chunk, HGRN_HEAD), 0)
    tt = lax.broadcasted_iota(jnp.int32, (chunk, chunk), 0)
    ss = lax.broadcasted_iota(jnp.int32, (chunk, chunk), 1)
    t_xor_s = tt ^ ss
    w = HGRN_WIDTH
    sizes = [1 << i for i in range(chunk.bit_length() - 1)]
    lower_rows = [(row & m) == 0 for m in sizes]
    lower_sel = [jnp.where(low, 1.0, 0.0).astype(BF16) for low in lower_rows]
    signed_log2e = [jnp.where(low, -LOG2E, LOG2E).astype(BF16) for low in lower_rows]
    level_pairs = [((t_xor_s & (-m)) == m) & ((tt & m) != 0) for m in sizes]
    diagonal = tt == ss

    def chunk_body(c, carry):
        r0 = pl.multiple_of(c * chunk, chunk)
        rows = pl.ds(r0, chunk)
        z = proj_ref[rows, w:2 * w]
        e = jnp.exp(-jnp.abs(z))
        inv = 1.0 / (1.0 + e)
        sig = jnp.where(z >= 0, inv, e * inv)
        nsig = jnp.where(z >= 0, e * inv, inv)
        log_f = jnp.log(lb_floor + one_minus_lb * sig)
        k_all = one_minus_lb * nsig - floor_gap
        lf_hi = log_f.astype(BF16)
        rest = log_f - lf_hi.astype(F32)
        lf_mid = rest.astype(BF16)
        lf_lo = (rest - lf_mid.astype(F32)).astype(BF16)
        tri = tri_ref[...]
        b_all = _dot(tri, lf_hi) + _dot(tri, lf_mid) + _dot(tri, lf_lo)
        b_ref[...] = b_all
        q_all = proj_ref[rows, 0:w].astype(BF16)
        q_all = q_all * _sigmoid(q_all)
        v_all = proj_ref[rows, 2 * w:3 * w]
        g_all = proj_ref[rows, 3 * w:4 * w].astype(BF16)
        g_all = g_all * _sigmoid(g_all)
        for hd in range(HGRN_HEADS):
            cols = slice(hd * HGRN_HEAD, (hd + 1) * HGRN_HEAD)
            b, qb = b_all[:, cols], q_all[:, cols]
            kb = k_all[:, cols].astype(BF16)
            vb = v_all[:, cols].astype(BF16)
            scores = jnp.where(diagonal, _dot_nt(qb, kb), 0.0)
            last_of_block = b
            for lvl, m in enumerate(sizes):
                if m < SUBLANES:
                    lower = lower_rows[lvl]
                    boundary = jnp.where(lower, last_of_block, pltpu.roll(last_of_block, m, 0))
                    if 2 * m < SUBLANES:
                        last_of_block = jnp.where(
                            lower, pltpu.roll(last_of_block, chunk - m, 0), last_of_block)
                else:
                    boundary = jnp.concatenate(
                        [jnp.broadcast_to(b_ref[r:r + 1, cols], (2 * m, HGRN_HEAD))
                         for r in range(m - 1, chunk, 2 * m)], axis=0)
                decay = jnp.exp2((b - boundary).astype(BF16) * signed_log2e[lvl])
                if m < 2 * SUBLANES:
                    source = jnp.where(lower_sel[lvl] > 0, kb, qb)
                else:
                    source = jnp.concatenate(
                        [(kb if (r // m) % 2 == 0 else qb)[r:r + m] for r in range(0, chunk, m)],
                        axis=0)
                side = source * decay
                scores = jnp.where(level_pairs[lvl], _dot_nt(side, side), scores)
            state_t = state_ref[hd]
            q_dec = qb * jnp.exp(b).astype(BF16)
            o = _dot(scores.astype(BF16), vb) + _dot_nt(q_dec, state_t.astype(BF16))
            b_last = b[chunk - 1:chunk, :]
            k_dec = kb * jnp.exp(b_last - b).astype(BF16)
            state_ref[hd] = state_t * jnp.exp(b_last) + _dot_tn(vb, k_dec)
            o = _rms_rows(o, og_ref[...]).astype(BF16) * g_all[:, cols]
            gated_ref[rows, cols] = o
        return carry

    lax.fori_loop(0, tile // chunk, chunk_body, 0, unroll=True)
    y_ref[0] = _dot(gated_ref[...], wp_ref[...]).astype(BF16)


def _hgrn_call(x, gain, w, lb, out_gain, w_proj, batch, seq, convert=()):
    tile, chunk = TOKEN_TILE, HGRN_CHUNK
    tiles_per_seq = seq // tile
    steps = batch * tiles_per_seq
    tri = jnp.tril(jnp.ones((chunk, chunk), BF16))
    blk = lambda width: pl.BlockSpec((1, tile, width), lambda b, i: (b, i, 0))
    flat = [a.reshape(-1, a.shape[-1]) for a in convert]
    assert all(a.shape[0] % (steps * 2 * SUBLANES) == 0 for a in flat)
    slab = lambda a: pl.BlockSpec((a.shape[0] // steps, a.shape[1]),
                                  lambda b, i: (b * tiles_per_seq + i, 0))
    y, *converted = pl.pallas_call(
        functools.partial(_hgrn_kernel, tile=tile, chunk=chunk, n_convert=len(flat)),
        grid=(batch, tiles_per_seq),
        in_specs=[blk(D_MODEL), _full((1, D_MODEL)), _full(w.shape), _full((1, HGRN_WIDTH)),
                  _full((1, HGRN_HEAD)), _full(w_proj.shape), _full((chunk, chunk))]
                 + [slab(a) for a in flat],
        out_specs=[blk(D_MODEL)] + [slab(a) for a in flat],
        out_shape=[jax.ShapeDtypeStruct((batch, seq, D_MODEL), BF16)]
                  + [jax.ShapeDtypeStruct(a.shape, BF16) for a in flat],
        scratch_shapes=[pltpu.VMEM((HGRN_HEADS, HGRN_HEAD, HGRN_HEAD), F32),
                        pltpu.VMEM((tile, 4 * HGRN_WIDTH), F32),
                        pltpu.VMEM((tile, HGRN_WIDTH), BF16),
                        pltpu.VMEM((chunk, HGRN_WIDTH), F32)],
        compiler_params=_params("parallel", "arbitrary"),
        name="hgrn_branch",
    )(x.reshape(batch, seq, D_MODEL), gain, w, lb, out_gain, w_proj, tri, *flat)
    return y.reshape(batch * seq, D_MODEL), [c.reshape(a.shape) for c, a in zip(converted, convert)]


def _top2_route(h, rw_hi, rw_lo):
    logits = _dot(h, rw_hi) + _dot(h, rw_lo)
    lane = lax.broadcasted_iota(jnp.int32, logits.shape, 1)
    logits = jnp.where(lane < N_EXPERTS, logits, -jnp.inf)
    m1 = jnp.max(logits, axis=-1, keepdims=True)
    i1 = jnp.min(jnp.where(logits == m1, lane, LANES), axis=-1, keepdims=True)
    rest = jnp.where(lane == i1, -jnp.inf, logits)
    m2 = jnp.max(rest, axis=-1, keepdims=True)
    i2 = jnp.min(jnp.where(rest == m2, lane, LANES), axis=-1, keepdims=True)
    e = jnp.exp(m2 - m1)
    g1 = 1.0 / (1.0 + e)
    g2 = e * g1
    return jnp.where(
        lane == 0, g1, jnp.where(lane == 1, g2, jnp.where(
            lane == 2, i1.astype(F32), jnp.where(lane == 3, i2.astype(F32), 0.0))))


def _merge_kernel(x_ref, g_ref, w_ref, cw_ref, oa_ref, yc_ref, wpa_ref, wpb_ref, wo_ref, *rest,
                  tile, tiles_per_seq, steps, with_router):
    if with_router:
        fg_ref, rw_hi_ref, rw_lo_ref, out_ref, hn_ref, route_ref, u_ext, prev_ref = rest
    else:
        out_ref, u_ext = rest
    step = pl.program_id(0)

    @pl.when(step % tiles_per_seq == 0)
    def _():
        u_ext[0:SUBLANES, :] = jnp.zeros((SUBLANES, CONV_WIDTH), F32)

    def route_previous_tile():
        hn = _rms_rows(prev_ref[...], fg_ref[...])
        hn_ref[...] = hn
        route_ref[...] = _top2_route(hn.astype(BF16), rw_hi_ref[...], rw_lo_ref[...])

    def merge_tile():
        x = x_ref[...]
        h = _rms_rows(x, g_ref[...]).astype(BF16)
        p = _dot(h, w_ref[...])
        cw = CONV_WIDTH
        u = p[:, 2 * cw:3 * cw] * p[:, 0:cw]
        u_ext[SUBLANES:SUBLANES + tile, :] = u
        conv = cw_ref[0:1, :] * u
        for tap in range(1, CONV_K):
            conv = conv + cw_ref[tap:tap + 1, :] * u_ext[SUBLANES - tap:SUBLANES - tap + tile, :]
        u_ext[0:SUBLANES, :] = u[tile - SUBLANES:, :]
        yb = _dot((p[:, cw:2 * cw] * conv).astype(BF16), wpb_ref[...])
        ya = _dot(oa_ref[...], wpa_ref[...])

        d = D_MODEL
        gates = p[:, 3 * cw:]
        merged = (_sigmoid(gates[:, 0:d]) * ya + _sigmoid(gates[:, d:2 * d]) * yb
                  + _sigmoid(gates[:, 2 * d:3 * d]) * yc_ref[...].astype(F32))
        x_new = x + _dot(merged.astype(BF16), wo_ref[...])
        out_ref[...] = x_new
        return x_new

    if not with_router:
        merge_tile()
        return

    @pl.when(step == 0)
    def _():
        prev_ref[...] = jnp.zeros_like(prev_ref)

    @pl.when(step < steps)
    def _():
        route_previous_tile()
        prev_ref[...] = merge_tile()

    @pl.when(step == steps)
    def _():
        route_previous_tile()


def _merge_call(x, gain, w, conv_w, oa, yc, wpa, wpb, wo, seq, router=None):
    n = x.shape[0]
    tile = TOKEN_TILE
    steps = n // tile
    row = lambda width: pl.BlockSpec((tile, width), lambda i: (jnp.minimum(i, steps - 1), 0))
    late = lambda width: pl.BlockSpec((tile, width), lambda i: (jnp.maximum(i - 1, 0), 0))
    in_specs = [row(D_MODEL), _full((1, D_MODEL)), _full(w.shape), _full(conv_w.shape),
                row(GROUP_WIDTH), row(D_MODEL), _full(wpa.shape), _full(wpb.shape),
                _full(wo.shape)]
    args = [x, gain, w, conv_w, oa, yc, wpa, wpb, wo]
    out_specs, out_shape = [row(D_MODEL)], [jax.ShapeDtypeStruct((n, D_MODEL), F32)]
    scratch = [pltpu.VMEM((SUBLANES + tile, CONV_WIDTH), F32)]
    if router is not None:
        in_specs += [_full(a.shape) for a in router]
        args += list(router)
        out_specs += [late(D_MODEL), late(LANES)]
        out_shape += [jax.ShapeDtypeStruct((n, D_MODEL), F32),
                      jax.ShapeDtypeStruct((n, LANES), F32)]
        scratch.append(pltpu.VMEM((tile, D_MODEL), F32))
    outs = pl.pallas_call(
        functools.partial(_merge_kernel, tile=tile, tiles_per_seq=seq // tile, steps=steps,
                          with_router=router is not None),
        grid=(steps + (router is not None),),
        in_specs=in_specs,
        out_specs=out_specs,
        out_shape=out_shape,
        scratch_shapes=scratch,
        compiler_params=_params("arbitrary"),
        name="merge_out",
    )(*args)
    return outs[0] if router is None else outs


def _ffn_kernel(x_ref, g_ref, w1_ref, w3_ref, w2_ref, out_ref):
    x = x_ref[...]
    h = _rms_rows(x, g_ref[...]).astype(BF16)
    a = _dot(h, w1_ref[...])
    z = (a * _sigmoid(a) * _dot(h, w3_ref[...])).astype(BF16)
    out_ref[...] = x + _dot(z, w2_ref[...])


def _ffn_call(x, gain, w1, w3, w2):
    n = x.shape[0]
    tile = TOKEN_TILE
    row = pl.BlockSpec((tile, D_MODEL), lambda i: (i, 0))
    return pl.pallas_call(
        _ffn_kernel,
        grid=(n // tile,),
        in_specs=[row, _full((1, D_MODEL)), _full(w1.shape), _full(w3.shape), _full(w2.shape)],
        out_specs=row,
        out_shape=jax.ShapeDtypeStruct((n, D_MODEL), F32),
        compiler_params=_params("parallel"),
        name="dense_ffn",
    )(x, gain, w1, w3, w2)


def _block_copies(n8, issue, blocks=GROUP_BLOCKS):
    done = jnp.int32(0)
    for bit in blocks:
        take = (n8 & bit) != 0

        @pl.when(take)
        def _(done=done, bit=bit):
            issue(pl.multiple_of(done * SUBLANES, SUBLANES), bit * SUBLANES)

        done = done + jnp.where(take, bit, 0)


def _slot_positions(route, tri, local_base):
    expert = jnp.concatenate([route[:, 2:3], route[:, 3:4]], axis=0)
    lane = lax.broadcasted_iota(jnp.int32, (expert.shape[0], LANES), 1).astype(F32)
    onehot = expert == lane
    seen = _dot(tri, jnp.where(onehot, 1.0, 0.0).astype(BF16))
    pos = jnp.sum(jnp.where(onehot, seen - 1.0 + local_base, 0.0), axis=1, keepdims=True)
    t = route.shape[0]
    return pos[:t], pos[t:]


def _local_base_row(plan_ref, tile_index):
    lane = lax.broadcasted_iota(jnp.int32, (1, LANES), 1)
    row = jnp.zeros((1, LANES), F32)
    for e in range(N_EXPERTS):
        row = jnp.where(lane == e, plan_ref[tile_index * N_EXPERTS + e].astype(F32), row)
    return row


def _dispatch_kernel(plan_ref, hn_ref, route_ref, tri_ref, out_hbm, lpos_ref, sorted_ref, zeros,
                     sems, fill_sem, *, tile, n_groups):
    i = pl.program_id(0)
    fill0 = 3 * n_groups

    def fills(wait):
        for e in range(N_EXPERTS):
            first = pl.multiple_of(plan_ref[fill0 + e], SUBLANES)

            def issue(off, rows, first=first):
                cp = pltpu.make_async_copy(zeros.at[pl.ds(0, rows), :],
                                           out_hbm.at[pl.ds(first + off, rows), :], fill_sem)
                cp.wait() if wait else cp.start()

            _block_copies(plan_ref[fill0 + N_EXPERTS + e], issue, FILL_BLOCKS)

        def tail(k, carry):
            start = pl.multiple_of(plan_ref[fill0 + 2 * N_EXPERTS] + k * tile, tile)
            cp = pltpu.make_async_copy(zeros, out_hbm.at[pl.ds(start, tile), :], fill_sem)
            cp.wait() if wait else cp.start()
            return carry

        lax.fori_loop(0, plan_ref[fill0 + 2 * N_EXPERTS + 1], tail, 0)

    @pl.when(i == 0)
    def _():
        zeros[...] = jnp.zeros_like(zeros)
        fills(wait=False)
        fills(wait=True)

    pos1, pos2 = _slot_positions(route_ref[...], tri_ref[...], _local_base_row(plan_ref, i))
    lane = lax.broadcasted_iota(jnp.int32, (tile, LANES), 1)
    lpos_ref[...] = jnp.where(lane == 0, pos1, jnp.where(lane == 1, pos2, 0.0))
    col = lax.broadcasted_iota(jnp.int32, (tile, LOCAL_ROWS), 1).astype(F32)
    select = jnp.where((pos1 == col) | (pos2 == col), 1.0, 0.0).astype(BF16)

    def groups(step, wait):
        slot = step % 2
        for e in range(N_EXPERTS):
            g = step * N_EXPERTS + e
            local = pl.multiple_of(plan_ref[g], SUBLANES)
            dest = pl.multiple_of(plan_ref[2 * n_groups + g], SUBLANES)

            def issue(off, rows, local=local, dest=dest):
                cp = pltpu.make_async_copy(sorted_ref.at[slot, pl.ds(local + off, rows), :],
                                           out_hbm.at[pl.ds(dest + off, rows), :], sems.at[slot])
                cp.wait() if wait else cp.start()

            _block_copies(plan_ref[n_groups + g], issue)

    @pl.when(i > 1)
    def _():
        groups(i - 2, wait=True)

    sorted_ref[i % 2] = _dot_tn(select, hn_ref[...].astype(BF16))
    groups(i, wait=False)

    @pl.when(i == pl.num_programs(0) - 1)
    def _():
        @pl.when(i > 0)
        def _():
            groups(i - 1, wait=True)

        groups(i, wait=True)


def _dispatch_call(hn, route, plan, tri, n_sorted):
    n = hn.shape[0]
    tile = TOKEN_TILE
    n_groups = (n // tile) * N_EXPERTS
    row = lambda width: pl.BlockSpec((tile, width), lambda i, plan: (i, 0))
    grid_spec = pltpu.PrefetchScalarGridSpec(
        num_scalar_prefetch=1,
        grid=(n // tile,),
        in_specs=[row(D_MODEL), row(LANES),
                  pl.BlockSpec(tri.shape, lambda i, plan: (0, 0))],
        out_specs=[pl.BlockSpec(memory_space=pl.ANY), row(LANES)],
        scratch_shapes=[pltpu.VMEM((2, LOCAL_ROWS, D_MODEL), F32), pltpu.VMEM((tile, D_MODEL), F32),
                        pltpu.SemaphoreType.DMA((2,)), pltpu.SemaphoreType.DMA(())],
    )
    return pl.pallas_call(
        functools.partial(_dispatch_kernel, tile=tile, n_groups=n_groups),
        grid_spec=grid_spec,
        out_shape=[jax.ShapeDtypeStruct((n_sorted, D_MODEL), F32),
                   jax.ShapeDtypeStruct((n, LANES), F32)],
        compiler_params=_params("arbitrary"),
        name="moe_dispatch",
    )(plan, hn, route, tri)


def _moe_kernel(tile_expert_ref, n_tiles_ref, h_ref, w1_ref, w3_ref, w2_ref, y_ref):
    i, f = pl.program_id(0), pl.program_id(1)
    active = i < n_tiles_ref[0]

    @pl.when((i < 2) & (f == 0))
    def _():
        y_ref[...] = jnp.zeros_like(y_ref)

    @pl.when(active)
    def _():
        h = h_ref[...].astype(BF16)
        a = _dot(h, w1_ref[...])
        z = (a * _sigmoid(a) * _dot(h, w3_ref[...])).astype(BF16)
        y_ref[...] = jnp.where(f == 0, 0.0, y_ref[...]) + _dot(z, w2_ref[...])

    @pl.when(jnp.logical_not(active) & (f == 0))
    def _():
        y_ref[...] = jnp.zeros_like(y_ref)


def _moe_call(hs, w1, w3, w2, tile_expert, n_tiles):
    p = hs.shape[0]
    d_ff = w1.shape[-1]
    tm, tf = MOE_ROW_TILE, MOE_FF_TILE
    nf = d_ff // tf

    def ff_block(i, f, n_tiles_ref):
        return jnp.where(i < n_tiles_ref[0], f, nf - 1)

    grid_spec = pltpu.PrefetchScalarGridSpec(
        num_scalar_prefetch=2,
        grid=(p // tm, nf),
        in_specs=[
            pl.BlockSpec((tm, D_MODEL), lambda i, f, te, nt: (jnp.minimum(i, nt[0] - 1), 0)),
            pl.BlockSpec((None, D_MODEL, tf), lambda i, f, te, nt: (te[i], 0, ff_block(i, f, nt))),
            pl.BlockSpec((None, D_MODEL, tf), lambda i, f, te, nt: (te[i], 0, ff_block(i, f, nt))),
            pl.BlockSpec((None, tf, D_MODEL), lambda i, f, te, nt: (te[i], ff_block(i, f, nt), 0)),
        ],
        out_specs=pl.BlockSpec((tm, D_MODEL), lambda i, f, te, nt: (i, 0)),
    )
    return pl.pallas_call(
        _moe_kernel,
        grid_spec=grid_spec,
        out_shape=jax.ShapeDtypeStruct((p, D_MODEL), F32),
        compiler_params=_params("arbitrary", "arbitrary"),
        name="moe_experts",
    )(tile_expert, n_tiles, hs, w1, w3, w2)


def _combine_kernel(plan_ref, x_ref, route_ref, lpos_ref, y_hbm, out_ref, ybuf, sems, *,
                    tile, n_groups):
    i = pl.program_id(0)

    def groups(step, wait):
        slot = step % 2
        for e in range(N_EXPERTS):
            g = step * N_EXPERTS + e
            local = pl.multiple_of(plan_ref[g], SUBLANES)
            src = pl.multiple_of(plan_ref[2 * n_groups + g], SUBLANES)

            def issue(off, rows, local=local, src=src):
                cp = pltpu.make_async_copy(y_hbm.at[pl.ds(src + off, rows), :],
                                           ybuf.at[slot, pl.ds(local + off, rows), :], sems.at[slot])
                cp.wait() if wait else cp.start()

            _block_copies(plan_ref[n_groups + g], issue)

    @pl.when(i == 0)
    def _():
        ybuf[...] = jnp.zeros_like(ybuf)
        groups(i, wait=False)

    @pl.when(i + 1 < pl.num_programs(0))
    def _():
        groups(i + 1, wait=False)

    col = lax.broadcasted_iota(jnp.int32, (tile, LOCAL_ROWS), 1).astype(F32)
    pick1 = jnp.where(lpos_ref[:, 0:1] == col, 1.0, 0.0).astype(BF16)
    pick2 = jnp.where(lpos_ref[:, 1:2] == col, 1.0, 0.0).astype(BF16)
    groups(i, wait=True)
    last = i * N_EXPERTS + N_EXPERTS - 1
    used = plan_ref[last] + plan_ref[n_groups + last] * SUBLANES
    row = lax.broadcasted_iota(jnp.int32, (LOCAL_ROWS, 1), 0)
    y = jnp.where(row < used, ybuf[i % 2], 0.0).astype(BF16)
    out_ref[...] = (x_ref[...] + route_ref[:, 0:1] * _dot(pick1, y)
                    + route_ref[:, 1:2] * _dot(pick2, y))


def _combine_call(x, route, lpos, y, plan):
    n = x.shape[0]
    tile = TOKEN_TILE
    n_groups = (n // tile) * N_EXPERTS
    row = lambda width: pl.BlockSpec((tile, width), lambda i, plan: (i, 0))
    grid_spec = pltpu.PrefetchScalarGridSpec(
        num_scalar_prefetch=1,
        grid=(n // tile,),
        in_specs=[row(D_MODEL), row(LANES), row(LANES), pl.BlockSpec(memory_space=pl.ANY)],
        out_specs=row(D_MODEL),
        scratch_shapes=[pltpu.VMEM((2, LOCAL_ROWS, D_MODEL), F32), pltpu.SemaphoreType.DMA((2,))],
    )
    return pl.pallas_call(
        functools.partial(_combine_kernel, tile=tile, n_groups=n_groups),
        grid_spec=grid_spec,
        out_shape=jax.ShapeDtypeStruct(x.shape, F32),
        compiler_params=_params("arbitrary"),
        name="moe_combine",
    )(plan, x, route, lpos, y)


def _routing_plan(route, n):
    tile, tm = TOKEN_TILE, MOE_ROW_TILE
    n_tiles_tok = n // tile
    slack = -(-(n_tiles_tok * N_EXPERTS * SUBLANES) // tm) * tm
    p = 2 * n + slack + N_EXPERTS * tm
    experts = jnp.arange(N_EXPERTS, dtype=jnp.int32)
    choice = route[:, 2:4].astype(jnp.int32).reshape(n_tiles_tok, tile, 2)
    counts = jnp.sum((choice[..., None] == experts).astype(jnp.int32), axis=(1, 2))
    size8 = (counts + SUBLANES - 1) // SUBLANES
    size = size8 * SUBLANES
    local = jnp.cumsum(size, axis=1) - size
    total = jnp.sum(size, axis=0)
    padded = ((total + tm - 1) // tm) * tm
    ends = jnp.cumsum(padded)
    starts = ends - padded
    dest = starts[None, :] + jnp.cumsum(size, axis=0) - size
    n_tiles = (ends[-1] // tm).astype(jnp.int32)
    tile_start = jnp.arange(p // tm, dtype=jnp.int32) * tm
    tile_expert = jnp.sum((tile_start[:, None] >= ends[None, :]).astype(jnp.int32), axis=1)
    last_expert = jnp.sum((ends[-1] - 1 >= ends).astype(jnp.int32))
    tile_expert = jnp.minimum(tile_expert, last_expert).astype(jnp.int32)
    tail = jnp.stack([ends[-1], (p - ends[-1]) // tile])
    plan = jnp.concatenate([local.reshape(-1), size8.reshape(-1), dest.reshape(-1),
                            starts + total, (padded - total) // SUBLANES, tail]).astype(jnp.int32)
    return plan, tile_expert, n_tiles.reshape(1), p


def _moe_layer(x, hn, route, w1, w3, w2):
    n = x.shape[0]
    plan, tile_expert, n_tiles, n_sorted = _routing_plan(route, n)
    tri = jnp.tril(jnp.ones((2 * TOKEN_TILE, 2 * TOKEN_TILE), BF16))
    hs, lpos = _dispatch_call(hn, route, plan, tri, n_sorted)
    ys = _moe_call(hs, w1.astype(BF16), w3.astype(BF16), w2.astype(BF16), tile_expert, n_tiles)
    return _combine_call(x, route, lpos, ys, plan)


def kernel(x, attn_norm, ffn_norm, w_in, q_norm, k_norm, conv_w, hgrn_lower_bounds,
           hgrn_out_norm, w_proj_a, w_proj_b, w_proj_c, w_out, dense_w1, dense_w3,
           dense_w2, router_w, moe_w1, moe_w3, moe_w2):
    batch, seq, d = x.shape
    n = batch * seq
    depth = w_in.shape[0]
    assert d == D_MODEL and seq % ATTN_TILE == 0 and seq % TOKEN_TILE == 0
    assert all(window // dilation == BAND for window, dilation in ATTN_GROUPS)

    lbs = jax.nn.softmax(hgrn_lower_bounds.astype(F32), axis=0)
    lower_bound = jnp.cumsum(lbs, axis=0) - lbs[0]
    head_of_col = jnp.arange(ATTN_WIDTH) // HEAD_DIM
    hsum = (head_of_col[:, None] == jnp.arange(LANES)[None, :]).astype(BF16)
    hexp = jnp.concatenate([hsum.T, hsum.T], axis=0)
    a0, a1 = 0, 3 * ATTN_WIDTH
    b1 = a1 + 3 * CONV_WIDTH
    c1 = b1 + 4 * HGRN_WIDTH

    xf = x.reshape(n, d)
    for layer in range(depth):
        w = w_in[layer]
        w_attn = w[:, a0:a1].astype(BF16)
        w_hgrn = w[:, b1:c1].astype(BF16)
        w_conv_gate = jnp.concatenate([w[:, a1:b1], w[:, c1:]], axis=1).astype(BF16)
        gain = attn_norm[layer].reshape(1, d)
        heads = ATTN_WIDTH // HEAD_DIM
        qg = jnp.tile(q_norm[layer] * (HEAD_DIM ** -0.5), heads).reshape(1, ATTN_WIDTH)
        kg = jnp.tile(k_norm[layer], heads).reshape(1, ATTN_WIDTH)

        qkv = _qkv_call(xf, gain, w_attn, qg, kg, hsum, hexp, batch, seq)
        oa = _attn_call(qkv, batch, seq)
        j = layer // 2
        routed = layer % 2 == 1
        expert_w = (moe_w1[j], moe_w3[j], moe_w2[j]) if routed else ()
        yc, expert_w = _hgrn_call(xf, gain, w_hgrn, lower_bound[layer].reshape(1, HGRN_WIDTH),
                                  hgrn_out_norm[layer].reshape(1, HGRN_HEAD),
                                  w_proj_c[layer].astype(BF16), batch, seq, expert_w)
        fgain = ffn_norm[layer].reshape(1, d)
        router = None
        if routed:
            rw = jnp.pad(router_w[j], ((0, 0), (0, LANES - N_EXPERTS)))
            rw_hi = rw.astype(BF16)
            router = (fgain, rw_hi, (rw - rw_hi.astype(F32)).astype(BF16))
        merged = _merge_call(xf, gain, w_conv_gate, conv_w[layer], oa, yc,
                             w_proj_a[layer].astype(BF16), w_proj_b[layer].astype(BF16),
                             w_out[layer].astype(BF16), seq, router)
        if routed:
            xf, hn, route = merged
            xf = _moe_layer(xf, hn, route, *expert_w)
        else:
            xf = _ffn_call(merged, fgain, dense_w1[j].astype(BF16), dense_w3[j].astype(BF16),
                           dense_w2[j].astype(BF16))
    return xf.reshape(batch, seq, d)
```

```python
import functools

import jax
import jax.numpy as jnp
from jax import lax
from jax.experimental import pallas as pl
from jax.experimental.pallas import tpu as pltpu

F32 = jnp.float32
BF16 = jnp.bfloat16

D_MODEL = 1024
HEAD_DIM = 64
ATTN_GROUPS = ((128, 1), (512, 4), (2048, 16))
HEADS_PER_GROUP = 4
GROUP_WIDTH = HEADS_PER_GROUP * HEAD_DIM
ATTN_WIDTH = GROUP_WIDTH * len(ATTN_GROUPS)
BAND = 128
CONV_WIDTH = 768
CONV_K = 3
HGRN_WIDTH = 768
HGRN_HEAD = 128
HGRN_HEADS = HGRN_WIDTH // HGRN_HEAD
LB_FLOOR = 1e-30
N_EXPERTS = 8
RMS_EPS = 1e-6
NEG_BIG = -1e30
LOG2E = 1.4426950408889634

LANES = 128
SUBLANES = 8

TOKEN_TILE = 512
ATTN_TILE = ATTN_GROUPS[-1][1] * BAND
HGRN_CHUNK = 256
MOE_ROW_TILE = 512
MOE_FF_TILE = 1792
LOCAL_ROWS = -(-(2 * TOKEN_TILE + N_EXPERTS * SUBLANES) // LANES) * LANES
GROUP_BLOCKS = tuple(1 << b for b in reversed(range(8)))
FILL_BLOCKS = tuple(b for b in GROUP_BLOCKS if b * SUBLANES < MOE_ROW_TILE)
VMEM_LIMIT = 56 * 1024 * 1024


def _dot(a, b):
    return jnp.dot(a, b, preferred_element_type=F32)


def _dot_nt(a, b):
    return lax.dot_general(a, b, (((1,), (1,)), ((), ())), preferred_element_type=F32)


def _dot_tn(a, b):
    return lax.dot_general(a, b, (((0,), (0,)), ((), ())), preferred_element_type=F32)


def _rms_rows(xf, gain):
    ms = jnp.mean(xf * xf, axis=-1, keepdims=True)
    return xf * lax.rsqrt(ms + RMS_EPS) * gain


def _sigmoid(z):
    return 1.0 / (1.0 + jnp.exp(-z))


def _params(*sem):
    return pltpu.CompilerParams(dimension_semantics=sem, vmem_limit_bytes=VMEM_LIMIT)


def _full(shape):
    return pl.BlockSpec(shape, lambda *_: (0,) * len(shape))


def _qkv_kernel(x_ref, g_ref, w_ref, qg_ref, kg_ref, hsum_ref, hexp_ref, *rest, tile):
    out_refs, stage = rest[:9], rest[9]
    h = _rms_rows(x_ref[0], g_ref[...]).astype(BF16)
    p = _dot(h, w_ref[...])

    def head_norm(t, gain):
        ms = _dot((t * t).astype(BF16), hsum_ref[...]) * (1.0 / HEAD_DIM)
        r = lax.rsqrt(ms + RMS_EPS)
        r_hi = r.astype(BF16)
        r_lo = (r - r_hi.astype(F32)).astype(BF16)
        r_cols = _dot(jnp.concatenate([r_hi, r_lo], axis=1), hexp_ref[...])
        return t * r_cols * gain

    parts = (head_norm(p[:, :ATTN_WIDTH], qg_ref[...]),
             head_norm(p[:, ATTN_WIDTH:2 * ATTN_WIDTH], kg_ref[...]),
             p[:, 2 * ATTN_WIDTH:])
    slot = 0
    for a, part in enumerate(parts):
        for group, (_, dilation) in enumerate(ATTN_GROUPS):
            out = out_refs[3 * group + a]
            c0 = group * GROUP_WIDTH
            if dilation == 1:
                out[0, 0] = part[:, c0:c0 + GROUP_WIDTH].astype(BF16)
                continue
            rows = tile // dilation
            for half in range(GROUP_WIDTH // LANES):
                stage[slot] = part[:, c0 + half * LANES:c0 + (half + 1) * LANES]
                for r in range(dilation):
                    out[0, r, :, half * LANES:(half + 1) * LANES] = (
                        stage[slot, pl.ds(r, rows, stride=dilation), :].astype(BF16))
                slot += 1


def _qkv_call(x, gain, w, qg, kg, hsum, hexp, batch, seq):
    t = TOKEN_TILE
    out_specs, out_shape = [], []
    for _, dilation in ATTN_GROUPS:
        for _ in range(3):
            out_specs.append(pl.BlockSpec((1, dilation, t // dilation, GROUP_WIDTH),
                                          lambda b, i: (b, 0, i, 0)))
            out_shape.append(jax.ShapeDtypeStruct(
                (batch, dilation, seq // dilation, GROUP_WIDTH), BF16))
    n_stage = 3 * (GROUP_WIDTH // LANES) * sum(1 for _, d in ATTN_GROUPS if d > 1)
    return pl.pallas_call(
        functools.partial(_qkv_kernel, tile=t),
        grid=(batch, seq // t),
        in_specs=[pl.BlockSpec((1, t, D_MODEL), lambda b, i: (b, i, 0)), _full((1, D_MODEL)),
                  _full(w.shape), _full((1, ATTN_WIDTH)), _full((1, ATTN_WIDTH)),
                  _full(hsum.shape), _full(hexp.shape)],
        out_specs=out_specs,
        out_shape=out_shape,
        scratch_shapes=[pltpu.VMEM((n_stage, t, LANES), F32)],
        compiler_params=_params("parallel", "parallel"),
        name="qkv_proj",
    )(x.reshape(batch, seq, D_MODEL), gain, w, qg, kg, hsum, hexp)


def _attn_kernel(*refs, n_groups):
    ins, o_ref, scr = refs[:5 * n_groups], refs[5 * n_groups], refs[5 * n_groups + 1:]
    kbufs, vbufs = scr[0:n_groups], scr[n_groups:2 * n_groups]
    o_tok, l_tok = scr[2 * n_groups:]
    first_tile = pl.program_id(1) == 0
    qi = lax.broadcasted_iota(jnp.int32, (BAND, 2 * BAND), 0)
    kj = lax.broadcasted_iota(jnp.int32, (BAND, 2 * BAND), 1)
    rel = qi + BAND - kj
    in_window = (rel >= 0) & (rel <= BAND)
    halves = GROUP_WIDTH // LANES
    lane = lax.broadcasted_iota(jnp.int32, (1, LANES), 1)
    head_lanes = [jnp.where(lane // HEAD_DIM == side, 1.0, 0.0).astype(BF16)
                  for side in range(LANES // HEAD_DIM)]
    first_head = lax.broadcasted_iota(jnp.int32, (BAND, LANES), 1) < HEAD_DIM

    for group, (_, dilation) in enumerate(ATTN_GROUPS):
        q_ref, kc_ref, vc_ref, kp_ref, vp_ref = ins[5 * group:5 * group + 5]
        kbuf, vbuf = kbufs[group], vbufs[group]
        length = ATTN_TILE // dilation
        nj = length // BAND
        kbuf[:, 0:BAND, :] = kp_ref[0]
        kbuf[:, BAND:BAND + length, :] = kc_ref[0]
        vbuf[:, 0:BAND, :] = vp_ref[0]
        vbuf[:, BAND:BAND + length, :] = vc_ref[0]

        for idx in range(ATTN_TILE // BAND):
            r, j = idx // nj, idx % nj
            j0 = j * BAND
            if j == 0:
                valid = in_window & (kj >= jnp.where(first_tile, BAND, 0))
            else:
                valid = in_window
            tokens = pl.ds(r + j0 * dilation, BAND, stride=dilation) if dilation > 1 \
                else pl.ds(j0, BAND)
            for half in range(halves):
                lanes = slice(half * LANES, (half + 1) * LANES)
                q2 = q_ref[0, r, j0:j0 + BAND, lanes]
                k2 = kbuf[r, j0:j0 + 2 * BAND, lanes]
                v2 = vbuf[r, j0:j0 + 2 * BAND, lanes]
                o2, l2 = None, None
                for side in range(LANES // HEAD_DIM):
                    s = _dot_nt(q2 * head_lanes[side], k2)
                    s = jnp.where(valid, s, NEG_BIG)
                    m = jnp.max(s, axis=-1, keepdims=True)
                    p = jnp.exp(s - m)
                    den = jnp.sum(p, axis=-1, keepdims=True)
                    o = _dot(p.astype(BF16), v2) * (1.0 / den)
                    lse = jnp.broadcast_to(m + jnp.log(den), (BAND, LANES))
                    o2 = o if side == 0 else jnp.where(first_head, o2, o)
                    l2 = lse if side == 0 else jnp.where(first_head, l2, lse)
                o_tok[group, half, tokens, :] = o2
                l_tok[group, half, tokens, :] = l2

    chunk = 2 * BAND

    def merge(c, carry):
        rows = pl.ds(pl.multiple_of(c * chunk, chunk), chunk)
        for half in range(halves):
            ls = [l_tok[g, half, rows, :] for g in range(n_groups)]
            top = functools.reduce(jnp.maximum, ls)
            es = [jnp.exp(l - top) for l in ls]
            acc = sum(e * o_tok[g, half, rows, :] for g, e in enumerate(es))
            o_ref[0, rows, half * LANES:(half + 1) * LANES] = (acc * (1.0 / sum(es))).astype(BF16)
        return carry

    lax.fori_loop(0, ATTN_TILE // chunk, merge, 0)


def _attn_call(qkv, batch, seq):
    n_groups = len(ATTN_GROUPS)
    in_specs, args, kbufs = [], [], []
    for group, (_, dilation) in enumerate(ATTN_GROUPS):
        length = ATTN_TILE // dilation
        cur = pl.BlockSpec((1, dilation, length, GROUP_WIDTH), lambda b, i: (b, 0, i, 0))
        halo = pl.BlockSpec(
            (1, dilation, BAND, GROUP_WIDTH),
            lambda b, i, nb=length // BAND: (b, 0, jnp.maximum(i * nb - 1, 0), 0))
        q, k, v = qkv[3 * group:3 * group + 3]
        in_specs += [cur, cur, cur, halo, halo]
        args += [q, k, v, k, v]
        kbufs.append(pltpu.VMEM((dilation, BAND + length, GROUP_WIDTH), BF16))
    halves = GROUP_WIDTH // LANES
    o = pl.pallas_call(
        functools.partial(_attn_kernel, n_groups=n_groups),
        grid=(batch, seq // ATTN_TILE),
        in_specs=in_specs,
        out_specs=pl.BlockSpec((1, ATTN_TILE, GROUP_WIDTH), lambda b, i: (b, i, 0)),
        out_shape=jax.ShapeDtypeStruct((batch, seq, GROUP_WIDTH), BF16),
        scratch_shapes=kbufs + kbufs + [
            pltpu.VMEM((n_groups, halves, ATTN_TILE, LANES), F32),
            pltpu.VMEM((n_groups, halves, ATTN_TILE, LANES), F32)],
        compiler_params=_params("parallel", "arbitrary"),
        name="band_attn",
    )(*args)
    return o.reshape(batch * seq, GROUP_WIDTH)


def _hgrn_kernel(x_ref, g_ref, w_ref, lb_ref, og_ref, wp_ref, tri_ref, *rest,
                 tile, chunk, n_convert):
    to_convert, rest = rest[:n_convert], rest[n_convert:]
    y_ref, converted = rest[0], rest[1:1 + n_convert]
    state_ref, proj_ref, gated_ref, b_ref = rest[1 + n_convert:]
    for src, dst in zip(to_convert, converted):
        dst[...] = src[...].astype(BF16)

    @pl.when(pl.program_id(1) == 0)
    def _():
        state_ref[...] = jnp.zeros_like(state_ref)

    h = _rms_rows(x_ref[0], g_ref[...]).astype(BF16)
    proj_ref[...] = _dot(h, w_ref[...])

    lb = lb_ref[...]
    lb_floor = jnp.maximum(lb, LB_FLOOR)
    one_minus_lb = 1.0 - lb
    floor_gap = lb_floor - lb
    row = lax.broadcasted_iota(jnp.int32, (chunk, HGRN_HEAD), 0)
    tt = lax.broadcasted_iota(jnp.int32, (chunk, chunk), 0)
    ss = lax.broadcasted_iota(jnp.int32, (chunk, chunk), 1)
    t_xor_s = tt ^ ss
    w = HGRN_WIDTH
    sizes = [1 << i for i in range(chunk.bit_length() - 1)]
    lower_rows = [(row & m) == 0 for m in sizes]
    lower_sel = [jnp.where(low, 1.0, 0.0).astype(BF16) for low in lower_rows]
    signed_log2e = [jnp.where(low, -LOG2E, LOG2E).astype(BF16) for low in lower_rows]
    level_pairs = [((t_xor_s & (-m)) == m) & ((tt & m) != 0) for m in sizes]
    diagonal = tt == ss

    def chunk_body(c, carry):
        r0 = pl.multiple_of(c * chunk, chunk)
        rows = pl.ds(r0, chunk)
        z = proj_ref[rows, w:2 * w]
        e = jnp.exp(-jnp.abs(z))
        inv = 1.0 / (1.0 + e)
        sig = jnp.where(z >= 0, inv, e * inv)
        nsig = jnp.where(z >= 0, e * inv, inv)
        log_f = jnp.log(lb_floor + one_minus_lb * sig)
        k_all = one_minus_lb * nsig - floor_gap
        lf_hi = log_f.astype(BF16)
        rest = log_f - lf_hi.astype(F32)
        lf_mid = rest.astype(BF16)
        lf_lo = (rest - lf_mid.astype(F32)).astype(BF16)
        tri = tri_ref[...]
        b_all = _dot(tri, lf_hi) + _dot(tri, lf_mid) + _dot(tri, lf_lo)
        b_ref[...] = b_all
        q_all = proj_ref[rows, 0:w].astype(BF16)
        q_all = q_all * _sigmoid(q_all)
        v_all = proj_ref[rows, 2 * w:3 * w]
        g_all = proj_ref[rows, 3 * w:4 * w].astype(BF16)
        g_all = g_all * _sigmoid(g_all)
        for hd in range(HGRN_HEADS):
            cols = slice(hd * HGRN_HEAD, (hd + 1) * HGRN_HEAD)
            b, qb = b_all[:, cols], q_all[:, cols]
            kb = k_all[:, cols].astype(BF16)
            vb = v_all[:, cols].astype(BF16)
            scores = jnp.where(diagonal, _dot_nt(qb, kb), 0.0)
            last_of_block = b
            for lvl, m in enumerate(sizes):
                if m < SUBLANES:
                    lower = lower_rows[lvl]
                    boundary = jnp.where(lower, last_of_block, pltpu.roll(last_of_block, m, 0))
                    if 2 * m < SUBLANES:
                        last_of_block = jnp.where(
                            lower, pltpu.roll(last_of_block, chunk - m, 0), last_of_block)
                else:
                    boundary = jnp.concatenate(
                        [jnp.broadcast_to(b_ref[r:r + 1, cols], (2 * m, HGRN_HEAD))
                         for r in range(m - 1, chunk, 2 * m)], axis=0)
                decay = jnp.exp2((b - boundary).astype(BF16) * signed_log2e[lvl])
                if m < 2 * SUBLANES:
                    source = jnp.where(lower_sel[lvl] > 0, kb, qb)
                else:
                    source = jnp.concatenate(
                        [(kb if (r // m) % 2 == 0 else qb)[r:r + m] for r in range(0, chunk, m)],
                        axis=0)
                side = source * decay
                scores = jnp.where(level_pairs[lvl], _dot_nt(side, side), scores)
            state_t = state_ref[hd]
            q_dec = qb * jnp.exp(b).astype(BF16)
            o = _dot(scores.astype(BF16), vb) + _dot_nt(q_dec, state_t.astype(BF16))
            b_last = b[chunk - 1:chunk, :]
            k_dec = kb * jnp.exp(b_last - b).astype(BF16)
            state_ref[hd] = state_t * jnp.exp(b_last) + _dot_tn(vb, k_dec)
            o = _rms_rows(o, og_ref[...]).astype(BF16) * g_all[:, cols]
            gated_ref[rows, cols] = o
        return carry

    lax.fori_loop(0, tile // chunk, chunk_body, 0, unroll=True)
    y_ref[0] = _dot(gated_ref[...], wp_ref[...]).astype(BF16)


def _hgrn_call(x, gain, w, lb, out_gain, w_proj, batch, seq, convert=()):
    tile, chunk = TOKEN_TILE, HGRN_CHUNK
    tiles_per_seq = seq // tile
    steps = batch * tiles_per_seq
    tri = jnp.tril(jnp.ones((chunk, chunk), BF16))
    blk = lambda width: pl.BlockSpec((1, tile, width), lambda b, i: (b, i, 0))
    flat = [a.reshape(-1, a.shape[-1]) for a in convert]
    assert all(a.shape[0] % (steps * 2 * SUBLANES) == 0 for a in flat)
    slab = lambda a: pl.BlockSpec((a.shape[0] // steps, a.shape[1]),
                                  lambda b, i: (b * tiles_per_seq + i, 0))
    y, *converted = pl.pallas_call(
        functools.partial(_hgrn_kernel, tile=tile, chunk=chunk, n_convert=len(flat)),
        grid=(batch, tiles_per_seq),
        in_specs=[blk(D_MODEL), _full((1, D_MODEL)), _full(w.shape), _full((1, HGRN_WIDTH)),
                  _full((1, HGRN_HEAD)), _full(w_proj.shape), _full((chunk, chunk))]
                 + [slab(a) for a in flat],
        out_specs=[blk(D_MODEL)] + [slab(a) for a in flat],
        out_shape=[jax.ShapeDtypeStruct((batch, seq, D_MODEL), BF16)]
                  + [jax.ShapeDtypeStruct(a.shape, BF16) for a in flat],
        scratch_shapes=[pltpu.VMEM((HGRN_HEADS, HGRN_HEAD, HGRN_HEAD), F32),
                        pltpu.VMEM((tile, 4 * HGRN_WIDTH), F32),
                        pltpu.VMEM((tile, HGRN_WIDTH), BF16),
                        pltpu.VMEM((chunk, HGRN_WIDTH), F32)],
        compiler_params=_params("parallel", "arbitrary"),
        name="hgrn_branch",
    )(x.reshape(batch, seq, D_MODEL), gain, w, lb, out_gain, w_proj, tri, *flat)
    return y.reshape(batch * seq, D_MODEL), [c.reshape(a.shape) for c, a in zip(converted, convert)]


def _top2_route(h, rw_hi, rw_lo):
    logits = _dot(h, rw_hi) + _dot(h, rw_lo)
    lane = lax.broadcasted_iota(jnp.int32, logits.shape, 1)
    logits = jnp.where(lane < N_EXPERTS, logits, -jnp.inf)
    m1 = jnp.max(logits, axis=-1, keepdims=True)
    i1 = jnp.min(jnp.where(logits == m1, lane, LANES), axis=-1, keepdims=True)
    rest = jnp.where(lane == i1, -jnp.inf, logits)
    m2 = jnp.max(rest, axis=-1, keepdims=True)
    i2 = jnp.min(jnp.where(rest == m2, lane, LANES), axis=-1, keepdims=True)
    e = jnp.exp(m2 - m1)
    g1 = 1.0 / (1.0 + e)
    g2 = e * g1
    return jnp.where(
        lane == 0, g1, jnp.where(lane == 1, g2, jnp.where(
            lane == 2, i1.astype(F32), jnp.where(lane == 3, i2.astype(F32), 0.0))))


def _merge_kernel(x_ref, g_ref, w_ref, cw_ref, oa_ref, yc_ref, wpa_ref, wpb_ref, wo_ref, *rest,
                  tile, tiles_per_seq, steps, with_router):
    if with_router:
        fg_ref, rw_hi_ref, rw_lo_ref, out_ref, hn_ref, route_ref, u_ext, prev_ref = rest
    else:
        out_ref, u_ext = rest
    step = pl.program_id(0)

    @pl.when(step % tiles_per_seq == 0)
    def _():
        u_ext[0:SUBLANES, :] = jnp.zeros((SUBLANES, CONV_WIDTH), F32)

    def route_previous_tile():
        hn = _rms_rows(prev_ref[...], fg_ref[...])
        hn_ref[...] = hn
        route_ref[...] = _top2_route(hn.astype(BF16), rw_hi_ref[...], rw_lo_ref[...])

    def merge_tile():
        x = x_ref[...]
        h = _rms_rows(x, g_ref[...]).astype(BF16)
        p = _dot(h, w_ref[...])
        cw = CONV_WIDTH
        u = p[:, 2 * cw:3 * cw] * p[:, 0:cw]
        u_ext[SUBLANES:SUBLANES + tile, :] = u
        conv = cw_ref[0:1, :] * u
        for tap in range(1, CONV_K):
            conv = conv + cw_ref[tap:tap + 1, :] * u_ext[SUBLANES - tap:SUBLANES - tap + tile, :]
        u_ext[0:SUBLANES, :] = u[tile - SUBLANES:, :]
        yb = _dot((p[:, cw:2 * cw] * conv).astype(BF16), wpb_ref[...])
        ya = _dot(oa_ref[...], wpa_ref[...])

        d = D_MODEL
        gates = p[:, 3 * cw:]
        merged = (_sigmoid(gates[:, 0:d]) * ya + _sigmoid(gates[:, d:2 * d]) * yb
                  + _sigmoid(gates[:, 2 * d:3 * d]) * yc_ref[...].astype(F32))
        x_new = x + _dot(merged.astype(BF16), wo_ref[...])
        out_ref[...] = x_new
        return x_new

    if not with_router:
        merge_tile()
        return

    @pl.when(step == 0)
    def _():
        prev_ref[...] = jnp.zeros_like(prev_ref)

    @pl.when(step < steps)
    def _():
        route_previous_tile()
        prev_ref[...] = merge_tile()

    @pl.when(step == steps)
    def _():
        route_previous_tile()


def _merge_call(x, gain, w, conv_w, oa, yc, wpa, wpb, wo, seq, router=None):
    n = x.shape[0]
    tile = TOKEN_TILE
    steps = n // tile
    row = lambda width: pl.BlockSpec((tile, width), lambda i: (jnp.minimum(i, steps - 1), 0))
    late = lambda width: pl.BlockSpec((tile, width), lambda i: (jnp.maximum(i - 1, 0), 0))
    in_specs = [row(D_MODEL), _full((1, D_MODEL)), _full(w.shape), _full(conv_w.shape),
                row(GROUP_WIDTH), row(D_MODEL), _full(wpa.shape), _full(wpb.shape),
                _full(wo.shape)]
    args = [x, gain, w, conv_w, oa, yc, wpa, wpb, wo]
    out_specs, out_shape = [row(D_MODEL)], [jax.ShapeDtypeStruct((n, D_MODEL), F32)]
    scratch = [pltpu.VMEM((SUBLANES + tile, CONV_WIDTH), F32)]
    if router is not None:
        in_specs += [_full(a.shape) for a in router]
        args += list(router)
        out_specs += [late(D_MODEL), late(LANES)]
        out_shape += [jax.ShapeDtypeStruct((n, D_MODEL), F32),
                      jax.ShapeDtypeStruct((n, LANES), F32)]
        scratch.append(pltpu.VMEM((tile, D_MODEL), F32))
    outs = pl.pallas_call(
        functools.partial(_merge_kernel, tile=tile, tiles_per_seq=seq // tile, steps=steps,
                          with_router=router is not None),
        grid=(steps + (router is not None),),
        in_specs=in_specs,
        out_specs=out_specs,
        out_shape=out_shape,
        scratch_shapes=scratch,
        compiler_params=_params("arbitrary"),
        name="merge_out",
    )(*args)
    return outs[0] if router is None else outs


def _ffn_kernel(x_ref, g_ref, w1_ref, w3_ref, w2_ref, out_ref):
    x = x_ref[...]
    h = _rms_rows(x, g_ref[...]).astype(BF16)
    a = _dot(h, w1_ref[...])
    z = (a * _sigmoid(a) * _dot(h, w3_ref[...])).astype(BF16)
    out_ref[...] = x + _dot(z, w2_ref[...])


def _ffn_call(x, gain, w1, w3, w2):
    n = x.shape[0]
    tile = TOKEN_TILE
    row = pl.BlockSpec((tile, D_MODEL), lambda i: (i, 0))
    return pl.pallas_call(
        _ffn_kernel,
        grid=(n // tile,),
        in_specs=[row, _full((1, D_MODEL)), _full(w1.shape), _full(w3.shape), _full(w2.shape)],
        out_specs=row,
        out_shape=jax.ShapeDtypeStruct((n, D_MODEL), F32),
        compiler_params=_params("parallel"),
        name="dense_ffn",
    )(x, gain, w1, w3, w2)


def _block_copies(n8, issue, blocks=GROUP_BLOCKS):
    done = jnp.int32(0)
    for bit in blocks:
        take = (n8 & bit) != 0

        @pl.when(take)
        def _(done=done, bit=bit):
            issue(pl.multiple_of(done * SUBLANES, SUBLANES), bit * SUBLANES)

        done = done + jnp.where(take, bit, 0)


def _slot_positions(route, tri, local_base):
    expert = jnp.concatenate([route[:, 2:3], route[:, 3:4]], axis=0)
    lane = lax.broadcasted_iota(jnp.int32, (expert.shape[0], LANES), 1).astype(F32)
    onehot = expert == lane
    seen = _dot(tri, jnp.where(onehot, 1.0, 0.0).astype(BF16))
    pos = jnp.sum(jnp.where(onehot, seen - 1.0 + local_base, 0.0), axis=1, keepdims=True)
    t = route.shape[0]
    return pos[:t], pos[t:]


def _local_base_row(plan_ref, tile_index):
    lane = lax.broadcasted_iota(jnp.int32, (1, LANES), 1)
    row = jnp.zeros((1, LANES), F32)
    for e in range(N_EXPERTS):
        row = jnp.where(lane == e, plan_ref[tile_index * N_EXPERTS + e].astype(F32), row)
    return row


def _dispatch_kernel(plan_ref, hn_ref, route_ref, tri_ref, out_hbm, lpos_ref, sorted_ref, zeros,
                     sems, fill_sem, *, tile, n_groups):
    i = pl.program_id(0)
    fill0 = 3 * n_groups

    def fills(wait):
        for e in range(N_EXPERTS):
            first = pl.multiple_of(plan_ref[fill0 + e], SUBLANES)

            def issue(off, rows, first=first):
                cp = pltpu.make_async_copy(zeros.at[pl.ds(0, rows), :],
                                           out_hbm.at[pl.ds(first + off, rows), :], fill_sem)
                cp.wait() if wait else cp.start()

            _block_copies(plan_ref[fill0 + N_EXPERTS + e], issue, FILL_BLOCKS)

        def tail(k, carry):
            start = pl.multiple_of(plan_ref[fill0 + 2 * N_EXPERTS] + k * tile, tile)
            cp = pltpu.make_async_copy(zeros, out_hbm.at[pl.ds(start, tile), :], fill_sem)
            cp.wait() if wait else cp.start()
            return carry

        lax.fori_loop(0, plan_ref[fill0 + 2 * N_EXPERTS + 1], tail, 0)

    @pl.when(i == 0)
    def _():
        zeros[...] = jnp.zeros_like(zeros)
        fills(wait=False)
        fills(wait=True)

    pos1, pos2 = _slot_positions(route_ref[...], tri_ref[...], _local_base_row(plan_ref, i))
    lane = lax.broadcasted_iota(jnp.int32, (tile, LANES), 1)
    lpos_ref[...] = jnp.where(lane == 0, pos1, jnp.where(lane == 1, pos2, 0.0))
    col = lax.broadcasted_iota(jnp.int32, (tile, LOCAL_ROWS), 1).astype(F32)
    select = jnp.where((pos1 == col) | (pos2 == col), 1.0, 0.0).astype(BF16)

    def groups(step, wait):
        slot = step % 2
        for e in range(N_EXPERTS):
            g = step * N_EXPERTS + e
            local = pl.multiple_of(plan_ref[g], SUBLANES)
            dest = pl.multiple_of(plan_ref[2 * n_groups + g], SUBLANES)

            def issue(off, rows, local=local, dest=dest):
                cp = pltpu.make_async_copy(sorted_ref.at[slot, pl.ds(local + off, rows), :],
                                           out_hbm.at[pl.ds(dest + off, rows), :], sems.at[slot])
                cp.wait() if wait else cp.start()

            _block_copies(plan_ref[n_groups + g], issue)

    @pl.when(i > 1)
    def _():
        groups(i - 2, wait=True)

    sorted_ref[i % 2] = _dot_tn(select, hn_ref[...].astype(BF16))
    groups(i, wait=False)

    @pl.when(i == pl.num_programs(0) - 1)
    def _():
        @pl.when(i > 0)
        def _():
            groups(i - 1, wait=True)

        groups(i, wait=True)


def _dispatch_call(hn, route, plan, tri, n_sorted):
    n = hn.shape[0]
    tile = TOKEN_TILE
    n_groups = (n // tile) * N_EXPERTS
    row = lambda width: pl.BlockSpec((tile, width), lambda i, plan: (i, 0))
    grid_spec = pltpu.PrefetchScalarGridSpec(
        num_scalar_prefetch=1,
        grid=(n // tile,),
        in_specs=[row(D_MODEL), row(LANES),
                  pl.BlockSpec(tri.shape, lambda i, plan: (0, 0))],
        out_specs=[pl.BlockSpec(memory_space=pl.ANY), row(LANES)],
        scratch_shapes=[pltpu.VMEM((2, LOCAL_ROWS, D_MODEL), F32), pltpu.VMEM((tile, D_MODEL), F32),
                        pltpu.SemaphoreType.DMA((2,)), pltpu.SemaphoreType.DMA(())],
    )
    return pl.pallas_call(
        functools.partial(_dispatch_kernel, tile=tile, n_groups=n_groups),
        grid_spec=grid_spec,
        out_shape=[jax.ShapeDtypeStruct((n_sorted, D_MODEL), F32),
                   jax.ShapeDtypeStruct((n, LANES), F32)],
        compiler_params=_params("arbitrary"),
        name="moe_dispatch",
    )(plan, hn, route, tri)


def _moe_kernel(tile_expert_ref, n_tiles_ref, h_ref, w1_ref, w3_ref, w2_ref, y_ref):
    i, f = pl.program_id(0), pl.program_id(1)
    active = i < n_tiles_ref[0]

    @pl.when((i < 2) & (f == 0))
    def _():
        y_ref[...] = jnp.zeros_like(y_ref)

    @pl.when(active)
    def _():
        h = h_ref[...].astype(BF16)
        a = _dot(h, w1_ref[...])
        z = (a * _sigmoid(a) * _dot(h, w3_ref[...])).astype(BF16)
        y_ref[...] = jnp.where(f == 0, 0.0, y_ref[...]) + _dot(z, w2_ref[...])

    @pl.when(jnp.logical_not(active) & (f == 0))
    def _():
        y_ref[...] = jnp.zeros_like(y_ref)


def _moe_call(hs, w1, w3, w2, tile_expert, n_tiles):
    p = hs.shape[0]
    d_ff = w1.shape[-1]
    tm, tf = MOE_ROW_TILE, MOE_FF_TILE
    nf = d_ff // tf

    def ff_block(i, f, n_tiles_ref):
        return jnp.where(i < n_tiles_ref[0], f, nf - 1)

    grid_spec = pltpu.PrefetchScalarGridSpec(
        num_scalar_prefetch=2,
        grid=(p // tm, nf),
        in_specs=[
            pl.BlockSpec((tm, D_MODEL), lambda i, f, te, nt: (jnp.minimum(i, nt[0] - 1), 0)),
            pl.BlockSpec((None, D_MODEL, tf), lambda i, f, te, nt: (te[i], 0, ff_block(i, f, nt))),
            pl.BlockSpec((None, D_MODEL, tf), lambda i, f, te, nt: (te[i], 0, ff_block(i, f, nt))),
            pl.BlockSpec((None, tf, D_MODEL), lambda i, f, te, nt: (te[i], ff_block(i, f, nt), 0)),
        ],
        out_specs=pl.BlockSpec((tm, D_MODEL), lambda i, f, te, nt: (i, 0)),
    )
    return pl.pallas_call(
        _moe_kernel,
        grid_spec=grid_spec,
        out_shape=jax.ShapeDtypeStruct((p, D_MODEL), F32),
        compiler_params=_params("arbitrary", "arbitrary"),
        name="moe_experts",
    )(tile_expert, n_tiles, hs, w1, w3, w2)


def _combine_kernel(plan_ref, x_ref, route_ref, lpos_ref, y_hbm, out_ref, ybuf, sems, *,
                    tile, n_groups):
    i = pl.program_id(0)

    def groups(step, wait):
        slot = step % 2
        for e in range(N_EXPERTS):
            g = step * N_EXPERTS + e
            local = pl.multiple_of(plan_ref[g], SUBLANES)
            src = pl.multiple_of(plan_ref[2 * n_groups + g], SUBLANES)

            def issue(off, rows, local=local, src=src):
                cp = pltpu.make_async_copy(y_hbm.at[pl.ds(src + off, rows), :],
                                           ybuf.at[slot, pl.ds(local + off, rows), :], sems.at[slot])
                cp.wait() if wait else cp.start()

            _block_copies(plan_ref[n_groups + g], issue)

    @pl.when(i == 0)
    def _():
        ybuf[...] = jnp.zeros_like(ybuf)
        groups(i, wait=False)

    @pl.when(i + 1 < pl.num_programs(0))
    def _():
        groups(i + 1, wait=False)

    col = lax.broadcasted_iota(jnp.int32, (tile, LOCAL_ROWS), 1).astype(F32)
    pick1 = jnp.where(lpos_ref[:, 0:1] == col, 1.0, 0.0).astype(BF16)
    pick2 = jnp.where(lpos_ref[:, 1:2] == col, 1.0, 0.0).astype(BF16)
    groups(i, wait=True)
    last = i * N_EXPERTS + N_EXPERTS - 1
    used = plan_ref[last] + plan_ref[n_groups + last] * SUBLANES
    row = lax.broadcasted_iota(jnp.int32, (LOCAL_ROWS, 1), 0)
    y = jnp.where(row < used, ybuf[i % 2], 0.0).astype(BF16)
    out_ref[...] = (x_ref[...] + route_ref[:, 0:1] * _dot(pick1, y)
                    + route_ref[:, 1:2] * _dot(pick2, y))


def _combine_call(x, route, lpos, y, plan):
    n = x.shape[0]
    tile = TOKEN_TILE
    n_groups = (n // tile) * N_EXPERTS
    row = lambda width: pl.BlockSpec((tile, width), lambda i, plan: (i, 0))
    grid_spec = pltpu.PrefetchScalarGridSpec(
        num_scalar_prefetch=1,
        grid=(n // tile,),
        in_specs=[row(D_MODEL), row(LANES), row(LANES), pl.BlockSpec(memory_space=pl.ANY)],
        out_specs=row(D_MODEL),
        scratch_shapes=[pltpu.VMEM((2, LOCAL_ROWS, D_MODEL), F32), pltpu.SemaphoreType.DMA((2,))],
    )
    return pl.pallas_call(
        functools.partial(_combine_kernel, tile=tile, n_groups=n_groups),
        grid_spec=grid_spec,
        out_shape=jax.ShapeDtypeStruct(x.shape, F32),
        compiler_params=_params("arbitrary"),
        name="moe_combine",
    )(plan, x, route, lpos, y)


def _routing_plan(route, n):
    tile, tm = TOKEN_TILE, MOE_ROW_TILE
    n_tiles_tok = n // tile
    slack = -(-(n_tiles_tok * N_EXPERTS * SUBLANES) // tm) * tm
    p = 2 * n + slack + N_EXPERTS * tm
    experts = jnp.arange(N_EXPERTS, dtype=jnp.int32)
    choice = route[:, 2:4].astype(jnp.int32).reshape(n_tiles_tok, tile, 2)
    counts = jnp.sum((choice[..., None] == experts).astype(jnp.int32), axis=(1, 2))
    size8 = (counts + SUBLANES - 1) // SUBLANES
    size = size8 * SUBLANES
    local = jnp.cumsum(size, axis=1) - size
    total = jnp.sum(size, axis=0)
    padded = ((total + tm - 1) // tm) * tm
    ends = jnp.cumsum(padded)
    starts = ends - padded
    dest = starts[None, :] + jnp.cumsum(size, axis=0) - size
    n_tiles = (ends[-1] // tm).astype(jnp.int32)
    tile_start = jnp.arange(p // tm, dtype=jnp.int32) * tm
    tile_expert = jnp.sum((tile_start[:, None] >= ends[None, :]).astype(jnp.int32), axis=1)
    last_expert = jnp.sum((ends[-1] - 1 >= ends).astype(jnp.int32))
    tile_expert = jnp.minimum(tile_expert, last_expert).astype(jnp.int32)
    tail = jnp.stack([ends[-1], (p - ends[-1]) // tile])
    plan = jnp.concatenate([local.reshape(-1), size8.reshape(-1), dest.reshape(-1),
                            starts + total, (padded - total) // SUBLANES, tail]).astype(jnp.int32)
    return plan, tile_expert, n_tiles.reshape(1), p


def _moe_layer(x, hn, route, w1, w3, w2):
    n = x.shape[0]
    plan, tile_expert, n_tiles, n_sorted = _routing_plan(route, n)
    tri = jnp.tril(jnp.ones((2 * TOKEN_TILE, 2 * TOKEN_TILE), BF16))
    hs, lpos = _dispatch_call(hn, route, plan, tri, n_sorted)
    ys = _moe_call(hs, w1.astype(BF16), w3.astype(BF16), w2.astype(BF16), tile_expert, n_tiles)
    return _combine_call(x, route, lpos, ys, plan)


def kernel(x, attn_norm, ffn_norm, w_in, q_norm, k_norm, conv_w, hgrn_lower_bounds,
           hgrn_out_norm, w_proj_a, w_proj_b, w_proj_c, w_out, dense_w1, dense_w3,
           dense_w2, router_w, moe_w1, moe_w3, moe_w2):
    batch, seq, d = x.shape
    n = batch * seq
    depth = w_in.shape[0]
    assert d == D_MODEL and seq % ATTN_TILE == 0 and seq % TOKEN_TILE == 0
    assert all(window // dilation == BAND for window, dilation in ATTN_GROUPS)

    lbs = jax.nn.softmax(hgrn_lower_bounds.astype(F32), axis=0)
    lower_bound = jnp.cumsum(lbs, axis=0) - lbs[0]
    head_of_col = jnp.arange(ATTN_WIDTH) // HEAD_DIM
    hsum = (head_of_col[:, None] == jnp.arange(LANES)[None, :]).astype(BF16)
    hexp = jnp.concatenate([hsum.T, hsum.T], axis=0)
    a0, a1 = 0, 3 * ATTN_WIDTH
    b1 = a1 + 3 * CONV_WIDTH
    c1 = b1 + 4 * HGRN_WIDTH

    xf = x.reshape(n, d)
    for layer in range(depth):
        w = w_in[layer]
        w_attn = w[:, a0:a1].astype(BF16)
        w_hgrn = w[:, b1:c1].astype(BF16)
        w_conv_gate = jnp.concatenate([w[:, a1:b1], w[:, c1:]], axis=1).astype(BF16)
        gain = attn_norm[layer].reshape(1, d)
        heads = ATTN_WIDTH // HEAD_DIM
        qg = jnp.tile(q_norm[layer] * (HEAD_DIM ** -0.5), heads).reshape(1, ATTN_WIDTH)
        kg = jnp.tile(k_norm[layer], heads).reshape(1, ATTN_WIDTH)

        qkv = _qkv_call(xf, gain, w_attn, qg, kg, hsum, hexp, batch, seq)
        oa = _attn_call(qkv, batch, seq)
        j = layer // 2
        routed = layer % 2 == 1
        expert_w = (moe_w1[j], moe_w3[j], moe_w2[j]) if routed else ()
        yc, expert_w = _hgrn_call(xf, gain, w_hgrn, lower_bound[layer].reshape(1, HGRN_WIDTH),
                                  hgrn_out_norm[layer].reshape(1, HGRN_HEAD),
                                  w_proj_c[layer].astype(BF16), batch, seq, expert_w)
        fgain = ffn_norm[layer].reshape(1, d)
        router = None
        if routed:
            rw = jnp.pad(router_w[j], ((0, 0), (0, LANES - N_EXPERTS)))
            rw_hi = rw.astype(BF16)
            router = (fgain, rw_hi, (rw - rw_hi.astype(F32)).astype(BF16))
        merged = _merge_call(xf, gain, w_conv_gate, conv_w[layer], oa, yc,
                             w_proj_a[layer].astype(BF16), w_proj_b[layer].astype(BF16),
                             w_out[layer].astype(BF16), seq, router)
        if routed:
            xf, hn, route = merged
            xf = _moe_layer(xf, hn, route, *expert_w)
        else:
            xf = _ffn_call(merged, fgain, dense_w1[j].astype(BF16), dense_w3[j].astype(BF16),
                           dense_w2[j].astype(BF16))
    return xf.reshape(batch, seq, d)
```

```python
import functools

import jax
import jax.numpy as jnp
from jax import lax
from jax.experimental import pallas as pl
from jax.experimental.pallas import tpu as pltpu

F32 = jnp.float32
BF16 = jnp.bfloat16

D_MODEL = 1024
HEAD_DIM = 64
ATTN_GROUPS = ((128, 1), (512, 4), (2048, 16))
HEADS_PER_GROUP = 4
GROUP_WIDTH = HEADS_PER_GROUP * HEAD_DIM
ATTN_WIDTH = GROUP_WIDTH * len(ATTN_GROUPS)
BAND = 128
CONV_WIDTH = 768
CONV_K = 3
HGRN_WIDTH = 768
HGRN_HEAD = 128
HGRN_HEADS = HGRN_WIDTH // HGRN_HEAD
LB_FLOOR = 1e-30
N_EXPERTS = 8
RMS_EPS = 1e-6
NEG_BIG = -1e30
LOG2E = 1.4426950408889634

LANES = 128
SUBLANES = 8

TOKEN_TILE = 512
ATTN_TILE = ATTN_GROUPS[-1][1] * BAND
HGRN_CHUNK = 256
MOE_ROW_TILE = 512
MOE_FF_TILE = 1792
LOCAL_ROWS = -(-(2 * TOKEN_TILE + N_EXPERTS * SUBLANES) // LANES) * LANES
GROUP_BLOCKS = tuple(1 << b for b in reversed(range(8)))
FILL_BLOCKS = tuple(b for b in GROUP_BLOCKS if b * SUBLANES < MOE_ROW_TILE)
VMEM_LIMIT = 56 * 1024 * 1024


def _dot(a, b):
    return jnp.dot(a, b, preferred_element_type=F32)


def _dot_nt(a, b):
    return lax.dot_general(a, b, (((1,), (1,)), ((), ())), preferred_element_type=F32)


def _dot_tn(a, b):
    return lax.dot_general(a, b, (((0,), (0,)), ((), ())), preferred_element_type=F32)


def _rms_rows(xf, gain):
    ms = jnp.mean(xf * xf, axis=-1, keepdims=True)
    return xf * lax.rsqrt(ms + RMS_EPS) * gain


def _sigmoid(z):
    return 1.0 / (1.0 + jnp.exp(-z))


def _params(*sem):
    return pltpu.CompilerParams(dimension_semantics=sem, vmem_limit_bytes=VMEM_LIMIT)


def _full(shape):
    return pl.BlockSpec(shape, lambda *_: (0,) * len(shape))


def _qkv_kernel(x_ref, g_ref, w_ref, qg_ref, kg_ref, hsum_ref, hexp_ref, *rest, tile):
    out_refs, stage = rest[:9], rest[9]
    h = _rms_rows(x_ref[0], g_ref[...]).astype(BF16)
    p = _dot(h, w_ref[...])

    def head_norm(t, gain):
        ms = _dot((t * t).astype(BF16), hsum_ref[...]) * (1.0 / HEAD_DIM)
        r = lax.rsqrt(ms + RMS_EPS)
        r_hi = r.astype(BF16)
        r_lo = (r - r_hi.astype(F32)).astype(BF16)
        r_cols = _dot(jnp.concatenate([r_hi, r_lo], axis=1), hexp_ref[...])
        return t * r_cols * gain

    parts = (head_norm(p[:, :ATTN_WIDTH], qg_ref[...]),
             head_norm(p[:, ATTN_WIDTH:2 * ATTN_WIDTH], kg_ref[...]),
             p[:, 2 * ATTN_WIDTH:])
    slot = 0
    for a, part in enumerate(parts):
        for group, (_, dilation) in enumerate(ATTN_GROUPS):
            out = out_refs[3 * group + a]
            c0 = group * GROUP_WIDTH
            if dilation == 1:
                out[0, 0] = part[:, c0:c0 + GROUP_WIDTH].astype(BF16)
                continue
            rows = tile // dilation
            for half in range(GROUP_WIDTH // LANES):
                stage[slot] = part[:, c0 + half * LANES:c0 + (half + 1) * LANES]
                for r in range(dilation):
                    out[0, r, :, half * LANES:(half + 1) * LANES] = (
                        stage[slot, pl.ds(r, rows, stride=dilation), :].astype(BF16))
                slot += 1


def _qkv_call(x, gain, w, qg, kg, hsum, hexp, batch, seq):
    t = TOKEN_TILE
    out_specs, out_shape = [], []
    for _, dilation in ATTN_GROUPS:
        for _ in range(3):
            out_specs.append(pl.BlockSpec((1, dilation, t // dilation, GROUP_WIDTH),
                                          lambda b, i: (b, 0, i, 0)))
            out_shape.append(jax.ShapeDtypeStruct(
                (batch, dilation, seq // dilation, GROUP_WIDTH), BF16))
    n_stage = 3 * (GROUP_WIDTH // LANES) * sum(1 for _, d in ATTN_GROUPS if d > 1)
    return pl.pallas_call(
        functools.partial(_qkv_kernel, tile=t),
        grid=(batch, seq // t),
        in_specs=[pl.BlockSpec((1, t, D_MODEL), lambda b, i: (b, i, 0)), _full((1, D_MODEL)),
                  _full(w.shape), _full((1, ATTN_WIDTH)), _full((1, ATTN_WIDTH)),
                  _full(hsum.shape), _full(hexp.shape)],
        out_specs=out_specs,
        out_shape=out_shape,
        scratch_shapes=[pltpu.VMEM((n_stage, t, LANES), F32)],
        compiler_params=_params("parallel", "parallel"),
        name="qkv_proj",
    )(x.reshape(batch, seq, D_MODEL), gain, w, qg, kg, hsum, hexp)


def _attn_kernel(*refs, n_groups):
    ins, o_ref = refs[:5 * n_groups], refs[5 * n_groups]
    o_tok, l_tok = refs[5 * n_groups + 1:]
    first_tile = pl.program_id(1) == 0
    qi = lax.broadcasted_iota(jnp.int32, (BAND, 2 * BAND), 0)
    kj = lax.broadcasted_iota(jnp.int32, (BAND, 2 * BAND), 1)
    rel = qi + BAND - kj
    in_window = (rel >= 0) & (rel <= BAND)
    halves = GROUP_WIDTH // LANES
    lane = lax.broadcasted_iota(jnp.int32, (1, LANES), 1)
    head_lanes = [jnp.where(lane // HEAD_DIM == side, 1.0, 0.0).astype(BF16)
                  for side in range(LANES // HEAD_DIM)]
    first_head = lax.broadcasted_iota(jnp.int32, (BAND, LANES), 1) < HEAD_DIM

    for group, (_, dilation) in enumerate(ATTN_GROUPS):
        q_ref, kc_ref, vc_ref, kp_ref, vp_ref = ins[5 * group:5 * group + 5]
        length = ATTN_TILE // dilation
        nj = length // BAND

        for idx in range(ATTN_TILE // BAND):
            r, j = idx // nj, idx % nj
            j0 = j * BAND
            if j == 0:
                valid = in_window & (kj >= jnp.where(first_tile, BAND, 0))
            else:
                valid = in_window
            tokens = pl.ds(r + j0 * dilation, BAND, stride=dilation) if dilation > 1 \
                else pl.ds(j0, BAND)
            for half in range(halves):
                lanes = slice(half * LANES, (half + 1) * LANES)
                q2 = q_ref[0, r, j0:j0 + BAND, lanes]
                if j == 0:
                    k2 = jnp.concatenate([kp_ref[0, r, :, lanes], kc_ref[0, r, 0:BAND, lanes]], axis=0)
                    v2 = jnp.concatenate([vp_ref[0, r, :, lanes], vc_ref[0, r, 0:BAND, lanes]], axis=0)
                else:
                    k2 = kc_ref[0, r, j0 - BAND:j0 + BAND, lanes]
                    v2 = vc_ref[0, r, j0 - BAND:j0 + BAND, lanes]
                o2, l2 = None, None
                for side in range(LANES // HEAD_DIM):
                    s = _dot_nt(q2 * head_lanes[side], k2)
                    s = jnp.where(valid, s, NEG_BIG)
                    m = jnp.max(s, axis=-1, keepdims=True)
                    p = jnp.exp(s - m)
                    den = jnp.sum(p, axis=-1, keepdims=True)
                    o = _dot(p.astype(BF16), v2) * (1.0 / den)
                    lse = jnp.broadcast_to(m + jnp.log(den), (BAND, LANES))
                    o2 = o if side == 0 else jnp.where(first_head, o2, o)
                    l2 = lse if side == 0 else jnp.where(first_head, l2, lse)
                o_tok[group, half, tokens, :] = o2
                l_tok[group, half, tokens, :] = l2

    chunk = 2 * BAND

    def merge(c, carry):
        rows = pl.ds(pl.multiple_of(c * chunk, chunk), chunk)
        for half in range(halves):
            ls = [l_tok[g, half, rows, :] for g in range(n_groups)]
            top = functools.reduce(jnp.maximum, ls)
            es = [jnp.exp(l - top) for l in ls]
            acc = sum(e * o_tok[g, half, rows, :] for g, e in enumerate(es))
            o_ref[0, rows, half * LANES:(half + 1) * LANES] = (acc * (1.0 / sum(es))).astype(BF16)
        return carry

    lax.fori_loop(0, ATTN_TILE // chunk, merge, 0)


def _attn_call(qkv, batch, seq):
    n_groups = len(ATTN_GROUPS)
    in_specs, args = [], []
    for group, (_, dilation) in enumerate(ATTN_GROUPS):
        length = ATTN_TILE // dilation
        cur = pl.BlockSpec((1, dilation, length, GROUP_WIDTH), lambda b, i: (b, 0, i, 0))
        halo = pl.BlockSpec(
            (1, dilation, BAND, GROUP_WIDTH),
            lambda b, i, nb=length // BAND: (b, 0, jnp.maximum(i * nb - 1, 0), 0))
        q, k, v = qkv[3 * group:3 * group + 3]
        in_specs += [cur, cur, cur, halo, halo]
        args += [q, k, v, k, v]
    halves = GROUP_WIDTH // LANES
    o = pl.pallas_call(
        functools.partial(_attn_kernel, n_groups=n_groups),
        grid=(batch, seq // ATTN_TILE),
        in_specs=in_specs,
        out_specs=pl.BlockSpec((1, ATTN_TILE, GROUP_WIDTH), lambda b, i: (b, i, 0)),
        out_shape=jax.ShapeDtypeStruct((batch, seq, GROUP_WIDTH), BF16),
        scratch_shapes=[pltpu.VMEM((n_groups, halves, ATTN_TILE, LANES), F32),
                        pltpu.VMEM((n_groups, halves, ATTN_TILE, LANES), F32)],
        compiler_params=_params("parallel", "arbitrary"),
        name="band_attn",
    )(*args)
    return o.reshape(batch * seq, GROUP_WIDTH)


def _hgrn_kernel(x_ref, g_ref, w_ref, lb_ref, og_ref, wp_ref, tri_ref, *rest,
                 tile, chunk, n_convert):
    to_convert, rest = rest[:n_convert], rest[n_convert:]
    y_ref, converted = rest[0], rest[1:1 + n_convert]
    state_ref, proj_ref, gated_ref, b_ref = rest[1 + n_convert:]
    for src, dst in zip(to_convert, converted):
        dst[...] = src[...].astype(BF16)

    @pl.when(pl.program_id(1) == 0)
    def _():
        state_ref[...] = jnp.zeros_like(state_ref)

    h = _rms_rows(x_ref[0], g_ref[...]).astype(BF16)
    proj_ref[...] = _dot(h, w_ref[...])

    lb = lb_ref[...]
    lb_floor = jnp.maximum(lb, LB_FLOOR)
    one_minus_lb = 1.0 - lb
    floor_gap = lb_floor - lb
    row = lax.broadcasted_iota(jnp.int32, (chunk, HGRN_HEAD), 0)
    tt = lax.broadcasted_iota(jnp.int32, (chunk, chunk), 0)
    ss = lax.broadcasted_iota(jnp.int32, (chunk, chunk), 1)
    t_xor_s = tt ^ ss
    w = HGRN_WIDTH
    sizes = [1 << i for i in range(chunk.bit_length() - 1)]
    lower_rows = [(row & m) == 0 for m in sizes]
    lower_sel = [jnp.where(low, 1.0, 0.0).astype(BF16) for low in lower_rows]
    signed_log2e = [jnp.where(low, -LOG2E, LOG2E).astype(BF16) for low in lower_rows]
    level_pairs = [((t_xor_s & (-m)) == m) & ((tt & m) != 0) for m in sizes]
    diagonal = tt == ss

    def chunk_body(c, carry):
        r0 = pl.multiple_of(c * chunk, chunk)
        rows = pl.ds(r0, chunk)
        z = proj_ref[rows, w:2 * w]
        e = jnp.exp(-jnp.abs(z))
        inv = 1.0 / (1.0 + e)
        sig = jnp.where(z >= 0, inv, e * inv)
        nsig = jnp.where(z >= 0, e * inv, inv)
        log_f = jnp.log(lb_floor + one_minus_lb * sig)
        k_all = one_minus_lb * nsig - floor_gap
        lf_hi = log_f.astype(BF16)
        rest = log_f - lf_hi.astype(F32)
        lf_mid = rest.astype(BF16)
        lf_lo = (rest - lf_mid.astype(F32)).astype(BF16)
        tri = tri_ref[...]
        b_all = _dot(tri, lf_hi) + _dot(tri, lf_mid) + _dot(tri, lf_lo)
        b_ref[...] = b_all
        q_all = proj_ref[rows, 0:w].astype(BF16)
        q_all = q_all * _sigmoid(q_all)
        v_all = proj_ref[rows, 2 * w:3 * w]
        g_all = proj_ref[rows, 3 * w:4 * w].astype(BF16)
        g_all = g_all * _sigmoid(g_all)
        for hd in range(HGRN_HEADS):
            cols = slice(hd * HGRN_HEAD, (hd + 1) * HGRN_HEAD)
            b, qb = b_all[:, cols], q_all[:, cols]
            kb = k_all[:, cols].astype(BF16)
            vb = v_all[:, cols].astype(BF16)
            scores = jnp.where(diagonal, _dot_nt(qb, kb), 0.0)
            last_of_block = b
            for lvl, m in enumerate(sizes):
                if m < SUBLANES:
                    lower = lower_rows[lvl]
                    boundary = jnp.where(lower, last_of_block, pltpu.roll(last_of_block, m, 0))
                    if 2 * m < SUBLANES:
                        last_of_block = jnp.where(
                            lower, pltpu.roll(last_of_block, chunk - m, 0), last_of_block)
                else:
                    boundary = jnp.concatenate(
                        [jnp.broadcast_to(b_ref[r:r + 1, cols], (2 * m, HGRN_HEAD))
                         for r in range(m - 1, chunk, 2 * m)], axis=0)
                decay = jnp.exp2((b - boundary).astype(BF16) * signed_log2e[lvl])
                if m < 2 * SUBLANES:
                    source = jnp.where(lower_sel[lvl] > 0, kb, qb)
                else:
                    source = jnp.concatenate(
                        [(kb if (r // m) % 2 == 0 else qb)[r:r + m] for r in range(0, chunk, m)],
                        axis=0)
                side = source * decay
                scores = jnp.where(level_pairs[lvl], _dot_nt(side, side), scores)
            state_t = state_ref[hd]
            q_dec = qb * jnp.exp(b).astype(BF16)
            o = _dot(scores.astype(BF16), vb) + _dot_nt(q_dec, state_t.astype(BF16))
            b_last = b[chunk - 1:chunk, :]
            k_dec = kb * jnp.exp(b_last - b).astype(BF16)
            state_ref[hd] = state_t * jnp.exp(b_last) + _dot_tn(vb, k_dec)
            o = _rms_rows(o, og_ref[...]).astype(BF16) * g_all[:, cols]
            gated_ref[rows, cols] = o
        return carry

    lax.fori_loop(0, tile // chunk, chunk_body, 0, unroll=True)
    y_ref[0] = _dot(gated_ref[...], wp_ref[...]).astype(BF16)


def _hgrn_call(x, gain, w, lb, out_gain, w_proj, batch, seq, convert=()):
    tile, chunk = TOKEN_TILE, HGRN_CHUNK
    tiles_per_seq = seq // tile
    steps = batch * tiles_per_seq
    tri = jnp.tril(jnp.ones((chunk, chunk), BF16))
    blk = lambda width: pl.BlockSpec((1, tile, width), lambda b, i: (b, i, 0))
    flat = [a.reshape(-1, a.shape[-1]) for a in convert]
    assert all(a.shape[0] % (steps * 2 * SUBLANES) == 0 for a in flat)
    slab = lambda a: pl.BlockSpec((a.shape[0] // steps, a.shape[1]),
                                  lambda b, i: (b * tiles_per_seq + i, 0))
    y, *converted = pl.pallas_call(
        functools.partial(_hgrn_kernel, tile=tile, chunk=chunk, n_convert=len(flat)),
        grid=(batch, tiles_per_seq),
        in_specs=[blk(D_MODEL), _full((1, D_MODEL)), _full(w.shape), _full((1, HGRN_WIDTH)),
                  _full((1, HGRN_HEAD)), _full(w_proj.shape), _full((chunk, chunk))]
                 + [slab(a) for a in flat],
        out_specs=[blk(D_MODEL)] + [slab(a) for a in flat],
        out_shape=[jax.ShapeDtypeStruct((batch, seq, D_MODEL), BF16)]
                  + [jax.ShapeDtypeStruct(a.shape, BF16) for a in flat],
        scratch_shapes=[pltpu.VMEM((HGRN_HEADS, HGRN_HEAD, HGRN_HEAD), F32),
                        pltpu.VMEM((tile, 4 * HGRN_WIDTH), F32),
                        pltpu.VMEM((tile, HGRN_WIDTH), BF16),
                        pltpu.VMEM((chunk, HGRN_WIDTH), F32)],
        compiler_params=_params("parallel", "arbitrary"),
        name="hgrn_branch",
    )(x.reshape(batch, seq, D_MODEL), gain, w, lb, out_gain, w_proj, tri, *flat)
    return y.reshape(batch * seq, D_MODEL), [c.reshape(a.shape) for c, a in zip(converted, convert)]


def _top2_route(h, rw_hi, rw_lo):
    logits = _dot(h, rw_hi) + _dot(h, rw_lo)
    lane = lax.broadcasted_iota(jnp.int32, logits.shape, 1)
    logits = jnp.where(lane < N_EXPERTS, logits, -jnp.inf)
    m1 = jnp.max(logits, axis=-1, keepdims=True)
    i1 = jnp.min(jnp.where(logits == m1, lane, LANES), axis=-1, keepdims=True)
    rest = jnp.where(lane == i1, -jnp.inf, logits)
    m2 = jnp.max(rest, axis=-1, keepdims=True)
    i2 = jnp.min(jnp.where(rest == m2, lane, LANES), axis=-1, keepdims=True)
    e = jnp.exp(m2 - m1)
    g1 = 1.0 / (1.0 + e)
    g2 = e * g1
    return jnp.where(
        lane == 0, g1, jnp.where(lane == 1, g2, jnp.where(
            lane == 2, i1.astype(F32), jnp.where(lane == 3, i2.astype(F32), 0.0))))


def _merge_kernel(x_ref, g_ref, w_ref, cw_ref, oa_ref, yc_ref, wpa_ref, wpb_ref, wo_ref, *rest,
                  tile, tiles_per_seq, steps, with_router):
    if with_router:
        fg_ref, rw_hi_ref, rw_lo_ref, out_ref, hn_ref, route_ref, u_ext, prev_ref = rest
    else:
        out_ref, u_ext = rest
    step = pl.program_id(0)

    @pl.when(step % tiles_per_seq == 0)
    def _():
        u_ext[0:SUBLANES, :] = jnp.zeros((SUBLANES, CONV_WIDTH), F32)

    def route_previous_tile():
        hn = _rms_rows(prev_ref[...], fg_ref[...])
        hn_ref[...] = hn
        route_ref[...] = _top2_route(hn.astype(BF16), rw_hi_ref[...], rw_lo_ref[...])

    def merge_tile():
        x = x_ref[...]
        h = _rms_rows(x, g_ref[...]).astype(BF16)
        p = _dot(h, w_ref[...])
        cw = CONV_WIDTH
        u = p[:, 2 * cw:3 * cw] * p[:, 0:cw]
        u_ext[SUBLANES:SUBLANES + tile, :] = u
        conv = cw_ref[0:1, :] * u
        for tap in range(1, CONV_K):
            conv = conv + cw_ref[tap:tap + 1, :] * u_ext[SUBLANES - tap:SUBLANES - tap + tile, :]
        u_ext[0:SUBLANES, :] = u[tile - SUBLANES:, :]
        yb = _dot((p[:, cw:2 * cw] * conv).astype(BF16), wpb_ref[...])
        ya = _dot(oa_ref[...], wpa_ref[...])

        d = D_MODEL
        gates = p[:, 3 * cw:]
        merged = (_sigmoid(gates[:, 0:d]) * ya + _sigmoid(gates[:, d:2 * d]) * yb
                  + _sigmoid(gates[:, 2 * d:3 * d]) * yc_ref[...].astype(F32))
        x_new = x + _dot(merged.astype(BF16), wo_ref[...])
        out_ref[...] = x_new
        return x_new

    if not with_router:
        merge_tile()
        return

    @pl.when(step == 0)
    def _():
        prev_ref[...] = jnp.zeros_like(prev_ref)

    @pl.when(step < steps)
    def _():
        route_previous_tile()
        prev_ref[...] = merge_tile()

    @pl.when(step == steps)
    def _():
        route_previous_tile()


def _merge_call(x, gain, w, conv_w, oa, yc, wpa, wpb, wo, seq, router=None):
    n = x.shape[0]
    tile = TOKEN_TILE
    steps = n // tile
    row = lambda width: pl.BlockSpec((tile, width), lambda i: (jnp.minimum(i, steps - 1), 0))
    late = lambda width: pl.BlockSpec((tile, width), lambda i: (jnp.maximum(i - 1, 0), 0))
    in_specs = [row(D_MODEL), _full((1, D_MODEL)), _full(w.shape), _full(conv_w.shape),
                row(GROUP_WIDTH), row(D_MODEL), _full(wpa.shape), _full(wpb.shape),
                _full(wo.shape)]
    args = [x, gain, w, conv_w, oa, yc, wpa, wpb, wo]
    out_specs, out_shape = [row(D_MODEL)], [jax.ShapeDtypeStruct((n, D_MODEL), F32)]
    scratch = [pltpu.VMEM((SUBLANES + tile, CONV_WIDTH), F32)]
    if router is not None:
        in_specs += [_full(a.shape) for a in router]
        args += list(router)
        out_specs += [late(D_MODEL), late(LANES)]
        out_shape += [jax.ShapeDtypeStruct((n, D_MODEL), F32),
                      jax.ShapeDtypeStruct((n, LANES), F32)]
        scratch.append(pltpu.VMEM((tile, D_MODEL), F32))
    outs = pl.pallas_call(
        functools.partial(_merge_kernel, tile=tile, tiles_per_seq=seq // tile, steps=steps,
                          with_router=router is not None),
        grid=(steps + (router is not None),),
        in_specs=in_specs,
        out_specs=out_specs,
        out_shape=out_shape,
        scratch_shapes=scratch,
        compiler_params=_params("arbitrary"),
        name="merge_out",
    )(*args)
    return outs[0] if router is None else outs


def _ffn_kernel(x_ref, g_ref, w1_ref, w3_ref, w2_ref, out_ref):
    x = x_ref[...]
    h = _rms_rows(x, g_ref[...]).astype(BF16)
    a = _dot(h, w1_ref[...])
    z = (a * _sigmoid(a) * _dot(h, w3_ref[...])).astype(BF16)
    out_ref[...] = x + _dot(z, w2_ref[...])


def _ffn_call(x, gain, w1, w3, w2):
    n = x.shape[0]
    tile = TOKEN_TILE
    row = pl.BlockSpec((tile, D_MODEL), lambda i: (i, 0))
    return pl.pallas_call(
        _ffn_kernel,
        grid=(n // tile,),
        in_specs=[row, _full((1, D_MODEL)), _full(w1.shape), _full(w3.shape), _full(w2.shape)],
        out_specs=row,
        out_shape=jax.ShapeDtypeStruct((n, D_MODEL), F32),
        compiler_params=_params("parallel"),
        name="dense_ffn",
    )(x, gain, w1, w3, w2)


def _block_copies(n8, issue, blocks=GROUP_BLOCKS):
    done = jnp.int32(0)
    for bit in blocks:
        take = (n8 & bit) != 0

        @pl.when(take)
        def _(done=done, bit=bit):
            issue(pl.multiple_of(done * SUBLANES, SUBLANES), bit * SUBLANES)

        done = done + jnp.where(take, bit, 0)


def _slot_positions(route, tri, local_base):
    expert = jnp.concatenate([route[:, 2:3], route[:, 3:4]], axis=0)
    lane = lax.broadcasted_iota(jnp.int32, (expert.shape[0], LANES), 1).astype(F32)
    onehot = expert == lane
    seen = _dot(tri, jnp.where(onehot, 1.0, 0.0).astype(BF16))
    pos = jnp.sum(jnp.where(onehot, seen - 1.0 + local_base, 0.0), axis=1, keepdims=True)
    t = route.shape[0]
    return pos[:t], pos[t:]


def _local_base_row(plan_ref, tile_index):
    lane = lax.broadcasted_iota(jnp.int32, (1, LANES), 1)
    row = jnp.zeros((1, LANES), F32)
    for e in range(N_EXPERTS):
        row = jnp.where(lane == e, plan_ref[tile_index * N_EXPERTS + e].astype(F32), row)
    return row


def _dispatch_kernel(plan_ref, hn_ref, route_ref, tri_ref, out_hbm, lpos_ref, sorted_ref, zeros,
                     sems, fill_sem, *, tile, n_groups):
    i = pl.program_id(0)
    fill0 = 3 * n_groups

    def fills(wait):
        for e in range(N_EXPERTS):
            first = pl.multiple_of(plan_ref[fill0 + e], SUBLANES)

            def issue(off, rows, first=first):
                cp = pltpu.make_async_copy(zeros.at[pl.ds(0, rows), :],
                                           out_hbm.at[pl.ds(first + off, rows), :], fill_sem)
                cp.wait() if wait else cp.start()

            _block_copies(plan_ref[fill0 + N_EXPERTS + e], issue, FILL_BLOCKS)

        def tail(k, carry):
            start = pl.multiple_of(plan_ref[fill0 + 2 * N_EXPERTS] + k * tile, tile)
            cp = pltpu.make_async_copy(zeros, out_hbm.at[pl.ds(start, tile), :], fill_sem)
            cp.wait() if wait else cp.start()
            return carry

        lax.fori_loop(0, plan_ref[fill0 + 2 * N_EXPERTS + 1], tail, 0)

    @pl.when(i == 0)
    def _():
        zeros[...] = jnp.zeros_like(zeros)
        fills(wait=False)
        fills(wait=True)

    pos1, pos2 = _slot_positions(route_ref[...], tri_ref[...], _local_base_row(plan_ref, i))
    lane = lax.broadcasted_iota(jnp.int32, (tile, LANES), 1)
    lpos_ref[...] = jnp.where(lane == 0, pos1, jnp.where(lane == 1, pos2, 0.0))
    col = lax.broadcasted_iota(jnp.int32, (tile, LOCAL_ROWS), 1).astype(F32)
    select = jnp.where((pos1 == col) | (pos2 == col), 1.0, 0.0).astype(BF16)

    def groups(step, wait):
        slot = step % 2
        for e in range(N_EXPERTS):
            g = step * N_EXPERTS + e
            local = pl.multiple_of(plan_ref[g], SUBLANES)
            dest = pl.multiple_of(plan_ref[2 * n_groups + g], SUBLANES)

            def issue(off, rows, local=local, dest=dest):
                cp = pltpu.make_async_copy(sorted_ref.at[slot, pl.ds(local + off, rows), :],
                                           out_hbm.at[pl.ds(dest + off, rows), :], sems.at[slot])
                cp.wait() if wait else cp.start()

            _block_copies(plan_ref[n_groups + g], issue)

    @pl.when(i > 1)
    def _():
        groups(i - 2, wait=True)

    sorted_ref[i % 2] = _dot_tn(select, hn_ref[...].astype(BF16))
    groups(i, wait=False)

    @pl.when(i == pl.num_programs(0) - 1)
    def _():
        @pl.when(i > 0)
        def _():
            groups(i - 1, wait=True)

        groups(i, wait=True)


def _dispatch_call(hn, route, plan, tri, n_sorted):
    n = hn.shape[0]
    tile = TOKEN_TILE
    n_groups = (n // tile) * N_EXPERTS
    row = lambda width: pl.BlockSpec((tile, width), lambda i, plan: (i, 0))
    grid_spec = pltpu.PrefetchScalarGridSpec(
        num_scalar_prefetch=1,
        grid=(n // tile,),
        in_specs=[row(D_MODEL), row(LANES),
                  pl.BlockSpec(tri.shape, lambda i, plan: (0, 0))],
        out_specs=[pl.BlockSpec(memory_space=pl.ANY), row(LANES)],
        scratch_shapes=[pltpu.VMEM((2, LOCAL_ROWS, D_MODEL), F32), pltpu.VMEM((tile, D_MODEL), F32),
                        pltpu.SemaphoreType.DMA((2,)), pltpu.SemaphoreType.DMA(())],
    )
    return pl.pallas_call(
        functools.partial(_dispatch_kernel, tile=tile, n_groups=n_groups),
        grid_spec=grid_spec,
        out_shape=[jax.ShapeDtypeStruct((n_sorted, D_MODEL), F32),
                   jax.ShapeDtypeStruct((n, LANES), F32)],
        compiler_params=_params("arbitrary"),
        name="moe_dispatch",
    )(plan, hn, route, tri)


def _moe_kernel(tile_expert_ref, n_tiles_ref, h_ref, w1_ref, w3_ref, w2_ref, y_ref):
    i, f = pl.program_id(0), pl.program_id(1)
    active = i < n_tiles_ref[0]

    @pl.when((i < 2) & (f == 0))
    def _():
        y_ref[...] = jnp.zeros_like(y_ref)

    @pl.when(active)
    def _():
        h = h_ref[...].astype(BF16)
        a = _dot(h, w1_ref[...])
        z = (a * _sigmoid(a) * _dot(h, w3_ref[...])).astype(BF16)
        y_ref[...] = jnp.where(f == 0, 0.0, y_ref[...]) + _dot(z, w2_ref[...])

    @pl.when(jnp.logical_not(active) & (f == 0))
    def _():
        y_ref[...] = jnp.zeros_like(y_ref)


def _moe_call(hs, w1, w3, w2, tile_expert, n_tiles):
    p = hs.shape[0]
    d_ff = w1.shape[-1]
    tm, tf = MOE_ROW_TILE, MOE_FF_TILE
    nf = d_ff // tf

    def ff_block(i, f, n_tiles_ref):
        return jnp.where(i < n_tiles_ref[0], f, nf - 1)

    grid_spec = pltpu.PrefetchScalarGridSpec(
        num_scalar_prefetch=2,
        grid=(p // tm, nf),
        in_specs=[
            pl.BlockSpec((tm, D_MODEL), lambda i, f, te, nt: (jnp.minimum(i, nt[0] - 1), 0)),
            pl.BlockSpec((None, D_MODEL, tf), lambda i, f, te, nt: (te[i], 0, ff_block(i, f, nt))),
            pl.BlockSpec((None, D_MODEL, tf), lambda i, f, te, nt: (te[i], 0, ff_block(i, f, nt))),
            pl.BlockSpec((None, tf, D_MODEL), lambda i, f, te, nt: (te[i], ff_block(i, f, nt), 0)),
        ],
        out_specs=pl.BlockSpec((tm, D_MODEL), lambda i, f, te, nt: (i, 0)),
    )
    return pl.pallas_call(
        _moe_kernel,
        grid_spec=grid_spec,
        out_shape=jax.ShapeDtypeStruct((p, D_MODEL), F32),
        compiler_params=_params("arbitrary", "arbitrary"),
        name="moe_experts",
    )(tile_expert, n_tiles, hs, w1, w3, w2)


def _combine_kernel(plan_ref, x_ref, route_ref, lpos_ref, y_hbm, out_ref, ybuf, sems, *,
                    tile, n_groups):
    i = pl.program_id(0)

    def groups(step, wait):
        slot = step % 2
        for e in range(N_EXPERTS):
            g = step * N_EXPERTS + e
            local = pl.multiple_of(plan_ref[g], SUBLANES)
            src = pl.multiple_of(plan_ref[2 * n_groups + g], SUBLANES)

            def issue(off, rows, local=local, src=src):
                cp = pltpu.make_async_copy(y_hbm.at[pl.ds(src + off, rows), :],
                                           ybuf.at[slot, pl.ds(local + off, rows), :], sems.at[slot])
                cp.wait() if wait else cp.start()

            _block_copies(plan_ref[n_groups + g], issue)

    @pl.when(i == 0)
    def _():
        ybuf[...] = jnp.zeros_like(ybuf)
        groups(i, wait=False)

    @pl.when(i + 1 < pl.num_programs(0))
    def _():
        groups(i + 1, wait=False)

    col = lax.broadcasted_iota(jnp.int32, (tile, LOCAL_ROWS), 1).astype(F32)
    pick1 = jnp.where(lpos_ref[:, 0:1] == col, 1.0, 0.0).astype(BF16)
    pick2 = jnp.where(lpos_ref[:, 1:2] == col, 1.0, 0.0).astype(BF16)
    groups(i, wait=True)
    last = i * N_EXPERTS + N_EXPERTS - 1
    used = plan_ref[last] + plan_ref[n_groups + last] * SUBLANES
    row = lax.broadcasted_iota(jnp.int32, (LOCAL_ROWS, 1), 0)
    y = jnp.where(row < used, ybuf[i % 2], 0.0).astype(BF16)
    out_ref[...] = (x_ref[...] + route_ref[:, 0:1] * _dot(pick1, y)
                    + route_ref[:, 1:2] * _dot(pick2, y))


def _combine_call(x, route, lpos, y, plan):
    n = x.shape[0]
    tile = TOKEN_TILE
    n_groups = (n // tile) * N_EXPERTS
    row = lambda width: pl.BlockSpec((tile, width), lambda i, plan: (i, 0))
    grid_spec = pltpu.PrefetchScalarGridSpec(
        num_scalar_prefetch=1,
        grid=(n // tile,),
        in_specs=[row(D_MODEL), row(LANES), row(LANES), pl.BlockSpec(memory_space=pl.ANY)],
        out_specs=row(D_MODEL),
        scratch_shapes=[pltpu.VMEM((2, LOCAL_ROWS, D_MODEL), F32), pltpu.SemaphoreType.DMA((2,))],
    )
    return pl.pallas_call(
        functools.partial(_combine_kernel, tile=tile, n_groups=n_groups),
        grid_spec=grid_spec,
        out_shape=jax.ShapeDtypeStruct(x.shape, F32),
        compiler_params=_params("arbitrary"),
        name="moe_combine",
    )(plan, x, route, lpos, y)


def _routing_plan(route, n):
    tile, tm = TOKEN_TILE, MOE_ROW_TILE
    n_tiles_tok = n // tile
    slack = -(-(n_tiles_tok * N_EXPERTS * SUBLANES) // tm) * tm
    p = 2 * n + slack + N_EXPERTS * tm
    experts = jnp.arange(N_EXPERTS, dtype=jnp.int32)
    choice = route[:, 2:4].astype(jnp.int32).reshape(n_tiles_tok, tile, 2)
    counts = jnp.sum((choice[..., None] == experts).astype(jnp.int32), axis=(1, 2))
    size8 = (counts + SUBLANES - 1) // SUBLANES
    size = size8 * SUBLANES
    local = jnp.cumsum(size, axis=1) - size
    total = jnp.sum(size, axis=0)
    padded = ((total + tm - 1) // tm) * tm
    ends = jnp.cumsum(padded)
    starts = ends - padded
    dest = starts[None, :] + jnp.cumsum(size, axis=0) - size
    n_tiles = (ends[-1] // tm).astype(jnp.int32)
    tile_start = jnp.arange(p // tm, dtype=jnp.int32) * tm
    tile_expert = jnp.sum((tile_start[:, None] >= ends[None, :]).astype(jnp.int32), axis=1)
    last_expert = jnp.sum((ends[-1] - 1 >= ends).astype(jnp.int32))
    tile_expert = jnp.minimum(tile_expert, last_expert).astype(jnp.int32)
    tail = jnp.stack([ends[-1], (p - ends[-1]) // tile])
    plan = jnp.concatenate([local.reshape(-1), size8.reshape(-1), dest.reshape(-1),
                            starts + total, (padded - total) // SUBLANES, tail]).astype(jnp.int32)
    return plan, tile_expert, n_tiles.reshape(1), p


def _moe_layer(x, hn, route, w1, w3, w2):
    n = x.shape[0]
    plan, tile_expert, n_tiles, n_sorted = _routing_plan(route, n)
    tri = jnp.tril(jnp.ones((2 * TOKEN_TILE, 2 * TOKEN_TILE), BF16))
    hs, lpos = _dispatch_call(hn, route, plan, tri, n_sorted)
    ys = _moe_call(hs, w1.astype(BF16), w3.astype(BF16), w2.astype(BF16), tile_expert, n_tiles)
    return _combine_call(x, route, lpos, ys, plan)


def kernel(x, attn_norm, ffn_norm, w_in, q_norm, k_norm, conv_w, hgrn_lower_bounds,
           hgrn_out_norm, w_proj_a, w_proj_b, w_proj_c, w_out, dense_w1, dense_w3,
           dense_w2, router_w, moe_w1, moe_w3, moe_w2):
    batch, seq, d = x.shape
    n = batch * seq
    depth = w_in.shape[0]
    assert d == D_MODEL and seq % ATTN_TILE == 0 and seq % TOKEN_TILE == 0
    assert all(window // dilation == BAND for window, dilation in ATTN_GROUPS)

    lbs = jax.nn.softmax(hgrn_lower_bounds.astype(F32), axis=0)
    lower_bound = jnp.cumsum(lbs, axis=0) - lbs[0]
    head_of_col = jnp.arange(ATTN_WIDTH) // HEAD_DIM
    hsum = (head_of_col[:, None] == jnp.arange(LANES)[None, :]).astype(BF16)
    hexp = jnp.concatenate([hsum.T, hsum.T], axis=0)
    a0, a1 = 0, 3 * ATTN_WIDTH
    b1 = a1 + 3 * CONV_WIDTH
    c1 = b1 + 4 * HGRN_WIDTH

    xf = x.reshape(n, d)
    for layer in range(depth):
        w = w_in[layer]
        w_attn = w[:, a0:a1].astype(BF16)
        w_hgrn = w[:, b1:c1].astype(BF16)
        w_conv_gate = jnp.concatenate([w[:, a1:b1], w[:, c1:]], axis=1).astype(BF16)
        gain = attn_norm[layer].reshape(1, d)
        heads = ATTN_WIDTH // HEAD_DIM
        qg = jnp.tile(q_norm[layer] * (HEAD_DIM ** -0.5), heads).reshape(1, ATTN_WIDTH)
        kg = jnp.tile(k_norm[layer], heads).reshape(1, ATTN_WIDTH)

        qkv = _qkv_call(xf, gain, w_attn, qg, kg, hsum, hexp, batch, seq)
        oa = _attn_call(qkv, batch, seq)
        j = layer // 2
        routed = layer % 2 == 1
        expert_w = (moe_w1[j], moe_w3[j], moe_w2[j]) if routed else ()
        yc, expert_w = _hgrn_call(xf, gain, w_hgrn, lower_bound[layer].reshape(1, HGRN_WIDTH),
                                  hgrn_out_norm[layer].reshape(1, HGRN_HEAD),
                                  w_proj_c[layer].astype(BF16), batch, seq, expert_w)
        fgain = ffn_norm[layer].reshape(1, d)
        router = None
        if routed:
            rw = jnp.pad(router_w[j], ((0, 0), (0, LANES - N_EXPERTS)))
            rw_hi = rw.astype(BF16)
            router = (fgain, rw_hi, (rw - rw_hi.astype(F32)).astype(BF16))
        merged = _merge_call(xf, gain, w_conv_gate, conv_w[layer], oa, yc,
                             w_proj_a[layer].astype(BF16), w_proj_b[layer].astype(BF16),
                             w_out[layer].astype(BF16), seq, router)
        if routed:
            xf, hn, route = merged
            xf = _moe_layer(xf, hn, route, *expert_w)
        else:
            xf = _ffn_call(merged, fgain, dense_w1[j].astype(BF16), dense_w3[j].astype(BF16),
                           dense_w2[j].astype(BF16))
    return xf.reshape(batch, seq, d)
```

```python
import functools

import jax
import jax.numpy as jnp
from jax import lax
from jax.experimental import pallas as pl
from jax.experimental.pallas import tpu as pltpu

F32 = jnp.float32
BF16 = jnp.bfloat16

D_MODEL = 1024
HEAD_DIM = 64
ATTN_GROUPS = ((128, 1), (512, 4), (2048, 16))
HEADS_PER_GROUP = 4
GROUP_WIDTH = HEADS_PER_GROUP * HEAD_DIM
ATTN_WIDTH = GROUP_WIDTH * len(ATTN_GROUPS)
BAND = 128
CONV_WIDTH = 768
CONV_K = 3
HGRN_WIDTH = 768
HGRN_HEAD = 128
HGRN_HEADS = HGRN_WIDTH // HGRN_HEAD
LB_FLOOR = 1e-30
N_EXPERTS = 8
RMS_EPS = 1e-6
NEG_BIG = -1e30
LOG2E = 1.4426950408889634

LANES = 128
SUBLANES = 8

TOKEN_TILE = 512
ATTN_TILE = ATTN_GROUPS[-1][1] * BAND
HGRN_CHUNK = 256
MOE_ROW_TILE = 512
MOE_FF_TILE = 1792
LOCAL_ROWS = -(-(2 * TOKEN_TILE + N_EXPERTS * SUBLANES) // LANES) * LANES
GROUP_BLOCKS = tuple(1 << b for b in reversed(range(8)))
FILL_BLOCKS = tuple(b for b in GROUP_BLOCKS if b * SUBLANES < MOE_ROW_TILE)
VMEM_LIMIT = 56 * 1024 * 1024


def _dot(a, b):
    return jnp.dot(a, b, preferred_element_type=F32)


def _dot_nt(a, b):
    return lax.dot_general(a, b, (((1,), (1,)), ((), ())), preferred_element_type=F32)


def _dot_tn(a, b):
    return lax.dot_general(a, b, (((0,), (0,)), ((), ())), preferred_element_type=F32)


def _rms_rows(xf, gain):
    ms = jnp.mean(xf * xf, axis=-1, keepdims=True)
    return xf * lax.rsqrt(ms + RMS_EPS) * gain


def _sigmoid(z):
    return 1.0 / (1.0 + jnp.exp(-z))


def _params(*sem):
    return pltpu.CompilerParams(dimension_semantics=sem, vmem_limit_bytes=VMEM_LIMIT)


def _full(shape):
    return pl.BlockSpec(shape, lambda *_: (0,) * len(shape))


def _qkv_kernel(x_ref, g_ref, w_ref, qg_ref, kg_ref, hsum_ref, hexp_ref, *rest, tile):
    out_refs, stage = rest[:9], rest[9]
    h = _rms_rows(x_ref[0], g_ref[...]).astype(BF16)
    p = _dot(h, w_ref[...])

    def head_norm(t, gain):
        ms = _dot((t * t).astype(BF16), hsum_ref[...]) * (1.0 / HEAD_DIM)
        r = lax.rsqrt(ms + RMS_EPS)
        r_hi = r.astype(BF16)
        r_lo = (r - r_hi.astype(F32)).astype(BF16)
        r_cols = _dot(jnp.concatenate([r_hi, r_lo], axis=1), hexp_ref[...])
        return t * r_cols * gain

    parts = (head_norm(p[:, :ATTN_WIDTH], qg_ref[...]),
             head_norm(p[:, ATTN_WIDTH:2 * ATTN_WIDTH], kg_ref[...]),
             p[:, 2 * ATTN_WIDTH:])
    slot = 0
    for a, part in enumerate(parts):
        for group, (_, dilation) in enumerate(ATTN_GROUPS):
            out = out_refs[3 * group + a]
            c0 = group * GROUP_WIDTH
            if dilation == 1:
                out[0, 0] = part[:, c0:c0 + GROUP_WIDTH].astype(BF16)
                continue
            rows = tile // dilation
            for half in range(GROUP_WIDTH // LANES):
                stage[slot] = part[:, c0 + half * LANES:c0 + (half + 1) * LANES]
                for r in range(dilation):
                    out[0, r, :, half * LANES:(half + 1) * LANES] = (
                        stage[slot, pl.ds(r, rows, stride=dilation), :].astype(BF16))
                slot += 1


def _qkv_call(x, gain, w, qg, kg, hsum, hexp, batch, seq):
    t = TOKEN_TILE
    out_specs, out_shape = [], []
    for _, dilation in ATTN_GROUPS:
        for _ in range(3):
            out_specs.append(pl.BlockSpec((1, dilation, t // dilation, GROUP_WIDTH),
                                          lambda b, i: (b, 0, i, 0)))
            out_shape.append(jax.ShapeDtypeStruct(
                (batch, dilation, seq // dilation, GROUP_WIDTH), BF16))
    n_stage = 3 * (GROUP_WIDTH // LANES) * sum(1 for _, d in ATTN_GROUPS if d > 1)
    return pl.pallas_call(
        functools.partial(_qkv_kernel, tile=t),
        grid=(batch, seq // t),
        in_specs=[pl.BlockSpec((1, t, D_MODEL), lambda b, i: (b, i, 0)), _full((1, D_MODEL)),
                  _full(w.shape), _full((1, ATTN_WIDTH)), _full((1, ATTN_WIDTH)),
                  _full(hsum.shape), _full(hexp.shape)],
        out_specs=out_specs,
        out_shape=out_shape,
        scratch_shapes=[pltpu.VMEM((n_stage, t, LANES), F32)],
        compiler_params=_params("parallel", "parallel"),
        name="qkv_proj",
    )(x.reshape(batch, seq, D_MODEL), gain, w, qg, kg, hsum, hexp)


def _attn_kernel(*refs, n_groups):
    ins, o_ref = refs[:5 * n_groups], refs[5 * n_groups]
    o_tok, l_tok = refs[5 * n_groups + 1:]
    first_tile = pl.program_id(1) == 0
    qi = lax.broadcasted_iota(jnp.int32, (BAND, 2 * BAND), 0)
    kj = lax.broadcasted_iota(jnp.int32, (BAND, 2 * BAND), 1)
    rel = qi + BAND - kj
    in_window = (rel >= 0) & (rel <= BAND)
    halves = GROUP_WIDTH // LANES
    lane = lax.broadcasted_iota(jnp.int32, (1, LANES), 1)
    head_lanes = [jnp.where(lane // HEAD_DIM == side, 1.0, 0.0).astype(BF16)
                  for side in range(LANES // HEAD_DIM)]
    first_head = lax.broadcasted_iota(jnp.int32, (BAND, LANES), 1) < HEAD_DIM

    for group, (_, dilation) in enumerate(ATTN_GROUPS):
        q_ref, kc_ref, vc_ref, kp_ref, vp_ref = ins[5 * group:5 * group + 5]
        length = ATTN_TILE // dilation
        nj = length // BAND

        for idx in range(ATTN_TILE // BAND):
            r, j = idx // nj, idx % nj
            j0 = j * BAND
            if j == 0:
                valid = in_window & (kj >= jnp.where(first_tile, BAND, 0))
            else:
                valid = in_window
            tokens = pl.ds(r + j0 * dilation, BAND, stride=dilation) if dilation > 1 \
                else pl.ds(j0, BAND)
            for half in range(halves):
                lanes = slice(half * LANES, (half + 1) * LANES)
                q2 = q_ref[0, r, j0:j0 + BAND, lanes]
                if j == 0:
                    k2 = jnp.concatenate([kp_ref[0, r, :, lanes], kc_ref[0, r, 0:BAND, lanes]], axis=0)
                    v2 = jnp.concatenate([vp_ref[0, r, :, lanes], vc_ref[0, r, 0:BAND, lanes]], axis=0)
                else:
                    k2 = kc_ref[0, r, j0 - BAND:j0 + BAND, lanes]
                    v2 = vc_ref[0, r, j0 - BAND:j0 + BAND, lanes]
                o2, l2 = None, None
                for side in range(LANES // HEAD_DIM):
                    s = _dot_nt(q2 * head_lanes[side], k2)
                    s = jnp.where(valid, s, NEG_BIG)
                    m = jnp.max(s, axis=-1, keepdims=True)
                    p = jnp.exp(s - m)
                    den = jnp.sum(p, axis=-1, keepdims=True)
                    o = _dot(p.astype(BF16), v2) * (1.0 / den)
                    lse = jnp.broadcast_to(m + jnp.log(den), (BAND, LANES))
                    o2 = o if side == 0 else jnp.where(first_head, o2, o)
                    l2 = lse if side == 0 else jnp.where(first_head, l2, lse)
                o_tok[group, half, tokens, :] = o2
                l_tok[group, half, tokens, :] = l2

    chunk = 2 * BAND

    def merge(c, carry):
        rows = pl.ds(pl.multiple_of(c * chunk, chunk), chunk)
        for half in range(halves):
            ls = [l_tok[g, half, rows, :] for g in range(n_groups)]
            top = functools.reduce(jnp.maximum, ls)
            es = [jnp.exp(l - top) for l in ls]
            acc = sum(e * o_tok[g, half, rows, :] for g, e in enumerate(es))
            o_ref[0, rows, half * LANES:(half + 1) * LANES] = (acc * (1.0 / sum(es))).astype(BF16)
        return carry

    lax.fori_loop(0, ATTN_TILE // chunk, merge, 0)


def _attn_call(qkv, batch, seq):
    n_groups = len(ATTN_GROUPS)
    in_specs, args = [], []
    for group, (_, dilation) in enumerate(ATTN_GROUPS):
        length = ATTN_TILE // dilation
        cur = pl.BlockSpec((1, dilation, length, GROUP_WIDTH), lambda b, i: (b, 0, i, 0))
        halo = pl.BlockSpec(
            (1, dilation, BAND, GROUP_WIDTH),
            lambda b, i, nb=length // BAND: (b, 0, jnp.maximum(i * nb - 1, 0), 0))
        q, k, v = qkv[3 * group:3 * group + 3]
        in_specs += [cur, cur, cur, halo, halo]
        args += [q, k, v, k, v]
    halves = GROUP_WIDTH // LANES
    o = pl.pallas_call(
        functools.partial(_attn_kernel, n_groups=n_groups),
        grid=(batch, seq // ATTN_TILE),
        in_specs=in_specs,
        out_specs=pl.BlockSpec((1, ATTN_TILE, GROUP_WIDTH), lambda b, i: (b, i, 0)),
        out_shape=jax.ShapeDtypeStruct((batch, seq, GROUP_WIDTH), BF16),
        scratch_shapes=[pltpu.VMEM((n_groups, halves, ATTN_TILE, LANES), F32),
                        pltpu.VMEM((n_groups, halves, ATTN_TILE, LANES), F32)],
        compiler_params=_params("parallel", "arbitrary"),
        name="band_attn",
    )(*args)
    return o.reshape(batch * seq, GROUP_WIDTH)


def _hgrn_kernel(x_ref, g_ref, w_ref, lb_ref, og_ref, wp_ref, tri_ref, *rest,
                 tile, chunk, n_convert):
    to_convert, rest = rest[:n_convert], rest[n_convert:]
    y_ref, converted = rest[0], rest[1:1 + n_convert]
    state_ref, proj_ref, gated_ref, b_ref = rest[1 + n_convert:]
    for src, dst in zip(to_convert, converted):
        dst[...] = src[...].astype(BF16)

    @pl.when(pl.program_id(1) == 0)
    def _():
        state_ref[...] = jnp.zeros_like(state_ref)

    h = _rms_rows(x_ref[0], g_ref[...]).astype(BF16)
    proj_ref[...] = _dot(h, w_ref[...])

    lb = lb_ref[...]
    lb_floor = jnp.maximum(lb, LB_FLOOR)
    one_minus_lb = 1.0 - lb
    floor_gap = lb_floor - lb
    row = lax.broadcasted_iota(jnp.int32, (chunk, HGRN_HEAD), 0)
    tt = lax.broadcasted_iota(jnp.int32, (chunk, chunk), 0)
    ss = lax.broadcasted_iota(jnp.int32, (chunk, chunk), 1)
    t_xor_s = tt ^ ss
    w = HGRN_WIDTH
    sizes = [1 << i for i in range(chunk.bit_length() - 1)]
    lower_rows = [(row & m) == 0 for m in sizes]
    lower_sel = [jnp.where(low, 1.0, 0.0).astype(BF16) for low in lower_rows]
    signed_log2e = [jnp.where(low, -LOG2E, LOG2E).astype(BF16) for low in lower_rows]
    level_pairs = [((t_xor_s & (-m)) == m) & ((tt & m) != 0) for m in sizes]
    diagonal = tt == ss

    def chunk_body(c, carry):
        r0 = pl.multiple_of(c * chunk, chunk)
        rows = pl.ds(r0, chunk)
        z = proj_ref[rows, w:2 * w]
        e = jnp.exp(-jnp.abs(z))
        inv = 1.0 / (1.0 + e)
        sig = jnp.where(z >= 0, inv, e * inv)
        nsig = jnp.where(z >= 0, e * inv, inv)
        log_f = jnp.log(lb_floor + one_minus_lb * sig)
        k_all = one_minus_lb * nsig - floor_gap
        lf_hi = log_f.astype(BF16)
        rest = log_f - lf_hi.astype(F32)
        lf_mid = rest.astype(BF16)
        lf_lo = (rest - lf_mid.astype(F32)).astype(BF16)
        tri = tri_ref[...]
        b_all = _dot(tri, lf_hi) + _dot(tri, lf_mid) + _dot(tri, lf_lo)
        b_ref[...] = b_all
        q_all = proj_ref[rows, 0:w].astype(BF16)
        q_all = q_all * _sigmoid(q_all)
        v_all = proj_ref[rows, 2 * w:3 * w]
        g_all = proj_ref[rows, 3 * w:4 * w].astype(BF16)
        g_all = g_all * _sigmoid(g_all)
        for hd in range(HGRN_HEADS):
            cols = slice(hd * HGRN_HEAD, (hd + 1) * HGRN_HEAD)
            b, qb = b_all[:, cols], q_all[:, cols]
            kb = k_all[:, cols].astype(BF16)
            vb = v_all[:, cols].astype(BF16)
            scores = jnp.where(diagonal, _dot_nt(qb, kb), 0.0)
            last_of_block = b
            for lvl, m in enumerate(sizes):
                if m < SUBLANES:
                    lower = lower_rows[lvl]
                    boundary = jnp.where(lower, last_of_block, pltpu.roll(last_of_block, m, 0))
                    if 2 * m < SUBLANES:
                        last_of_block = jnp.where(
                            lower, pltpu.roll(last_of_block, chunk - m, 0), last_of_block)
                else:
                    boundary = jnp.concatenate(
                        [jnp.broadcast_to(b_ref[r:r + 1, cols], (2 * m, HGRN_HEAD))
                         for r in range(m - 1, chunk, 2 * m)], axis=0)
                decay = jnp.exp2((b - boundary).astype(BF16) * signed_log2e[lvl])
                if m < 2 * SUBLANES:
                    source = jnp.where(lower_sel[lvl] > 0, kb, qb)
                else:
                    source = jnp.concatenate(
                        [(kb if (r // m) % 2 == 0 else qb)[r:r + m] for r in range(0, chunk, m)],
                        axis=0)
                side = source * decay
                scores = jnp.where(level_pairs[lvl], _dot_nt(side, side), scores)
            state_t = state_ref[hd]
            q_dec = qb * jnp.exp(b).astype(BF16)
            o = _dot(scores.astype(BF16), vb) + _dot_nt(q_dec, state_t.astype(BF16))
            b_last = b[chunk - 1:chunk, :]
            k_dec = kb * jnp.exp(b_last - b).astype(BF16)
            state_ref[hd] = state_t * jnp.exp(b_last) + _dot_tn(vb, k_dec)
            o = _rms_rows(o, og_ref[...]).astype(BF16) * g_all[:, cols]
            gated_ref[rows, cols] = o
        return carry

    lax.fori_loop(0, tile // chunk, chunk_body, 0, unroll=True)
    y_ref[0] = _dot(gated_ref[...], wp_ref[...]).astype(BF16)


def _hgrn_call(x, gain, w, lb, out_gain, w_proj, batch, seq, convert=()):
    tile, chunk = TOKEN_TILE, HGRN_CHUNK
    tiles_per_seq = seq // tile
    steps = batch * tiles_per_seq
    tri = jnp.tril(jnp.ones((chunk, chunk), BF16))
    blk = lambda width: pl.BlockSpec((1, tile, width), lambda b, i: (b, i, 0))
    flat = [a.reshape(-1, a.shape[-1]) for a in convert]
    assert all(a.shape[0] % (steps * 2 * SUBLANES) == 0 for a in flat)
    slab = lambda a: pl.BlockSpec((a.shape[0] // steps, a.shape[1]),
                                  lambda b, i: (b * tiles_per_seq + i, 0))
    y, *converted = pl.pallas_call(
        functools.partial(_hgrn_kernel, tile=tile, chunk=chunk, n_convert=len(flat)),
        grid=(batch, tiles_per_seq),
        in_specs=[blk(D_MODEL), _full((1, D_MODEL)), _full(w.shape), _full((1, HGRN_WIDTH)),
                  _full((1, HGRN_HEAD)), _full(w_proj.shape), _full((chunk, chunk))]
                 + [slab(a) for a in flat],
        out_specs=[blk(D_MODEL)] + [slab(a) for a in flat],
        out_shape=[jax.ShapeDtypeStruct((batch, seq, D_MODEL), BF16)]
                  + [jax.ShapeDtypeStruct(a.shape, BF16) for a in flat],
        scratch_shapes=[pltpu.VMEM((HGRN_HEADS, HGRN_HEAD, HGRN_HEAD), F32),
                        pltpu.VMEM((tile, 4 * HGRN_WIDTH), F32),
                        pltpu.VMEM((tile, HGRN_WIDTH), BF16),
                        pltpu.VMEM((chunk, HGRN_WIDTH), F32)],
        compiler_params=_params("parallel", "arbitrary"),
        name="hgrn_branch",
    )(x.reshape(batch, seq, D_MODEL), gain, w, lb, out_gain, w_proj, tri, *flat)
    return y.reshape(batch * seq, D_MODEL), [c.reshape(a.shape) for c, a in zip(converted, convert)]


def _top2_route(h, rw_hi, rw_lo):
    logits = _dot(h, rw_hi) + _dot(h, rw_lo)
    lane = lax.broadcasted_iota(jnp.int32, logits.shape, 1)
    logits = jnp.where(lane < N_EXPERTS, logits, -jnp.inf)
    m1 = jnp.max(logits, axis=-1, keepdims=True)
    i1 = jnp.min(jnp.where(logits == m1, lane, LANES), axis=-1, keepdims=True)
    rest = jnp.where(lane == i1, -jnp.inf, logits)
    m2 = jnp.max(rest, axis=-1, keepdims=True)
    i2 = jnp.min(jnp.where(rest == m2, lane, LANES), axis=-1, keepdims=True)
    e = jnp.exp(m2 - m1)
    g1 = 1.0 / (1.0 + e)
    g2 = e * g1
    return jnp.where(
        lane == 0, g1, jnp.where(lane == 1, g2, jnp.where(
            lane == 2, i1.astype(F32), jnp.where(lane == 3, i2.astype(F32), 0.0))))


def _merge_kernel(x_ref, g_ref, w_ref, cw_ref, oa_ref, yc_ref, wpa_ref, wpb_ref, wo_ref, *rest,
                  tile, tiles_per_seq, steps, with_router):
    if with_router:
        fg_ref, rw_hi_ref, rw_lo_ref, out_ref, hn_ref, route_ref, u_ext, prev_ref = rest
    else:
        out_ref, u_ext = rest
    step = pl.program_id(0)

    @pl.when(step % tiles_per_seq == 0)
    def _():
        u_ext[0:SUBLANES, :] = jnp.zeros((SUBLANES, CONV_WIDTH), F32)

    def route_previous_tile():
        hn = _rms_rows(prev_ref[...], fg_ref[...])
        hn_ref[...] = hn
        route_ref[...] = _top2_route(hn.astype(BF16), rw_hi_ref[...], rw_lo_ref[...])

    def merge_tile():
        x = x_ref[...]
        h = _rms_rows(x, g_ref[...]).astype(BF16)
        p = _dot(h, w_ref[...])
        cw = CONV_WIDTH
        u = p[:, 2 * cw:3 * cw] * p[:, 0:cw]
        u_ext[SUBLANES:SUBLANES + tile, :] = u
        conv = cw_ref[0:1, :] * u
        for tap in range(1, CONV_K):
            conv = conv + cw_ref[tap:tap + 1, :] * u_ext[SUBLANES - tap:SUBLANES - tap + tile, :]
        u_ext[0:SUBLANES, :] = u[tile - SUBLANES:, :]
        yb = _dot((p[:, cw:2 * cw] * conv).astype(BF16), wpb_ref[...])
        ya = _dot(oa_ref[...], wpa_ref[...])

        d = D_MODEL
        gates = p[:, 3 * cw:]
        merged = (_sigmoid(gates[:, 0:d]) * ya + _sigmoid(gates[:, d:2 * d]) * yb
                  + _sigmoid(gates[:, 2 * d:3 * d]) * yc_ref[...].astype(F32))
        x_new = x + _dot(merged.astype(BF16), wo_ref[...])
        out_ref[...] = x_new
        return x_new

    if not with_router:
        merge_tile()
        return

    @pl.when(step == 0)
    def _():
        prev_ref[...] = jnp.zeros_like(prev_ref)

    @pl.when(step < steps)
    def _():
        route_previous_tile()
        prev_ref[...] = merge_tile()

    @pl.when(step == steps)
    def _():
        route_previous_tile()


def _merge_call(x, gain, w, conv_w, oa, yc, wpa, wpb, wo, seq, router=None):
    n = x.shape[0]
    tile = TOKEN_TILE
    steps = n // tile
    row = lambda width: pl.BlockSpec((tile, width), lambda i: (jnp.minimum(i, steps - 1), 0))
    late = lambda width: pl.BlockSpec((tile, width), lambda i: (jnp.maximum(i - 1, 0), 0))
    in_specs = [row(D_MODEL), _full((1, D_MODEL)), _full(w.shape), _full(conv_w.shape),
                row(GROUP_WIDTH), row(D_MODEL), _full(wpa.shape), _full(wpb.shape),
                _full(wo.shape)]
    args = [x, gain, w, conv_w, oa, yc, wpa, wpb, wo]
    out_specs, out_shape = [row(D_MODEL)], [jax.ShapeDtypeStruct((n, D_MODEL), F32)]
    scratch = [pltpu.VMEM((SUBLANES + tile, CONV_WIDTH), F32)]
    if router is not None:
        in_specs += [_full(a.shape) for a in router]
        args += list(router)
        out_specs += [late(D_MODEL), late(LANES)]
        out_shape += [jax.ShapeDtypeStruct((n, D_MODEL), F32),
                      jax.ShapeDtypeStruct((n, LANES), F32)]
        scratch.append(pltpu.VMEM((tile, D_MODEL), F32))
    outs = pl.pallas_call(
        functools.partial(_merge_kernel, tile=tile, tiles_per_seq=seq // tile, steps=steps,
                          with_router=router is not None),
        grid=(steps + (router is not None),),
        in_specs=in_specs,
        out_specs=out_specs,
        out_shape=out_shape,
        scratch_shapes=scratch,
        compiler_params=_params("arbitrary"),
        name="merge_out",
    )(*args)
    return outs[0] if router is None else outs


def _ffn_kernel(x_ref, g_ref, w1_ref, w3_ref, w2_ref, out_ref):
    x = x_ref[...]
    h = _rms_rows(x, g_ref[...]).astype(BF16)
    a = _dot(h, w1_ref[...])
    z = (a * _sigmoid(a) * _dot(h, w3_ref[...])).astype(BF16)
    out_ref[...] = x + _dot(z, w2_ref[...])


def _ffn_call(x, gain, w1, w3, w2):
    n = x.shape[0]
    tile = TOKEN_TILE
    row = pl.BlockSpec((tile, D_MODEL), lambda i: (i, 0))
    return pl.pallas_call(
        _ffn_kernel,
        grid=(n // tile,),
        in_specs=[row, _full((1, D_MODEL)), _full(w1.shape), _full(w3.shape), _full(w2.shape)],
        out_specs=row,
        out_shape=jax.ShapeDtypeStruct((n, D_MODEL), F32),
        compiler_params=_params("parallel"),
        name="dense_ffn",
    )(x, gain, w1, w3, w2)


def _block_copies(n8, issue, blocks=GROUP_BLOCKS):
    done = jnp.int32(0)
    for bit in blocks:
        take = (n8 & bit) != 0

        @pl.when(take)
        def _(done=done, bit=bit):
            issue(pl.multiple_of(done * SUBLANES, SUBLANES), bit * SUBLANES)

        done = done + jnp.where(take, bit, 0)


def _slot_positions(route, tri, local_base):
    expert = jnp.concatenate([route[:, 2:3], route[:, 3:4]], axis=0)
    lane = lax.broadcasted_iota(jnp.int32, (expert.shape[0], LANES), 1).astype(F32)
    onehot = expert == lane
    seen = _dot(tri, jnp.where(onehot, 1.0, 0.0).astype(BF16))
    pos = jnp.sum(jnp.where(onehot, seen - 1.0 + local_base, 0.0), axis=1, keepdims=True)
    t = route.shape[0]
    return pos[:t], pos[t:]


def _local_base_row(plan_ref, tile_index):
    lane = lax.broadcasted_iota(jnp.int32, (1, LANES), 1)
    row = jnp.zeros((1, LANES), F32)
    for e in range(N_EXPERTS):
        row = jnp.where(lane == e, plan_ref[tile_index * N_EXPERTS + e].astype(F32), row)
    return row


def _dispatch_kernel(plan_ref, hn_ref, route_ref, tri_ref, out_hbm, lpos_ref, sorted_ref, zeros,
                     sems, fill_sem, *, tile, n_groups):
    i = pl.program_id(0)
    fill0 = 3 * n_groups

    def fills(wait):
        for e in range(N_EXPERTS):
            first = pl.multiple_of(plan_ref[fill0 + e], SUBLANES)

            def issue(off, rows, first=first):
                cp = pltpu.make_async_copy(zeros.at[pl.ds(0, rows), :],
                                           out_hbm.at[pl.ds(first + off, rows), :], fill_sem)
                cp.wait() if wait else cp.start()

            _block_copies(plan_ref[fill0 + N_EXPERTS + e], issue, FILL_BLOCKS)

        def tail(k, carry):
            start = pl.multiple_of(plan_ref[fill0 + 2 * N_EXPERTS] + k * tile, tile)
            cp = pltpu.make_async_copy(zeros, out_hbm.at[pl.ds(start, tile), :], fill_sem)
            cp.wait() if wait else cp.start()
            return carry

        lax.fori_loop(0, plan_ref[fill0 + 2 * N_EXPERTS + 1], tail, 0)

    @pl.when(i == 0)
    def _():
        zeros[...] = jnp.zeros_like(zeros)
        fills(wait=False)
        fills(wait=True)

    pos1, pos2 = _slot_positions(route_ref[...], tri_ref[...], _local_base_row(plan_ref, i))
    lane = lax.broadcasted_iota(jnp.int32, (tile, LANES), 1)
    lpos_ref[...] = jnp.where(lane == 0, pos1, jnp.where(lane == 1, pos2, 0.0))
    col = lax.broadcasted_iota(jnp.int32, (tile, LOCAL_ROWS), 1).astype(F32)
    select = jnp.where((pos1 == col) | (pos2 == col), 1.0, 0.0).astype(BF16)

    def groups(step, wait):
        slot = step % 2
        for e in range(N_EXPERTS):
            g = step * N_EXPERTS + e
            local = pl.multiple_of(plan_ref[g], SUBLANES)
            dest = pl.multiple_of(plan_ref[2 * n_groups + g], SUBLANES)

            def issue(off, rows, local=local, dest=dest):
                cp = pltpu.make_async_copy(sorted_ref.at[slot, pl.ds(local + off, rows), :],
                                           out_hbm.at[pl.ds(dest + off, rows), :], sems.at[slot])
                cp.wait() if wait else cp.start()

            _block_copies(plan_ref[n_groups + g], issue)

    @pl.when(i > 1)
    def _():
        groups(i - 2, wait=True)

    sorted_ref[i % 2] = _dot_tn(select, hn_ref[...].astype(BF16))
    groups(i, wait=False)

    @pl.when(i == pl.num_programs(0) - 1)
    def _():
        @pl.when(i > 0)
        def _():
            groups(i - 1, wait=True)

        groups(i, wait=True)


def _dispatch_call(hn, route, plan, tri, n_sorted):
    n = hn.shape[0]
    tile = TOKEN_TILE
    n_groups = (n // tile) * N_EXPERTS
    row = lambda width: pl.BlockSpec((tile, width), lambda i, plan: (i, 0))
    grid_spec = pltpu.PrefetchScalarGridSpec(
        num_scalar_prefetch=1,
        grid=(n // tile,),
        in_specs=[row(D_MODEL), row(LANES),
                  pl.BlockSpec(tri.shape, lambda i, plan: (0, 0))],
        out_specs=[pl.BlockSpec(memory_space=pl.ANY), row(LANES)],
        scratch_shapes=[pltpu.VMEM((2, LOCAL_ROWS, D_MODEL), F32), pltpu.VMEM((tile, D_MODEL), F32),
                        pltpu.SemaphoreType.DMA((2,)), pltpu.SemaphoreType.DMA(())],
    )
    return pl.pallas_call(
        functools.partial(_dispatch_kernel, tile=tile, n_groups=n_groups),
        grid_spec=grid_spec,
        out_shape=[jax.ShapeDtypeStruct((n_sorted, D_MODEL), F32),
                   jax.ShapeDtypeStruct((n, LANES), F32)],
        compiler_params=_params("arbitrary"),
        name="moe_dispatch",
    )(plan, hn, route, tri)


def _moe_kernel(tile_expert_ref, n_tiles_ref, h_ref, w1_ref, w3_ref, w2_ref, y_ref):
    i, f = pl.program_id(0), pl.program_id(1)
    active = i < n_tiles_ref[0]

    @pl.when((i < 2) & (f == 0))
    def _():
        y_ref[...] = jnp.zeros_like(y_ref)

    @pl.when(active)
    def _():
        h = h_ref[...].astype(BF16)
        a = _dot(h, w1_ref[...])
        z = (a * _sigmoid(a) * _dot(h, w3_ref[...])).astype(BF16)
        y_ref[...] = jnp.where(f == 0, 0.0, y_ref[...]) + _dot(z, w2_ref[...])

    @pl.when(jnp.logical_not(active) & (f == 0))
    def _():
        y_ref[...] = jnp.zeros_like(y_ref)


def _moe_call(hs, w1, w3, w2, tile_expert, n_tiles):
    p = hs.shape[0]
    d_ff = w1.shape[-1]
    tm, tf = MOE_ROW_TILE, MOE_FF_TILE
    nf = d_ff // tf

    def ff_block(i, f, n_tiles_ref):
        return jnp.where(i < n_tiles_ref[0], f, nf - 1)

    grid_spec = pltpu.PrefetchScalarGridSpec(
        num_scalar_prefetch=2,
        grid=(p // tm, nf),
        in_specs=[
            pl.BlockSpec((tm, D_MODEL), lambda i, f, te, nt: (jnp.minimum(i, nt[0] - 1), 0)),
            pl.BlockSpec((None, D_MODEL, tf), lambda i, f, te, nt: (te[i], 0, ff_block(i, f, nt))),
            pl.BlockSpec((None, D_MODEL, tf), lambda i, f, te, nt: (te[i], 0, ff_block(i, f, nt))),
            pl.BlockSpec((None, tf, D_MODEL), lambda i, f, te, nt: (te[i], ff_block(i, f, nt), 0)),
        ],
        out_specs=pl.BlockSpec((tm, D_MODEL), lambda i, f, te, nt: (i, 0)),
    )
    return pl.pallas_call(
        _moe_kernel,
        grid_spec=grid_spec,
        out_shape=jax.ShapeDtypeStruct((p, D_MODEL), F32),
        compiler_params=_params("arbitrary", "arbitrary"),
        name="moe_experts",
    )(tile_expert, n_tiles, hs, w1, w3, w2)


def _combine_kernel(plan_ref, x_ref, route_ref, lpos_ref, y_hbm, out_ref, ybuf, sems, *,
                    tile, n_groups):
    i = pl.program_id(0)

    def groups(step, wait):
        slot = step % 2
        for e in range(N_EXPERTS):
            g = step * N_EXPERTS + e
            local = pl.multiple_of(plan_ref[g], SUBLANES)
            src = pl.multiple_of(plan_ref[2 * n_groups + g], SUBLANES)

            def issue(off, rows, local=local, src=src):
                cp = pltpu.make_async_copy(y_hbm.at[pl.ds(src + off, rows), :],
                                           ybuf.at[slot, pl.ds(local + off, rows), :], sems.at[slot])
                cp.wait() if wait else cp.start()

            _block_copies(plan_ref[n_groups + g], issue)

    @pl.when(i == 0)
    def _():
        ybuf[...] = jnp.zeros_like(ybuf)
        groups(i, wait=False)

    @pl.when(i + 1 < pl.num_programs(0))
    def _():
        groups(i + 1, wait=False)

    col = lax.broadcasted_iota(jnp.int32, (tile, LOCAL_ROWS), 1).astype(F32)
    pick1 = jnp.where(lpos_ref[:, 0:1] == col, 1.0, 0.0).astype(BF16)
    pick2 = jnp.where(lpos_ref[:, 1:2] == col, 1.0, 0.0).astype(BF16)
    groups(i, wait=True)
    last = i * N_EXPERTS + N_EXPERTS - 1
    used = plan_ref[last] + plan_ref[n_groups + last] * SUBLANES
    row = lax.broadcasted_iota(jnp.int32, (LOCAL_ROWS, 1), 0)
    y = jnp.where(row < used, ybuf[i % 2], 0.0).astype(BF16)
    out_ref[...] = (x_ref[...] + route_ref[:, 0:1] * _dot(pick1, y)
                    + route_ref[:, 1:2] * _dot(pick2, y))


def _combine_call(x, route, lpos, y, plan):
    n = x.shape[0]
    tile = TOKEN_TILE
    n_groups = (n // tile) * N_EXPERTS
    row = lambda width: pl.BlockSpec((tile, width), lambda i, plan: (i, 0))
    grid_spec = pltpu.PrefetchScalarGridSpec(
        num_scalar_prefetch=1,
        grid=(n // tile,),
        in_specs=[row(D_MODEL), row(LANES), row(LANES), pl.BlockSpec(memory_space=pl.ANY)],
        out_specs=row(D_MODEL),
        scratch_shapes=[pltpu.VMEM((2, LOCAL_ROWS, D_MODEL), F32), pltpu.SemaphoreType.DMA((2,))],
    )
    return pl.pallas_call(
        functools.partial(_combine_kernel, tile=tile, n_groups=n_groups),
        grid_spec=grid_spec,
        out_shape=jax.ShapeDtypeStruct(x.shape, F32),
        compiler_params=_params("arbitrary"),
        name="moe_combine",
    )(plan, x, route, lpos, y)


def _routing_plan(route, n):
    tile, tm = TOKEN_TILE, MOE_ROW_TILE
    n_tiles_tok = n // tile
    slack = -(-(n_tiles_tok * N_EXPERTS * SUBLANES) // tm) * tm
    p = 2 * n + slack + N_EXPERTS * tm
    experts = jnp.arange(N_EXPERTS, dtype=jnp.int32)
    choice = route[:, 2:4].astype(jnp.int32).reshape(n_tiles_tok, tile, 2)
    counts = jnp.sum((choice[..., None] == experts).astype(jnp.int32), axis=(1, 2))
    size8 = (counts + SUBLANES - 1) // SUBLANES
    size = size8 * SUBLANES
    local = jnp.cumsum(size, axis=1) - size
    total = jnp.sum(size, axis=0)
    padded = ((total + tm - 1) // tm) * tm
    ends = jnp.cumsum(padded)
    starts = ends - padded
    dest = starts[None, :] + jnp.cumsum(size, axis=0) - size
    n_tiles = (ends[-1] // tm).astype(jnp.int32)
    tile_start = jnp.arange(p // tm, dtype=jnp.int32) * tm
    tile_expert = jnp.sum((tile_start[:, None] >= ends[None, :]).astype(jnp.int32), axis=1)
    last_expert = jnp.sum((ends[-1] - 1 >= ends).astype(jnp.int32))
    tile_expert = jnp.minimum(tile_expert, last_expert).astype(jnp.int32)
    tail = jnp.stack([ends[-1], (p - ends[-1]) // tile])
    plan = jnp.concatenate([local.reshape(-1), size8.reshape(-1), dest.reshape(-1),
                            starts + total, (padded - total) // SUBLANES, tail]).astype(jnp.int32)
    return plan, tile_expert, n_tiles.reshape(1), p


def _moe_layer(x, hn, route, w1, w3, w2):
    n = x.shape[0]
    plan, tile_expert, n_tiles, n_sorted = _routing_plan(route, n)
    tri = jnp.tril(jnp.ones((2 * TOKEN_TILE, 2 * TOKEN_TILE), BF16))
    hs, lpos = _dispatch_call(hn, route, plan, tri, n_sorted)
    ys = _moe_call(hs, w1.astype(BF16), w3.astype(BF16), w2.astype(BF16), tile_expert, n_tiles)
    return _combine_call(x, route, lpos, ys, plan)


def _split_w_in_kernel(w_ref, attn_ref, hgrn_ref, conv_gate_ref, *, cuts):
    a0, a1, b1, c1 = cuts
    attn_ref[...] = w_ref[:, a0:a1].astype(BF16)
    hgrn_ref[...] = w_ref[:, b1:c1].astype(BF16)
    conv_gate_ref[:, :b1 - a1] = w_ref[:, a1:b1].astype(BF16)
    conv_gate_ref[:, b1 - a1:] = w_ref[:, c1:].astype(BF16)


def _split_w_in_call(w_in, layer, cuts):
    _, rows, cols = w_in.shape
    a0, a1, b1, c1 = cuts
    widths = (a1 - a0, c1 - b1, (b1 - a1) + (cols - c1))
    block = rows // 8
    return pl.pallas_call(
        functools.partial(_split_w_in_kernel, cuts=cuts),
        grid=(rows // block,),
        in_specs=[pl.BlockSpec((None, block, cols), lambda i: (layer, i, 0))],
        out_specs=[pl.BlockSpec((block, width), lambda i: (i, 0)) for width in widths],
        out_shape=[jax.ShapeDtypeStruct((rows, width), BF16) for width in widths],
        compiler_params=_params("parallel"),
        name="split_w_in",
    )(w_in)


def kernel(x, attn_norm, ffn_norm, w_in, q_norm, k_norm, conv_w, hgrn_lower_bounds,
           hgrn_out_norm, w_proj_a, w_proj_b, w_proj_c, w_out, dense_w1, dense_w3,
           dense_w2, router_w, moe_w1, moe_w3, moe_w2):
    batch, seq, d = x.shape
    n = batch * seq
    depth = w_in.shape[0]
    assert d == D_MODEL and seq % ATTN_TILE == 0 and seq % TOKEN_TILE == 0
    assert all(window // dilation == BAND for window, dilation in ATTN_GROUPS)

    lbs = jax.nn.softmax(hgrn_lower_bounds.astype(F32), axis=0)
    lower_bound = jnp.cumsum(lbs, axis=0) - lbs[0]
    head_of_col = jnp.arange(ATTN_WIDTH) // HEAD_DIM
    hsum = (head_of_col[:, None] == jnp.arange(LANES)[None, :]).astype(BF16)
    hexp = jnp.concatenate([hsum.T, hsum.T], axis=0)
    a0, a1 = 0, 3 * ATTN_WIDTH
    b1 = a1 + 3 * CONV_WIDTH
    c1 = b1 + 4 * HGRN_WIDTH

    xf = x.reshape(n, d)
    for layer in range(depth):
        w_attn, w_hgrn, w_conv_gate = _split_w_in_call(w_in, layer, (a0, a1, b1, c1))
        gain = attn_norm[layer].reshape(1, d)
        heads = ATTN_WIDTH // HEAD_DIM
        qg = jnp.tile(q_norm[layer] * (HEAD_DIM ** -0.5), heads).reshape(1, ATTN_WIDTH)
        kg = jnp.tile(k_norm[layer], heads).reshape(1, ATTN_WIDTH)

        qkv = _qkv_call(xf, gain, w_attn, qg, kg, hsum, hexp, batch, seq)
        oa = _attn_call(qkv, batch, seq)
        j = layer // 2
        routed = layer % 2 == 1
        expert_w = (moe_w1[j], moe_w3[j], moe_w2[j]) if routed else ()
        yc, expert_w = _hgrn_call(xf, gain, w_hgrn, lower_bound[layer].reshape(1, HGRN_WIDTH),
                                  hgrn_out_norm[layer].reshape(1, HGRN_HEAD),
                                  w_proj_c[layer].astype(BF16), batch, seq, expert_w)
        fgain = ffn_norm[layer].reshape(1, d)
        router = None
        if routed:
            rw = jnp.pad(router_w[j], ((0, 0), (0, LANES - N_EXPERTS)))
            rw_hi = rw.astype(BF16)
            router = (fgain, rw_hi, (rw - rw_hi.astype(F32)).astype(BF16))
        merged = _merge_call(xf, gain, w_conv_gate, conv_w[layer], oa, yc,
                             w_proj_a[layer].astype(BF16), w_proj_b[layer].astype(BF16),
                             w_out[layer].astype(BF16), seq, router)
        if routed:
            xf, hn, route = merged
            xf = _moe_layer(xf, hn, route, *expert_w)
        else:
            xf = _ffn_call(merged, fgain, dense_w1[j].astype(BF16), dense_w3[j].astype(BF16),
                           dense_w2[j].astype(BF16))
    return xf.reshape(batch, seq, d)
```
